```python
import math
import jax, jax.numpy as jnp
from jax import lax
import numpy as np

D_MODEL = 1024
BATCH = 16
SEQ = 2048
DEPTH = 2

N_MIXERS = 2
N_DIFF = (DEPTH + 1) // 2
N_RET = DEPTH // 2

DA_HEADS = 8
DA_HEAD_DIM = 64
DA_V_DIM = 2 * DA_HEAD_DIM
Q_BLOCK = 128
RET_HEADS = 4
RET_DK = 256
RET_DV = 512
RET_CHUNK = 128
N_EXPERTS = 64
TOP_K = 8
N_GROUPS = 8
TOPK_GROUPS = 4
EXPERT_FF = 256
SHARED_FF = 256
ROUTED_SCALE = 2.5
DISPATCH_BLOCK = 128
EPS = 1e-6

kernel_name = "hybrid_diffattn_retnet_moe_adaln"


def rmsnorm(x, g):
    xf = x.astype(jnp.float32)
    y = xf * lax.rsqrt(jnp.mean(xf * xf, axis=-1, keepdims=True) + EPS)
    return (y * g.astype(jnp.float32)).astype(x.dtype)


def alibi_slopes(n_heads):
    return 2.0 ** (-8.0 * jnp.arange(1, n_heads + 1, dtype=jnp.float32) / n_heads)


def diff_attention(xn, w_in, w_out, lq1, lk1, lq2, lk2, subln_g, layer_idx):
    B, S, _ = xn.shape
    H, dh, dv = DA_HEADS, DA_HEAD_DIM, DA_V_DIM
    proj = xn @ w_in
    q, k, v = jnp.split(proj, [2 * H * dh, 4 * H * dh], axis=-1)
    q = q.reshape(B, S, 2 * H, dh).transpose(0, 2, 1, 3) * (dh ** -0.5)
    k = k.reshape(B, S, 2 * H, dh).transpose(0, 2, 1, 3)
    v = v.reshape(B, S, H, dv).transpose(0, 2, 1, 3)
    lam_init = 0.8 - 0.6 * math.exp(-0.3 * layer_idx)
    lam = (jnp.exp(jnp.sum(lq1.astype(jnp.float32) * lk1.astype(jnp.float32)))
           - jnp.exp(jnp.sum(lq2.astype(jnp.float32) * lk2.astype(jnp.float32))) + lam_init)
    slopes2 = jnp.repeat(alibi_slopes(H), 2)
    n_blk = S // Q_BLOCK
    q_blocks = q.reshape(B, 2 * H, n_blk, Q_BLOCK, dh).transpose(2, 0, 1, 3, 4)
    kpos = jnp.arange(S)

    def one_block(args):
        qb, b_idx = args
        qpos = b_idx * Q_BLOCK + jnp.arange(Q_BLOCK)
        dist = (qpos[:, None] - kpos[None, :]).astype(jnp.float32)
        s = jnp.einsum('bhqd,bhkd->bhqk', qb, k).astype(jnp.float32) - slopes2[:, None, None] * dist
        s = jnp.where(dist >= 0, s, -jnp.inf)
        p = jax.nn.softmax(s, axis=-1).reshape(B, H, 2, Q_BLOCK, S)
        a = p[:, :, 0] - lam * p[:, :, 1]
        return jnp.einsum('bhqk,bhkv->bhqv', a.astype(v.dtype), v)

    o = lax.map(one_block, (q_blocks, jnp.arange(n_blk)))
    o = o.transpose(1, 0, 3, 2, 4).reshape(B, S, H, dv)
    o = rmsnorm(o, subln_g) * (1.0 - lam_init)
    return o.reshape(B, S, H * dv) @ w_out


def retention(xn, w_in, w_out, gn_g):
    B, S, _ = xn.shape
    H, dk, dv, C = RET_HEADS, RET_DK, RET_DV, RET_CHUNK
    n_c = S // C
    proj = (xn @ w_in).astype(jnp.float32)
    q, k, v, g = jnp.split(proj, [H * dk, 2 * H * dk, 2 * H * dk + H * dv], axis=-1)
    q = q.reshape(B, n_c, C, H, dk).transpose(1, 0, 3, 2, 4)
    k = k.reshape(B, n_c, C, H, dk).transpose(1, 0, 3, 2, 4) * (dk ** -0.5)
    v = v.reshape(B, n_c, C, H, dv).transpose(1, 0, 3, 2, 4)
    log_g = jnp.log1p(-(2.0 ** (-5.0 - jnp.arange(H, dtype=jnp.float32))))
    idx = jnp.arange(C, dtype=jnp.float32)
    diff = idx[:, None] - idx[None, :]
    decay = jnp.where(diff >= 0, jnp.exp(log_g[:, None, None] * jnp.maximum(diff, 0.0)), 0.0)
    zeta = jnp.exp(log_g[:, None] * (C - 1 - idx))
    xi = jnp.exp(log_g[:, None] * (idx + 1))
    g_c = jnp.exp(log_g * C)

    def step(R, inp):
        qc, kc, vc = inp
        inner = jnp.einsum('bhid,bhjd->bhij', qc, kc) * decay
        o = (jnp.einsum('bhij,bhjv->bhiv', inner, vc)
             + jnp.einsum('bhid,bhdv->bhiv', qc, R) * xi[:, :, None])
        R = g_c[:, None, None] * R + jnp.einsum('bhjd,bhjv->bhdv', kc * zeta[:, :, None], vc)
        return R, o

    R0 = jnp.zeros((B, H, dk, dv), jnp.float32)
    _, o = lax.scan(step, R0, (q, k, v))
    o = o.transpose(1, 0, 3, 2, 4).reshape(B, S, H, dv)
    o = rmsnorm(o, gn_g.reshape(H, dv))
    y = jax.nn.silu(g) * o.reshape(B, S, H * dv)
    return y.astype(xn.dtype) @ w_out


def moe(xm, router_w, router_b, w1, w3, w2, s_w1, s_w3, s_w2):
    B, S, D = xm.shape
    T = B * S
    E, K, BLK = N_EXPERTS, TOP_K, DISPATCH_BLOCK
    xt = xm.reshape(T, D)
    scores = jax.nn.sigmoid((xt @ router_w).astype(jnp.float32))
    choice = scores + router_b.astype(jnp.float32)
    grp = choice.reshape(T, N_GROUPS, E // N_GROUPS)
    grp_score = lax.top_k(grp, 2)[0].sum(-1)
    _, grp_idx = lax.top_k(grp_score, TOPK_GROUPS)
    grp_mask = jax.nn.one_hot(grp_idx, N_GROUPS, dtype=jnp.float32).sum(1) > 0
    choice = jnp.where(jnp.repeat(grp_mask, E // N_GROUPS, axis=1), choice, -jnp.inf)
    _, top_idx = lax.top_k(choice, K)
    top_w = jnp.take_along_axis(scores, top_idx, axis=1)
    top_w = top_w / jnp.sum(top_w, axis=-1, keepdims=True) * ROUTED_SCALE

    A = T * K
    flat_e = top_idx.reshape(A).astype(jnp.int32)
    flat_tok = jnp.arange(A, dtype=jnp.int32) // K
    flat_w = top_w.reshape(A)
    order = jnp.argsort(flat_e)
    e_s, tok_s, w_s = flat_e[order], flat_tok[order], flat_w[order]
    counts = jnp.bincount(flat_e, length=E).astype(jnp.int32)
    starts = jnp.cumsum(counts) - counts
    pad_counts = (counts + BLK - 1) // BLK * BLK
    pad_ends = jnp.cumsum(pad_counts)
    pad_starts = pad_ends - pad_counts
    dest = pad_starts[e_s] + jnp.arange(A, dtype=jnp.int32) - starts[e_s]
    n_blocks = -(-A // BLK) + E
    P = n_blocks * BLK
    buf_tok = jnp.full((P,), T, jnp.int32).at[dest].set(tok_s)
    buf_w = jnp.zeros((P,), jnp.float32).at[dest].set(w_s)
    blk_e = jnp.minimum(jnp.searchsorted(pad_ends, jnp.arange(n_blocks, dtype=jnp.int32) * BLK,
                                         side='right'), E - 1)
    x_pad = jnp.concatenate([xt, jnp.zeros((1, D), xt.dtype)], axis=0)

    def blk_step(acc, inp):
        tok, wgt, e = inp
        xb = x_pad[tok]
        h = jax.nn.silu(xb @ w1[e]) * (xb @ w3[e])
        yb = (h @ w2[e]) * wgt[:, None].astype(xb.dtype)
        return acc.at[tok].add(yb), None

    acc0 = jnp.zeros((T + 1, D), xt.dtype)
    acc, _ = lax.scan(blk_step, acc0,
                      (buf_tok.reshape(n_blocks, BLK), buf_w.reshape(n_blocks, BLK), blk_e))
    shared = (jax.nn.silu(xt @ s_w1) * (xt @ s_w3)) @ s_w2
    return (acc[:T] + shared).reshape(B, S, D)


def setup_inputs(seed: int = 0) -> dict:
    key = jax.random.key(seed)
    ks = jax.random.split(key, 25)
    D, F, SF, E = D_MODEL, EXPERT_FF, SHARED_FF, N_EXPERTS
    nrm = lambda k, shape, s: jax.random.normal(k, shape, jnp.float32) * s
    gain = lambda k, shape: 1.0 + 0.02 * jax.random.normal(k, shape, jnp.float32)
    ret_in = 2 * RET_HEADS * RET_DK + 2 * RET_HEADS * RET_DV
    return {
        "x": nrm(ks[0], (BATCH, SEQ, D), 1.0),
        "c": nrm(ks[1], (BATCH, D), 1.0),
        "norm_mix": gain(ks[2], (DEPTH, D)),
        "norm_ffn": gain(ks[3], (DEPTH, D)),
        "ada_w": nrm(ks[4], (DEPTH, D, 6 * D), 0.5 * D ** -0.5),
        "ada_b": nrm(ks[5], (DEPTH, 6 * D), 0.02),
        "da_w_in": nrm(ks[6], (N_DIFF, D, 6 * DA_HEADS * DA_HEAD_DIM), D ** -0.5),
        "da_w_out": nrm(ks[7], (N_DIFF, DA_HEADS * DA_V_DIM, D), (DA_HEADS * DA_V_DIM) ** -0.5),
        "da_lq1": nrm(ks[8], (N_DIFF, DA_HEAD_DIM), 0.1),
        "da_lk1": nrm(ks[9], (N_DIFF, DA_HEAD_DIM), 0.1),
        "da_lq2": nrm(ks[10], (N_DIFF, DA_HEAD_DIM), 0.1),
        "da_lk2": nrm(ks[11], (N_DIFF, DA_HEAD_DIM), 0.1),
        "da_subln": gain(ks[12], (N_DIFF, DA_V_DIM)),
        "ret_w_in": nrm(ks[13], (N_RET, D, ret_in), D ** -0.5),
        "ret_w_out": nrm(ks[14], (N_RET, RET_HEADS * RET_DV, D), (RET_HEADS * RET_DV) ** -0.5),
        "ret_gn": gain(ks[15], (N_RET, RET_HEADS * RET_DV)),
        "router_w": nrm(ks[16], (DEPTH, D, E), D ** -0.5),
        "router_b": nrm(ks[17], (DEPTH, E), 0.01),
        "exp_w1": nrm(ks[18], (DEPTH, E, D, F), D ** -0.5),
        "exp_w3": nrm(ks[19], (DEPTH, E, D, F), D ** -0.5),
        "exp_w2": nrm(ks[20], (DEPTH, E, F, D), F ** -0.5),
        "sh_w1": nrm(ks[21], (DEPTH, D, SF), D ** -0.5),
        "sh_w3": nrm(ks[22], (DEPTH, D, SF), D ** -0.5),
        "sh_w2": nrm(ks[23], (DEPTH, SF, D), SF ** -0.5),
        "final_norm": gain(ks[24], (D,)),
    }


def reference(x, c, norm_mix, norm_ffn, ada_w, ada_b, da_w_in, da_w_out, da_lq1, da_lk1,
              da_lq2, da_lk2, da_subln, ret_w_in, ret_w_out, ret_gn, router_w, router_b,
              exp_w1, exp_w3, exp_w2, sh_w1, sh_w3, sh_w2, final_norm):
    for i in range(DEPTH):
        mod = jax.nn.silu(c) @ ada_w[i] + ada_b[i]
        sh_a, sc_a, g_a, sh_f, sc_f, g_f = jnp.split(mod[:, None, :], 6, axis=-1)
        h = rmsnorm(x, norm_mix[i]) * (1.0 + sc_a) + sh_a
        j = i // N_MIXERS
        if i % N_MIXERS == 0:
            y = diff_attention(h, da_w_in[j], da_w_out[j], da_lq1[j], da_lk1[j],
                               da_lq2[j], da_lk2[j], da_subln[j], i)
        else:
            y = retention(h, ret_w_in[j], ret_w_out[j], ret_gn[j])
        x = x + g_a * y
        h = rmsnorm(x, norm_ffn[i]) * (1.0 + sc_f) + sh_f
        x = x + g_f * moe(h, router_w[i], router_b[i], exp_w1[i], exp_w3[i], exp_w2[i],
                          sh_w1[i], sh_w3[i], sh_w2[i])
    return rmsnorm(x, final_norm)
```

```python
import functools
import math

import jax
import jax.numpy as jnp
from jax import lax
from jax.experimental import pallas as pl
from jax.experimental.pallas import tpu as pltpu

EPS = 1e-6
DA_HEADS = 8
DA_HEAD_DIM = 64
DA_V_DIM = 2 * DA_HEAD_DIM
RET_HEADS = 4
RET_DK = 256
RET_DV = 512
RET_CHUNK = 128
N_EXPERTS = 64
TOP_K = 8
N_GROUPS = 8
GROUP_SIZE = N_EXPERTS // N_GROUPS
TOPK_GROUPS = 4
ROUTED_SCALE = 2.5
ROW_BLOCK = 256

VMEM_LIMIT = 56 * 1024 * 1024
NEG_BIG = -1e30

_f32 = jnp.float32
_bf16 = jnp.bfloat16


def _cparams(*sem):
    return pltpu.CompilerParams(dimension_semantics=sem, vmem_limit_bytes=VMEM_LIMIT)


def _silu(v):
    return v * (1.0 / (1.0 + jnp.exp(-v)))


def _dot(a, b):
    return jnp.dot(a, b, preferred_element_type=_f32)


def _dot_nt(a, b):
    return lax.dot_general(a, b, (((1,), (1,)), ((), ())), preferred_element_type=_f32)


def _dot_tn(a, b):
    return lax.dot_general(a, b, (((0,), (0,)), ((), ())), preferred_element_type=_f32)


def _ada_kernel(c_ref, w_ref, b_ref, o_ref):
    a = _silu(c_ref[...]).astype(_bf16)
    o_ref[0] = _dot(a, w_ref[0].astype(_bf16)) + b_ref[0]


def ada_modulation(c, ada_w, ada_b):
    depth, d, n = ada_w.shape
    b = c.shape[0]
    tn = 1024
    return pl.pallas_call(
        _ada_kernel,
        grid=(depth, n // tn),
        in_specs=[
            pl.BlockSpec((b, d), lambda l, j: (0, 0)),
            pl.BlockSpec((1, d, tn), lambda l, j: (l, 0, j)),
            pl.BlockSpec((1, 1, tn), lambda l, j: (l, 0, j)),
        ],
        out_specs=pl.BlockSpec((1, b, tn), lambda l, j: (l, 0, j)),
        out_shape=jax.ShapeDtypeStruct((depth, b, n), _f32),
        compiler_params=_cparams("parallel", "parallel"),
        name="ada_modulation",
    )(c, ada_w, ada_b.reshape(depth, 1, n))


def _norm_mod(x, g, sc, sh):
    y = x * lax.rsqrt(jnp.mean(x * x, axis=-1, keepdims=True) + EPS)
    return (y * g) * (1.0 + sc) + sh


def _norm_proj_kernel(x_ref, g_ref, sc_ref, sh_ref, w_ref, o_ref, h_ref):
    @pl.when(pl.program_id(2) == 0)
    def _():
        h_ref[...] = _norm_mod(x_ref[0], g_ref[...], sc_ref[0], sh_ref[0]).astype(_bf16)

    o_ref[0] = _dot(h_ref[...], w_ref[...]).astype(o_ref.dtype)


def norm_proj(x, g, sc, sh, w):
    b, s, d = x.shape
    n = w.shape[1]
    tm = min(1024, s)
    tn = 512
    return pl.pallas_call(
        _norm_proj_kernel,
        grid=(b, s // tm, n // tn),
        in_specs=[
            pl.BlockSpec((1, tm, d), lambda i, m, j: (i, m, 0)),
            pl.BlockSpec((1, d), lambda i, m, j: (0, 0)),
            pl.BlockSpec((1, 1, d), lambda i, m, j: (i, 0, 0)),
            pl.BlockSpec((1, 1, d), lambda i, m, j: (i, 0, 0)),
            pl.BlockSpec((d, tn), lambda i, m, j: (0, j)),
        ],
        out_specs=pl.BlockSpec((1, tm, tn), lambda i, m, j: (i, m, j)),
        out_shape=jax.ShapeDtypeStruct((b, s, n), _bf16),
        scratch_shapes=[pltpu.VMEM((tm, d), _bf16)],
        compiler_params=_cparams("parallel", "parallel", "arbitrary"),
        name="norm_proj",
    )(x, g.reshape(1, d), sc, sh, w)


def _out_proj_kernel(y_ref, w_ref, x_ref, g_ref, o_ref):
    o_ref[0] = x_ref[0] + g_ref[0] * _dot(y_ref[0], w_ref[...])


def out_proj_residual(y, w, x, gate):
    b, s, k = y.shape
    d = w.shape[1]
    tm = min(512, s)
    return pl.pallas_call(
        _out_proj_kernel,
        grid=(b, s // tm),
        in_specs=[
            pl.BlockSpec((1, tm, k), lambda i, m: (i, m, 0)),
            pl.BlockSpec((k, d), lambda i, m: (0, 0)),
            pl.BlockSpec((1, tm, d), lambda i, m: (i, m, 0)),
            pl.BlockSpec((1, 1, d), lambda i, m: (i, 0, 0)),
        ],
        out_specs=pl.BlockSpec((1, tm, d), lambda i, m: (i, m, 0)),
        out_shape=jax.ShapeDtypeStruct((b, s, d), _f32),
        compiler_params=_cparams("parallel", "parallel"),
        name="out_proj_residual",
    )(y, w, x, gate)


def _diff_attn_kernel(q_ref, k_ref, v_ref, lq1_ref, lk1_ref, lq2_ref, lk2_ref, sg_ref, o_ref,
                      *, tq, lam_init):
    h = pl.program_id(1)
    qi = pl.program_id(2)
    dh, dv = DA_HEAD_DIM, DA_V_DIM

    q = q_ref[0] * (dh ** -0.5)
    lane = lax.broadcasted_iota(jnp.int32, q.shape, 1)
    zero = jnp.zeros_like(q)
    q_maps = (jnp.where(lane < dh, q, zero), jnp.where(lane >= dh, q, zero))

    hp1 = jnp.full((1, 1), h + 1, jnp.int32).astype(_f32)
    slope = jnp.exp2(hp1 * (-8.0 / DA_HEADS))
    row = lax.broadcasted_iota(jnp.int32, (tq, tq), 0)
    col = lax.broadcasted_iota(jnp.int32, (tq, tq), 1)
    rel = row - col

    def body(j, carry):
        kj = k_ref[0, pl.ds(pl.multiple_of(j * tq, tq), tq), :]
        vj = v_ref[0, pl.ds(pl.multiple_of(j * tq, tq), tq), :]
        dist = (rel + (qi - j) * tq).astype(_f32)
        bias = jnp.where(dist >= 0, -slope * dist, NEG_BIG)
        out = []
        for mi in range(2):
            m, l, acc = carry[mi]
            s = _dot_nt(q_maps[mi], kj) + bias
            m_new = jnp.maximum(m, jnp.max(s, axis=-1, keepdims=True))
            alpha = jnp.exp(m - m_new)
            p = jnp.exp(s - m_new)
            l = alpha * l + jnp.sum(p, axis=-1, keepdims=True)
            acc = alpha * acc + _dot(p.astype(_bf16), vj)
            out.append((m_new, l, acc))
        return tuple(out)

    init = tuple((jnp.full((tq, 1), NEG_BIG, _f32), jnp.zeros((tq, 1), _f32),
                  jnp.zeros((tq, dv), _f32)) for _ in range(2))
    (_, l1, a1), (_, l2, a2) = lax.fori_loop(0, qi + 1, body, init)

    lam = (jnp.exp(jnp.sum(lq1_ref[...] * lk1_ref[...], axis=-1, keepdims=True))
           - jnp.exp(jnp.sum(lq2_ref[...] * lk2_ref[...], axis=-1, keepdims=True)) + lam_init)
    o = a1 / l1 - lam * (a2 / l2)
    o = o * lax.rsqrt(jnp.mean(o * o, axis=-1, keepdims=True) + EPS)
    o_ref[0] = ((o * sg_ref[...]) * (1.0 - lam_init)).astype(o_ref.dtype)


def diff_attention_core(proj, lq1, lk1, lq2, lk2, subln_g, layer_idx):
    b, s, _ = proj.shape
    nh, dh, dv = DA_HEADS, DA_HEAD_DIM, DA_V_DIM
    tq = min(256, s)
    lam_init = 0.8 - 0.6 * math.exp(-0.3 * layer_idx)
    vec = lambda n: pl.BlockSpec((1, n), lambda i, h, m: (0, 0))
    return pl.pallas_call(
        functools.partial(_diff_attn_kernel, tq=tq, lam_init=lam_init),
        grid=(b, nh, s // tq),
        in_specs=[
            pl.BlockSpec((1, tq, 2 * dh), lambda i, h, m: (i, m, h)),
            pl.BlockSpec((1, s, 2 * dh), lambda i, h, m: (i, 0, nh + h)),
            pl.BlockSpec((1, s, dv), lambda i, h, m: (i, 0, 2 * nh + h)),
            vec(dh), vec(dh), vec(dh), vec(dh), vec(dv),
        ],
        out_specs=pl.BlockSpec((1, tq, dv), lambda i, h, m: (i, m, h)),
        out_shape=jax.ShapeDtypeStruct((b, s, nh * dv), _bf16),
        compiler_params=_cparams("parallel", "parallel", "parallel"),
        name="diff_attention",
    )(proj, proj, proj, lq1.reshape(1, dh), lk1.reshape(1, dh), lq2.reshape(1, dh),
      lk2.reshape(1, dh), subln_g.reshape(1, dv))


def _retention_kernel(q_ref, k_ref, v_ref, g_ref, gn_ref, o_ref, r_ref, *, chunk):
    h = pl.program_id(1)
    c = pl.program_id(2)
    dk = RET_DK

    @pl.when(c == 0)
    def _():
        r_ref[...] = jnp.zeros_like(r_ref)

    hf = jnp.full((1, 1), h, jnp.int32).astype(_f32)
    log_g = jnp.log(1.0 - jnp.exp2(-5.0 - hf))
    ri = lax.broadcasted_iota(jnp.int32, (chunk, chunk), 0)
    ci = lax.broadcasted_iota(jnp.int32, (chunk, chunk), 1)
    diff = (ri - ci).astype(_f32)
    decay = jnp.where(diff >= 0, jnp.exp(log_g * jnp.maximum(diff, 0.0)), 0.0)
    idx = lax.broadcasted_iota(jnp.int32, (chunk, 1), 0).astype(_f32)
    zeta = jnp.exp(log_g * (chunk - 1 - idx))
    xi = jnp.exp(log_g * (idx + 1.0))
    g_c = jnp.exp(log_g * chunk)

    q = q_ref[0]
    k = k_ref[0]
    v = v_ref[0]
    r = r_ref[...]
    inner = _dot_nt(q, k) * (decay * (dk ** -0.5))
    o = _dot(inner.astype(_bf16), v) + _dot(q, r.astype(_bf16)) * xi
    kz = (k.astype(_f32) * (zeta * (dk ** -0.5))).astype(_bf16)
    r_ref[...] = g_c * r + _dot_tn(kz, v)

    o = o * lax.rsqrt(jnp.mean(o * o, axis=-1, keepdims=True) + EPS) * gn_ref[...]
    o_ref[0] = (_silu(g_ref[0].astype(_f32)) * o).astype(o_ref.dtype)


def retention_core(proj, gn_g):
    b, s, _ = proj.shape
    nh, dk, dv = RET_HEADS, RET_DK, RET_DV
    chunk = RET_CHUNK
    return pl.pallas_call(
        functools.partial(_retention_kernel, chunk=chunk),
        grid=(b, nh, s // chunk),
        in_specs=[
            pl.BlockSpec((1, chunk, dk), lambda i, h, c: (i, c, h)),
            pl.BlockSpec((1, chunk, dk), lambda i, h, c: (i, c, nh + h)),
            pl.BlockSpec((1, chunk, dv), lambda i, h, c: (i, c, (2 * nh * dk) // dv + h)),
            pl.BlockSpec((1, chunk, dv), lambda i, h, c: (i, c, (2 * nh * dk) // dv + nh + h)),
            pl.BlockSpec((1, dv), lambda i, h, c: (0, h)),
        ],
        out_specs=pl.BlockSpec((1, chunk, dv), lambda i, h, c: (i, c, h)),
        out_shape=jax.ShapeDtypeStruct((b, s, nh * dv), _bf16),
        scratch_shapes=[pltpu.VMEM((dk, dv), _f32)],
        compiler_params=_cparams("parallel", "parallel", "arbitrary"),
        name="retention",
    )(proj, proj, proj, proj, gn_g.reshape(1, nh * dv))


def _pack_rows(h):
    n = h.shape[1] // 2
    lo = pltpu.bitcast(h[:, :n].astype(_bf16).astype(_f32), jnp.uint32)
    hi = pltpu.bitcast(h[:, n:].astype(_bf16).astype(_f32), jnp.uint32)
    return (lo >> 16) | (hi & jnp.uint32(0xFFFF0000))


def _unpack_rows(p):
    lo = pltpu.bitcast(p << 16, _f32).astype(_bf16)
    hi = pltpu.bitcast(p & jnp.uint32(0xFFFF0000), _f32).astype(_bf16)
    return lo, hi


def _first_argmax(v, iota, axis, big):
    m = jnp.max(v, axis=axis, keepdims=True)
    i = jnp.min(jnp.where(v == m, iota, big), axis=axis, keepdims=True)
    return m, i


def _router_kernel(x_ref, g_ref, sc_ref, sh_ref, rwt_ref, rb_ref, s1_ref, s3_ref, s2_ref,
                   hp_ref, idx_ref, wgt_ref, shared_ref):
    tm = x_ref.shape[1]
    h = _norm_mod(x_ref[0], g_ref[...], sc_ref[0], sh_ref[0])
    hp_ref[...] = _pack_rows(h)
    hb = h.astype(_bf16)
    act = (_silu(_dot(hb, s1_ref[...])) * _dot(hb, s3_ref[...])).astype(_bf16)
    shared_ref[...] = _dot(act, s2_ref[...])

    scores = 1.0 / (1.0 + jnp.exp(-_dot_nt(rwt_ref[...], h)))
    choice = scores + rb_ref[...]
    c3 = choice.reshape(N_GROUPS, GROUP_SIZE, tm)
    mem = lax.broadcasted_iota(jnp.int32, c3.shape, 1)
    m1, i1 = _first_argmax(c3, mem, 1, GROUP_SIZE)
    m2 = jnp.max(jnp.where(mem == i1, -jnp.inf, c3), axis=1, keepdims=True)
    gs = m1 + m2
    gi = lax.broadcasted_iota(jnp.int32, gs.shape, 0)
    sel = jnp.zeros(gs.shape, _f32)
    for _ in range(TOPK_GROUPS):
        _, ig = _first_argmax(gs, gi, 0, N_GROUPS)
        hit = gi == ig
        sel = jnp.where(hit, 1.0, sel)
        gs = jnp.where(hit, -jnp.inf, gs)
    cm = jnp.where(sel > 0.0, c3, -jnp.inf).reshape(N_EXPERTS, tm)
    ei = lax.broadcasted_iota(jnp.int32, cm.shape, 0)
    idxs, wgts = [], []
    for _ in range(TOP_K):
        _, ie = _first_argmax(cm, ei, 0, N_EXPERTS)
        hit = ei == ie
        idxs.append(ie)
        wgts.append(jnp.sum(jnp.where(hit, scores, 0.0), axis=0, keepdims=True))
        cm = jnp.where(hit, -jnp.inf, cm)
    w = jnp.concatenate(wgts, axis=0)
    idx_ref[...] = jnp.concatenate(idxs, axis=0)
    wgt_ref[...] = w / jnp.sum(w, axis=0, keepdims=True) * ROUTED_SCALE


def moe_router(x, g, sc, sh, router_w, router_b, s_w1, s_w3, s_w2):
    b, s, d = x.shape
    t = b * s
    tm = min(256, s)
    n_s = s // tm
    sf = s_w1.shape[1]
    const = lambda shape: pl.BlockSpec(shape, lambda i, m: (0,) * len(shape))
    return pl.pallas_call(
        _router_kernel,
        grid=(b, n_s),
        in_specs=[
            pl.BlockSpec((1, tm, d), lambda i, m: (i, m, 0)),
            const((1, d)),
            pl.BlockSpec((1, 1, d), lambda i, m: (i, 0, 0)),
            pl.BlockSpec((1, 1, d), lambda i, m: (i, 0, 0)),
            const((N_EXPERTS, d)), const((N_EXPERTS, 1)),
            const((d, sf)), const((d, sf)), const((sf, d)),
        ],
        out_specs=[
            pl.BlockSpec((tm, d // 2), lambda i, m: (i * n_s + m, 0)),
            pl.BlockSpec((TOP_K, tm), lambda i, m: (0, i * n_s + m)),
            pl.BlockSpec((TOP_K, tm), lambda i, m: (0, i * n_s + m)),
            pl.BlockSpec((tm, d), lambda i, m: (i * n_s + m, 0)),
        ],
        out_shape=[
            jax.ShapeDtypeStruct((t, d // 2), jnp.uint32),
            jax.ShapeDtypeStruct((TOP_K, t), jnp.int32),
            jax.ShapeDtypeStruct((TOP_K, t), _f32),
            jax.ShapeDtypeStruct((t, d), _f32),
        ],
        compiler_params=_cparams("parallel", "parallel"),
        name="moe_router",
    )(x, g.reshape(1, d), sc, sh, router_w.T, router_b.reshape(N_EXPERTS, 1),
      s_w1.astype(_bf16), s_w3.astype(_bf16), s_w2.astype(_bf16))


def _rank_kernel(idx_ref, rank_ref, cnt_ref, base_ref):
    tm = idx_ref.shape[1]

    @pl.when(pl.program_id(0) == 0)
    def _():
        base_ref[...] = jnp.zeros_like(base_ref)

    idx = idx_ref[...]
    ei = lax.broadcasted_iota(jnp.int32, (N_EXPERTS, tm), 0)
    hits = [ei == idx[k:k + 1, :] for k in range(TOP_K)]
    member = functools.reduce(jnp.logical_or, hits)
    memf = jnp.where(member, 1.0, 0.0)
    tri = jnp.where(lax.broadcasted_iota(jnp.int32, (tm, tm), 0)
                    < lax.broadcasted_iota(jnp.int32, (tm, tm), 1), 1.0, 0.0).astype(_bf16)
    before = _dot(memf.astype(_bf16), tri) + base_ref[...]
    rank_ref[...] = jnp.concatenate(
        [jnp.sum(jnp.where(hits[k], before, 0.0), axis=0, keepdims=True) for k in range(TOP_K)],
        axis=0).astype(jnp.int32)
    base_ref[...] += jnp.sum(memf, axis=1, keepdims=True)
    cnt_ref[...] = base_ref[...].astype(jnp.int32)


def moe_rank(top_idx):
    k, t = top_idx.shape
    tm = min(512, t)
    return pl.pallas_call(
        _rank_kernel,
        grid=(t // tm,),
        in_specs=[pl.BlockSpec((k, tm), lambda i: (0, i))],
        out_specs=[pl.BlockSpec((k, tm), lambda i: (0, i)),
                   pl.BlockSpec((N_EXPERTS, 1), lambda i: (0, 0))],
        out_shape=[jax.ShapeDtypeStruct((k, t), jnp.int32),
                   jax.ShapeDtypeStruct((N_EXPERTS, 1), jnp.int32)],
        scratch_shapes=[pltpu.VMEM((N_EXPERTS, 1), _f32)],
        compiler_params=_cparams("arbitrary"),
        name="moe_rank",
    )(top_idx)


def _dispatch_kernel(dest_ref, hp_ref, zero_ref, xs_ref, sem, *, tm):
    del zero_ref
    base = pl.program_id(0) * tm

    def issue(i, carry):
        src = hp_ref.at[pl.ds(base + i, 1)]
        for k in range(TOP_K):
            pltpu.make_async_copy(src, xs_ref.at[pl.ds(dest_ref[0, 0, i * TOP_K + k], 1)],
                                  sem).start()
        return carry

    lax.fori_loop(0, tm, issue, 0)

    def drain(i, carry):
        pltpu.make_async_copy(hp_ref.at[pl.ds(0, 1)], xs_ref.at[pl.ds(0, 1)], sem).wait()
        return carry

    lax.fori_loop(0, tm * TOP_K, drain, 0)


def moe_dispatch(dest, hp, n_rows):
    t, w = hp.shape
    tm = min(512, t)
    zeros = jnp.zeros((n_rows, w), hp.dtype)
    return pl.pallas_call(
        functools.partial(_dispatch_kernel, tm=tm),
        grid=(t // tm,),
        in_specs=[
            pl.BlockSpec((1, 1, tm * TOP_K), lambda i: (i, 0, 0), memory_space=pltpu.SMEM),
            pl.BlockSpec(memory_space=pl.ANY),
            pl.BlockSpec(memory_space=pl.ANY),
        ],
        out_specs=pl.BlockSpec(memory_space=pl.ANY),
        out_shape=jax.ShapeDtypeStruct((n_rows, w), hp.dtype),
        scratch_shapes=[pltpu.SemaphoreType.DMA(())],
        input_output_aliases={2: 0},
        compiler_params=_cparams("arbitrary"),
        name="moe_dispatch",
    )(dest.reshape(t // tm, 1, tm * TOP_K), hp, zeros)


def _expert_kernel(be_ref, nb_ref, xs_ref, w1_ref, w3_ref, w2_ref, y_ref):
    i = pl.program_id(0)

    @pl.when(i < nb_ref[0])
    def _():
        lo, hi = _unpack_rows(xs_ref[...])
        n = lo.shape[1]
        h1 = _dot(lo, w1_ref[0, :n, :]) + _dot(hi, w1_ref[0, n:, :])
        h3 = _dot(lo, w3_ref[0, :n, :]) + _dot(hi, w3_ref[0, n:, :])
        y_ref[...] = _dot((_silu(h1) * h3).astype(_bf16), w2_ref[0])

    @pl.when(i >= nb_ref[0])
    def _():
        y_ref[...] = jnp.zeros_like(y_ref)


def moe_experts(xs, blk_e, n_used, w1, w3, w2):
    p, half = xs.shape
    d = 2 * half
    ff = w1.shape[2]
    nb = p // ROW_BLOCK
    row = lambda i, be, nu: (jnp.minimum(i, nu[0] - 1), 0)
    grid_spec = pltpu.PrefetchScalarGridSpec(
        num_scalar_prefetch=2,
        grid=(nb,),
        in_specs=[
            pl.BlockSpec((ROW_BLOCK, half), row),
            pl.BlockSpec((1, d, ff), lambda i, be, nu: (be[i], 0, 0)),
            pl.BlockSpec((1, d, ff), lambda i, be, nu: (be[i], 0, 0)),
            pl.BlockSpec((1, ff, d), lambda i, be, nu: (be[i], 0, 0)),
        ],
        out_specs=pl.BlockSpec((ROW_BLOCK, d), lambda i, be, nu: (i, 0)),
    )
    return pl.pallas_call(
        _expert_kernel,
        grid_spec=grid_spec,
        out_shape=jax.ShapeDtypeStruct((p, d), _f32),
        compiler_params=_cparams("arbitrary"),
        name="moe_experts",
    )(blk_e, n_used, xs, w1, w3, w2)


def _combine_kernel(dest_ref, ys_ref, w_ref, shared_ref, x_ref, gate_ref, fn_ref, o_ref,
                    buf_ref, sem, *, tm, final_norm):
    def issue(i, carry):
        for k in range(TOP_K):
            pltpu.make_async_copy(ys_ref.at[pl.ds(dest_ref[0, 0, i * TOP_K + k], 1)],
                                  buf_ref.at[k, pl.ds(i, 1)], sem).start()
        return carry

    lax.fori_loop(0, tm, issue, 0)

    def drain(i, carry):
        pltpu.make_async_copy(ys_ref.at[pl.ds(0, 1)], buf_ref.at[0, pl.ds(0, 1)], sem).wait()
        return carry

    lax.fori_loop(0, tm * TOP_K, drain, 0)

    w = w_ref[...]
    acc = shared_ref[...]
    for k in range(TOP_K):
        acc = acc + w[:, k:k + 1] * buf_ref[k]
    xn = x_ref[0] + gate_ref[0] * acc
    if final_norm:
        xn = xn * lax.rsqrt(jnp.mean(xn * xn, axis=-1, keepdims=True) + EPS) * fn_ref[...]
    o_ref[0] = xn


def moe_combine(dest, ys, top_w, shared, x, gate, fn_g, final_norm):
    b, s, d = x.shape
    t = b * s
    tm = min(256, s)
    n_s = s // tm
    return pl.pallas_call(
        functools.partial(_combine_kernel, tm=tm, final_norm=final_norm),
        grid=(b, n_s),
        in_specs=[
            pl.BlockSpec((1, 1, tm * TOP_K), lambda i, m: (i * n_s + m, 0, 0),
                         memory_space=pltpu.SMEM),
            pl.BlockSpec(memory_space=pl.ANY),
            pl.BlockSpec((tm, TOP_K), lambda i, m: (i * n_s + m, 0)),
            pl.BlockSpec((tm, d), lambda i, m: (i * n_s + m, 0)),
            pl.BlockSpec((1, tm, d), lambda i, m: (i, m, 0)),
            pl.BlockSpec((1, 1, d), lambda i, m: (i, 0, 0)),
            pl.BlockSpec((1, d), lambda i, m: (0, 0)),
        ],
        out_specs=pl.BlockSpec((1, tm, d), lambda i, m: (i, m, 0)),
        out_shape=jax.ShapeDtypeStruct((b, s, d), _f32),
        scratch_shapes=[pltpu.VMEM((TOP_K, tm, d), _f32), pltpu.SemaphoreType.DMA(())],
        compiler_params=_cparams("arbitrary", "arbitrary"),
        name="moe_combine",
    )(dest.reshape(t // tm, 1, tm * TOP_K), ys, top_w, shared, x, gate, fn_g.reshape(1, d))


def moe_layer(x, g, sc, sh, gate, router_w, router_b, w1, w3, w2, s_w1, s_w3, s_w2,
              fn_g, final_norm):
    b, s, d = x.shape
    t = b * s
    hp, top_idx, top_w, shared = moe_router(x, g, sc, sh, router_w, router_b, s_w1, s_w3, s_w2)
    rank, counts = moe_rank(top_idx)

    counts = counts.reshape(N_EXPERTS)
    pad_counts = (counts + ROW_BLOCK - 1) // ROW_BLOCK * ROW_BLOCK
    pad_ends = jnp.cumsum(pad_counts)
    pad_starts = pad_ends - pad_counts
    n_blocks = -(-(t * TOP_K) // ROW_BLOCK) + N_EXPERTS
    blk_e = jnp.minimum(
        jnp.searchsorted(pad_ends, jnp.arange(n_blocks, dtype=jnp.int32) * ROW_BLOCK, side="right"),
        N_EXPERTS - 1).astype(jnp.int32)
    n_used = (pad_ends[-1:] // ROW_BLOCK).astype(jnp.int32)
    dest = (pad_starts[top_idx] + rank).T.reshape(t * TOP_K)

    xs = moe_dispatch(dest, hp, n_blocks * ROW_BLOCK)
    ys = moe_experts(xs, blk_e, n_used, w1, w3, w2)
    return moe_combine(dest, ys, top_w.T, shared, x, gate, fn_g, final_norm)


def kernel(x, c, norm_mix, norm_ffn, ada_w, ada_b, da_w_in, da_w_out, da_lq1, da_lk1, da_lq2,
           da_lk2, da_subln, ret_w_in, ret_w_out, ret_gn, router_w, router_b, exp_w1, exp_w3,
           exp_w2, sh_w1, sh_w3, sh_w2, final_norm):
    depth, d = norm_mix.shape
    b = x.shape[0]
    mod = ada_modulation(c, ada_w, ada_b).reshape(depth, b, 6, 1, d)
    for i in range(depth):
        sh_a, sc_a, g_a, sh_f, sc_f, g_f = (mod[i, :, j] for j in range(6))
        j = i // 2
        if i % 2 == 0:
            proj = norm_proj(x, norm_mix[i], sc_a, sh_a, da_w_in[j].astype(_bf16))
            y = diff_attention_core(proj, da_lq1[j], da_lk1[j], da_lq2[j], da_lk2[j],
                                    da_subln[j], i)
            x = out_proj_residual(y, da_w_out[j].astype(_bf16), x, g_a)
        else:
            proj = norm_proj(x, norm_mix[i], sc_a, sh_a, ret_w_in[j].astype(_bf16))
            y = retention_core(proj, ret_gn[j])
            x = out_proj_residual(y, ret_w_out[j].astype(_bf16), x, g_a)
        x = moe_layer(x, norm_ffn[i], sc_f, sh_f, g_f, router_w[i], router_b[i],
                      exp_w1[i].astype(_bf16), exp_w3[i].astype(_bf16), exp_w2[i].astype(_bf16),
                      sh_w1[i], sh_w3[i], sh_w2[i], final_norm, i == depth - 1)
    return x
```

```python
import functools
import math

import jax
import jax.numpy as jnp
from jax import lax
from jax.experimental import pallas as pl
from jax.experimental.pallas import tpu as pltpu

EPS = 1e-6
DA_HEADS = 8
DA_HEAD_DIM = 64
DA_V_DIM = 2 * DA_HEAD_DIM
RET_HEADS = 4
RET_DK = 256
RET_DV = 512
RET_CHUNK = 128
N_EXPERTS = 64
TOP_K = 8
N_GROUPS = 8
GROUP_SIZE = N_EXPERTS // N_GROUPS
TOPK_GROUPS = 4
ROUTED_SCALE = 2.5
ROW_BLOCK_LOG2 = 8
ROW_BLOCK = 1 << ROW_BLOCK_LOG2

VMEM_LIMIT = 56 * 1024 * 1024
NEG_BIG = -1e30

_f32 = jnp.float32
_bf16 = jnp.bfloat16


def _cparams(*sem):
    return pltpu.CompilerParams(dimension_semantics=sem, vmem_limit_bytes=VMEM_LIMIT)


def _silu(v):
    return v * (1.0 / (1.0 + jnp.exp(-v)))


def _dot(a, b):
    return jnp.dot(a, b, preferred_element_type=_f32)


def _dot_nt(a, b):
    return lax.dot_general(a, b, (((1,), (1,)), ((), ())), preferred_element_type=_f32)


def _dot_tn(a, b):
    return lax.dot_general(a, b, (((0,), (0,)), ((), ())), preferred_element_type=_f32)


def _ada_kernel(c_ref, w_ref, b_ref, o_ref):
    a = _silu(c_ref[...]).astype(_bf16)
    o_ref[0] = _dot(a, w_ref[0].astype(_bf16)) + b_ref[0]


def ada_modulation(c, ada_w, ada_b):
    depth, d, n = ada_w.shape
    b = c.shape[0]
    tn = 1024
    return pl.pallas_call(
        _ada_kernel,
        grid=(depth, n // tn),
        in_specs=[
            pl.BlockSpec((b, d), lambda l, j: (0, 0)),
            pl.BlockSpec((1, d, tn), lambda l, j: (l, 0, j)),
            pl.BlockSpec((1, 1, tn), lambda l, j: (l, 0, j)),
        ],
        out_specs=pl.BlockSpec((1, b, tn), lambda l, j: (l, 0, j)),
        out_shape=jax.ShapeDtypeStruct((depth, b, n), _f32),
        compiler_params=_cparams("parallel", "parallel"),
        name="ada_modulation",
    )(c, ada_w, ada_b.reshape(depth, 1, n))


def _norm_mod(x, g, sc, sh):
    y = x * lax.rsqrt(jnp.mean(x * x, axis=-1, keepdims=True) + EPS)
    return (y * g) * (1.0 + sc) + sh


def _norm_proj_kernel(x_ref, g_ref, sc_ref, sh_ref, w_ref, o_ref, h_ref):
    @pl.when(pl.program_id(2) == 0)
    def _():
        h_ref[...] = _norm_mod(x_ref[0], g_ref[...], sc_ref[0], sh_ref[0]).astype(_bf16)

    o_ref[0] = _dot(h_ref[...], w_ref[...]).astype(o_ref.dtype)


def norm_proj(x, g, sc, sh, w):
    b, s, d = x.shape
    n = w.shape[1]
    tm = min(1024, s)
    tn = 512
    return pl.pallas_call(
        _norm_proj_kernel,
        grid=(b, s // tm, n // tn),
        in_specs=[
            pl.BlockSpec((1, tm, d), lambda i, m, j: (i, m, 0)),
            pl.BlockSpec((1, d), lambda i, m, j: (0, 0)),
            pl.BlockSpec((1, 1, d), lambda i, m, j: (i, 0, 0)),
            pl.BlockSpec((1, 1, d), lambda i, m, j: (i, 0, 0)),
            pl.BlockSpec((d, tn), lambda i, m, j: (0, j)),
        ],
        out_specs=pl.BlockSpec((1, tm, tn), lambda i, m, j: (i, m, j)),
        out_shape=jax.ShapeDtypeStruct((b, s, n), _bf16),
        scratch_shapes=[pltpu.VMEM((tm, d), _bf16)],
        compiler_params=_cparams("parallel", "parallel", "arbitrary"),
        name="norm_proj",
    )(x, g.reshape(1, d), sc, sh, w)


def _out_proj_kernel(y_ref, w_ref, x_ref, g_ref, o_ref):
    o_ref[0] = x_ref[0] + g_ref[0] * _dot(y_ref[0], w_ref[...])


def out_proj_residual(y, w, x, gate):
    b, s, k = y.shape
    d = w.shape[1]
    tm = min(512, s)
    return pl.pallas_call(
        _out_proj_kernel,
        grid=(b, s // tm),
        in_specs=[
            pl.BlockSpec((1, tm, k), lambda i, m: (i, m, 0)),
            pl.BlockSpec((k, d), lambda i, m: (0, 0)),
            pl.BlockSpec((1, tm, d), lambda i, m: (i, m, 0)),
            pl.BlockSpec((1, 1, d), lambda i, m: (i, 0, 0)),
        ],
        out_specs=pl.BlockSpec((1, tm, d), lambda i, m: (i, m, 0)),
        out_shape=jax.ShapeDtypeStruct((b, s, d), _f32),
        compiler_params=_cparams("parallel", "parallel"),
        name="out_proj_residual",
    )(y, w, x, gate)


def _diff_attn_kernel(q_ref, k_ref, v_ref, lq1_ref, lk1_ref, lq2_ref, lk2_ref, sg_ref, o_ref,
                      bias_ref, m_ref, l_ref, acc_ref, *, tq, lam_init):
    h = pl.program_id(1)
    qi = pl.program_id(2)
    dh, dv = DA_HEAD_DIM, DA_V_DIM

    q = q_ref[0] * (dh ** -0.5)
    lane = lax.broadcasted_iota(jnp.int32, q.shape, 1)
    zero = jnp.zeros_like(q)
    q_maps = (jnp.where(lane < dh, q, zero), jnp.where(lane >= dh, q, zero))

    hp1 = jnp.full((1, 1), h + 1, jnp.int32).astype(_f32)
    slope = jnp.exp2(hp1 * (-8.0 / DA_HEADS))
    rel = (lax.broadcasted_iota(jnp.int32, (tq, 2 * tq), 0)
           - lax.broadcasted_iota(jnp.int32, (tq, 2 * tq), 1)).astype(_f32)
    bias_ref[...] = -slope * rel

    m_ref[...] = jnp.full(m_ref.shape, NEG_BIG, _f32)
    l_ref[...] = jnp.zeros(l_ref.shape, _f32)
    acc_ref[...] = jnp.zeros(acc_ref.shape, _f32)

    def span(j, width, bias, shift):
        start = pl.multiple_of(j * tq, tq)
        kj = k_ref[0, pl.ds(start, width), :]
        vj = v_ref[0, pl.ds(start, width), :]
        for mi in range(2):
            s = _dot_nt(q_maps[mi], kj) + bias
            m_prev = m_ref[mi]
            m_new = jnp.maximum(m_prev, jnp.max(s, axis=-1, keepdims=True) + shift)
            alpha = jnp.exp(m_prev - m_new)
            m_loc = m_new - shift
            p = jnp.exp(s - jnp.concatenate([m_loc] * (width // 128), axis=1))
            l_ref[mi] = alpha * l_ref[mi] + jnp.sum(p, axis=-1, keepdims=True)
            acc_ref[mi] = alpha * acc_ref[mi] + _dot(p.astype(_bf16), vj)
            m_ref[mi] = m_new

    def body(jj, carry):
        shift = -slope * jnp.full((1, 1), (qi - 2 * jj) * tq, jnp.int32).astype(_f32)
        span(2 * jj, 2 * tq, bias_ref[...], shift)
        return carry

    lax.fori_loop(0, qi // 2, body, 0)

    @pl.when(qi % 2 == 1)
    def _():
        span(qi - 1, 2 * tq, jnp.where(rel + tq >= 0, bias_ref[...], NEG_BIG), -slope * tq)

    @pl.when(qi % 2 == 0)
    def _():
        span(qi, tq, jnp.where(rel[:, :tq] >= 0, bias_ref[:, :tq], NEG_BIG), 0.0)

    lam = (jnp.exp(jnp.sum(lq1_ref[...] * lk1_ref[...], axis=-1, keepdims=True))
           - jnp.exp(jnp.sum(lq2_ref[...] * lk2_ref[...], axis=-1, keepdims=True)) + lam_init)
    o = acc_ref[0] / l_ref[0] - lam * (acc_ref[1] / l_ref[1])
    o = o * lax.rsqrt(jnp.mean(o * o, axis=-1, keepdims=True) + EPS)
    o_ref[0] = ((o * sg_ref[...]) * (1.0 - lam_init)).astype(o_ref.dtype)


def diff_attention_core(proj, lq1, lk1, lq2, lk2, subln_g, layer_idx):
    b, s, _ = proj.shape
    nh, dh, dv = DA_HEADS, DA_HEAD_DIM, DA_V_DIM
    tq = min(512, s)
    lam_init = 0.8 - 0.6 * math.exp(-0.3 * layer_idx)
    vec = lambda n: pl.BlockSpec((1, n), lambda i, h, m: (0, 0))
    return pl.pallas_call(
        functools.partial(_diff_attn_kernel, tq=tq, lam_init=lam_init),
        grid=(b, nh, s // tq),
        in_specs=[
            pl.BlockSpec((1, tq, 2 * dh), lambda i, h, m: (i, m, h)),
            pl.BlockSpec((1, s, 2 * dh), lambda i, h, m: (i, 0, nh + h)),
            pl.BlockSpec((1, s, dv), lambda i, h, m: (i, 0, 2 * nh + h)),
            vec(dh), vec(dh), vec(dh), vec(dh), vec(dv),
        ],
        out_specs=pl.BlockSpec((1, tq, dv), lambda i, h, m: (i, m, h)),
        out_shape=jax.ShapeDtypeStruct((b, s, nh * dv), _bf16),
        scratch_shapes=[pltpu.VMEM((tq, 2 * tq), _f32), pltpu.VMEM((2, tq, dv), _f32),
                        pltpu.VMEM((2, tq, dv), _f32), pltpu.VMEM((2, tq, dv), _f32)],
        compiler_params=_cparams("parallel", "parallel", "parallel"),
        name="diff_attention",
    )(proj, proj, proj, lq1.reshape(1, dh), lk1.reshape(1, dh), lq2.reshape(1, dh),
      lk2.reshape(1, dh), subln_g.reshape(1, dv))


def _retention_kernel(q_ref, k_ref, v_ref, g_ref, gn_ref, o_ref, r_ref, *, chunk):
    h = pl.program_id(1)
    c = pl.program_id(2)
    dk = RET_DK

    @pl.when(c == 0)
    def _():
        r_ref[...] = jnp.zeros_like(r_ref)

    hf = jnp.full((1, 1), h, jnp.int32).astype(_f32)
    log_g = jnp.log(1.0 - jnp.exp2(-5.0 - hf))
    ri = lax.broadcasted_iota(jnp.int32, (chunk, chunk), 0)
    ci = lax.broadcasted_iota(jnp.int32, (chunk, chunk), 1)
    diff = (ri - ci).astype(_f32)
    decay = jnp.where(diff >= 0, jnp.exp(log_g * jnp.maximum(diff, 0.0)), 0.0)
    idx = lax.broadcasted_iota(jnp.int32, (chunk, 1), 0).astype(_f32)
    zeta = jnp.exp(log_g * (chunk - 1 - idx))
    xi = jnp.exp(log_g * (idx + 1.0))
    g_c = jnp.exp(log_g * chunk)

    q = q_ref[0]
    k = k_ref[0]
    v = v_ref[0]
    r = r_ref[...]
    inner = _dot_nt(q, k) * (decay * (dk ** -0.5))
    o = _dot(inner.astype(_bf16), v) + _dot(q, r.astype(_bf16)) * xi
    kz = (k.astype(_f32) * (zeta * (dk ** -0.5))).astype(_bf16)
    r_ref[...] = g_c * r + _dot_tn(kz, v)

    o = o * lax.rsqrt(jnp.mean(o * o, axis=-1, keepdims=True) + EPS) * gn_ref[...]
    o_ref[0] = (_silu(g_ref[0].astype(_f32)) * o).astype(o_ref.dtype)


def retention_core(proj, gn_g):
    b, s, _ = proj.shape
    nh, dk, dv = RET_HEADS, RET_DK, RET_DV
    chunk = RET_CHUNK
    return pl.pallas_call(
        functools.partial(_retention_kernel, chunk=chunk),
        grid=(b, nh, s // chunk),
        in_specs=[
            pl.BlockSpec((1, chunk, dk), lambda i, h, c: (i, c, h)),
            pl.BlockSpec((1, chunk, dk), lambda i, h, c: (i, c, nh + h)),
            pl.BlockSpec((1, chunk, dv), lambda i, h, c: (i, c, (2 * nh * dk) // dv + h)),
            pl.BlockSpec((1, chunk, dv), lambda i, h, c: (i, c, (2 * nh * dk) // dv + nh + h)),
            pl.BlockSpec((1, dv), lambda i, h, c: (0, h)),
        ],
        out_specs=pl.BlockSpec((1, chunk, dv), lambda i, h, c: (i, c, h)),
        out_shape=jax.ShapeDtypeStruct((b, s, nh * dv), _bf16),
        scratch_shapes=[pltpu.VMEM((dk, dv), _f32)],
        compiler_params=_cparams("parallel", "parallel", "arbitrary"),
        name="retention",
    )(proj, proj, proj, proj, gn_g.reshape(1, nh * dv))


def _pack_rows(h):
    n = h.shape[1] // 2
    lo = pltpu.bitcast(h[:, :n].astype(_bf16).astype(_f32), jnp.uint32)
    hi = pltpu.bitcast(h[:, n:].astype(_bf16).astype(_f32), jnp.uint32)
    return (lo >> 16) | (hi & jnp.uint32(0xFFFF0000))


def _unpack_rows(p):
    lo = pltpu.bitcast(p << 16, _f32).astype(_bf16)
    hi = pltpu.bitcast(p & jnp.uint32(0xFFFF0000), _f32).astype(_bf16)
    return lo, hi


def _first_argmax(v, iota, axis, big):
    m = jnp.max(v, axis=axis, keepdims=True)
    i = jnp.min(jnp.where(v == m, iota, big), axis=axis, keepdims=True)
    return m, i


def _router_kernel(x_ref, g_ref, sc_ref, sh_ref, rwt_ref, rb_ref, s1_ref, s3_ref, s2_ref,
                   hp_ref, idx_ref, wgt_ref, shared_ref):
    tm = x_ref.shape[1]
    h = _norm_mod(x_ref[0], g_ref[...], sc_ref[0], sh_ref[0])
    hp_ref[...] = _pack_rows(h)
    hb = h.astype(_bf16)
    act = (_silu(_dot(hb, s1_ref[...])) * _dot(hb, s3_ref[...])).astype(_bf16)
    shared_ref[...] = _dot(act, s2_ref[...])

    scores = 1.0 / (1.0 + jnp.exp(-_dot_nt(rwt_ref[...], h)))
    choice = scores + rb_ref[...]
    c3 = choice.reshape(N_GROUPS, GROUP_SIZE, tm)
    mem = lax.broadcasted_iota(jnp.int32, c3.shape, 1)
    m1, i1 = _first_argmax(c3, mem, 1, GROUP_SIZE)
    m2 = jnp.max(jnp.where(mem == i1, -jnp.inf, c3), axis=1, keepdims=True)
    gs = m1 + m2
    gi = lax.broadcasted_iota(jnp.int32, gs.shape, 0)
    sel = jnp.zeros(gs.shape, _f32)
    for _ in range(TOPK_GROUPS):
        _, ig = _first_argmax(gs, gi, 0, N_GROUPS)
        hit = gi == ig
        sel = jnp.where(hit, 1.0, sel)
        gs = jnp.where(hit, -jnp.inf, gs)
    cm = jnp.where(sel > 0.0, c3, -jnp.inf).reshape(N_EXPERTS, tm)
    ei = lax.broadcasted_iota(jnp.int32, cm.shape, 0)
    idxs, wgts = [], []
    for _ in range(TOP_K):
        _, ie = _first_argmax(cm, ei, 0, N_EXPERTS)
        hit = ei == ie
        idxs.append(ie)
        wgts.append(jnp.sum(jnp.where(hit, scores, 0.0), axis=0, keepdims=True))
        cm = jnp.where(hit, -jnp.inf, cm)
    w = jnp.concatenate(wgts, axis=0)
    idx_ref[...] = jnp.concatenate(idxs, axis=0)
    wgt_ref[...] = w / jnp.sum(w, axis=0, keepdims=True) * ROUTED_SCALE


def moe_router(x, g, sc, sh, router_w, router_b, s_w1, s_w3, s_w2):
    b, s, d = x.shape
    t = b * s
    tm = min(256, s)
    n_s = s // tm
    sf = s_w1.shape[1]
    const = lambda shape: pl.BlockSpec(shape, lambda i, m: (0,) * len(shape))
    return pl.pallas_call(
        _router_kernel,
        grid=(b, n_s),
        in_specs=[
            pl.BlockSpec((1, tm, d), lambda i, m: (i, m, 0)),
            const((1, d)),
            pl.BlockSpec((1, 1, d), lambda i, m: (i, 0, 0)),
            pl.BlockSpec((1, 1, d), lambda i, m: (i, 0, 0)),
            const((N_EXPERTS, d)), const((N_EXPERTS, 1)),
            const((d, sf)), const((d, sf)), const((sf, d)),
        ],
        out_specs=[
            pl.BlockSpec((tm, d // 2), lambda i, m: (i * n_s + m, 0)),
            pl.BlockSpec((TOP_K, tm), lambda i, m: (0, i * n_s + m)),
            pl.BlockSpec((TOP_K, tm), lambda i, m: (0, i * n_s + m)),
            pl.BlockSpec((tm, d), lambda i, m: (i * n_s + m, 0)),
        ],
        out_shape=[
            jax.ShapeDtypeStruct((t, d // 2), jnp.uint32),
            jax.ShapeDtypeStruct((TOP_K, t), jnp.int32),
            jax.ShapeDtypeStruct((TOP_K, t), _f32),
            jax.ShapeDtypeStruct((t, d), _f32),
        ],
        compiler_params=_cparams("parallel", "parallel"),
        name="moe_router",
    )(x, g.reshape(1, d), sc, sh, router_w.T, router_b.reshape(N_EXPERTS, 1),
      s_w1.astype(_bf16), s_w3.astype(_bf16), s_w2.astype(_bf16))


def _strict_lower(n):
    return jnp.where(lax.broadcasted_iota(jnp.int32, (n, n), 1)
                     < lax.broadcasted_iota(jnp.int32, (n, n), 0), 1.0, 0.0).astype(_bf16)


def _rank_kernel(idx_ref, dest_ref, cnt_ref, base_ref):
    phase = pl.program_id(0)
    i = pl.program_id(1)
    tm = idx_ref.shape[1]

    idx = idx_ref[...]
    ei = lax.broadcasted_iota(jnp.int32, (N_EXPERTS, tm), 0)
    hits = [ei == idx[k:k + 1, :] for k in range(TOP_K)]
    memf = jnp.where(functools.reduce(jnp.logical_or, hits), 1.0, 0.0)
    tile_cnt = jnp.sum(memf, axis=1, keepdims=True)

    @pl.when(jnp.logical_and(phase == 0, i == 0))
    def _():
        base_ref[...] = jnp.zeros_like(base_ref)

    @pl.when(phase == 0)
    def _():
        base_ref[...] += tile_cnt

    @pl.when(jnp.logical_and(phase == 1, i == 0))
    def _():
        cnt = base_ref[...].astype(jnp.int32)
        cnt_ref[...] = cnt
        nblk = (cnt + (ROW_BLOCK - 1)) >> ROW_BLOCK_LOG2
        hi = jnp.broadcast_to((nblk >> 5).astype(_f32), (N_EXPERTS, 128)).astype(_bf16)
        lo = jnp.broadcast_to((nblk & 31).astype(_f32), (N_EXPERTS, 128)).astype(_bf16)
        low_tri = _strict_lower(N_EXPERTS)
        start_blk = 32.0 * _dot(low_tri, hi) + _dot(low_tri, lo)
        base_ref[...] = start_blk[:, :1] * float(ROW_BLOCK)

    @pl.when(phase == 1)
    def _():
        before = _dot_nt(memf.astype(_bf16), _strict_lower(tm)) + base_ref[...]
        dest_ref[...] = jnp.concatenate(
            [jnp.sum(jnp.where(hits[k], before, 0.0), axis=0, keepdims=True)
             for k in range(TOP_K)], axis=0).astype(jnp.int32)
        base_ref[...] += tile_cnt


def moe_rank(top_idx):
    k, t = top_idx.shape
    tm = min(512, t)
    return pl.pallas_call(
        _rank_kernel,
        grid=(2, t // tm),
        in_specs=[pl.BlockSpec((k, tm), lambda p, i: (0, i))],
        out_specs=[pl.BlockSpec((k, tm), lambda p, i: (0, i * p)),
                   pl.BlockSpec((N_EXPERTS, 1), lambda p, i: (0, 0))],
        out_shape=[jax.ShapeDtypeStruct((k, t), jnp.int32),
                   jax.ShapeDtypeStruct((N_EXPERTS, 1), jnp.int32)],
        scratch_shapes=[pltpu.VMEM((N_EXPERTS, 1), _f32)],
        compiler_params=_cparams("arbitrary", "arbitrary"),
        name="moe_rank",
    )(top_idx)


def _row_chunks(n):
    return [ROW_BLOCK] * (n // ROW_BLOCK) + ([n % ROW_BLOCK] if n % ROW_BLOCK else [])


def _dispatch_kernel(pad_from_ref, pad_n_ref, nb_ref, dest_ref, hp_ref, xs_ref, zero_ref,
                     sem, zsem, *, n_pad_rows):
    tm = hp_ref.shape[0]
    n_blocks = xs_ref.shape[0] // ROW_BLOCK
    zero_rows = lambda n: zero_ref.at[pl.ds(0, n)]

    @pl.when(pl.program_id(0) == 0)
    def _():
        zero_ref[...] = jnp.zeros_like(zero_ref)

        def pad_expert(e, carry):
            off = pad_from_ref[e]

            def pad_row(r, c):
                pltpu.make_async_copy(zero_rows(1), xs_ref.at[pl.ds(off + r, 1)], zsem).start()
                return c

            return lax.fori_loop(0, pad_n_ref[e], pad_row, carry)

        lax.fori_loop(0, N_EXPERTS, pad_expert, 0)

        def pad_tail(b, carry):
            pltpu.make_async_copy(zero_ref, xs_ref.at[pl.ds(b * ROW_BLOCK, ROW_BLOCK)],
                                  zsem).start()
            return carry

        lax.fori_loop(nb_ref[0], n_blocks, pad_tail, 0)
        for n in _row_chunks(n_pad_rows):
            pltpu.make_async_copy(zero_rows(n), xs_ref.at[pl.ds(0, n)], zsem).wait()

    def issue(i, carry):
        src = hp_ref.at[pl.ds(i, 1)]
        for k in range(TOP_K):
            pltpu.make_async_copy(src, xs_ref.at[pl.ds(dest_ref[k, i], 1)], sem).start()
        return carry

    lax.fori_loop(0, tm, issue, 0)
    for _ in range(TOP_K):
        pltpu.make_async_copy(hp_ref, xs_ref.at[pl.ds(0, tm)], sem).wait()


def moe_dispatch(dest, hp, pad_from, pad_n, n_used, n_blocks):
    t, w = hp.shape
    tm = min(512, t)
    n_rows = n_blocks * ROW_BLOCK
    grid_spec = pltpu.PrefetchScalarGridSpec(
        num_scalar_prefetch=3,
        grid=(t // tm,),
        in_specs=[
            pl.BlockSpec((TOP_K, tm), lambda i, *_: (0, i), memory_space=pltpu.SMEM),
            pl.BlockSpec((tm, w), lambda i, *_: (i, 0)),
        ],
        out_specs=pl.BlockSpec(memory_space=pl.ANY),
        scratch_shapes=[pltpu.VMEM((ROW_BLOCK, w), hp.dtype), pltpu.SemaphoreType.DMA(()),
                        pltpu.SemaphoreType.DMA(())],
    )
    return pl.pallas_call(
        functools.partial(_dispatch_kernel, n_pad_rows=n_rows - t * TOP_K),
        grid_spec=grid_spec,
        out_shape=jax.ShapeDtypeStruct((n_rows, w), hp.dtype),
        compiler_params=_cparams("arbitrary"),
        name="moe_dispatch",
    )(pad_from, pad_n, n_used, dest, hp)


def _expert_kernel(be_ref, nb_ref, xs_ref, w1_ref, w3_ref, w2_ref, y_ref):
    i = pl.program_id(0)

    @pl.when(i < nb_ref[0])
    def _():
        lo, hi = _unpack_rows(xs_ref[...])
        n = lo.shape[1]
        h1 = _dot(lo, w1_ref[0, :n, :]) + _dot(hi, w1_ref[0, n:, :])
        h3 = _dot(lo, w3_ref[0, :n, :]) + _dot(hi, w3_ref[0, n:, :])
        y_ref[...] = _dot((_silu(h1) * h3).astype(_bf16), w2_ref[0])

    @pl.when(i >= nb_ref[0])
    def _():
        y_ref[...] = jnp.zeros_like(y_ref)


def moe_experts(xs, blk_e, n_used, w1, w3, w2):
    p, half = xs.shape
    d = 2 * half
    ff = w1.shape[2]
    nb = p // ROW_BLOCK
    row = lambda i, be, nu: (jnp.minimum(i, nu[0] - 1), 0)
    wsel = lambda i, be, nu: (be[i], 0, 0)
    grid_spec = pltpu.PrefetchScalarGridSpec(
        num_scalar_prefetch=2,
        grid=(nb,),
        in_specs=[
            pl.BlockSpec((ROW_BLOCK, half), row),
            pl.BlockSpec((1, d, ff), wsel),
            pl.BlockSpec((1, d, ff), wsel),
            pl.BlockSpec((1, ff, d), wsel),
        ],
        out_specs=pl.BlockSpec((ROW_BLOCK, d), lambda i, be, nu: (i, 0)),
    )
    return pl.pallas_call(
        _expert_kernel,
        grid_spec=grid_spec,
        out_shape=jax.ShapeDtypeStruct((p, d), _f32),
        compiler_params=_cparams("arbitrary"),
        name="moe_experts",
    )(blk_e, n_used, xs, w1, w3, w2)


def _combine_kernel(dest_ref, ys_ref, w_ref, shared_ref, x_ref, gate_ref, fn_ref, o_ref,
                    buf_ref, sem, *, tm, final_norm):
    def issue(i, carry):
        for k in range(TOP_K):
            pltpu.make_async_copy(ys_ref.at[pl.ds(dest_ref[k, i], 1)],
                                  buf_ref.at[k, pl.ds(i, 1)], sem).start()
        return carry

    lax.fori_loop(0, tm, issue, 0)
    for k in range(TOP_K):
        pltpu.make_async_copy(ys_ref.at[pl.ds(0, tm)], buf_ref.at[k], sem).wait()

    w = w_ref[...]
    acc = shared_ref[...]
    for k in range(TOP_K):
        acc = acc + w[:, k:k + 1] * buf_ref[k]
    xn = x_ref[0] + gate_ref[0] * acc
    if final_norm:
        xn = xn * lax.rsqrt(jnp.mean(xn * xn, axis=-1, keepdims=True) + EPS) * fn_ref[...]
    o_ref[0] = xn


def moe_combine(dest, ys, top_w, shared, x, gate, fn_g, final_norm):
    b, s, d = x.shape
    t = b * s
    tm = min(256, s)
    n_s = s // tm
    return pl.pallas_call(
        functools.partial(_combine_kernel, tm=tm, final_norm=final_norm),
        grid=(b, n_s),
        in_specs=[
            pl.BlockSpec((TOP_K, tm), lambda i, m: (0, i * n_s + m), memory_space=pltpu.SMEM),
            pl.BlockSpec(memory_space=pl.ANY),
            pl.BlockSpec((tm, TOP_K), lambda i, m: (i * n_s + m, 0)),
            pl.BlockSpec((tm, d), lambda i, m: (i * n_s + m, 0)),
            pl.BlockSpec((1, tm, d), lambda i, m: (i, m, 0)),
            pl.BlockSpec((1, 1, d), lambda i, m: (i, 0, 0)),
            pl.BlockSpec((1, d), lambda i, m: (0, 0)),
        ],
        out_specs=pl.BlockSpec((1, tm, d), lambda i, m: (i, m, 0)),
        out_shape=jax.ShapeDtypeStruct((b, s, d), _f32),
        scratch_shapes=[pltpu.VMEM((TOP_K, tm, d), _f32), pltpu.SemaphoreType.DMA(())],
        compiler_params=_cparams("arbitrary", "arbitrary"),
        name="moe_combine",
    )(dest, ys, top_w, shared, x, gate, fn_g.reshape(1, d))


def moe_layer(x, g, sc, sh, gate, router_w, router_b, w1, w3, w2, s_w1, s_w3, s_w2,
              fn_g, final_norm):
    b, s, d = x.shape
    t = b * s
    hp, top_idx, top_w, shared = moe_router(x, g, sc, sh, router_w, router_b, s_w1, s_w3, s_w2)
    dest, counts = moe_rank(top_idx)

    counts = counts.reshape(N_EXPERTS)
    nblk = (counts + ROW_BLOCK - 1) // ROW_BLOCK
    blk_ends = jnp.cumsum(nblk)
    n_blocks = -(-(t * TOP_K) // ROW_BLOCK) + N_EXPERTS
    blk = jnp.arange(n_blocks, dtype=jnp.int32)
    blk_e = jnp.minimum(jnp.sum(blk[:, None] >= blk_ends[None, :], axis=1),
                        N_EXPERTS - 1).astype(jnp.int32)
    n_used = blk_ends[-1:].astype(jnp.int32)
    pad_from = ((blk_ends - nblk) * ROW_BLOCK + counts).astype(jnp.int32)
    pad_n = (nblk * ROW_BLOCK - counts).astype(jnp.int32)

    xs = moe_dispatch(dest, hp, pad_from, pad_n, n_used, n_blocks)
    ys = moe_experts(xs, blk_e, n_used, w1, w3, w2)
    return moe_combine(dest, ys, top_w.T, shared, x, gate, fn_g, final_norm)


def kernel(x, c, norm_mix, norm_ffn, ada_w, ada_b, da_w_in, da_w_out, da_lq1, da_lk1, da_lq2,
           da_lk2, da_subln, ret_w_in, ret_w_out, ret_gn, router_w, router_b, exp_w1, exp_w3,
           exp_w2, sh_w1, sh_w3, sh_w2, final_norm):
    depth, d = norm_mix.shape
    b = x.shape[0]
    mod = ada_modulation(c, ada_w, ada_b).reshape(depth, b, 6, 1, d)
    for i in range(depth):
        sh_a, sc_a, g_a, sh_f, sc_f, g_f = (mod[i, :, j] for j in range(6))
        j = i // 2
        if i % 2 == 0:
            proj = norm_proj(x, norm_mix[i], sc_a, sh_a, da_w_in[j].astype(_bf16))
            y = diff_attention_core(proj, da_lq1[j], da_lk1[j], da_lq2[j], da_lk2[j],
                                    da_subln[j], i)
            x = out_proj_residual(y, da_w_out[j].astype(_bf16), x, g_a)
        else:
            proj = norm_proj(x, norm_mix[i], sc_a, sh_a, ret_w_in[j].astype(_bf16))
            y = retention_core(proj, ret_gn[j])
            x = out_proj_residual(y, ret_w_out[j].astype(_bf16), x, g_a)
        x = moe_layer(x, norm_ffn[i], sc_f, sh_f, g_f, router_w[i], router_b[i],
                      exp_w1[i].astype(_bf16), exp_w3[i].astype(_bf16), exp_w2[i].astype(_bf16),
                      sh_w1[i], sh_w3[i], sh_w2[i], final_norm, i == depth - 1)
    return x
```

```python
import functools
import math

import jax
import jax.numpy as jnp
from jax import lax
from jax.experimental import pallas as pl
from jax.experimental.pallas import tpu as pltpu

EPS = 1e-6
DA_HEADS = 8
DA_HEAD_DIM = 64
DA_V_DIM = 2 * DA_HEAD_DIM
RET_HEADS = 4
RET_DK = 256
RET_DV = 512
RET_CHUNK = 128
N_EXPERTS = 64
TOP_K = 8
N_GROUPS = 8
GROUP_SIZE = N_EXPERTS // N_GROUPS
TOPK_GROUPS = 4
ROUTED_SCALE = 2.5
ROW_BLOCK_LOG2 = 9
ROW_BLOCK = 1 << ROW_BLOCK_LOG2
EXPERT_ROWS = 256

VMEM_LIMIT = 56 * 1024 * 1024
NEG_BIG = -1e30

_f32 = jnp.float32
_bf16 = jnp.bfloat16


def _cparams(*sem):
    return pltpu.CompilerParams(dimension_semantics=sem, vmem_limit_bytes=VMEM_LIMIT)


def _silu(v):
    return v * (1.0 / (1.0 + jnp.exp(-v)))


def _dot(a, b):
    return jnp.dot(a, b, preferred_element_type=_f32)


def _dot_nt(a, b):
    return lax.dot_general(a, b, (((1,), (1,)), ((), ())), preferred_element_type=_f32)


def _dot_tn(a, b):
    return lax.dot_general(a, b, (((0,), (0,)), ((), ())), preferred_element_type=_f32)


def _ada_kernel(c_ref, w_ref, b_ref, o_ref):
    a = _silu(c_ref[...]).astype(_bf16)
    o_ref[0] = _dot(a, w_ref[0].astype(_bf16)) + b_ref[0]


def ada_modulation(c, ada_w, ada_b):
    depth, d, n = ada_w.shape
    b = c.shape[0]
    tn = 1024
    return pl.pallas_call(
        _ada_kernel,
        grid=(depth, n // tn),
        in_specs=[
            pl.BlockSpec((b, d), lambda l, j: (0, 0)),
            pl.BlockSpec((1, d, tn), lambda l, j: (l, 0, j)),
            pl.BlockSpec((1, 1, tn), lambda l, j: (l, 0, j)),
        ],
        out_specs=pl.BlockSpec((1, b, tn), lambda l, j: (l, 0, j)),
        out_shape=jax.ShapeDtypeStruct((depth, b, n), _f32),
        compiler_params=_cparams("parallel", "parallel"),
        name="ada_modulation",
    )(c, ada_w, ada_b.reshape(depth, 1, n))


def _norm_mod(x, g, sc, sh):
    y = x * lax.rsqrt(jnp.mean(x * x, axis=-1, keepdims=True) + EPS)
    return (y * g) * (1.0 + sc) + sh


def _norm_proj_kernel(x_ref, g_ref, sc_ref, sh_ref, w_ref, o_ref, h_ref):
    @pl.when(pl.program_id(2) == 0)
    def _():
        h_ref[...] = _norm_mod(x_ref[0], g_ref[...], sc_ref[0], sh_ref[0]).astype(_bf16)

    o_ref[0] = _dot(h_ref[...], w_ref[...]).astype(o_ref.dtype)


def norm_proj(x, g, sc, sh, w):
    b, s, d = x.shape
    n = w.shape[1]
    tm = min(1024, s)
    tn = 512
    return pl.pallas_call(
        _norm_proj_kernel,
        grid=(b, s // tm, n // tn),
        in_specs=[
            pl.BlockSpec((1, tm, d), lambda i, m, j: (i, m, 0)),
            pl.BlockSpec((1, d), lambda i, m, j: (0, 0)),
            pl.BlockSpec((1, 1, d), lambda i, m, j: (i, 0, 0)),
            pl.BlockSpec((1, 1, d), lambda i, m, j: (i, 0, 0)),
            pl.BlockSpec((d, tn), lambda i, m, j: (0, j)),
        ],
        out_specs=pl.BlockSpec((1, tm, tn), lambda i, m, j: (i, m, j)),
        out_shape=jax.ShapeDtypeStruct((b, s, n), _bf16),
        scratch_shapes=[pltpu.VMEM((tm, d), _bf16)],
        compiler_params=_cparams("parallel", "parallel", "arbitrary"),
        name="norm_proj",
    )(x, g.reshape(1, d), sc, sh, w)


def _out_proj_kernel(y_ref, w_ref, x_ref, g_ref, o_ref):
    o_ref[0] = x_ref[0] + g_ref[0] * _dot(y_ref[0], w_ref[...])


def out_proj_residual(y, w, x, gate):
    b, s, k = y.shape
    d = w.shape[1]
    tm = min(512, s)
    return pl.pallas_call(
        _out_proj_kernel,
        grid=(b, s // tm),
        in_specs=[
            pl.BlockSpec((1, tm, k), lambda i, m: (i, m, 0)),
            pl.BlockSpec((k, d), lambda i, m: (0, 0)),
            pl.BlockSpec((1, tm, d), lambda i, m: (i, m, 0)),
            pl.BlockSpec((1, 1, d), lambda i, m: (i, 0, 0)),
        ],
        out_specs=pl.BlockSpec((1, tm, d), lambda i, m: (i, m, 0)),
        out_shape=jax.ShapeDtypeStruct((b, s, d), _f32),
        compiler_params=_cparams("parallel", "parallel"),
        name="out_proj_residual",
    )(y, w, x, gate)


def _diff_attn_kernel(q_ref, k_ref, v_ref, lq1_ref, lk1_ref, lq2_ref, lk2_ref, sg_ref, o_ref,
                      bias_ref, m_ref, l_ref, acc_ref, *, tq, lam_init):
    h = pl.program_id(1)
    qi = pl.program_id(2)
    dh, dv = DA_HEAD_DIM, DA_V_DIM

    q = q_ref[0] * (dh ** -0.5)
    lane = lax.broadcasted_iota(jnp.int32, q.shape, 1)
    zero = jnp.zeros_like(q)
    q_maps = (jnp.where(lane < dh, q, zero), jnp.where(lane >= dh, q, zero))

    hp1 = jnp.full((1, 1), h + 1, jnp.int32).astype(_f32)
    slope = jnp.exp2(hp1 * (-8.0 / DA_HEADS))
    rel = (lax.broadcasted_iota(jnp.int32, (tq, 2 * tq), 0)
           - lax.broadcasted_iota(jnp.int32, (tq, 2 * tq), 1)).astype(_f32)
    bias_ref[...] = -slope * rel

    m_ref[...] = jnp.full(m_ref.shape, NEG_BIG, _f32)
    l_ref[...] = jnp.zeros(l_ref.shape, _f32)
    acc_ref[...] = jnp.zeros(acc_ref.shape, _f32)

    def span(j, width, bias, shift):
        start = pl.multiple_of(j * tq, tq)
        kj = k_ref[0, pl.ds(start, width), :]
        vj = v_ref[0, pl.ds(start, width), :]
        for mi in range(2):
            s = _dot_nt(q_maps[mi], kj) + bias
            m_prev = m_ref[mi]
            m_new = jnp.maximum(m_prev, jnp.max(s, axis=-1, keepdims=True) + shift)
            alpha = jnp.exp(m_prev - m_new)
            m_loc = m_new - shift
            p = jnp.exp(s - jnp.concatenate([m_loc] * (width // 128), axis=1))
            l_ref[mi] = alpha * l_ref[mi] + jnp.sum(p, axis=-1, keepdims=True)
            acc_ref[mi] = alpha * acc_ref[mi] + _dot(p.astype(_bf16), vj)
            m_ref[mi] = m_new

    def body(jj, carry):
        shift = -slope * jnp.full((1, 1), (qi - 2 * jj) * tq, jnp.int32).astype(_f32)
        span(2 * jj, 2 * tq, bias_ref[...], shift)
        return carry

    lax.fori_loop(0, qi // 2, body, 0)

    @pl.when(qi % 2 == 1)
    def _():
        span(qi - 1, 2 * tq, jnp.where(rel + tq >= 0, bias_ref[...], NEG_BIG), -slope * tq)

    @pl.when(qi % 2 == 0)
    def _():
        span(qi, tq, jnp.where(rel[:, :tq] >= 0, bias_ref[:, :tq], NEG_BIG), 0.0)

    lam = (jnp.exp(jnp.sum(lq1_ref[...] * lk1_ref[...], axis=-1, keepdims=True))
           - jnp.exp(jnp.sum(lq2_ref[...] * lk2_ref[...], axis=-1, keepdims=True)) + lam_init)
    o = acc_ref[0] / l_ref[0] - lam * (acc_ref[1] / l_ref[1])
    o = o * lax.rsqrt(jnp.mean(o * o, axis=-1, keepdims=True) + EPS)
    o_ref[0] = ((o * sg_ref[...]) * (1.0 - lam_init)).astype(o_ref.dtype)


def diff_attention_core(proj, lq1, lk1, lq2, lk2, subln_g, layer_idx):
    b, s, _ = proj.shape
    nh, dh, dv = DA_HEADS, DA_HEAD_DIM, DA_V_DIM
    tq = min(512, s)
    lam_init = 0.8 - 0.6 * math.exp(-0.3 * layer_idx)
    vec = lambda n: pl.BlockSpec((1, n), lambda i, h, m: (0, 0))
    return pl.pallas_call(
        functools.partial(_diff_attn_kernel, tq=tq, lam_init=lam_init),
        grid=(b, nh, s // tq),
        in_specs=[
            pl.BlockSpec((1, tq, 2 * dh), lambda i, h, m: (i, m, h)),
            pl.BlockSpec((1, s, 2 * dh), lambda i, h, m: (i, 0, nh + h)),
            pl.BlockSpec((1, s, dv), lambda i, h, m: (i, 0, 2 * nh + h)),
            vec(dh), vec(dh), vec(dh), vec(dh), vec(dv),
        ],
        out_specs=pl.BlockSpec((1, tq, dv), lambda i, h, m: (i, m, h)),
        out_shape=jax.ShapeDtypeStruct((b, s, nh * dv), _bf16),
        scratch_shapes=[pltpu.VMEM((tq, 2 * tq), _f32), pltpu.VMEM((2, tq, dv), _f32),
                        pltpu.VMEM((2, tq, dv), _f32), pltpu.VMEM((2, tq, dv), _f32)],
        compiler_params=_cparams("parallel", "parallel", "parallel"),
        name="diff_attention",
    )(proj, proj, proj, lq1.reshape(1, dh), lk1.reshape(1, dh), lq2.reshape(1, dh),
      lk2.reshape(1, dh), subln_g.reshape(1, dv))


def _retention_kernel(q_ref, k_ref, v_ref, g_ref, gn_ref, o_ref, r_ref, *, chunk):
    h = pl.program_id(1)
    c = pl.program_id(2)
    dk = RET_DK

    @pl.when(c == 0)
    def _():
        r_ref[...] = jnp.zeros_like(r_ref)

    hf = jnp.full((1, 1), h, jnp.int32).astype(_f32)
    log_g = jnp.log(1.0 - jnp.exp2(-5.0 - hf))
    ri = lax.broadcasted_iota(jnp.int32, (chunk, chunk), 0)
    ci = lax.broadcasted_iota(jnp.int32, (chunk, chunk), 1)
    diff = (ri - ci).astype(_f32)
    decay = jnp.where(diff >= 0, jnp.exp(log_g * jnp.maximum(diff, 0.0)), 0.0)
    idx = lax.broadcasted_iota(jnp.int32, (chunk, 1), 0).astype(_f32)
    zeta = jnp.exp(log_g * (chunk - 1 - idx))
    xi = jnp.exp(log_g * (idx + 1.0))
    g_c = jnp.exp(log_g * chunk)

    decay = decay * (dk ** -0.5)
    zeta = zeta * (dk ** -0.5)
    for sub in range(q_ref.shape[1] // chunk):
        rows = pl.ds(sub * chunk, chunk)
        q = q_ref[0, rows, :]
        k = k_ref[0, rows, :]
        v = v_ref[0, rows, :]
        r = r_ref[...]
        inner = _dot_nt(q, k) * decay
        o = _dot(inner.astype(_bf16), v) + _dot(q, r.astype(_bf16)) * xi
        kz = (k.astype(_f32) * zeta).astype(_bf16)
        r_ref[...] = g_c * r + _dot_tn(kz, v)

        o = o * lax.rsqrt(jnp.mean(o * o, axis=-1, keepdims=True) + EPS) * gn_ref[...]
        o_ref[0, rows, :] = (_silu(g_ref[0, rows, :].astype(_f32)) * o).astype(o_ref.dtype)


def retention_core(proj, gn_g):
    b, s, _ = proj.shape
    nh, dk, dv = RET_HEADS, RET_DK, RET_DV
    chunk = RET_CHUNK
    rows = min(4 * chunk, s)
    return pl.pallas_call(
        functools.partial(_retention_kernel, chunk=chunk),
        grid=(b, nh, s // rows),
        in_specs=[
            pl.BlockSpec((1, rows, dk), lambda i, h, c: (i, c, h)),
            pl.BlockSpec((1, rows, dk), lambda i, h, c: (i, c, nh + h)),
            pl.BlockSpec((1, rows, dv), lambda i, h, c: (i, c, (2 * nh * dk) // dv + h)),
            pl.BlockSpec((1, rows, dv), lambda i, h, c: (i, c, (2 * nh * dk) // dv + nh + h)),
            pl.BlockSpec((1, dv), lambda i, h, c: (0, h)),
        ],
        out_specs=pl.BlockSpec((1, rows, dv), lambda i, h, c: (i, c, h)),
        out_shape=jax.ShapeDtypeStruct((b, s, nh * dv), _bf16),
        scratch_shapes=[pltpu.VMEM((dk, dv), _f32)],
        compiler_params=_cparams("parallel", "parallel", "arbitrary"),
        name="retention",
    )(proj, proj, proj, proj, gn_g.reshape(1, nh * dv))


def _pack_rows(h):
    n = h.shape[1] // 2
    lo = pltpu.bitcast(h[:, :n].astype(_bf16).astype(_f32), jnp.uint32)
    hi = pltpu.bitcast(h[:, n:].astype(_bf16).astype(_f32), jnp.uint32)
    return (lo >> 16) | (hi & jnp.uint32(0xFFFF0000))


def _unpack_rows(p):
    lo = pltpu.bitcast(p << 16, _f32).astype(_bf16)
    hi = pltpu.bitcast(p & jnp.uint32(0xFFFF0000), _f32).astype(_bf16)
    return lo, hi


def _first_argmax(v, iota, axis, big):
    m = jnp.max(v, axis=axis, keepdims=True)
    i = jnp.min(jnp.where(v == m, iota, big), axis=axis, keepdims=True)
    return m, i


def _router_kernel(x_ref, g_ref, sc_ref, sh_ref, rwt_ref, rb_ref, s1_ref, s3_ref, s2_ref,
                   hp_ref, idx_ref, wgt_ref, shared_ref):
    tm = x_ref.shape[1]
    h = _norm_mod(x_ref[0], g_ref[...], sc_ref[0], sh_ref[0])
    hp_ref[...] = _pack_rows(h)
    hb = h.astype(_bf16)
    act = (_silu(_dot(hb, s1_ref[...])) * _dot(hb, s3_ref[...])).astype(_bf16)
    shared_ref[...] = _dot(act, s2_ref[...])

    scores = 1.0 / (1.0 + jnp.exp(-_dot_nt(rwt_ref[...], h)))
    choice = scores + rb_ref[...]
    c3 = choice.reshape(N_GROUPS, GROUP_SIZE, tm)
    mem = lax.broadcasted_iota(jnp.int32, c3.shape, 1)
    m1, i1 = _first_argmax(c3, mem, 1, GROUP_SIZE)
    m2 = jnp.max(jnp.where(mem == i1, -jnp.inf, c3), axis=1, keepdims=True)
    gs = m1 + m2
    gi = lax.broadcasted_iota(jnp.int32, gs.shape, 0)
    sel = jnp.zeros(gs.shape, _f32)
    for _ in range(TOPK_GROUPS):
        _, ig = _first_argmax(gs, gi, 0, N_GROUPS)
        hit = gi == ig
        sel = jnp.where(hit, 1.0, sel)
        gs = jnp.where(hit, -jnp.inf, gs)
    cm = jnp.where(sel > 0.0, c3, -jnp.inf).reshape(N_EXPERTS, tm)
    ei = lax.broadcasted_iota(jnp.int32, cm.shape, 0)
    idxs, wgts = [], []
    for _ in range(TOP_K):
        _, ie = _first_argmax(cm, ei, 0, N_EXPERTS)
        hit = ei == ie
        idxs.append(ie)
        wgts.append(jnp.sum(jnp.where(hit, scores, 0.0), axis=0, keepdims=True))
        cm = jnp.where(hit, -jnp.inf, cm)
    w = jnp.concatenate(wgts, axis=0)
    idx_ref[...] = jnp.concatenate(idxs, axis=0)
    wgt_ref[...] = w / jnp.sum(w, axis=0, keepdims=True) * ROUTED_SCALE


def moe_router(x, g, sc, sh, router_w, router_b, s_w1, s_w3, s_w2):
    b, s, d = x.shape
    t = b * s
    tm = min(256, s)
    n_s = s // tm
    sf = s_w1.shape[1]
    const = lambda shape: pl.BlockSpec(shape, lambda i, m: (0,) * len(shape))
    return pl.pallas_call(
        _router_kernel,
        grid=(b, n_s),
        in_specs=[
            pl.BlockSpec((1, tm, d), lambda i, m: (i, m, 0)),
            const((1, d)),
            pl.BlockSpec((1, 1, d), lambda i, m: (i, 0, 0)),
            pl.BlockSpec((1, 1, d), lambda i, m: (i, 0, 0)),
            const((N_EXPERTS, d)), const((N_EXPERTS, 1)),
            const((d, sf)), const((d, sf)), const((sf, d)),
        ],
        out_specs=[
            pl.BlockSpec((tm, d // 2), lambda i, m: (i * n_s + m, 0)),
            pl.BlockSpec((TOP_K, tm), lambda i, m: (0, i * n_s + m)),
            pl.BlockSpec((TOP_K, tm), lambda i, m: (0, i * n_s + m)),
            pl.BlockSpec((tm, d), lambda i, m: (i * n_s + m, 0)),
        ],
        out_shape=[
            jax.ShapeDtypeStruct((t, d // 2), jnp.uint32),
            jax.ShapeDtypeStruct((TOP_K, t), jnp.int32),
            jax.ShapeDtypeStruct((TOP_K, t), _f32),
            jax.ShapeDtypeStruct((t, d), _f32),
        ],
        compiler_params=_cparams("parallel", "parallel"),
        name="moe_router",
    )(x, g.reshape(1, d), sc, sh, router_w.T, router_b.reshape(N_EXPERTS, 1),
      s_w1.astype(_bf16), s_w3.astype(_bf16), s_w2.astype(_bf16))


def _strict_lower(n):
    return jnp.where(lax.broadcasted_iota(jnp.int32, (n, n), 1)
                     < lax.broadcasted_iota(jnp.int32, (n, n), 0), 1.0, 0.0).astype(_bf16)


def _rank_kernel(idx_ref, dest_ref, cnt_ref, base_ref):
    phase = pl.program_id(0)
    i = pl.program_id(1)
    tm = idx_ref.shape[1]

    idx = idx_ref[...]
    ei = lax.broadcasted_iota(jnp.int32, (N_EXPERTS, tm), 0)
    hits = [ei == idx[k:k + 1, :] for k in range(TOP_K)]
    memf = jnp.where(functools.reduce(jnp.logical_or, hits), 1.0, 0.0)
    tile_cnt = jnp.sum(memf, axis=1, keepdims=True)

    @pl.when(jnp.logical_and(phase == 0, i == 0))
    def _():
        base_ref[...] = jnp.zeros_like(base_ref)

    @pl.when(phase == 0)
    def _():
        base_ref[...] += tile_cnt

    @pl.when(jnp.logical_and(phase == 1, i == 0))
    def _():
        cnt = base_ref[...].astype(jnp.int32)
        cnt_ref[...] = cnt
        nblk = (cnt + (ROW_BLOCK - 1)) >> ROW_BLOCK_LOG2
        hi = jnp.broadcast_to((nblk >> 5).astype(_f32), (N_EXPERTS, 128)).astype(_bf16)
        lo = jnp.broadcast_to((nblk & 31).astype(_f32), (N_EXPERTS, 128)).astype(_bf16)
        low_tri = _strict_lower(N_EXPERTS)
        start_blk = 32.0 * _dot(low_tri, hi) + _dot(low_tri, lo)
        base_ref[...] = start_blk[:, :1] * float(ROW_BLOCK)

    @pl.when(phase == 1)
    def _():
        before = _dot_nt(memf.astype(_bf16), _strict_lower(tm)) + base_ref[...]
        dest_ref[...] = jnp.concatenate(
            [jnp.sum(jnp.where(hits[k], before, 0.0), axis=0, keepdims=True)
             for k in range(TOP_K)], axis=0).astype(jnp.int32)
        base_ref[...] += tile_cnt


def moe_rank(top_idx):
    k, t = top_idx.shape
    tm = min(512, t)
    return pl.pallas_call(
        _rank_kernel,
        grid=(2, t // tm),
        in_specs=[pl.BlockSpec((k, tm), lambda p, i: (0, i))],
        out_specs=[pl.BlockSpec((k, tm), lambda p, i: (0, i * p)),
                   pl.BlockSpec((N_EXPERTS, 1), lambda p, i: (0, 0))],
        out_shape=[jax.ShapeDtypeStruct((k, t), jnp.int32),
                   jax.ShapeDtypeStruct((N_EXPERTS, 1), jnp.int32)],
        scratch_shapes=[pltpu.VMEM((N_EXPERTS, 1), _f32)],
        compiler_params=_cparams("arbitrary", "arbitrary"),
        name="moe_rank",
    )(top_idx)


def _row_chunks(n):
    return [ROW_BLOCK] * (n // ROW_BLOCK) + ([n % ROW_BLOCK] if n % ROW_BLOCK else [])


def _dispatch_kernel(pad_from_ref, pad_n_ref, nb_ref, dest_ref, hp_ref, xs_ref, zero_ref,
                     sem, zsem, *, n_pad_rows):
    tm = hp_ref.shape[0]
    n_blocks = xs_ref.shape[0] // ROW_BLOCK
    zero_rows = lambda n: zero_ref.at[pl.ds(0, n)]

    @pl.when(pl.program_id(0) == 0)
    def _():
        zero_ref[...] = jnp.zeros_like(zero_ref)

        def pad_expert(e, carry):
            off, n = pad_from_ref[e], pad_n_ref[e]
            head = jnp.minimum((-off) & 7, n)

            def pad_row(r, c):
                pltpu.make_async_copy(zero_rows(1), xs_ref.at[pl.ds(off + r, 1)], zsem).start()
                return c

            def pad_rows8(r, c):
                dst = xs_ref.at[pl.ds(pl.multiple_of(off + head + 8 * r, 8), 8)]
                pltpu.make_async_copy(zero_rows(8), dst, zsem).start()
                return c

            carry = lax.fori_loop(0, head, pad_row, carry)
            return lax.fori_loop(0, (n - head) >> 3, pad_rows8, carry)

        lax.fori_loop(0, N_EXPERTS, pad_expert, 0)

        def pad_tail(b, carry):
            pltpu.make_async_copy(zero_ref, xs_ref.at[pl.ds(b * ROW_BLOCK, ROW_BLOCK)],
                                  zsem).start()
            return carry

        lax.fori_loop(nb_ref[0], n_blocks, pad_tail, 0)
        for n in _row_chunks(n_pad_rows):
            pltpu.make_async_copy(zero_rows(n), xs_ref.at[pl.ds(0, n)], zsem).wait()

    def issue(i, carry):
        src = hp_ref.at[pl.ds(i, 1)]
        for k in range(TOP_K):
            pltpu.make_async_copy(src, xs_ref.at[pl.ds(dest_ref[k, i], 1)], sem).start()
        return carry

    lax.fori_loop(0, tm, issue, 0)
    for _ in range(TOP_K):
        pltpu.make_async_copy(hp_ref, xs_ref.at[pl.ds(0, tm)], sem).wait()


def moe_dispatch(dest, hp, pad_from, pad_n, n_used, n_blocks):
    t, w = hp.shape
    tm = min(512, t)
    n_rows = n_blocks * ROW_BLOCK
    grid_spec = pltpu.PrefetchScalarGridSpec(
        num_scalar_prefetch=3,
        grid=(t // tm,),
        in_specs=[
            pl.BlockSpec((TOP_K, tm), lambda i, *_: (0, i), memory_space=pltpu.SMEM),
            pl.BlockSpec((tm, w), lambda i, *_: (i, 0)),
        ],
        out_specs=pl.BlockSpec(memory_space=pl.ANY),
        scratch_shapes=[pltpu.VMEM((ROW_BLOCK, w), hp.dtype), pltpu.SemaphoreType.DMA(()),
                        pltpu.SemaphoreType.DMA(())],
    )
    return pl.pallas_call(
        functools.partial(_dispatch_kernel, n_pad_rows=n_rows - t * TOP_K),
        grid_spec=grid_spec,
        out_shape=jax.ShapeDtypeStruct((n_rows, w), hp.dtype),
        compiler_params=_cparams("arbitrary"),
        name="moe_dispatch",
    )(pad_from, pad_n, n_used, dest, hp)


def _expert_kernel(be_ref, nb_ref, xs_ref, w1_ref, w3_ref, w2_ref, y_ref):
    i = pl.program_id(0)

    @pl.when(i < nb_ref[0])
    def _():
        for c in range(ROW_BLOCK // EXPERT_ROWS):
            rows = pl.ds(c * EXPERT_ROWS, EXPERT_ROWS)
            x = jnp.concatenate(_unpack_rows(xs_ref[rows, :]), axis=1)
            act = (_silu(_dot(x, w1_ref[0])) * _dot(x, w3_ref[0])).astype(_bf16)
            y_ref[rows, :] = _pack_rows(_dot(act, w2_ref[0]))

    @pl.when(i >= nb_ref[0])
    def _():
        y_ref[...] = jnp.zeros_like(y_ref)


def moe_experts(xs, blk_e, n_used, w1, w3, w2):
    p, half = xs.shape
    d = 2 * half
    ff = w1.shape[2]
    nb = p // ROW_BLOCK
    row = lambda i, be, nu: (jnp.minimum(i, nu[0] - 1), 0)
    wsel = lambda i, be, nu: (be[i], 0, 0)
    grid_spec = pltpu.PrefetchScalarGridSpec(
        num_scalar_prefetch=2,
        grid=(nb,),
        in_specs=[
            pl.BlockSpec((ROW_BLOCK, half), row),
            pl.BlockSpec((1, d, ff), wsel),
            pl.BlockSpec((1, d, ff), wsel),
            pl.BlockSpec((1, ff, d), wsel),
        ],
        out_specs=pl.BlockSpec((ROW_BLOCK, half), lambda i, be, nu: (i, 0)),
    )
    return pl.pallas_call(
        _expert_kernel,
        grid_spec=grid_spec,
        out_shape=jax.ShapeDtypeStruct((p, half), jnp.uint32),
        compiler_params=_cparams("arbitrary"),
        name="moe_experts",
    )(blk_e, n_used, xs, w1, w3, w2)


def _combine_kernel(dest_ref, ys_ref, w_ref, shared_ref, x_ref, gate_ref, fn_ref, o_ref,
                    buf_ref, sem, *, tm, final_norm):
    def issue(i, carry):
        for k in range(TOP_K):
            pltpu.make_async_copy(ys_ref.at[pl.ds(dest_ref[k, i], 1)],
                                  buf_ref.at[k, pl.ds(i, 1)], sem).start()
        return carry

    lax.fori_loop(0, tm, issue, 0)
    for k in range(TOP_K):
        pltpu.make_async_copy(ys_ref.at[pl.ds(0, tm)], buf_ref.at[k], sem).wait()

    w = w_ref[...]
    half = buf_ref.shape[2]
    acc_lo, acc_hi = shared_ref[:, :half], shared_ref[:, half:]
    for k in range(TOP_K):
        p = buf_ref[k]
        acc_lo = acc_lo + w[:, k:k + 1] * pltpu.bitcast(p << 16, _f32)
        acc_hi = acc_hi + w[:, k:k + 1] * pltpu.bitcast(p & jnp.uint32(0xFFFF0000), _f32)
    xn = x_ref[0] + gate_ref[0] * jnp.concatenate([acc_lo, acc_hi], axis=1)
    if final_norm:
        xn = xn * lax.rsqrt(jnp.mean(xn * xn, axis=-1, keepdims=True) + EPS) * fn_ref[...]
    o_ref[0] = xn


def moe_combine(dest, ys, top_w, shared, x, gate, fn_g, final_norm):
    b, s, d = x.shape
    t = b * s
    tm = min(512, s)
    n_s = s // tm
    return pl.pallas_call(
        functools.partial(_combine_kernel, tm=tm, final_norm=final_norm),
        grid=(b, n_s),
        in_specs=[
            pl.BlockSpec((TOP_K, tm), lambda i, m: (0, i * n_s + m), memory_space=pltpu.SMEM),
            pl.BlockSpec(memory_space=pl.ANY),
            pl.BlockSpec((tm, TOP_K), lambda i, m: (i * n_s + m, 0)),
            pl.BlockSpec((tm, d), lambda i, m: (i * n_s + m, 0)),
            pl.BlockSpec((1, tm, d), lambda i, m: (i, m, 0)),
            pl.BlockSpec((1, 1, d), lambda i, m: (i, 0, 0)),
            pl.BlockSpec((1, d), lambda i, m: (0, 0)),
        ],
        out_specs=pl.BlockSpec((1, tm, d), lambda i, m: (i, m, 0)),
        out_shape=jax.ShapeDtypeStruct((b, s, d), _f32),
        scratch_shapes=[pltpu.VMEM((TOP_K, tm, d // 2), jnp.uint32), pltpu.SemaphoreType.DMA(())],
        compiler_params=_cparams("arbitrary", "arbitrary"),
        name="moe_combine",
    )(dest, ys, top_w, shared, x, gate, fn_g.reshape(1, d))


def moe_layer(x, g, sc, sh, gate, router_w, router_b, w1, w3, w2, s_w1, s_w3, s_w2,
              fn_g, final_norm):
    b, s, d = x.shape
    t = b * s
    hp, top_idx, top_w, shared = moe_router(x, g, sc, sh, router_w, router_b, s_w1, s_w3, s_w2)
    dest, counts = moe_rank(top_idx)

    counts = counts.reshape(N_EXPERTS)
    nblk = (counts + ROW_BLOCK - 1) // ROW_BLOCK
    blk_ends = jnp.cumsum(nblk)
    n_blocks = -(-(t * TOP_K) // ROW_BLOCK) + N_EXPERTS
    blk = jnp.arange(n_blocks, dtype=jnp.int32)
    blk_e = jnp.minimum(jnp.sum(blk[:, None] >= blk_ends[None, :], axis=1),
                        N_EXPERTS - 1).astype(jnp.int32)
    n_used = blk_ends[-1:].astype(jnp.int32)
    pad_from = ((blk_ends - nblk) * ROW_BLOCK + counts).astype(jnp.int32)
    pad_n = (nblk * ROW_BLOCK - counts).astype(jnp.int32)

    xs = moe_dispatch(dest, hp, pad_from, pad_n, n_used, n_blocks)
    ys = moe_experts(xs, blk_e, n_used, w1, w3, w2)
    return moe_combine(dest, ys, top_w.T, shared, x, gate, fn_g, final_norm)


def kernel(x, c, norm_mix, norm_ffn, ada_w, ada_b, da_w_in, da_w_out, da_lq1, da_lk1, da_lq2,
           da_lk2, da_subln, ret_w_in, ret_w_out, ret_gn, router_w, router_b, exp_w1, exp_w3,
           exp_w2, sh_w1, sh_w3, sh_w2, final_norm):
    depth, d = norm_mix.shape
    b = x.shape[0]
    mod = ada_modulation(c, ada_w, ada_b).reshape(depth, b, 6, 1, d)
    for i in range(depth):
        sh_a, sc_a, g_a, sh_f, sc_f, g_f = (mod[i, :, j] for j in range(6))
        j = i // 2
        if i % 2 == 0:
            proj = norm_proj(x, norm_mix[i], sc_a, sh_a, da_w_in[j].astype(_bf16))
            y = diff_attention_core(proj, da_lq1[j], da_lk1[j], da_lq2[j], da_lk2[j],
                                    da_subln[j], i)
            x = out_proj_residual(y, da_w_out[j].astype(_bf16), x, g_a)
        else:
            proj = norm_proj(x, norm_mix[i], sc_a, sh_a, ret_w_in[j].astype(_bf16))
            y = retention_core(proj, ret_gn[j])
            x = out_proj_residual(y, ret_w_out[j].astype(_bf16), x, g_a)
        x = moe_layer(x, norm_ffn[i], sc_f, sh_f, g_f, router_w[i], router_b[i],
                      exp_w1[i].astype(_bf16), exp_w3[i].astype(_bf16), exp_w2[i].astype(_bf16),
                      sh_w1[i], sh_w3[i], sh_w2[i], final_norm, i == depth - 1)
    return x
```

```python
import functools
import math

import jax
import jax.numpy as jnp
from jax import lax
from jax.experimental import pallas as pl
from jax.experimental.pallas import tpu as pltpu

EPS = 1e-6
DA_HEADS = 8
DA_HEAD_DIM = 64
DA_V_DIM = 2 * DA_HEAD_DIM
RET_HEADS = 4
RET_DK = 256
RET_DV = 512
RET_CHUNK = 128
N_EXPERTS = 64
TOP_K = 8
N_GROUPS = 8
GROUP_SIZE = N_EXPERTS // N_GROUPS
TOPK_GROUPS = 4
ROUTED_SCALE = 2.5
ROW_BLOCK_LOG2 = 9
ROW_BLOCK = 1 << ROW_BLOCK_LOG2
EXPERT_ROWS = 256

VMEM_LIMIT = 56 * 1024 * 1024
NEG_BIG = -1e30

_f32 = jnp.float32
_bf16 = jnp.bfloat16


def _cparams(*sem):
    return pltpu.CompilerParams(dimension_semantics=sem, vmem_limit_bytes=VMEM_LIMIT)


def _silu(v):
    return v * (1.0 / (1.0 + jnp.exp(-v)))


def _dot(a, b):
    return jnp.dot(a, b, preferred_element_type=_f32)


def _dot_nt(a, b):
    return lax.dot_general(a, b, (((1,), (1,)), ((), ())), preferred_element_type=_f32)


def _dot_tn(a, b):
    return lax.dot_general(a, b, (((0,), (0,)), ((), ())), preferred_element_type=_f32)


def _ada_kernel(c_ref, w_ref, b_ref, o_ref):
    a = _silu(c_ref[...]).astype(_bf16)
    o_ref[0] = _dot(a, w_ref[0].astype(_bf16)) + b_ref[0]


def ada_modulation(c, ada_w, ada_b):
    depth, d, n = ada_w.shape
    b = c.shape[0]
    tn = 1024
    return pl.pallas_call(
        _ada_kernel,
        grid=(depth, n // tn),
        in_specs=[
            pl.BlockSpec((b, d), lambda l, j: (0, 0)),
            pl.BlockSpec((1, d, tn), lambda l, j: (l, 0, j)),
            pl.BlockSpec((1, 1, tn), lambda l, j: (l, 0, j)),
        ],
        out_specs=pl.BlockSpec((1, b, tn), lambda l, j: (l, 0, j)),
        out_shape=jax.ShapeDtypeStruct((depth, b, n), _f32),
        compiler_params=_cparams("parallel", "parallel"),
        name="ada_modulation",
    )(c, ada_w, ada_b.reshape(depth, 1, n))


def _norm_mod(x, g, sc, sh):
    y = x * lax.rsqrt(jnp.mean(x * x, axis=-1, keepdims=True) + EPS)
    return (y * g) * (1.0 + sc) + sh


def _norm_proj_kernel(x_ref, g_ref, sc_ref, sh_ref, w_ref, o_ref, h_ref):
    @pl.when(pl.program_id(2) == 0)
    def _():
        h_ref[...] = _norm_mod(x_ref[0], g_ref[...], sc_ref[0], sh_ref[0]).astype(_bf16)

    o_ref[0] = _dot(h_ref[...], w_ref[...]).astype(o_ref.dtype)


def norm_proj(x, g, sc, sh, w):
    b, s, d = x.shape
    n = w.shape[1]
    tm = min(1024, s)
    tn = 1024
    return pl.pallas_call(
        _norm_proj_kernel,
        grid=(b, s // tm, n // tn),
        in_specs=[
            pl.BlockSpec((1, tm, d), lambda i, m, j: (i, m, 0)),
            pl.BlockSpec((1, d), lambda i, m, j: (0, 0)),
            pl.BlockSpec((1, 1, d), lambda i, m, j: (i, 0, 0)),
            pl.BlockSpec((1, 1, d), lambda i, m, j: (i, 0, 0)),
            pl.BlockSpec((d, tn), lambda i, m, j: (0, j)),
        ],
        out_specs=pl.BlockSpec((1, tm, tn), lambda i, m, j: (i, m, j)),
        out_shape=jax.ShapeDtypeStruct((b, s, n), _bf16),
        scratch_shapes=[pltpu.VMEM((tm, d), _bf16)],
        compiler_params=_cparams("parallel", "parallel", "arbitrary"),
        name="norm_proj",
    )(x, g.reshape(1, d), sc, sh, w)


def _out_proj_kernel(y_ref, w_ref, x_ref, g_ref, o_ref):
    o_ref[0] = x_ref[0] + g_ref[0] * _dot(y_ref[0], w_ref[...])


def out_proj_residual(y, w, x, gate):
    b, s, k = y.shape
    d = w.shape[1]
    tm = min(512, s)
    return pl.pallas_call(
        _out_proj_kernel,
        grid=(b, s // tm),
        in_specs=[
            pl.BlockSpec((1, tm, k), lambda i, m: (i, m, 0)),
            pl.BlockSpec((k, d), lambda i, m: (0, 0)),
            pl.BlockSpec((1, tm, d), lambda i, m: (i, m, 0)),
            pl.BlockSpec((1, 1, d), lambda i, m: (i, 0, 0)),
        ],
        out_specs=pl.BlockSpec((1, tm, d), lambda i, m: (i, m, 0)),
        out_shape=jax.ShapeDtypeStruct((b, s, d), _f32),
        compiler_params=_cparams("parallel", "parallel"),
        name="out_proj_residual",
    )(y, w, x, gate)


def _diff_attn_kernel(q_ref, k_ref, v_ref, lq1_ref, lk1_ref, lq2_ref, lk2_ref, sg_ref, o_ref,
                      bias_ref, m_ref, l_ref, acc_ref, *, tq, lam_init):
    h = pl.program_id(1)
    qi = pl.program_id(2)
    dh, dv = DA_HEAD_DIM, DA_V_DIM

    q = q_ref[0] * (dh ** -0.5)
    lane = lax.broadcasted_iota(jnp.int32, q.shape, 1)
    zero = jnp.zeros_like(q)
    q_maps = (jnp.where(lane < dh, q, zero), jnp.where(lane >= dh, q, zero))

    hp1 = jnp.full((1, 1), h + 1, jnp.int32).astype(_f32)
    slope = jnp.exp2(hp1 * (-8.0 / DA_HEADS))
    rel = (lax.broadcasted_iota(jnp.int32, (tq, 2 * tq), 0)
           - lax.broadcasted_iota(jnp.int32, (tq, 2 * tq), 1)).astype(_f32)
    bias_ref[...] = -slope * rel

    m_ref[...] = jnp.full(m_ref.shape, NEG_BIG, _f32)
    l_ref[...] = jnp.zeros(l_ref.shape, _f32)
    acc_ref[...] = jnp.zeros(acc_ref.shape, _f32)

    def span(j, width, bias, shift):
        start = pl.multiple_of(j * tq, tq)
        kj = k_ref[0, pl.ds(start, width), :]
        vj = v_ref[0, pl.ds(start, width), :]
        for mi in range(2):
            s = _dot_nt(q_maps[mi], kj) + bias
            m_prev = m_ref[mi]
            m_new = jnp.maximum(m_prev, jnp.max(s, axis=-1, keepdims=True) + shift)
            alpha = jnp.exp(m_prev - m_new)
            m_loc = m_new - shift
            p = jnp.exp(s - jnp.concatenate([m_loc] * (width // 128), axis=1))
            l_ref[mi] = alpha * l_ref[mi] + jnp.sum(p, axis=-1, keepdims=True)
            acc_ref[mi] = alpha * acc_ref[mi] + _dot(p.astype(_bf16), vj)
            m_ref[mi] = m_new

    def body(jj, carry):
        shift = -slope * jnp.full((1, 1), (qi - 2 * jj) * tq, jnp.int32).astype(_f32)
        span(2 * jj, 2 * tq, bias_ref[...], shift)
        return carry

    lax.fori_loop(0, qi // 2, body, 0)

    @pl.when(qi % 2 == 1)
    def _():
        span(qi - 1, 2 * tq, jnp.where(rel + tq >= 0, bias_ref[...], NEG_BIG), -slope * tq)

    @pl.when(qi % 2 == 0)
    def _():
        span(qi, tq, jnp.where(rel[:, :tq] >= 0, bias_ref[:, :tq], NEG_BIG), 0.0)

    lam = (jnp.exp(jnp.sum(lq1_ref[...] * lk1_ref[...], axis=-1, keepdims=True))
           - jnp.exp(jnp.sum(lq2_ref[...] * lk2_ref[...], axis=-1, keepdims=True)) + lam_init)
    o = acc_ref[0] / l_ref[0] - lam * (acc_ref[1] / l_ref[1])
    o = o * lax.rsqrt(jnp.mean(o * o, axis=-1, keepdims=True) + EPS)
    o_ref[0] = ((o * sg_ref[...]) * (1.0 - lam_init)).astype(o_ref.dtype)


def diff_attention_core(proj, lq1, lk1, lq2, lk2, subln_g, layer_idx):
    b, s, _ = proj.shape
    nh, dh, dv = DA_HEADS, DA_HEAD_DIM, DA_V_DIM
    tq = min(512, s)
    lam_init = 0.8 - 0.6 * math.exp(-0.3 * layer_idx)
    vec = lambda n: pl.BlockSpec((1, n), lambda i, h, m: (0, 0))
    return pl.pallas_call(
        functools.partial(_diff_attn_kernel, tq=tq, lam_init=lam_init),
        grid=(b, nh, s // tq),
        in_specs=[
            pl.BlockSpec((1, tq, 2 * dh), lambda i, h, m: (i, m, h)),
            pl.BlockSpec((1, s, 2 * dh), lambda i, h, m: (i, 0, nh + h)),
            pl.BlockSpec((1, s, dv), lambda i, h, m: (i, 0, 2 * nh + h)),
            vec(dh), vec(dh), vec(dh), vec(dh), vec(dv),
        ],
        out_specs=pl.BlockSpec((1, tq, dv), lambda i, h, m: (i, m, h)),
        out_shape=jax.ShapeDtypeStruct((b, s, nh * dv), _bf16),
        scratch_shapes=[pltpu.VMEM((tq, 2 * tq), _f32), pltpu.VMEM((2, tq, dv), _f32),
                        pltpu.VMEM((2, tq, dv), _f32), pltpu.VMEM((2, tq, dv), _f32)],
        compiler_params=_cparams("parallel", "parallel", "parallel"),
        name="diff_attention",
    )(proj, proj, proj, lq1.reshape(1, dh), lk1.reshape(1, dh), lq2.reshape(1, dh),
      lk2.reshape(1, dh), subln_g.reshape(1, dv))


def _retention_kernel(q_ref, k_ref, v_ref, g_ref, gn_ref, o_ref, r_ref, *, chunk):
    h = pl.program_id(1)
    c = pl.program_id(2)
    dk = RET_DK

    @pl.when(c == 0)
    def _():
        r_ref[...] = jnp.zeros_like(r_ref)

    hf = jnp.full((1, 1), h, jnp.int32).astype(_f32)
    log_g = jnp.log(1.0 - jnp.exp2(-5.0 - hf))
    ri = lax.broadcasted_iota(jnp.int32, (chunk, chunk), 0)
    ci = lax.broadcasted_iota(jnp.int32, (chunk, chunk), 1)
    diff = (ri - ci).astype(_f32)
    decay = jnp.where(diff >= 0, jnp.exp(log_g * jnp.maximum(diff, 0.0)), 0.0)
    idx = lax.broadcasted_iota(jnp.int32, (chunk, 1), 0).astype(_f32)
    zeta = jnp.exp(log_g * (chunk - 1 - idx))
    xi = jnp.exp(log_g * (idx + 1.0))
    g_c = jnp.exp(log_g * chunk)

    decay = decay * (dk ** -0.5)
    zeta = zeta * (dk ** -0.5)
    for sub in range(q_ref.shape[1] // chunk):
        rows = pl.ds(sub * chunk, chunk)
        q = q_ref[0, rows, :]
        k = k_ref[0, rows, :]
        v = v_ref[0, rows, :]
        r = r_ref[...]
        inner = _dot_nt(q, k) * decay
        o = _dot(inner.astype(_bf16), v) + _dot(q, r.astype(_bf16)) * xi
        kz = (k.astype(_f32) * zeta).astype(_bf16)
        r_ref[...] = g_c * r + _dot_tn(kz, v)

        o = o * lax.rsqrt(jnp.mean(o * o, axis=-1, keepdims=True) + EPS) * gn_ref[...]
        o_ref[0, rows, :] = (_silu(g_ref[0, rows, :].astype(_f32)) * o).astype(o_ref.dtype)


def retention_core(proj, gn_g):
    b, s, _ = proj.shape
    nh, dk, dv = RET_HEADS, RET_DK, RET_DV
    chunk = RET_CHUNK
    rows = min(4 * chunk, s)
    return pl.pallas_call(
        functools.partial(_retention_kernel, chunk=chunk),
        grid=(b, nh, s // rows),
        in_specs=[
            pl.BlockSpec((1, rows, dk), lambda i, h, c: (i, c, h)),
            pl.BlockSpec((1, rows, dk), lambda i, h, c: (i, c, nh + h)),
            pl.BlockSpec((1, rows, dv), lambda i, h, c: (i, c, (2 * nh * dk) // dv + h)),
            pl.BlockSpec((1, rows, dv), lambda i, h, c: (i, c, (2 * nh * dk) // dv + nh + h)),
            pl.BlockSpec((1, dv), lambda i, h, c: (0, h)),
        ],
        out_specs=pl.BlockSpec((1, rows, dv), lambda i, h, c: (i, c, h)),
        out_shape=jax.ShapeDtypeStruct((b, s, nh * dv), _bf16),
        scratch_shapes=[pltpu.VMEM((dk, dv), _f32)],
        compiler_params=_cparams("parallel", "parallel", "arbitrary"),
        name="retention",
    )(proj, proj, proj, proj, gn_g.reshape(1, nh * dv))


def _pack_rows(h):
    n = h.shape[1] // 2
    lo = pltpu.bitcast(h[:, :n].astype(_bf16).astype(_f32), jnp.uint32)
    hi = pltpu.bitcast(h[:, n:].astype(_bf16).astype(_f32), jnp.uint32)
    return (lo >> 16) | (hi & jnp.uint32(0xFFFF0000))


def _unpack_rows(p):
    lo = pltpu.bitcast(p << 16, _f32).astype(_bf16)
    hi = pltpu.bitcast(p & jnp.uint32(0xFFFF0000), _f32).astype(_bf16)
    return lo, hi


def _first_argmax(v, iota, axis, big):
    m = jnp.max(v, axis=axis, keepdims=True)
    i = jnp.min(jnp.where(v == m, iota, big), axis=axis, keepdims=True)
    return m, i


def _router_kernel(x_ref, g_ref, sc_ref, sh_ref, rwt_ref, rb_ref, s1_ref, s3_ref, s2_ref,
                   hp_ref, idx_ref, wgt_ref, shared_ref):
    tm = x_ref.shape[1]
    h = _norm_mod(x_ref[0], g_ref[...], sc_ref[0], sh_ref[0])
    hp_ref[...] = _pack_rows(h)
    hb = h.astype(_bf16)
    act = (_silu(_dot(hb, s1_ref[...])) * _dot(hb, s3_ref[...])).astype(_bf16)
    shared_ref[...] = _dot(act, s2_ref[...])

    scores = 1.0 / (1.0 + jnp.exp(-_dot_nt(rwt_ref[...], h)))
    choice = scores + rb_ref[...]
    c3 = choice.reshape(N_GROUPS, GROUP_SIZE, tm)
    mem = lax.broadcasted_iota(jnp.int32, c3.shape, 1)
    m1, i1 = _first_argmax(c3, mem, 1, GROUP_SIZE)
    m2 = jnp.max(jnp.where(mem == i1, -jnp.inf, c3), axis=1, keepdims=True)
    gs = m1 + m2
    gi = lax.broadcasted_iota(jnp.int32, gs.shape, 0)
    sel = jnp.zeros(gs.shape, _f32)
    for _ in range(TOPK_GROUPS):
        _, ig = _first_argmax(gs, gi, 0, N_GROUPS)
        hit = gi == ig
        sel = jnp.where(hit, 1.0, sel)
        gs = jnp.where(hit, -jnp.inf, gs)
    cm = jnp.where(sel > 0.0, c3, -jnp.inf).reshape(N_EXPERTS, tm)
    ei = lax.broadcasted_iota(jnp.int32, cm.shape, 0)
    idxs, wgts = [], []
    for _ in range(TOP_K):
        _, ie = _first_argmax(cm, ei, 0, N_EXPERTS)
        hit = ei == ie
        idxs.append(ie)
        wgts.append(jnp.sum(jnp.where(hit, scores, 0.0), axis=0, keepdims=True))
        cm = jnp.where(hit, -jnp.inf, cm)
    w = jnp.concatenate(wgts, axis=0)
    idx_ref[...] = jnp.concatenate(idxs, axis=0)
    wgt_ref[...] = w / jnp.sum(w, axis=0, keepdims=True) * ROUTED_SCALE


def moe_router(x, g, sc, sh, router_w, router_b, s_w1, s_w3, s_w2):
    b, s, d = x.shape
    t = b * s
    tm = min(256, s)
    n_s = s // tm
    sf = s_w1.shape[1]
    const = lambda shape: pl.BlockSpec(shape, lambda i, m: (0,) * len(shape))
    return pl.pallas_call(
        _router_kernel,
        grid=(b, n_s),
        in_specs=[
            pl.BlockSpec((1, tm, d), lambda i, m: (i, m, 0)),
            const((1, d)),
            pl.BlockSpec((1, 1, d), lambda i, m: (i, 0, 0)),
            pl.BlockSpec((1, 1, d), lambda i, m: (i, 0, 0)),
            const((N_EXPERTS, d)), const((N_EXPERTS, 1)),
            const((d, sf)), const((d, sf)), const((sf, d)),
        ],
        out_specs=[
            pl.BlockSpec((tm, d // 2), lambda i, m: (i * n_s + m, 0)),
            pl.BlockSpec((TOP_K, tm), lambda i, m: (0, i * n_s + m)),
            pl.BlockSpec((TOP_K, tm), lambda i, m: (0, i * n_s + m)),
            pl.BlockSpec((tm, d), lambda i, m: (i * n_s + m, 0)),
        ],
        out_shape=[
            jax.ShapeDtypeStruct((t, d // 2), jnp.uint32),
            jax.ShapeDtypeStruct((TOP_K, t), jnp.int32),
            jax.ShapeDtypeStruct((TOP_K, t), _f32),
            jax.ShapeDtypeStruct((t, d), _f32),
        ],
        compiler_params=_cparams("parallel", "parallel"),
        name="moe_router",
    )(x, g.reshape(1, d), sc, sh, router_w.T, router_b.reshape(N_EXPERTS, 1),
      s_w1.astype(_bf16), s_w3.astype(_bf16), s_w2.astype(_bf16))


def _strict_lower(n):
    return jnp.where(lax.broadcasted_iota(jnp.int32, (n, n), 1)
                     < lax.broadcasted_iota(jnp.int32, (n, n), 0), 1.0, 0.0).astype(_bf16)


def _rank_kernel(idx_ref, dest_ref, cnt_ref, base_ref):
    phase = pl.program_id(0)
    i = pl.program_id(1)
    tm = idx_ref.shape[1]

    idx = idx_ref[...]
    ei = lax.broadcasted_iota(jnp.int32, (N_EXPERTS, tm), 0)
    hits = [ei == idx[k:k + 1, :] for k in range(TOP_K)]
    memf = jnp.where(functools.reduce(jnp.logical_or, hits), 1.0, 0.0)
    tile_cnt = jnp.sum(memf, axis=1, keepdims=True)

    @pl.when(jnp.logical_and(phase == 0, i == 0))
    def _():
        base_ref[...] = jnp.zeros_like(base_ref)

    @pl.when(phase == 0)
    def _():
        base_ref[...] += tile_cnt

    @pl.when(jnp.logical_and(phase == 1, i == 0))
    def _():
        cnt = base_ref[...].astype(jnp.int32)
        cnt_ref[...] = cnt
        nblk = (cnt + (ROW_BLOCK - 1)) >> ROW_BLOCK_LOG2
        hi = jnp.broadcast_to((nblk >> 5).astype(_f32), (N_EXPERTS, 128)).astype(_bf16)
        lo = jnp.broadcast_to((nblk & 31).astype(_f32), (N_EXPERTS, 128)).astype(_bf16)
        low_tri = _strict_lower(N_EXPERTS)
        start_blk = 32.0 * _dot(low_tri, hi) + _dot(low_tri, lo)
        base_ref[...] = start_blk[:, :1] * float(ROW_BLOCK)

    @pl.when(phase == 1)
    def _():
        before = _dot_nt(memf.astype(_bf16), _strict_lower(tm)) + base_ref[...]
        dest_ref[...] = jnp.concatenate(
            [jnp.sum(jnp.where(hits[k], before, 0.0), axis=0, keepdims=True)
             for k in range(TOP_K)], axis=0).astype(jnp.int32)
        base_ref[...] += tile_cnt


def moe_rank(top_idx):
    k, t = top_idx.shape
    tm = min(512, t)
    return pl.pallas_call(
        _rank_kernel,
        grid=(2, t // tm),
        in_specs=[pl.BlockSpec((k, tm), lambda p, i: (0, i))],
        out_specs=[pl.BlockSpec((k, tm), lambda p, i: (0, i * p)),
                   pl.BlockSpec((N_EXPERTS, 1), lambda p, i: (0, 0))],
        out_shape=[jax.ShapeDtypeStruct((k, t), jnp.int32),
                   jax.ShapeDtypeStruct((N_EXPERTS, 1), jnp.int32)],
        scratch_shapes=[pltpu.VMEM((N_EXPERTS, 1), _f32)],
        compiler_params=_cparams("arbitrary", "arbitrary"),
        name="moe_rank",
    )(top_idx)


def _row_chunks(n):
    return [ROW_BLOCK] * (n // ROW_BLOCK) + ([n % ROW_BLOCK] if n % ROW_BLOCK else [])


def _dispatch_kernel(pad_from_ref, pad_n_ref, nb_ref, dest_ref, hp_ref, xs_ref, zero_ref,
                     sem, zsem, *, n_pad_rows):
    tm = hp_ref.shape[0]
    n_blocks = xs_ref.shape[0] // ROW_BLOCK
    zero_rows = lambda n: zero_ref.at[pl.ds(0, n)]

    @pl.when(pl.program_id(0) == 0)
    def _():
        zero_ref[...] = jnp.zeros_like(zero_ref)

        def pad_expert(e, carry):
            off, n = pad_from_ref[e], pad_n_ref[e]
            head = jnp.minimum((-off) & 7, n)

            def pad_row(r, c):
                pltpu.make_async_copy(zero_rows(1), xs_ref.at[pl.ds(off + r, 1)], zsem).start()
                return c

            def pad_rows8(r, c):
                dst = xs_ref.at[pl.ds(pl.multiple_of(off + head + 8 * r, 8), 8)]
                pltpu.make_async_copy(zero_rows(8), dst, zsem).start()
                return c

            carry = lax.fori_loop(0, head, pad_row, carry)
            return lax.fori_loop(0, (n - head) >> 3, pad_rows8, carry)

        lax.fori_loop(0, N_EXPERTS, pad_expert, 0)

        def pad_tail(b, carry):
            pltpu.make_async_copy(zero_ref, xs_ref.at[pl.ds(b * ROW_BLOCK, ROW_BLOCK)],
                                  zsem).start()
            return carry

        lax.fori_loop(nb_ref[0], n_blocks, pad_tail, 0)
        for n in _row_chunks(n_pad_rows):
            pltpu.make_async_copy(zero_rows(n), xs_ref.at[pl.ds(0, n)], zsem).wait()

    def issue(g, carry):
        row0 = pl.multiple_of(g * 8, 8)
        for s in range(8):
            src = hp_ref.at[pl.ds(row0 + s, 1)]
            for k in range(TOP_K):
                dst = xs_ref.at[pl.ds(dest_ref[row0 * TOP_K + (s * TOP_K + k)], 1)]
                pltpu.make_async_copy(src, dst, sem).start()
        return carry

    lax.fori_loop(0, tm // 8, issue, 0)
    for _ in range(TOP_K):
        pltpu.make_async_copy(hp_ref, xs_ref.at[pl.ds(0, tm)], sem).wait()


def moe_dispatch(dest, hp, pad_from, pad_n, n_used, n_blocks):
    t, w = hp.shape
    tm = min(512, t)
    n_rows = n_blocks * ROW_BLOCK
    grid_spec = pltpu.PrefetchScalarGridSpec(
        num_scalar_prefetch=3,
        grid=(t // tm,),
        in_specs=[
            pl.BlockSpec((tm * TOP_K,), lambda i, *_: (i,), memory_space=pltpu.SMEM),
            pl.BlockSpec((tm, w), lambda i, *_: (i, 0)),
        ],
        out_specs=pl.BlockSpec(memory_space=pl.ANY),
        scratch_shapes=[pltpu.VMEM((ROW_BLOCK, w), hp.dtype), pltpu.SemaphoreType.DMA(()),
                        pltpu.SemaphoreType.DMA(())],
    )
    return pl.pallas_call(
        functools.partial(_dispatch_kernel, n_pad_rows=n_rows - t * TOP_K),
        grid_spec=grid_spec,
        out_shape=jax.ShapeDtypeStruct((n_rows, w), hp.dtype),
        compiler_params=_cparams("arbitrary"),
        name="moe_dispatch",
    )(pad_from, pad_n, n_used, dest, hp)


def _expert_kernel(be_ref, nb_ref, xs_ref, w1_ref, w3_ref, w2_ref, y_ref, w1b, w3b, w2b):
    i = pl.program_id(0)

    @pl.when(jnp.logical_or(i == 0, be_ref[i] != be_ref[jnp.maximum(i - 1, 0)]))
    def _():
        w1b[...] = w1_ref[0].astype(_bf16)
        w3b[...] = w3_ref[0].astype(_bf16)
        w2b[...] = w2_ref[0].astype(_bf16)

    @pl.when(i < nb_ref[0])
    def _():
        chains = [pl.ds(c * EXPERT_ROWS, EXPERT_ROWS) for c in range(ROW_BLOCK // EXPERT_ROWS)]
        xs = [jnp.concatenate(_unpack_rows(xs_ref[rows, :]), axis=1) for rows in chains]
        acts = [(_silu(_dot(x, w1b[...])) * _dot(x, w3b[...])).astype(_bf16) for x in xs]
        for rows, act in zip(chains, acts):
            y_ref[rows, :] = _pack_rows(_dot(act, w2b[...]))

    @pl.when(i >= nb_ref[0])
    def _():
        y_ref[...] = jnp.zeros_like(y_ref)


def moe_experts(xs, blk_e, n_used, w1, w3, w2):
    p, half = xs.shape
    d = 2 * half
    ff = w1.shape[2]
    nb = p // ROW_BLOCK
    row = lambda i, be, nu: (jnp.minimum(i, nu[0] - 1), 0)
    wsel = lambda i, be, nu: (be[i], 0, 0)
    grid_spec = pltpu.PrefetchScalarGridSpec(
        num_scalar_prefetch=2,
        grid=(nb,),
        in_specs=[
            pl.BlockSpec((ROW_BLOCK, half), row),
            pl.BlockSpec((1, d, ff), wsel),
            pl.BlockSpec((1, d, ff), wsel),
            pl.BlockSpec((1, ff, d), wsel),
        ],
        out_specs=pl.BlockSpec((ROW_BLOCK, half), lambda i, be, nu: (i, 0)),
        scratch_shapes=[pltpu.VMEM((d, ff), _bf16), pltpu.VMEM((d, ff), _bf16),
                        pltpu.VMEM((ff, d), _bf16)],
    )
    return pl.pallas_call(
        _expert_kernel,
        grid_spec=grid_spec,
        out_shape=jax.ShapeDtypeStruct((p, half), jnp.uint32),
        compiler_params=_cparams("arbitrary"),
        name="moe_experts",
    )(blk_e, n_used, xs, w1, w3, w2)


def _combine_kernel(dest_ref, ys_ref, w_ref, shared_ref, x_ref, gate_ref, fn_ref, o_ref,
                    buf_ref, sem, *, tm, final_norm):
    def issue(g, carry):
        row0 = pl.multiple_of(g * 8, 8)
        for s in range(8):
            for k in range(TOP_K):
                src = ys_ref.at[pl.ds(dest_ref[row0 * TOP_K + (s * TOP_K + k)], 1)]
                pltpu.make_async_copy(src, buf_ref.at[k, pl.ds(row0 + s, 1)], sem).start()
        return carry

    lax.fori_loop(0, tm // 8, issue, 0)
    for k in range(TOP_K):
        pltpu.make_async_copy(ys_ref.at[pl.ds(0, tm)], buf_ref.at[k], sem).wait()

    w = w_ref[...]
    half = buf_ref.shape[2]
    acc_lo, acc_hi = shared_ref[:, :half], shared_ref[:, half:]
    for k in range(TOP_K):
        p = buf_ref[k]
        acc_lo = acc_lo + w[:, k:k + 1] * pltpu.bitcast(p << 16, _f32)
        acc_hi = acc_hi + w[:, k:k + 1] * pltpu.bitcast(p & jnp.uint32(0xFFFF0000), _f32)
    xn = x_ref[0] + gate_ref[0] * jnp.concatenate([acc_lo, acc_hi], axis=1)
    if final_norm:
        xn = xn * lax.rsqrt(jnp.mean(xn * xn, axis=-1, keepdims=True) + EPS) * fn_ref[...]
    o_ref[0] = xn


def moe_combine(dest, ys, top_w, shared, x, gate, fn_g, final_norm):
    b, s, d = x.shape
    t = b * s
    tm = min(512, s)
    n_s = s // tm
    return pl.pallas_call(
        functools.partial(_combine_kernel, tm=tm, final_norm=final_norm),
        grid=(b, n_s),
        in_specs=[
            pl.BlockSpec((tm * TOP_K,), lambda i, m: (i * n_s + m,), memory_space=pltpu.SMEM),
            pl.BlockSpec(memory_space=pl.ANY),
            pl.BlockSpec((tm, TOP_K), lambda i, m: (i * n_s + m, 0)),
            pl.BlockSpec((tm, d), lambda i, m: (i * n_s + m, 0)),
            pl.BlockSpec((1, tm, d), lambda i, m: (i, m, 0)),
            pl.BlockSpec((1, 1, d), lambda i, m: (i, 0, 0)),
            pl.BlockSpec((1, d), lambda i, m: (0, 0)),
        ],
        out_specs=pl.BlockSpec((1, tm, d), lambda i, m: (i, m, 0)),
        out_shape=jax.ShapeDtypeStruct((b, s, d), _f32),
        scratch_shapes=[pltpu.VMEM((TOP_K, tm, d // 2), jnp.uint32), pltpu.SemaphoreType.DMA(())],
        compiler_params=_cparams("arbitrary", "arbitrary"),
        name="moe_combine",
    )(dest, ys, top_w, shared, x, gate, fn_g.reshape(1, d))


def moe_layer(x, g, sc, sh, gate, router_w, router_b, w1, w3, w2, s_w1, s_w3, s_w2,
              fn_g, final_norm):
    b, s, d = x.shape
    t = b * s
    hp, top_idx, top_w, shared = moe_router(x, g, sc, sh, router_w, router_b, s_w1, s_w3, s_w2)
    dest, counts = moe_rank(top_idx)

    counts = counts.reshape(N_EXPERTS)
    nblk = (counts + ROW_BLOCK - 1) // ROW_BLOCK
    blk_ends = jnp.cumsum(nblk)
    n_blocks = -(-(t * TOP_K) // ROW_BLOCK) + N_EXPERTS
    blk = jnp.arange(n_blocks, dtype=jnp.int32)
    blk_e = jnp.minimum(jnp.sum(blk[:, None] >= blk_ends[None, :], axis=1),
                        N_EXPERTS - 1).astype(jnp.int32)
    n_used = blk_ends[-1:].astype(jnp.int32)
    pad_from = ((blk_ends - nblk) * ROW_BLOCK + counts).astype(jnp.int32)
    pad_n = (nblk * ROW_BLOCK - counts).astype(jnp.int32)

    dest = dest.T.reshape(t * TOP_K)
    xs = moe_dispatch(dest, hp, pad_from, pad_n, n_used, n_blocks)
    ys = moe_experts(xs, blk_e, n_used, w1, w3, w2)
    return moe_combine(dest, ys, top_w.T, shared, x, gate, fn_g, final_norm)


def kernel(x, c, norm_mix, norm_ffn, ada_w, ada_b, da_w_in, da_w_out, da_lq1, da_lk1, da_lq2,
           da_lk2, da_subln, ret_w_in, ret_w_out, ret_gn, router_w, router_b, exp_w1, exp_w3,
           exp_w2, sh_w1, sh_w3, sh_w2, final_norm):
    depth, d = norm_mix.shape
    b = x.shape[0]
    mod = ada_modulation(c, ada_w, ada_b).reshape(depth, b, 6, 1, d)
    for i in range(depth):
        sh_a, sc_a, g_a, sh_f, sc_f, g_f = (mod[i, :, j] for j in range(6))
        j = i // 2
        if i % 2 == 0:
            proj = norm_proj(x, norm_mix[i], sc_a, sh_a, da_w_in[j].astype(_bf16))
            y = diff_attention_core(proj, da_lq1[j], da_lk1[j], da_lq2[j], da_lk2[j],
                                    da_subln[j], i)
            x = out_proj_residual(y, da_w_out[j].astype(_bf16), x, g_a)
        else:
            proj = norm_proj(x, norm_mix[i], sc_a, sh_a, ret_w_in[j].astype(_bf16))
            y = retention_core(proj, ret_gn[j])
            x = out_proj_residual(y, ret_w_out[j].astype(_bf16), x, g_a)
        x = moe_layer(x, norm_ffn[i], sc_f, sh_f, g_f, router_w[i], router_b[i],
                      exp_w1[i], exp_w3[i], exp_w2[i],
                      sh_w1[i], sh_w3[i], sh_w2[i], final_norm, i == depth - 1)
    return x
```

```python
import functools
import math

import jax
import jax.numpy as jnp
from jax import lax
from jax.experimental import pallas as pl
from jax.experimental.pallas import tpu as pltpu

EPS = 1e-6
DA_HEADS = 8
DA_HEAD_DIM = 64
DA_V_DIM = 2 * DA_HEAD_DIM
RET_HEADS = 4
RET_DK = 256
RET_DV = 512
RET_CHUNK = 128
N_EXPERTS = 64
TOP_K = 8
N_GROUPS = 8
GROUP_SIZE = N_EXPERTS // N_GROUPS
TOPK_GROUPS = 4
ROUTED_SCALE = 2.5
ROW_BLOCK_LOG2 = 9
ROW_BLOCK = 1 << ROW_BLOCK_LOG2
EXPERT_ROWS = 256
TOKEN_TILE = 256
SEG_ALIGN = 8
PERM_ROWS = 512
PIECE_SLOTS = 384

VMEM_LIMIT = 56 * 1024 * 1024
NEG_BIG = -1e30

_f32 = jnp.float32
_bf16 = jnp.bfloat16


def _cparams(*sem):
    return pltpu.CompilerParams(dimension_semantics=sem, vmem_limit_bytes=VMEM_LIMIT)


def _silu(v):
    return v * (1.0 / (1.0 + jnp.exp(-v)))


def _dot(a, b):
    return jnp.dot(a, b, preferred_element_type=_f32)


def _dot_nt(a, b):
    return lax.dot_general(a, b, (((1,), (1,)), ((), ())), preferred_element_type=_f32)


def _dot_tn(a, b):
    return lax.dot_general(a, b, (((0,), (0,)), ((), ())), preferred_element_type=_f32)


def _ada_kernel(c_ref, w_ref, b_ref, o_ref):
    a = _silu(c_ref[...]).astype(_bf16)
    o_ref[0] = _dot(a, w_ref[0].astype(_bf16)) + b_ref[0]


def ada_modulation(c, ada_w, ada_b):
    depth, d, n = ada_w.shape
    b = c.shape[0]
    tn = 1024
    return pl.pallas_call(
        _ada_kernel,
        grid=(depth, n // tn),
        in_specs=[
            pl.BlockSpec((b, d), lambda l, j: (0, 0)),
            pl.BlockSpec((1, d, tn), lambda l, j: (l, 0, j)),
            pl.BlockSpec((1, 1, tn), lambda l, j: (l, 0, j)),
        ],
        out_specs=pl.BlockSpec((1, b, tn), lambda l, j: (l, 0, j)),
        out_shape=jax.ShapeDtypeStruct((depth, b, n), _f32),
        compiler_params=_cparams("parallel", "parallel"),
        name="ada_modulation",
    )(c, ada_w, ada_b.reshape(depth, 1, n))


def _norm_mod(x, g, sc, sh):
    y = x * lax.rsqrt(jnp.mean(x * x, axis=-1, keepdims=True) + EPS)
    return (y * g) * (1.0 + sc) + sh


def _norm_proj_kernel(x_ref, g_ref, sc_ref, sh_ref, w_ref, o_ref, h_ref):
    @pl.when(pl.program_id(2) == 0)
    def _():
        h_ref[...] = _norm_mod(x_ref[0], g_ref[...], sc_ref[0], sh_ref[0]).astype(_bf16)

    o_ref[0] = _dot(h_ref[...], w_ref[...]).astype(o_ref.dtype)


def norm_proj(x, g, sc, sh, w):
    b, s, d = x.shape
    n = w.shape[1]
    tm = min(1024, s)
    tn = 1024
    return pl.pallas_call(
        _norm_proj_kernel,
        grid=(b, s // tm, n // tn),
        in_specs=[
            pl.BlockSpec((1, tm, d), lambda i, m, j: (i, m, 0)),
            pl.BlockSpec((1, d), lambda i, m, j: (0, 0)),
            pl.BlockSpec((1, 1, d), lambda i, m, j: (i, 0, 0)),
            pl.BlockSpec((1, 1, d), lambda i, m, j: (i, 0, 0)),
            pl.BlockSpec((d, tn), lambda i, m, j: (0, j)),
        ],
        out_specs=pl.BlockSpec((1, tm, tn), lambda i, m, j: (i, m, j)),
        out_shape=jax.ShapeDtypeStruct((b, s, n), _bf16),
        scratch_shapes=[pltpu.VMEM((tm, d), _bf16)],
        compiler_params=_cparams("parallel", "parallel", "arbitrary"),
        name="norm_proj",
    )(x, g.reshape(1, d), sc, sh, w)


def _out_proj_kernel(y_ref, w_ref, x_ref, g_ref, o_ref):
    o_ref[0] = x_ref[0] + g_ref[0] * _dot(y_ref[0], w_ref[...])


def out_proj_residual(y, w, x, gate):
    b, s, k = y.shape
    d = w.shape[1]
    tm = min(512, s)
    return pl.pallas_call(
        _out_proj_kernel,
        grid=(b, s // tm),
        in_specs=[
            pl.BlockSpec((1, tm, k), lambda i, m: (i, m, 0)),
            pl.BlockSpec((k, d), lambda i, m: (0, 0)),
            pl.BlockSpec((1, tm, d), lambda i, m: (i, m, 0)),
            pl.BlockSpec((1, 1, d), lambda i, m: (i, 0, 0)),
        ],
        out_specs=pl.BlockSpec((1, tm, d), lambda i, m: (i, m, 0)),
        out_shape=jax.ShapeDtypeStruct((b, s, d), _f32),
        compiler_params=_cparams("parallel", "parallel"),
        name="out_proj_residual",
    )(y, w, x, gate)


def _diff_attn_kernel(q_ref, k_ref, v_ref, lq1_ref, lk1_ref, lq2_ref, lk2_ref, sg_ref, o_ref,
                      bias_ref, m_ref, l_ref, acc_ref, *, tq, lam_init):
    h = pl.program_id(1)
    qi = pl.program_id(2)
    dh, dv = DA_HEAD_DIM, DA_V_DIM

    q = q_ref[0] * (dh ** -0.5)
    lane = lax.broadcasted_iota(jnp.int32, q.shape, 1)
    zero = jnp.zeros_like(q)
    q_maps = (jnp.where(lane < dh, q, zero), jnp.where(lane >= dh, q, zero))

    hp1 = jnp.full((1, 1), h + 1, jnp.int32).astype(_f32)
    slope = jnp.exp2(hp1 * (-8.0 / DA_HEADS))
    rel = (lax.broadcasted_iota(jnp.int32, (tq, 2 * tq), 0)
           - lax.broadcasted_iota(jnp.int32, (tq, 2 * tq), 1)).astype(_f32)
    bias_ref[...] = -slope * rel

    m_ref[...] = jnp.full(m_ref.shape, NEG_BIG, _f32)
    l_ref[...] = jnp.zeros(l_ref.shape, _f32)
    acc_ref[...] = jnp.zeros(acc_ref.shape, _f32)

    def span(j, width, bias, shift):
        start = pl.multiple_of(j * tq, tq)
        kj = k_ref[0, pl.ds(start, width), :]
        vj = v_ref[0, pl.ds(start, width), :]
        for mi in range(2):
            s = _dot_nt(q_maps[mi], kj) + bias
            m_prev = m_ref[mi]
            m_new = jnp.maximum(m_prev, jnp.max(s, axis=-1, keepdims=True) + shift)
            alpha = jnp.exp(m_prev - m_new)
            m_loc = m_new - shift
            p = jnp.exp(s - jnp.concatenate([m_loc] * (width // 128), axis=1))
            l_ref[mi] = alpha * l_ref[mi] + jnp.sum(p, axis=-1, keepdims=True)
            acc_ref[mi] = alpha * acc_ref[mi] + _dot(p.astype(_bf16), vj)
            m_ref[mi] = m_new

    def body(jj, carry):
        shift = -slope * jnp.full((1, 1), (qi - 2 * jj) * tq, jnp.int32).astype(_f32)
        span(2 * jj, 2 * tq, bias_ref[...], shift)
        return carry

    lax.fori_loop(0, qi // 2, body, 0)

    @pl.when(qi % 2 == 1)
    def _():
        span(qi - 1, 2 * tq, jnp.where(rel + tq >= 0, bias_ref[...], NEG_BIG), -slope * tq)

    @pl.when(qi % 2 == 0)
    def _():
        span(qi, tq, jnp.where(rel[:, :tq] >= 0, bias_ref[:, :tq], NEG_BIG), 0.0)

    lam = (jnp.exp(jnp.sum(lq1_ref[...] * lk1_ref[...], axis=-1, keepdims=True))
           - jnp.exp(jnp.sum(lq2_ref[...] * lk2_ref[...], axis=-1, keepdims=True)) + lam_init)
    o = acc_ref[0] / l_ref[0] - lam * (acc_ref[1] / l_ref[1])
    o = o * lax.rsqrt(jnp.mean(o * o, axis=-1, keepdims=True) + EPS)
    o_ref[0] = ((o * sg_ref[...]) * (1.0 - lam_init)).astype(o_ref.dtype)


def diff_attention_core(proj, lq1, lk1, lq2, lk2, subln_g, layer_idx):
    b, s, _ = proj.shape
    nh, dh, dv = DA_HEADS, DA_HEAD_DIM, DA_V_DIM
    tq = min(512, s)
    lam_init = 0.8 - 0.6 * math.exp(-0.3 * layer_idx)
    vec = lambda n: pl.BlockSpec((1, n), lambda i, h, m: (0, 0))
    return pl.pallas_call(
        functools.partial(_diff_attn_kernel, tq=tq, lam_init=lam_init),
        grid=(b, nh, s // tq),
        in_specs=[
            pl.BlockSpec((1, tq, 2 * dh), lambda i, h, m: (i, m, h)),
            pl.BlockSpec((1, s, 2 * dh), lambda i, h, m: (i, 0, nh + h)),
            pl.BlockSpec((1, s, dv), lambda i, h, m: (i, 0, 2 * nh + h)),
            vec(dh), vec(dh), vec(dh), vec(dh), vec(dv),
        ],
        out_specs=pl.BlockSpec((1, tq, dv), lambda i, h, m: (i, m, h)),
        out_shape=jax.ShapeDtypeStruct((b, s, nh * dv), _bf16),
        scratch_shapes=[pltpu.VMEM((tq, 2 * tq), _f32), pltpu.VMEM((2, tq, dv), _f32),
                        pltpu.VMEM((2, tq, dv), _f32), pltpu.VMEM((2, tq, dv), _f32)],
        compiler_params=_cparams("parallel", "parallel", "parallel"),
        name="diff_attention",
    )(proj, proj, proj, lq1.reshape(1, dh), lk1.reshape(1, dh), lq2.reshape(1, dh),
      lk2.reshape(1, dh), subln_g.reshape(1, dv))


def _retention_kernel(q_ref, k_ref, v_ref, g_ref, gn_ref, o_ref, r_ref, *, chunk):
    h = pl.program_id(1)
    c = pl.program_id(2)
    dk = RET_DK

    @pl.when(c == 0)
    def _():
        r_ref[...] = jnp.zeros_like(r_ref)

    hf = jnp.full((1, 1), h, jnp.int32).astype(_f32)
    log_g = jnp.log(1.0 - jnp.exp2(-5.0 - hf))
    ri = lax.broadcasted_iota(jnp.int32, (chunk, chunk), 0)
    ci = lax.broadcasted_iota(jnp.int32, (chunk, chunk), 1)
    diff = (ri - ci).astype(_f32)
    decay = jnp.where(diff >= 0, jnp.exp(log_g * jnp.maximum(diff, 0.0)), 0.0)
    idx = lax.broadcasted_iota(jnp.int32, (chunk, 1), 0).astype(_f32)
    zeta = jnp.exp(log_g * (chunk - 1 - idx))
    xi = jnp.exp(log_g * (idx + 1.0))
    g_c = jnp.exp(log_g * chunk)

    decay = decay * (dk ** -0.5)
    zeta = zeta * (dk ** -0.5)
    for sub in range(q_ref.shape[1] // chunk):
        rows = pl.ds(sub * chunk, chunk)
        q = q_ref[0, rows, :]
        k = k_ref[0, rows, :]
        v = v_ref[0, rows, :]
        r = r_ref[...]
        inner = _dot_nt(q, k) * decay
        o = _dot(inner.astype(_bf16), v) + _dot(q, r.astype(_bf16)) * xi
        kz = (k.astype(_f32) * zeta).astype(_bf16)
        r_ref[...] = g_c * r + _dot_tn(kz, v)

        o = o * lax.rsqrt(jnp.mean(o * o, axis=-1, keepdims=True) + EPS) * gn_ref[...]
        o_ref[0, rows, :] = (_silu(g_ref[0, rows, :].astype(_f32)) * o).astype(o_ref.dtype)


def retention_core(proj, gn_g):
    b, s, _ = proj.shape
    nh, dk, dv = RET_HEADS, RET_DK, RET_DV
    chunk = RET_CHUNK
    rows = min(4 * chunk, s)
    return pl.pallas_call(
        functools.partial(_retention_kernel, chunk=chunk),
        grid=(b, nh, s // rows),
        in_specs=[
            pl.BlockSpec((1, rows, dk), lambda i, h, c: (i, c, h)),
            pl.BlockSpec((1, rows, dk), lambda i, h, c: (i, c, nh + h)),
            pl.BlockSpec((1, rows, dv), lambda i, h, c: (i, c, (2 * nh * dk) // dv + h)),
            pl.BlockSpec((1, rows, dv), lambda i, h, c: (i, c, (2 * nh * dk) // dv + nh + h)),
            pl.BlockSpec((1, dv), lambda i, h, c: (0, h)),
        ],
        out_specs=pl.BlockSpec((1, rows, dv), lambda i, h, c: (i, c, h)),
        out_shape=jax.ShapeDtypeStruct((b, s, nh * dv), _bf16),
        scratch_shapes=[pltpu.VMEM((dk, dv), _f32)],
        compiler_params=_cparams("parallel", "parallel", "arbitrary"),
        name="retention",
    )(proj, proj, proj, proj, gn_g.reshape(1, nh * dv))


def _pack_rows(h):
    n = h.shape[1] // 2
    lo = pltpu.bitcast(h[:, :n].astype(_bf16).astype(_f32), jnp.uint32)
    hi = pltpu.bitcast(h[:, n:].astype(_bf16).astype(_f32), jnp.uint32)
    return (lo >> 16) | (hi & jnp.uint32(0xFFFF0000))


def _unpack_rows(p):
    lo = pltpu.bitcast(p << 16, _f32).astype(_bf16)
    hi = pltpu.bitcast(p & jnp.uint32(0xFFFF0000), _f32).astype(_bf16)
    return lo, hi


def _first_argmax(v, iota, axis, big):
    m = jnp.max(v, axis=axis, keepdims=True)
    i = jnp.min(jnp.where(v == m, iota, big), axis=axis, keepdims=True)
    return m, i


def _router_kernel(x_ref, g_ref, sc_ref, sh_ref, rwt_ref, rb_ref, s1_ref, s3_ref, s2_ref,
                   hb_ref, idx_ref, wgt_ref, shared_ref):
    tm = x_ref.shape[1]
    h = _norm_mod(x_ref[0], g_ref[...], sc_ref[0], sh_ref[0])
    hb = h.astype(_bf16)
    hb_ref[...] = hb
    act = (_silu(_dot(hb, s1_ref[...])) * _dot(hb, s3_ref[...])).astype(_bf16)
    shared_ref[...] = _dot(act, s2_ref[...])

    scores = 1.0 / (1.0 + jnp.exp(-_dot_nt(rwt_ref[...], h)))
    choice = scores + rb_ref[...]
    c3 = choice.reshape(N_GROUPS, GROUP_SIZE, tm)
    mem = lax.broadcasted_iota(jnp.int32, c3.shape, 1)
    m1, i1 = _first_argmax(c3, mem, 1, GROUP_SIZE)
    m2 = jnp.max(jnp.where(mem == i1, -jnp.inf, c3), axis=1, keepdims=True)
    gs = m1 + m2
    gi = lax.broadcasted_iota(jnp.int32, gs.shape, 0)
    sel = jnp.zeros(gs.shape, _f32)
    for _ in range(TOPK_GROUPS):
        _, ig = _first_argmax(gs, gi, 0, N_GROUPS)
        hit = gi == ig
        sel = jnp.where(hit, 1.0, sel)
        gs = jnp.where(hit, -jnp.inf, gs)
    cm = jnp.where(sel > 0.0, c3, -jnp.inf).reshape(N_EXPERTS, tm)
    ei = lax.broadcasted_iota(jnp.int32, cm.shape, 0)
    idxs, wgts = [], []
    for _ in range(TOP_K):
        _, ie = _first_argmax(cm, ei, 0, N_EXPERTS)
        hit = ei == ie
        idxs.append(ie)
        wgts.append(jnp.sum(jnp.where(hit, scores, 0.0), axis=0, keepdims=True))
        cm = jnp.where(hit, -jnp.inf, cm)
    w = jnp.concatenate(wgts, axis=0)
    idx_ref[...] = jnp.concatenate(idxs, axis=0)
    wgt_ref[...] = w / jnp.sum(w, axis=0, keepdims=True) * ROUTED_SCALE


def moe_router(x, g, sc, sh, router_w, router_b, s_w1, s_w3, s_w2):
    b, s, d = x.shape
    t = b * s
    tm = min(256, s)
    n_s = s // tm
    sf = s_w1.shape[1]
    const = lambda shape: pl.BlockSpec(shape, lambda i, m: (0,) * len(shape))
    return pl.pallas_call(
        _router_kernel,
        grid=(b, n_s),
        in_specs=[
            pl.BlockSpec((1, tm, d), lambda i, m: (i, m, 0)),
            const((1, d)),
            pl.BlockSpec((1, 1, d), lambda i, m: (i, 0, 0)),
            pl.BlockSpec((1, 1, d), lambda i, m: (i, 0, 0)),
            const((N_EXPERTS, d)), const((N_EXPERTS, 1)),
            const((d, sf)), const((d, sf)), const((sf, d)),
        ],
        out_specs=[
            pl.BlockSpec((tm, d), lambda i, m: (i * n_s + m, 0)),
            pl.BlockSpec((TOP_K, tm), lambda i, m: (0, i * n_s + m)),
            pl.BlockSpec((TOP_K, tm), lambda i, m: (0, i * n_s + m)),
            pl.BlockSpec((tm, d), lambda i, m: (i * n_s + m, 0)),
        ],
        out_shape=[
            jax.ShapeDtypeStruct((t, d), _bf16),
            jax.ShapeDtypeStruct((TOP_K, t), jnp.int32),
            jax.ShapeDtypeStruct((TOP_K, t), _f32),
            jax.ShapeDtypeStruct((t, d), _f32),
        ],
        compiler_params=_cparams("parallel", "parallel"),
        name="moe_router",
    )(x, g.reshape(1, d), sc, sh, router_w.T, router_b.reshape(N_EXPERTS, 1),
      s_w1.astype(_bf16), s_w3.astype(_bf16), s_w2.astype(_bf16))


def _strict_lower(n):
    return jnp.where(lax.broadcasted_iota(jnp.int32, (n, n), 1)
                     < lax.broadcasted_iota(jnp.int32, (n, n), 0), 1.0, 0.0).astype(_bf16)


def _prefix_over_experts(col):
    wide = jnp.broadcast_to(col, (N_EXPERTS, 128)).astype(_bf16)
    return _dot(_strict_lower(N_EXPERTS), wide)[:, :1]


def _rank_kernel(idx_ref, piece_ref, tot_ref, base_ref):
    phase = pl.program_id(0)
    i = pl.program_id(1)
    tm = idx_ref.shape[1]

    idx = idx_ref[...]
    ei = lax.broadcasted_iota(jnp.int32, (N_EXPERTS, tm), 0)
    memf = jnp.zeros((N_EXPERTS, tm), _f32)
    for k in range(TOP_K):
        memf = jnp.where(ei == idx[k:k + 1, :], 1.0, memf)
    tile_cnt = jnp.sum(memf, axis=1, keepdims=True).astype(jnp.int32)
    seg_rows = ((tile_cnt + (SEG_ALIGN - 1)) & (-SEG_ALIGN)).astype(_f32)

    @pl.when(jnp.logical_and(phase == 0, i == 0))
    def _():
        base_ref[...] = jnp.zeros_like(base_ref)

    @pl.when(phase == 0)
    def _():
        base_ref[...] += seg_rows

    @pl.when(jnp.logical_and(phase == 1, i == 0))
    def _():
        tot = base_ref[...].astype(jnp.int32)
        tot_ref[...] = tot
        nblk = (tot + (ROW_BLOCK - 1)) >> ROW_BLOCK_LOG2
        start_blk = (32.0 * _prefix_over_experts((nblk >> 5).astype(_f32))
                     + _prefix_over_experts((nblk & 31).astype(_f32)))
        base_ref[...] = start_blk * float(ROW_BLOCK)

    @pl.when(phase == 1)
    def _():
        seg_i = seg_rows.astype(jnp.int32)
        lo = (16.0 * _prefix_over_experts((seg_i >> 4).astype(_f32))
              + _prefix_over_experts((seg_i & 15).astype(_f32)))
        lane = lax.broadcasted_iota(jnp.int32, (N_EXPERTS, PIECE_SLOTS), 1)
        row = (lane * SEG_ALIGN).astype(_f32)
        inside = jnp.logical_and(row >= lo, row < lo + seg_rows)
        dst = jnp.sum(jnp.where(inside, base_ref[...] + (row - lo), 0.0), axis=0, keepdims=True)
        total = jnp.sum(seg_rows, axis=0, keepdims=True)
        piece_ref[0] = jnp.where(lane[:1] == PIECE_SLOTS - 1, total, dst).astype(jnp.int32)
        base_ref[...] += seg_rows


def moe_rank(top_idx):
    k, t = top_idx.shape
    tm = TOKEN_TILE
    assert _tile_buffer_rows(tm) // SEG_ALIGN < PIECE_SLOTS
    return pl.pallas_call(
        _rank_kernel,
        grid=(2, t // tm),
        in_specs=[pl.BlockSpec((k, tm), lambda p, i: (0, i))],
        out_specs=[pl.BlockSpec((1, 1, PIECE_SLOTS), lambda p, i: (i * p, 0, 0)),
                   pl.BlockSpec((N_EXPERTS, 1), lambda p, i: (0, 0))],
        out_shape=[jax.ShapeDtypeStruct((t // tm, 1, PIECE_SLOTS), jnp.int32),
                   jax.ShapeDtypeStruct((N_EXPERTS, 1), jnp.int32)],
        scratch_shapes=[pltpu.VMEM((N_EXPERTS, 1), _f32)],
        compiler_params=_cparams("arbitrary", "arbitrary"),
        name="moe_rank",
    )(top_idx)


def _aligned_rows(ref, start, n):
    return ref.at[pl.ds(pl.multiple_of(start, SEG_ALIGN), n)]


def _tile_rows(piece_ref):
    return piece_ref[0, 0, PIECE_SLOTS - 1]


def _for_each_piece(piece_ref, fn):
    def piece(g, c):
        fn(g * SEG_ALIGN, piece_ref[0, 0, g])
        return c

    lax.fori_loop(0, _tile_rows(piece_ref) // SEG_ALIGN, piece, 0)


def _wait_pieces(n_pieces, make_copy):
    for bit in reversed(range(PIECE_SLOTS.bit_length())):
        @pl.when((n_pieces >> bit) & 1 == 1)
        def _(bit=bit):
            make_copy(1 << bit).wait()


def _prefix_over_lanes(row):
    n = row.shape[1]
    wide = jnp.broadcast_to(row, (SEG_ALIGN, n)).astype(_bf16)
    return _dot_nt(wide, _strict_lower(n))[:1, :]


def _tile_layout(idx_t):
    tm = idx_t.shape[0]
    lane_e = lax.broadcasted_iota(jnp.int32, (tm, N_EXPERTS), 1)
    member = jnp.zeros((tm, N_EXPERTS), _f32)
    for k in range(TOP_K):
        member = jnp.where(lane_e == idx_t[:, k:k + 1], 1.0, member)
    before = _dot(_strict_lower(tm), member.astype(_bf16))
    count = jnp.sum(member, axis=0, keepdims=True).astype(jnp.int32)
    seg_rows = (count + (SEG_ALIGN - 1)) & (-SEG_ALIGN)
    seg_lo = (16.0 * _prefix_over_lanes((seg_rows >> 4).astype(_f32))
              + _prefix_over_lanes((seg_rows & 15).astype(_f32)))
    return member, before, seg_lo, seg_lo + seg_rows.astype(_f32)


def _rows_of_chunk(r0, seg_lo, seg_hi):
    row = (lax.broadcasted_iota(jnp.int32, (PERM_ROWS, N_EXPERTS), 0) + r0).astype(_f32)
    inside = jnp.logical_and(row >= seg_lo, row < seg_hi)
    rank = jnp.sum(jnp.where(inside, row - seg_lo, 0.0), axis=1, keepdims=True)
    return jnp.where(inside, 1.0, 0.0).astype(_bf16), rank


def _dispatch_kernel(pad_from_ref, pad_n_ref, nb_ref, piece_ref, idx_ref, hb_ref, xs_ref,
                     xbuf_ref, zero_ref, sent_ref, sem, zsem):
    tm = hb_ref.shape[0]
    n_blocks = xs_ref.shape[0] // ROW_BLOCK

    @pl.when(pl.program_id(0) == 0)
    def _():
        zero_ref[...] = jnp.zeros_like(zero_ref)

        def pad_expert(e, carry):
            off = pad_from_ref[e]

            def pad_piece(r, c):
                pltpu.make_async_copy(_aligned_rows(zero_ref, 0, SEG_ALIGN),
                                      _aligned_rows(xs_ref, off + SEG_ALIGN * r, SEG_ALIGN),
                                      zsem).start()
                return c + 1

            return lax.fori_loop(0, pad_n_ref[e] // SEG_ALIGN, pad_piece, carry)

        n_pieces = lax.fori_loop(0, N_EXPERTS, pad_expert, 0)

        def pad_tail(b, carry):
            pltpu.make_async_copy(zero_ref, xs_ref.at[pl.ds(b * ROW_BLOCK, ROW_BLOCK)],
                                  zsem).start()
            return carry

        lax.fori_loop(nb_ref[0], n_blocks, pad_tail, 0)

        def wait_piece(r, c):
            pltpu.make_async_copy(_aligned_rows(zero_ref, 0, SEG_ALIGN),
                                  _aligned_rows(xs_ref, 0, SEG_ALIGN), zsem).wait()
            return c

        def wait_tail(b, c):
            pltpu.make_async_copy(zero_ref, xs_ref.at[pl.ds(0, ROW_BLOCK)], zsem).wait()
            return c

        lax.fori_loop(0, n_pieces, wait_piece, 0)
        lax.fori_loop(nb_ref[0], n_blocks, wait_tail, 0)

    step = pl.program_id(0)
    slot = step % 2
    total = _tile_rows(piece_ref)
    member, before, seg_lo, seg_hi = _tile_layout(idx_ref[...])
    member_b, before_b = member.astype(_bf16), before.astype(_bf16)

    def permute(c, carry):
        r0 = pl.multiple_of(c * PERM_ROWS, PERM_ROWS)
        expert_of_row, rank_of_row = _rows_of_chunk(r0, seg_lo, seg_hi)
        hit = _dot_nt(expert_of_row, before_b) == rank_of_row
        sel = jnp.where(hit, _dot_nt(expert_of_row, member_b), 0.0)
        picked = _dot(sel.astype(_bf16), hb_ref[...])
        half = picked.shape[1] // 2
        xbuf_ref[slot, pl.ds(r0, PERM_ROWS), :] = (
            (pltpu.bitcast(picked[:, :half], jnp.uint32) >> 16)
            | pltpu.bitcast(picked[:, half:], jnp.uint32))
        return carry

    lax.fori_loop(0, (total + PERM_ROWS - 1) // PERM_ROWS, permute, 0)

    wait_sent = lambda n_pieces: _wait_pieces(n_pieces, lambda k: pltpu.make_async_copy(
        xbuf_ref.at[slot, pl.ds(0, k * SEG_ALIGN)], xs_ref.at[pl.ds(0, k * SEG_ALIGN)], sem))

    @pl.when(step > 0)
    def _():
        wait_sent(sent_ref[0])

    def send(off, dst):
        pltpu.make_async_copy(_aligned_rows(xbuf_ref.at[slot], off, SEG_ALIGN),
                              _aligned_rows(xs_ref, dst, SEG_ALIGN), sem).start()

    _for_each_piece(piece_ref, send)
    sent_ref[0] = total // SEG_ALIGN

    @pl.when(step == pl.num_programs(0) - 1)
    def _():
        wait_sent(total // SEG_ALIGN)


def _tile_buffer_rows(tm):
    worst = tm * TOP_K + N_EXPERTS * (SEG_ALIGN - 1)
    return -(-worst // PERM_ROWS) * PERM_ROWS


def moe_dispatch(pieces, idx_t, hb, pad_from, pad_n, n_used, n_blocks):
    t, d = hb.shape
    tm = TOKEN_TILE
    grid_spec = pltpu.PrefetchScalarGridSpec(
        num_scalar_prefetch=3,
        grid=(t // tm,),
        in_specs=[
            pl.BlockSpec((1, 1, PIECE_SLOTS), lambda i, *_: (i, 0, 0), memory_space=pltpu.SMEM),
            pl.BlockSpec((tm, TOP_K), lambda i, *_: (i, 0)),
            pl.BlockSpec((tm, d), lambda i, *_: (i, 0)),
        ],
        out_specs=pl.BlockSpec(memory_space=pl.ANY),
        scratch_shapes=[pltpu.VMEM((2, _tile_buffer_rows(tm), d // 2), jnp.uint32),
                        pltpu.VMEM((ROW_BLOCK, d // 2), jnp.uint32),
                        pltpu.SMEM((1,), jnp.int32),
                        pltpu.SemaphoreType.DMA(()), pltpu.SemaphoreType.DMA(())],
    )
    return pl.pallas_call(
        _dispatch_kernel,
        grid_spec=grid_spec,
        out_shape=jax.ShapeDtypeStruct((n_blocks * ROW_BLOCK, d // 2), jnp.uint32),
        compiler_params=_cparams("arbitrary"),
        name="moe_dispatch",
    )(pad_from, pad_n, n_used, pieces, idx_t, hb)


def _expert_kernel(be_ref, nb_ref, xs_ref, w1_ref, w3_ref, w2_ref, y_ref, w1b, w3b, w2b):
    i = pl.program_id(0)

    @pl.when(jnp.logical_or(i == 0, be_ref[i] != be_ref[jnp.maximum(i - 1, 0)]))
    def _():
        w1b[...] = w1_ref[0, 0].astype(_bf16)
        w3b[...] = w3_ref[0, 0].astype(_bf16)
        w2b[...] = w2_ref[0, 0].astype(_bf16)

    @pl.when(i < nb_ref[0])
    def _():
        chains = [pl.ds(c * EXPERT_ROWS, EXPERT_ROWS) for c in range(ROW_BLOCK // EXPERT_ROWS)]
        xs = [jnp.concatenate(_unpack_rows(xs_ref[rows, :]), axis=1) for rows in chains]
        acts = [(_silu(_dot(x, w1b[...])) * _dot(x, w3b[...])).astype(_bf16) for x in xs]
        for rows, act in zip(chains, acts):
            y_ref[rows, :] = _pack_rows(_dot(act, w2b[...]))

    @pl.when(i >= nb_ref[0])
    def _():
        y_ref[...] = jnp.zeros_like(y_ref)


def moe_experts(xs, blk_e, n_used, w1, w3, w2, layer):
    p, half = xs.shape
    d = 2 * half
    ff = w1.shape[3]
    nb = p // ROW_BLOCK
    row = lambda i, be, nu: (jnp.minimum(i, nu[0] - 1), 0)
    wsel = lambda i, be, nu: (layer, be[i], 0, 0)
    grid_spec = pltpu.PrefetchScalarGridSpec(
        num_scalar_prefetch=2,
        grid=(nb,),
        in_specs=[
            pl.BlockSpec((ROW_BLOCK, half), row),
            pl.BlockSpec((1, 1, d, ff), wsel),
            pl.BlockSpec((1, 1, d, ff), wsel),
            pl.BlockSpec((1, 1, ff, d), wsel),
        ],
        out_specs=pl.BlockSpec((ROW_BLOCK, half), lambda i, be, nu: (i, 0)),
        scratch_shapes=[pltpu.VMEM((d, ff), _bf16), pltpu.VMEM((d, ff), _bf16),
                        pltpu.VMEM((ff, d), _bf16)],
    )
    return pl.pallas_call(
        _expert_kernel,
        grid_spec=grid_spec,
        out_shape=jax.ShapeDtypeStruct((p, half), jnp.uint32),
        compiler_params=_cparams("arbitrary"),
        name="moe_experts",
    )(blk_e, n_used, xs, w1, w3, w2)


def _combine_kernel(piece_ref, next_piece_ref, ys_ref, idx_ref, w_ref, shared_ref, x_ref, gate_ref,
                    fn_ref, o_ref, ybuf_ref, acc_ref, sem, *, final_norm):
    tm = x_ref.shape[1]
    half = ybuf_ref.shape[2]
    step = pl.program_id(0) * pl.num_programs(1) + pl.program_id(1)
    n_steps = pl.num_programs(0) * pl.num_programs(1)
    slot = step % 2

    def fetch(table_ref, s):
        def one(off, src):
            pltpu.make_async_copy(_aligned_rows(ys_ref, src, SEG_ALIGN),
                                  _aligned_rows(ybuf_ref.at[s], off, SEG_ALIGN), sem.at[s]).start()

        _for_each_piece(table_ref, one)

    @pl.when(step == 0)
    def _():
        fetch(piece_ref, slot)

    @pl.when(step + 1 < n_steps)
    def _():
        fetch(next_piece_ref, 1 - slot)

    total = _tile_rows(piece_ref)
    _wait_pieces(total // SEG_ALIGN, lambda k: pltpu.make_async_copy(
        ys_ref.at[pl.ds(0, k * SEG_ALIGN)], ybuf_ref.at[slot, pl.ds(0, k * SEG_ALIGN)],
        sem.at[slot]))

    idx_t = idx_ref[...]
    member, before, seg_lo, seg_hi = _tile_layout(idx_t)
    before_b = before.astype(_bf16)
    lane_e = lax.broadcasted_iota(jnp.int32, member.shape, 1)
    w_of_expert = jnp.zeros(member.shape, _f32)
    for k in range(TOP_K):
        w_of_expert = jnp.where(lane_e == idx_t[:, k:k + 1], w_ref[:, k:k + 1], w_of_expert)
    split = lambda v: (v.astype(_bf16), (v - v.astype(_bf16).astype(_f32)).astype(_bf16))
    we_head, we_rest = split(w_of_expert)
    lo8 = jnp.broadcast_to(seg_lo, (SEG_ALIGN, N_EXPERTS)).astype(jnp.int32) >> 3
    lo_head, lo_rest = (lo8 >> 4).astype(_f32).astype(_bf16), (lo8 & 15).astype(_f32).astype(_bf16)
    acc_ref[...] = shared_ref[...]

    def reduce(c, carry):
        r0 = pl.multiple_of(c * PERM_ROWS, PERM_ROWS)
        packed = ybuf_ref[slot, pl.ds(r0, PERM_ROWS), :]
        row = lax.broadcasted_iota(jnp.int32, packed.shape, 0) + r0
        packed = jnp.where(row < total, packed, jnp.uint32(0))
        y_lo, y_hi = _unpack_rows(packed)
        expert_of_row, rank_of_row = _rows_of_chunk(r0, seg_lo, seg_hi)
        del rank_of_row
        start = (128.0 * _dot_nt(lo_head, expert_of_row) + 8.0 * _dot_nt(lo_rest, expert_of_row))
        rank_row = (lax.broadcasted_iota(jnp.int32, (1, PERM_ROWS), 1) + r0).astype(_f32) - start[:1]
        hit = _dot_nt(before_b, expert_of_row) == rank_row
        wm = jnp.where(hit, _dot_nt(we_head, expert_of_row) + _dot_nt(we_rest, expert_of_row), 0.0)
        w_head, w_rest = split(wm)
        acc_ref[:, :half] += _dot(w_head, y_lo) + _dot(w_rest, y_lo)
        acc_ref[:, half:] += _dot(w_head, y_hi) + _dot(w_rest, y_hi)
        return carry

    lax.fori_loop(0, (total + PERM_ROWS - 1) // PERM_ROWS, reduce, 0)

    xn = x_ref[0] + gate_ref[0] * acc_ref[...]
    if final_norm:
        xn = xn * lax.rsqrt(jnp.mean(xn * xn, axis=-1, keepdims=True) + EPS) * fn_ref[...]
    o_ref[0] = xn


def moe_combine(pieces, idx_t, ys, top_w, shared, x, gate, fn_g, final_norm):
    b, s, d = x.shape
    tm = TOKEN_TILE
    n_s = s // tm
    tile = lambda i, m: (i * n_s + m, 0)
    last_tile = b * n_s - 1
    return pl.pallas_call(
        functools.partial(_combine_kernel, final_norm=final_norm),
        grid=(b, n_s),
        in_specs=[
            pl.BlockSpec((1, 1, PIECE_SLOTS), lambda i, m: (i * n_s + m, 0, 0),
                         memory_space=pltpu.SMEM),
            pl.BlockSpec((1, 1, PIECE_SLOTS),
                         lambda i, m: (jnp.minimum(i * n_s + m + 1, last_tile), 0, 0),
                         memory_space=pltpu.SMEM),
            pl.BlockSpec(memory_space=pl.ANY),
            pl.BlockSpec((tm, TOP_K), tile),
            pl.BlockSpec((tm, TOP_K), tile),
            pl.BlockSpec((tm, d), tile),
            pl.BlockSpec((1, tm, d), lambda i, m: (i, m, 0)),
            pl.BlockSpec((1, 1, d), lambda i, m: (i, 0, 0)),
            pl.BlockSpec((1, d), lambda i, m: (0, 0)),
        ],
        out_specs=pl.BlockSpec((1, tm, d), lambda i, m: (i, m, 0)),
        out_shape=jax.ShapeDtypeStruct((b, s, d), _f32),
        scratch_shapes=[pltpu.VMEM((2, _tile_buffer_rows(tm), d // 2), jnp.uint32),
                        pltpu.VMEM((tm, d), _f32), pltpu.SemaphoreType.DMA((2,))],
        compiler_params=_cparams("arbitrary", "arbitrary"),
        name="moe_combine",
    )(pieces, pieces, ys, idx_t, top_w, shared, x, gate, fn_g.reshape(1, d))


def moe_layer(x, g, sc, sh, gate, router_w, router_b, w1, w3, w2, layer, s_w1, s_w3, s_w2,
              fn_g, final_norm):
    b, s, d = x.shape
    t = b * s
    hb, top_idx, top_w, shared = moe_router(x, g, sc, sh, router_w, router_b, s_w1, s_w3, s_w2)
    pieces, rows = moe_rank(top_idx)
    idx_t = top_idx.T
    n_tiles = pieces.shape[0]

    rows = rows.reshape(N_EXPERTS)
    nblk = (rows + ROW_BLOCK - 1) // ROW_BLOCK
    blk_ends = jnp.cumsum(nblk)
    max_rows = t * TOP_K + n_tiles * N_EXPERTS * (SEG_ALIGN - 1)
    n_blocks = -(-max_rows // ROW_BLOCK) + N_EXPERTS
    blk = jnp.arange(n_blocks, dtype=jnp.int32)
    blk_e = jnp.minimum(jnp.sum(blk[:, None] >= blk_ends[None, :], axis=1),
                        N_EXPERTS - 1).astype(jnp.int32)
    n_used = blk_ends[-1:].astype(jnp.int32)
    pad_from = ((blk_ends - nblk) * ROW_BLOCK + rows).astype(jnp.int32)
    pad_n = (nblk * ROW_BLOCK - rows).astype(jnp.int32)

    xs = moe_dispatch(pieces, idx_t, hb, pad_from, pad_n, n_used, n_blocks)
    ys = moe_experts(xs, blk_e, n_used, w1, w3, w2, layer)
    return moe_combine(pieces, idx_t, ys, top_w.T, shared, x, gate, fn_g, final_norm)


def kernel(x, c, norm_mix, norm_ffn, ada_w, ada_b, da_w_in, da_w_out, da_lq1, da_lk1, da_lq2,
           da_lk2, da_subln, ret_w_in, ret_w_out, ret_gn, router_w, router_b, exp_w1, exp_w3,
           exp_w2, sh_w1, sh_w3, sh_w2, final_norm):
    depth, d = norm_mix.shape
    b = x.shape[0]
    mod = ada_modulation(c, ada_w, ada_b).reshape(depth, b, 6, 1, d)
    for i in range(depth):
        sh_a, sc_a, g_a, sh_f, sc_f, g_f = (mod[i, :, j] for j in range(6))
        j = i // 2
        if i % 2 == 0:
            proj = norm_proj(x, norm_mix[i], sc_a, sh_a, da_w_in[j].astype(_bf16))
            y = diff_attention_core(proj, da_lq1[j], da_lk1[j], da_lq2[j], da_lk2[j],
                                    da_subln[j], i)
            x = out_proj_residual(y, da_w_out[j].astype(_bf16), x, g_a)
        else:
            proj = norm_proj(x, norm_mix[i], sc_a, sh_a, ret_w_in[j].astype(_bf16))
            y = retention_core(proj, ret_gn[j])
            x = out_proj_residual(y, ret_w_out[j].astype(_bf16), x, g_a)
        x = moe_layer(x, norm_ffn[i], sc_f, sh_f, g_f, router_w[i], router_b[i],
                      exp_w1, exp_w3, exp_w2, i,
                      sh_w1[i], sh_w3[i], sh_w2[i], final_norm, i == depth - 1)
    return x
```

```python
import functools
import math

import jax
import jax.numpy as jnp
from jax import lax
from jax.experimental import pallas as pl
from jax.experimental.pallas import tpu as pltpu

EPS = 1e-6
DA_HEADS = 8
DA_HEAD_DIM = 64
DA_V_DIM = 2 * DA_HEAD_DIM
RET_HEADS = 4
RET_DK = 256
RET_DV = 512
RET_CHUNK = 128
N_EXPERTS = 64
TOP_K = 8
N_GROUPS = 8
GROUP_SIZE = N_EXPERTS // N_GROUPS
TOPK_GROUPS = 4
ROUTED_SCALE = 2.5
ROW_BLOCK_LOG2 = 9
ROW_BLOCK = 1 << ROW_BLOCK_LOG2
EXPERT_ROWS = 256
TOKEN_TILE = 256
SEG_ALIGN = 8
PERM_ROWS = 512
PIECE_SLOTS = 384

VMEM_LIMIT = 56 * 1024 * 1024
NEG_BIG = -1e30

_f32 = jnp.float32
_bf16 = jnp.bfloat16


def _cparams(*sem):
    return pltpu.CompilerParams(dimension_semantics=sem, vmem_limit_bytes=VMEM_LIMIT)


def _silu(v):
    return v * (1.0 / (1.0 + jnp.exp(-v)))


def _dot(a, b):
    return jnp.dot(a, b, preferred_element_type=_f32)


def _dot_nt(a, b):
    return lax.dot_general(a, b, (((1,), (1,)), ((), ())), preferred_element_type=_f32)


def _dot_tn(a, b):
    return lax.dot_general(a, b, (((0,), (0,)), ((), ())), preferred_element_type=_f32)


def _ada_kernel(c_ref, w_ref, b_ref, o_ref):
    a = _silu(c_ref[...]).astype(_bf16)
    o_ref[0] = _dot(a, w_ref[0].astype(_bf16)) + b_ref[0]


def ada_modulation(c, ada_w, ada_b):
    depth, d, n = ada_w.shape
    b = c.shape[0]
    tn = 1024
    return pl.pallas_call(
        _ada_kernel,
        grid=(depth, n // tn),
        in_specs=[
            pl.BlockSpec((b, d), lambda l, j: (0, 0)),
            pl.BlockSpec((1, d, tn), lambda l, j: (l, 0, j)),
            pl.BlockSpec((1, 1, tn), lambda l, j: (l, 0, j)),
        ],
        out_specs=pl.BlockSpec((1, b, tn), lambda l, j: (l, 0, j)),
        out_shape=jax.ShapeDtypeStruct((depth, b, n), _f32),
        compiler_params=_cparams("parallel", "parallel"),
        name="ada_modulation",
    )(c, ada_w, ada_b.reshape(depth, 1, n))


def _norm_mod(x, g, sc, sh):
    y = x * lax.rsqrt(jnp.mean(x * x, axis=-1, keepdims=True) + EPS)
    return (y * g) * (1.0 + sc) + sh


def _norm_proj_kernel(x_ref, g_ref, sc_ref, sh_ref, w_ref, o_ref, h_ref):
    @pl.when(pl.program_id(2) == 0)
    def _():
        h_ref[...] = _norm_mod(x_ref[0], g_ref[...], sc_ref[0], sh_ref[0]).astype(_bf16)

    o_ref[0] = _dot(h_ref[...], w_ref[...]).astype(o_ref.dtype)


def norm_proj(x, g, sc, sh, w):
    b, s, d = x.shape
    n = w.shape[1]
    tm = min(1024, s)
    tn = 1024
    return pl.pallas_call(
        _norm_proj_kernel,
        grid=(b, s // tm, n // tn),
        in_specs=[
            pl.BlockSpec((1, tm, d), lambda i, m, j: (i, m, 0)),
            pl.BlockSpec((1, d), lambda i, m, j: (0, 0)),
            pl.BlockSpec((1, 1, d), lambda i, m, j: (i, 0, 0)),
            pl.BlockSpec((1, 1, d), lambda i, m, j: (i, 0, 0)),
            pl.BlockSpec((d, tn), lambda i, m, j: (0, j)),
        ],
        out_specs=pl.BlockSpec((1, tm, tn), lambda i, m, j: (i, m, j)),
        out_shape=jax.ShapeDtypeStruct((b, s, n), _bf16),
        scratch_shapes=[pltpu.VMEM((tm, d), _bf16)],
        compiler_params=_cparams("parallel", "parallel", "arbitrary"),
        name="norm_proj",
    )(x, g.reshape(1, d), sc, sh, w)


def _out_proj_kernel(y_ref, w_ref, x_ref, g_ref, o_ref):
    o_ref[0] = x_ref[0] + g_ref[0] * _dot(y_ref[0], w_ref[...])


def out_proj_residual(y, w, x, gate):
    b, s, k = y.shape
    d = w.shape[1]
    tm = min(512, s)
    return pl.pallas_call(
        _out_proj_kernel,
        grid=(b, s // tm),
        in_specs=[
            pl.BlockSpec((1, tm, k), lambda i, m: (i, m, 0)),
            pl.BlockSpec((k, d), lambda i, m: (0, 0)),
            pl.BlockSpec((1, tm, d), lambda i, m: (i, m, 0)),
            pl.BlockSpec((1, 1, d), lambda i, m: (i, 0, 0)),
        ],
        out_specs=pl.BlockSpec((1, tm, d), lambda i, m: (i, m, 0)),
        out_shape=jax.ShapeDtypeStruct((b, s, d), _f32),
        compiler_params=_cparams("parallel", "parallel"),
        name="out_proj_residual",
    )(y, w, x, gate)


def _diff_attn_kernel(q_ref, k_ref, v_ref, lq1_ref, lk1_ref, lq2_ref, lk2_ref, sg_ref, o_ref,
                      bias_ref, m_ref, l_ref, acc_ref, *, tq, lam_init):
    h = pl.program_id(1)
    qi = pl.program_id(2)
    dh, dv = DA_HEAD_DIM, DA_V_DIM

    q = q_ref[0] * (dh ** -0.5)
    lane = lax.broadcasted_iota(jnp.int32, q.shape, 1)
    zero = jnp.zeros_like(q)
    q_maps = (jnp.where(lane < dh, q, zero), jnp.where(lane >= dh, q, zero))

    hp1 = jnp.full((1, 1), h + 1, jnp.int32).astype(_f32)
    slope = jnp.exp2(hp1 * (-8.0 / DA_HEADS))
    rel = (lax.broadcasted_iota(jnp.int32, (tq, 2 * tq), 0)
           - lax.broadcasted_iota(jnp.int32, (tq, 2 * tq), 1)).astype(_f32)
    bias_ref[...] = -slope * rel

    m_ref[...] = jnp.full(m_ref.shape, NEG_BIG, _f32)
    l_ref[...] = jnp.zeros(l_ref.shape, _f32)
    acc_ref[...] = jnp.zeros(acc_ref.shape, _f32)

    def span(j, width, bias, shift):
        start = pl.multiple_of(j * tq, tq)
        kj = k_ref[0, pl.ds(start, width), :]
        vj = v_ref[0, pl.ds(start, width), :]
        for mi in range(2):
            s = _dot_nt(q_maps[mi], kj) + bias
            m_prev = m_ref[mi]
            m_new = jnp.maximum(m_prev, jnp.max(s, axis=-1, keepdims=True) + shift)
            alpha = jnp.exp(m_prev - m_new)
            m_loc = m_new - shift
            p = jnp.exp(s - jnp.concatenate([m_loc] * (width // 128), axis=1))
            l_ref[mi] = alpha * l_ref[mi] + jnp.sum(p, axis=-1, keepdims=True)
            acc_ref[mi] = alpha * acc_ref[mi] + _dot(p.astype(_bf16), vj)
            m_ref[mi] = m_new

    def body(jj, carry):
        shift = -slope * jnp.full((1, 1), (qi - 2 * jj) * tq, jnp.int32).astype(_f32)
        span(2 * jj, 2 * tq, bias_ref[...], shift)
        return carry

    lax.fori_loop(0, qi // 2, body, 0)

    @pl.when(qi % 2 == 1)
    def _():
        span(qi - 1, 2 * tq, jnp.where(rel + tq >= 0, bias_ref[...], NEG_BIG), -slope * tq)

    @pl.when(qi % 2 == 0)
    def _():
        span(qi, tq, jnp.where(rel[:, :tq] >= 0, bias_ref[:, :tq], NEG_BIG), 0.0)

    lam = (jnp.exp(jnp.sum(lq1_ref[...] * lk1_ref[...], axis=-1, keepdims=True))
           - jnp.exp(jnp.sum(lq2_ref[...] * lk2_ref[...], axis=-1, keepdims=True)) + lam_init)
    o = acc_ref[0] / l_ref[0] - lam * (acc_ref[1] / l_ref[1])
    o = o * lax.rsqrt(jnp.mean(o * o, axis=-1, keepdims=True) + EPS)
    o_ref[0] = ((o * sg_ref[...]) * (1.0 - lam_init)).astype(o_ref.dtype)


def diff_attention_core(proj, lq1, lk1, lq2, lk2, subln_g, layer_idx):
    b, s, _ = proj.shape
    nh, dh, dv = DA_HEADS, DA_HEAD_DIM, DA_V_DIM
    tq = min(512, s)
    lam_init = 0.8 - 0.6 * math.exp(-0.3 * layer_idx)
    vec = lambda n: pl.BlockSpec((1, n), lambda i, h, m: (0, 0))
    return pl.pallas_call(
        functools.partial(_diff_attn_kernel, tq=tq, lam_init=lam_init),
        grid=(b, nh, s // tq),
        in_specs=[
            pl.BlockSpec((1, tq, 2 * dh), lambda i, h, m: (i, m, h)),
            pl.BlockSpec((1, s, 2 * dh), lambda i, h, m: (i, 0, nh + h)),
            pl.BlockSpec((1, s, dv), lambda i, h, m: (i, 0, 2 * nh + h)),
            vec(dh), vec(dh), vec(dh), vec(dh), vec(dv),
        ],
        out_specs=pl.BlockSpec((1, tq, dv), lambda i, h, m: (i, m, h)),
        out_shape=jax.ShapeDtypeStruct((b, s, nh * dv), _bf16),
        scratch_shapes=[pltpu.VMEM((tq, 2 * tq), _f32), pltpu.VMEM((2, tq, dv), _f32),
                        pltpu.VMEM((2, tq, dv), _f32), pltpu.VMEM((2, tq, dv), _f32)],
        compiler_params=_cparams("parallel", "parallel", "parallel"),
        name="diff_attention",
    )(proj, proj, proj, lq1.reshape(1, dh), lk1.reshape(1, dh), lq2.reshape(1, dh),
      lk2.reshape(1, dh), subln_g.reshape(1, dv))


def _retention_kernel(q_ref, k_ref, v_ref, g_ref, gn_ref, o_ref, r_ref, *, chunk):
    h = pl.program_id(1)
    c = pl.program_id(2)
    dk = RET_DK

    @pl.when(c == 0)
    def _():
        r_ref[...] = jnp.zeros_like(r_ref)

    hf = jnp.full((1, 1), h, jnp.int32).astype(_f32)
    log_g = jnp.log(1.0 - jnp.exp2(-5.0 - hf))
    ri = lax.broadcasted_iota(jnp.int32, (chunk, chunk), 0)
    ci = lax.broadcasted_iota(jnp.int32, (chunk, chunk), 1)
    diff = (ri - ci).astype(_f32)
    decay = jnp.where(diff >= 0, jnp.exp(log_g * jnp.maximum(diff, 0.0)), 0.0)
    idx = lax.broadcasted_iota(jnp.int32, (chunk, 1), 0).astype(_f32)
    zeta = jnp.exp(log_g * (chunk - 1 - idx))
    xi = jnp.exp(log_g * (idx + 1.0))
    g_c = jnp.exp(log_g * chunk)

    decay = decay * (dk ** -0.5)
    zeta = zeta * (dk ** -0.5)
    for sub in range(q_ref.shape[1] // chunk):
        rows = pl.ds(sub * chunk, chunk)
        q = q_ref[0, rows, :]
        k = k_ref[0, rows, :]
        v = v_ref[0, rows, :]
        r = r_ref[...]
        inner = _dot_nt(q, k) * decay
        o = _dot(inner.astype(_bf16), v) + _dot(q, r.astype(_bf16)) * xi
        kz = (k.astype(_f32) * zeta).astype(_bf16)
        r_ref[...] = g_c * r + _dot_tn(kz, v)

        o = o * lax.rsqrt(jnp.mean(o * o, axis=-1, keepdims=True) + EPS) * gn_ref[...]
        o_ref[0, rows, :] = (_silu(g_ref[0, rows, :].astype(_f32)) * o).astype(o_ref.dtype)


def retention_core(proj, gn_g):
    b, s, _ = proj.shape
    nh, dk, dv = RET_HEADS, RET_DK, RET_DV
    chunk = RET_CHUNK
    rows = min(4 * chunk, s)
    return pl.pallas_call(
        functools.partial(_retention_kernel, chunk=chunk),
        grid=(b, nh, s // rows),
        in_specs=[
            pl.BlockSpec((1, rows, dk), lambda i, h, c: (i, c, h)),
            pl.BlockSpec((1, rows, dk), lambda i, h, c: (i, c, nh + h)),
            pl.BlockSpec((1, rows, dv), lambda i, h, c: (i, c, (2 * nh * dk) // dv + h)),
            pl.BlockSpec((1, rows, dv), lambda i, h, c: (i, c, (2 * nh * dk) // dv + nh + h)),
            pl.BlockSpec((1, dv), lambda i, h, c: (0, h)),
        ],
        out_specs=pl.BlockSpec((1, rows, dv), lambda i, h, c: (i, c, h)),
        out_shape=jax.ShapeDtypeStruct((b, s, nh * dv), _bf16),
        scratch_shapes=[pltpu.VMEM((dk, dv), _f32)],
        compiler_params=_cparams("parallel", "parallel", "arbitrary"),
        name="retention",
    )(proj, proj, proj, proj, gn_g.reshape(1, nh * dv))


def _pack_rows(h):
    n = h.shape[1] // 2
    lo = pltpu.bitcast(h[:, :n].astype(_bf16).astype(_f32), jnp.uint32)
    hi = pltpu.bitcast(h[:, n:].astype(_bf16).astype(_f32), jnp.uint32)
    return (lo >> 16) | (hi & jnp.uint32(0xFFFF0000))


def _unpack_rows(p):
    lo = pltpu.bitcast(p << 16, _f32).astype(_bf16)
    hi = pltpu.bitcast(p & jnp.uint32(0xFFFF0000), _f32).astype(_bf16)
    return lo, hi


def _first_argmax(v, iota, axis, big):
    m = jnp.max(v, axis=axis, keepdims=True)
    i = jnp.min(jnp.where(v == m, iota, big), axis=axis, keepdims=True)
    return m, i


def _router_kernel(x_ref, g_ref, sc_ref, sh_ref, rwt_ref, rb_ref, s1_ref, s3_ref, s2_ref,
                   hb_ref, idx_ref, wgt_ref, shared_ref):
    tm = x_ref.shape[1]
    h = _norm_mod(x_ref[0], g_ref[...], sc_ref[0], sh_ref[0])
    hb = h.astype(_bf16)
    hb_ref[...] = hb
    act = (_silu(_dot(hb, s1_ref[...])) * _dot(hb, s3_ref[...])).astype(_bf16)
    shared_ref[...] = _dot(act, s2_ref[...])

    scores = 1.0 / (1.0 + jnp.exp(-_dot_nt(rwt_ref[...], h)))
    choice = scores + rb_ref[...]
    c3 = choice.reshape(N_GROUPS, GROUP_SIZE, tm)
    mem = lax.broadcasted_iota(jnp.int32, c3.shape, 1)
    m1, i1 = _first_argmax(c3, mem, 1, GROUP_SIZE)
    m2 = jnp.max(jnp.where(mem == i1, -jnp.inf, c3), axis=1, keepdims=True)
    gs = m1 + m2
    gi = lax.broadcasted_iota(jnp.int32, gs.shape, 0)
    sel = jnp.zeros(gs.shape, _f32)
    for _ in range(TOPK_GROUPS):
        _, ig = _first_argmax(gs, gi, 0, N_GROUPS)
        hit = gi == ig
        sel = jnp.where(hit, 1.0, sel)
        gs = jnp.where(hit, -jnp.inf, gs)
    cm = jnp.where(sel > 0.0, c3, -jnp.inf).reshape(N_EXPERTS, tm)
    ei = lax.broadcasted_iota(jnp.int32, cm.shape, 0)
    idxs, wgts = [], []
    for _ in range(TOP_K):
        _, ie = _first_argmax(cm, ei, 0, N_EXPERTS)
        hit = ei == ie
        idxs.append(ie)
        wgts.append(jnp.sum(jnp.where(hit, scores, 0.0), axis=0, keepdims=True))
        cm = jnp.where(hit, -jnp.inf, cm)
    w = jnp.concatenate(wgts, axis=0)
    idx_ref[...] = jnp.concatenate(idxs, axis=0)
    wgt_ref[...] = w / jnp.sum(w, axis=0, keepdims=True) * ROUTED_SCALE


def moe_router(x, g, sc, sh, router_w, router_b, s_w1, s_w3, s_w2):
    b, s, d = x.shape
    t = b * s
    tm = min(256, s)
    n_s = s // tm
    sf = s_w1.shape[1]
    const = lambda shape: pl.BlockSpec(shape, lambda i, m: (0,) * len(shape))
    return pl.pallas_call(
        _router_kernel,
        grid=(b, n_s),
        in_specs=[
            pl.BlockSpec((1, tm, d), lambda i, m: (i, m, 0)),
            const((1, d)),
            pl.BlockSpec((1, 1, d), lambda i, m: (i, 0, 0)),
            pl.BlockSpec((1, 1, d), lambda i, m: (i, 0, 0)),
            const((N_EXPERTS, d)), const((N_EXPERTS, 1)),
            const((d, sf)), const((d, sf)), const((sf, d)),
        ],
        out_specs=[
            pl.BlockSpec((tm, d), lambda i, m: (i * n_s + m, 0)),
            pl.BlockSpec((TOP_K, tm), lambda i, m: (0, i * n_s + m)),
            pl.BlockSpec((TOP_K, tm), lambda i, m: (0, i * n_s + m)),
            pl.BlockSpec((tm, d), lambda i, m: (i * n_s + m, 0)),
        ],
        out_shape=[
            jax.ShapeDtypeStruct((t, d), _bf16),
            jax.ShapeDtypeStruct((TOP_K, t), jnp.int32),
            jax.ShapeDtypeStruct((TOP_K, t), _f32),
            jax.ShapeDtypeStruct((t, d), _f32),
        ],
        compiler_params=_cparams("parallel", "parallel"),
        name="moe_router",
    )(x, g.reshape(1, d), sc, sh, router_w.T, router_b.reshape(N_EXPERTS, 1),
      s_w1.astype(_bf16), s_w3.astype(_bf16), s_w2.astype(_bf16))


def _strict_lower(n):
    return jnp.where(lax.broadcasted_iota(jnp.int32, (n, n), 1)
                     < lax.broadcasted_iota(jnp.int32, (n, n), 0), 1.0, 0.0).astype(_bf16)


def _prefix_over_experts(col):
    wide = jnp.broadcast_to(col, (N_EXPERTS, 128)).astype(_bf16)
    return _dot(_strict_lower(N_EXPERTS), wide)[:, :1]


def _rank_kernel(idx_ref, piece_ref, tot_ref, base_ref):
    phase = pl.program_id(0)
    i = pl.program_id(1)
    tm = idx_ref.shape[1]

    idx = idx_ref[...]
    ei = lax.broadcasted_iota(jnp.int32, (N_EXPERTS, tm), 0)
    memf = jnp.zeros((N_EXPERTS, tm), _f32)
    for k in range(TOP_K):
        memf = jnp.where(ei == idx[k:k + 1, :], 1.0, memf)
    tile_cnt = jnp.sum(memf, axis=1, keepdims=True).astype(jnp.int32)
    seg_rows = ((tile_cnt + (SEG_ALIGN - 1)) & (-SEG_ALIGN)).astype(_f32)

    @pl.when(jnp.logical_and(phase == 0, i == 0))
    def _():
        base_ref[...] = jnp.zeros_like(base_ref)

    @pl.when(phase == 0)
    def _():
        base_ref[...] += seg_rows

    @pl.when(jnp.logical_and(phase == 1, i == 0))
    def _():
        tot = base_ref[...].astype(jnp.int32)
        tot_ref[...] = tot
        nblk = (tot + (ROW_BLOCK - 1)) >> ROW_BLOCK_LOG2
        start_blk = (32.0 * _prefix_over_experts((nblk >> 5).astype(_f32))
                     + _prefix_over_experts((nblk & 31).astype(_f32)))
        base_ref[...] = start_blk * float(ROW_BLOCK)

    @pl.when(phase == 1)
    def _():
        seg_i = seg_rows.astype(jnp.int32)
        lo = (16.0 * _prefix_over_experts((seg_i >> 4).astype(_f32))
              + _prefix_over_experts((seg_i & 15).astype(_f32)))
        lane = lax.broadcasted_iota(jnp.int32, (N_EXPERTS, PIECE_SLOTS), 1)
        row = (lane * SEG_ALIGN).astype(_f32)
        inside = jnp.logical_and(row >= lo, row < lo + seg_rows)
        dst = jnp.sum(jnp.where(inside, base_ref[...] + (row - lo), 0.0), axis=0, keepdims=True)
        total = jnp.sum(seg_rows, axis=0, keepdims=True)
        piece_ref[0] = jnp.where(lane[:1] == PIECE_SLOTS - 1, total, dst).astype(jnp.int32)
        base_ref[...] += seg_rows


def moe_rank(top_idx):
    k, t = top_idx.shape
    tm = TOKEN_TILE
    assert _tile_buffer_rows(tm) // SEG_ALIGN < PIECE_SLOTS
    return pl.pallas_call(
        _rank_kernel,
        grid=(2, t // tm),
        in_specs=[pl.BlockSpec((k, tm), lambda p, i: (0, i))],
        out_specs=[pl.BlockSpec((1, 1, PIECE_SLOTS), lambda p, i: (i * p, 0, 0)),
                   pl.BlockSpec((N_EXPERTS, 1), lambda p, i: (0, 0))],
        out_shape=[jax.ShapeDtypeStruct((t // tm, 1, PIECE_SLOTS), jnp.int32),
                   jax.ShapeDtypeStruct((N_EXPERTS, 1), jnp.int32)],
        scratch_shapes=[pltpu.VMEM((N_EXPERTS, 1), _f32)],
        compiler_params=_cparams("arbitrary", "arbitrary"),
        name="moe_rank",
    )(top_idx)


def _aligned_rows(ref, start, n):
    return ref.at[pl.ds(pl.multiple_of(start, SEG_ALIGN), n)]


def _tile_rows(piece_ref):
    return piece_ref[0, 0, PIECE_SLOTS - 1]


def _for_each_piece(piece_ref, fn):
    def piece(g, c):
        fn(g * SEG_ALIGN, piece_ref[0, 0, g])
        return c

    lax.fori_loop(0, _tile_rows(piece_ref) // SEG_ALIGN, piece, 0)


def _wait_pieces(n_pieces, make_copy):
    for bit in reversed(range(PIECE_SLOTS.bit_length())):
        @pl.when((n_pieces >> bit) & 1 == 1)
        def _(bit=bit):
            make_copy(1 << bit).wait()


def _prefix_over_lanes(row):
    n = row.shape[1]
    wide = jnp.broadcast_to(row, (SEG_ALIGN, n)).astype(_bf16)
    return _dot_nt(wide, _strict_lower(n))[:1, :]


def _tile_layout(idx_t):
    tm = idx_t.shape[0]
    lane_e = lax.broadcasted_iota(jnp.int32, (tm, N_EXPERTS), 1)
    member = jnp.zeros((tm, N_EXPERTS), _f32)
    for k in range(TOP_K):
        member = jnp.where(lane_e == idx_t[:, k:k + 1], 1.0, member)
    before = _dot(_strict_lower(tm), member.astype(_bf16))
    count = jnp.sum(member, axis=0, keepdims=True).astype(jnp.int32)
    seg_rows = (count + (SEG_ALIGN - 1)) & (-SEG_ALIGN)
    seg_lo = (16.0 * _prefix_over_lanes((seg_rows >> 4).astype(_f32))
              + _prefix_over_lanes((seg_rows & 15).astype(_f32)))
    return member, before, seg_lo, seg_lo + seg_rows.astype(_f32)


def _rows_of_chunk(r0, seg_lo, seg_hi):
    row = (lax.broadcasted_iota(jnp.int32, (PERM_ROWS, N_EXPERTS), 0) + r0).astype(_f32)
    inside = jnp.logical_and(row >= seg_lo, row < seg_hi)
    rank = jnp.sum(jnp.where(inside, row - seg_lo, 0.0), axis=1, keepdims=True)
    return jnp.where(inside, 1.0, 0.0).astype(_bf16), rank


def _dispatch_kernel(pad_from_ref, pad_n_ref, nb_ref, piece_ref, idx_ref, hb_ref, xs_ref,
                     xbuf_ref, zero_ref, sent_ref, sem, zsem):
    tm = hb_ref.shape[0]
    n_blocks = xs_ref.shape[0] // ROW_BLOCK

    @pl.when(pl.program_id(0) == 0)
    def _():
        zero_ref[...] = jnp.zeros_like(zero_ref)

        def pad_expert(e, carry):
            off = pad_from_ref[e]

            def pad_piece(r, c):
                pltpu.make_async_copy(_aligned_rows(zero_ref, 0, SEG_ALIGN),
                                      _aligned_rows(xs_ref, off + SEG_ALIGN * r, SEG_ALIGN),
                                      zsem).start()
                return c + 1

            return lax.fori_loop(0, pad_n_ref[e] // SEG_ALIGN, pad_piece, carry)

        n_pieces = lax.fori_loop(0, N_EXPERTS, pad_expert, 0)

        def pad_tail(b, carry):
            pltpu.make_async_copy(zero_ref, xs_ref.at[pl.ds(b * ROW_BLOCK, ROW_BLOCK)],
                                  zsem).start()
            return carry

        lax.fori_loop(nb_ref[0], n_blocks, pad_tail, 0)

        def wait_piece(r, c):
            pltpu.make_async_copy(_aligned_rows(zero_ref, 0, SEG_ALIGN),
                                  _aligned_rows(xs_ref, 0, SEG_ALIGN), zsem).wait()
            return c

        def wait_tail(b, c):
            pltpu.make_async_copy(zero_ref, xs_ref.at[pl.ds(0, ROW_BLOCK)], zsem).wait()
            return c

        lax.fori_loop(0, n_pieces, wait_piece, 0)
        lax.fori_loop(nb_ref[0], n_blocks, wait_tail, 0)

    step = pl.program_id(0)
    slot = step % 2
    total = _tile_rows(piece_ref)
    member, before, seg_lo, seg_hi = _tile_layout(idx_ref[...])
    place = jnp.where(member > 0.0, before + 1.0, -1.0).astype(_bf16)

    def permute(c, carry):
        r0 = c * PERM_ROWS
        expert_of_row, rank_of_row = _rows_of_chunk(r0, seg_lo, seg_hi)
        row = lax.broadcasted_iota(jnp.int32, rank_of_row.shape, 0) + r0
        want = jnp.where(row < total, rank_of_row + 1.0, -5.0)
        sel = jnp.where(_dot_nt(expert_of_row, place) == want, 1.0, 0.0)
        picked = _dot(sel.astype(_bf16), hb_ref[...])
        half = picked.shape[1] // 2
        xbuf_ref[slot, pl.ds(r0, PERM_ROWS), :] = (
            (pltpu.bitcast(picked[:, :half], jnp.uint32) >> 16)
            | pltpu.bitcast(picked[:, half:], jnp.uint32))
        return carry

    for c in range(xbuf_ref.shape[1] // PERM_ROWS):
        permute(c, 0)

    wait_sent = lambda n_pieces: _wait_pieces(n_pieces, lambda k: pltpu.make_async_copy(
        xbuf_ref.at[slot, pl.ds(0, k * SEG_ALIGN)], xs_ref.at[pl.ds(0, k * SEG_ALIGN)], sem))

    @pl.when(step > 0)
    def _():
        wait_sent(sent_ref[0])

    for s in range(2):
        @pl.when(slot == s)
        def _(s=s):
            def send(off, dst):
                pltpu.make_async_copy(_aligned_rows(xbuf_ref.at[s], off, SEG_ALIGN),
                                      _aligned_rows(xs_ref, dst, SEG_ALIGN), sem).start()

            _for_each_piece(piece_ref, send)

    sent_ref[0] = total // SEG_ALIGN

    @pl.when(step == pl.num_programs(0) - 1)
    def _():
        wait_sent(total // SEG_ALIGN)


def _tile_buffer_rows(tm):
    worst = tm * TOP_K + N_EXPERTS * (SEG_ALIGN - 1)
    return -(-worst // PERM_ROWS) * PERM_ROWS


def moe_dispatch(pieces, idx_t, hb, pad_from, pad_n, n_used, n_blocks):
    t, d = hb.shape
    tm = TOKEN_TILE
    grid_spec = pltpu.PrefetchScalarGridSpec(
        num_scalar_prefetch=3,
        grid=(t // tm,),
        in_specs=[
            pl.BlockSpec((1, 1, PIECE_SLOTS), lambda i, *_: (i, 0, 0), memory_space=pltpu.SMEM),
            pl.BlockSpec((tm, TOP_K), lambda i, *_: (i, 0)),
            pl.BlockSpec((tm, d), lambda i, *_: (i, 0)),
        ],
        out_specs=pl.BlockSpec(memory_space=pl.ANY),
        scratch_shapes=[pltpu.VMEM((2, _tile_buffer_rows(tm), d // 2), jnp.uint32),
                        pltpu.VMEM((ROW_BLOCK, d // 2), jnp.uint32),
                        pltpu.SMEM((1,), jnp.int32),
                        pltpu.SemaphoreType.DMA(()), pltpu.SemaphoreType.DMA(())],
    )
    return pl.pallas_call(
        _dispatch_kernel,
        grid_spec=grid_spec,
        out_shape=jax.ShapeDtypeStruct((n_blocks * ROW_BLOCK, d // 2), jnp.uint32),
        compiler_params=_cparams("arbitrary"),
        name="moe_dispatch",
    )(pad_from, pad_n, n_used, pieces, idx_t, hb)


def _expert_kernel(be_ref, nb_ref, xs_ref, w1_ref, w3_ref, w2_ref, y_ref, w1b, w3b, w2b):
    i = pl.program_id(0)

    @pl.when(jnp.logical_or(i == 0, be_ref[i] != be_ref[jnp.maximum(i - 1, 0)]))
    def _():
        w1b[...] = w1_ref[0, 0].astype(_bf16)
        w3b[...] = w3_ref[0, 0].astype(_bf16)
        w2b[...] = w2_ref[0, 0].astype(_bf16)

    @pl.when(i < nb_ref[0])
    def _():
        chains = [pl.ds(c * EXPERT_ROWS, EXPERT_ROWS) for c in range(ROW_BLOCK // EXPERT_ROWS)]
        xs = [jnp.concatenate(_unpack_rows(xs_ref[rows, :]), axis=1) for rows in chains]
        acts = [(_silu(_dot(x, w1b[...])) * _dot(x, w3b[...])).astype(_bf16) for x in xs]
        for rows, act in zip(chains, acts):
            y_ref[rows, :] = _pack_rows(_dot(act, w2b[...]))

    @pl.when(i >= nb_ref[0])
    def _():
        y_ref[...] = jnp.zeros_like(y_ref)


def moe_experts(xs, blk_e, n_used, w1, w3, w2, layer):
    p, half = xs.shape
    d = 2 * half
    ff = w1.shape[3]
    nb = p // ROW_BLOCK
    row = lambda i, be, nu: (jnp.minimum(i, nu[0] - 1), 0)
    wsel = lambda i, be, nu: (layer, be[i], 0, 0)
    grid_spec = pltpu.PrefetchScalarGridSpec(
        num_scalar_prefetch=2,
        grid=(nb,),
        in_specs=[
            pl.BlockSpec((ROW_BLOCK, half), row),
            pl.BlockSpec((1, 1, d, ff), wsel),
            pl.BlockSpec((1, 1, d, ff), wsel),
            pl.BlockSpec((1, 1, ff, d), wsel),
        ],
        out_specs=pl.BlockSpec((ROW_BLOCK, half), lambda i, be, nu: (i, 0)),
        scratch_shapes=[pltpu.VMEM((d, ff), _bf16), pltpu.VMEM((d, ff), _bf16),
                        pltpu.VMEM((ff, d), _bf16)],
    )
    return pl.pallas_call(
        _expert_kernel,
        grid_spec=grid_spec,
        out_shape=jax.ShapeDtypeStruct((p, half), jnp.uint32),
        compiler_params=_cparams("arbitrary"),
        name="moe_experts",
    )(blk_e, n_used, xs, w1, w3, w2)


def _combine_kernel(piece_ref, next_piece_ref, ys_ref, idx_ref, w_ref, shared_ref, x_ref, gate_ref,
                    fn_ref, o_ref, ybuf_ref, acc_ref, sem, *, final_norm):
    tm = x_ref.shape[1]
    half = ybuf_ref.shape[2]
    step = pl.program_id(0) * pl.num_programs(1) + pl.program_id(1)
    n_steps = pl.num_programs(0) * pl.num_programs(1)
    slot = step % 2

    def fetch(table_ref, s):
        def one(off, src):
            pltpu.make_async_copy(_aligned_rows(ys_ref, src, SEG_ALIGN),
                                  _aligned_rows(ybuf_ref.at[s], off, SEG_ALIGN), sem.at[s]).start()

        _for_each_piece(table_ref, one)

    @pl.when(step == 0)
    def _():
        fetch(piece_ref, 0)

    for s in range(2):
        @pl.when(jnp.logical_and(step + 1 < n_steps, slot == 1 - s))
        def _(s=s):
            fetch(next_piece_ref, s)

    total = _tile_rows(piece_ref)
    _wait_pieces(total // SEG_ALIGN, lambda k: pltpu.make_async_copy(
        ys_ref.at[pl.ds(0, k * SEG_ALIGN)], ybuf_ref.at[slot, pl.ds(0, k * SEG_ALIGN)],
        sem.at[slot]))

    idx_t = idx_ref[...]
    member, before, seg_lo, seg_hi = _tile_layout(idx_t)
    before_b = before.astype(_bf16)
    lane_e = lax.broadcasted_iota(jnp.int32, member.shape, 1)
    w_of_expert = jnp.zeros(member.shape, _f32)
    for k in range(TOP_K):
        w_of_expert = jnp.where(lane_e == idx_t[:, k:k + 1], w_ref[:, k:k + 1], w_of_expert)
    w_of_expert = w_of_expert.astype(_bf16)
    lo8 = jnp.broadcast_to(seg_lo, (SEG_ALIGN, N_EXPERTS)).astype(jnp.int32) >> 3
    lo_head, lo_rest = (lo8 >> 4).astype(_f32).astype(_bf16), (lo8 & 15).astype(_f32).astype(_bf16)
    acc_ref[...] = shared_ref[...]

    def reduce(c, carry):
        r0 = c * PERM_ROWS
        packed = ybuf_ref[slot, pl.ds(r0, PERM_ROWS), :]
        row = lax.broadcasted_iota(jnp.int32, packed.shape, 0) + r0
        packed = jnp.where(row < total, packed, jnp.uint32(0))
        y_lo, y_hi = _unpack_rows(packed)
        expert_of_row, rank_of_row = _rows_of_chunk(r0, seg_lo, seg_hi)
        del rank_of_row
        start = (128.0 * _dot_nt(lo_head, expert_of_row) + 8.0 * _dot_nt(lo_rest, expert_of_row))
        rank_row = (lax.broadcasted_iota(jnp.int32, (1, PERM_ROWS), 1) + r0).astype(_f32) - start[:1]
        hit = _dot_nt(before_b, expert_of_row) == rank_row
        wm = jnp.where(hit, _dot_nt(w_of_expert, expert_of_row), 0.0).astype(_bf16)
        acc_ref[:, :half] += _dot(wm, y_lo)
        acc_ref[:, half:] += _dot(wm, y_hi)
        return carry

    for c in range(ybuf_ref.shape[1] // PERM_ROWS):
        reduce(c, 0)

    xn = x_ref[0] + gate_ref[0] * acc_ref[...]
    if final_norm:
        xn = xn * lax.rsqrt(jnp.mean(xn * xn, axis=-1, keepdims=True) + EPS) * fn_ref[...]
    o_ref[0] = xn


def moe_combine(pieces, idx_t, ys, top_w, shared, x, gate, fn_g, final_norm):
    b, s, d = x.shape
    tm = TOKEN_TILE
    n_s = s // tm
    tile = lambda i, m: (i * n_s + m, 0)
    last_tile = b * n_s - 1
    return pl.pallas_call(
        functools.partial(_combine_kernel, final_norm=final_norm),
        grid=(b, n_s),
        in_specs=[
            pl.BlockSpec((1, 1, PIECE_SLOTS), lambda i, m: (i * n_s + m, 0, 0),
                         memory_space=pltpu.SMEM),
            pl.BlockSpec((1, 1, PIECE_SLOTS),
                         lambda i, m: (jnp.minimum(i * n_s + m + 1, last_tile), 0, 0),
                         memory_space=pltpu.SMEM),
            pl.BlockSpec(memory_space=pl.ANY),
            pl.BlockSpec((tm, TOP_K), tile),
            pl.BlockSpec((tm, TOP_K), tile),
            pl.BlockSpec((tm, d), tile),
            pl.BlockSpec((1, tm, d), lambda i, m: (i, m, 0)),
            pl.BlockSpec((1, 1, d), lambda i, m: (i, 0, 0)),
            pl.BlockSpec((1, d), lambda i, m: (0, 0)),
        ],
        out_specs=pl.BlockSpec((1, tm, d), lambda i, m: (i, m, 0)),
        out_shape=jax.ShapeDtypeStruct((b, s, d), _f32),
        scratch_shapes=[pltpu.VMEM((2, _tile_buffer_rows(tm), d // 2), jnp.uint32),
                        pltpu.VMEM((tm, d), _f32), pltpu.SemaphoreType.DMA((2,))],
        compiler_params=_cparams("arbitrary", "arbitrary"),
        name="moe_combine",
    )(pieces, pieces, ys, idx_t, top_w, shared, x, gate, fn_g.reshape(1, d))


def moe_layer(x, g, sc, sh, gate, router_w, router_b, w1, w3, w2, layer, s_w1, s_w3, s_w2,
              fn_g, final_norm):
    b, s, d = x.shape
    t = b * s
    hb, top_idx, top_w, shared = moe_router(x, g, sc, sh, router_w, router_b, s_w1, s_w3, s_w2)
    pieces, rows = moe_rank(top_idx)
    idx_t = top_idx.T
    n_tiles = pieces.shape[0]

    rows = rows.reshape(N_EXPERTS)
    nblk = (rows + ROW_BLOCK - 1) // ROW_BLOCK
    blk_ends = jnp.cumsum(nblk)
    max_rows = t * TOP_K + n_tiles * N_EXPERTS * (SEG_ALIGN - 1)
    n_blocks = -(-max_rows // ROW_BLOCK) + N_EXPERTS
    blk = jnp.arange(n_blocks, dtype=jnp.int32)
    blk_e = jnp.minimum(jnp.sum(blk[:, None] >= blk_ends[None, :], axis=1),
                        N_EXPERTS - 1).astype(jnp.int32)
    n_used = blk_ends[-1:].astype(jnp.int32)
    pad_from = ((blk_ends - nblk) * ROW_BLOCK + rows).astype(jnp.int32)
    pad_n = (nblk * ROW_BLOCK - rows).astype(jnp.int32)

    xs = moe_dispatch(pieces, idx_t, hb, pad_from, pad_n, n_used, n_blocks)
    ys = moe_experts(xs, blk_e, n_used, w1, w3, w2, layer)
    return moe_combine(pieces, idx_t, ys, top_w.T, shared, x, gate, fn_g, final_norm)


def kernel(x, c, norm_mix, norm_ffn, ada_w, ada_b, da_w_in, da_w_out, da_lq1, da_lk1, da_lq2,
           da_lk2, da_subln, ret_w_in, ret_w_out, ret_gn, router_w, router_b, exp_w1, exp_w3,
           exp_w2, sh_w1, sh_w3, sh_w2, final_norm):
    depth, d = norm_mix.shape
    b = x.shape[0]
    mod = ada_modulation(c, ada_w, ada_b).reshape(depth, b, 6, 1, d)
    for i in range(depth):
        sh_a, sc_a, g_a, sh_f, sc_f, g_f = (mod[i, :, j] for j in range(6))
        j = i // 2
        if i % 2 == 0:
            proj = norm_proj(x, norm_mix[i], sc_a, sh_a, da_w_in[j].astype(_bf16))
            y = diff_attention_core(proj, da_lq1[j], da_lk1[j], da_lq2[j], da_lk2[j],
                                    da_subln[j], i)
            x = out_proj_residual(y, da_w_out[j].astype(_bf16), x, g_a)
        else:
            proj = norm_proj(x, norm_mix[i], sc_a, sh_a, ret_w_in[j].astype(_bf16))
            y = retention_core(proj, ret_gn[j])
            x = out_proj_residual(y, ret_w_out[j].astype(_bf16), x, g_a)
        x = moe_layer(x, norm_ffn[i], sc_f, sh_f, g_f, router_w[i], router_b[i],
                      exp_w1, exp_w3, exp_w2, i,
                      sh_w1[i], sh_w3[i], sh_w2[i], final_norm, i == depth - 1)
    return x
```

```python
import functools
import math

import jax
import jax.numpy as jnp
from jax import lax
from jax.experimental import pallas as pl
from jax.experimental.pallas import tpu as pltpu

EPS = 1e-6
DA_HEADS = 8
DA_HEAD_DIM = 64
DA_V_DIM = 2 * DA_HEAD_DIM
RET_HEADS = 4
RET_DK = 256
RET_DV = 512
RET_CHUNK = 128
N_EXPERTS = 64
TOP_K = 8
N_GROUPS = 8
GROUP_SIZE = N_EXPERTS // N_GROUPS
TOPK_GROUPS = 4
ROUTED_SCALE = 2.5
ROW_BLOCK_LOG2 = 9
ROW_BLOCK = 1 << ROW_BLOCK_LOG2
EXPERT_ROWS = 256
TOKEN_TILE = 256
SEG_ALIGN = 8
PERM_ROWS = 512
PIECE_SLOTS = 384

VMEM_LIMIT = 56 * 1024 * 1024
NEG_BIG = -1e30

_f32 = jnp.float32
_bf16 = jnp.bfloat16


def _cparams(*sem):
    return pltpu.CompilerParams(dimension_semantics=sem, vmem_limit_bytes=VMEM_LIMIT)


def _silu(v):
    return v * (1.0 / (1.0 + jnp.exp(-v)))


def _dot(a, b):
    return jnp.dot(a, b, preferred_element_type=_f32)


def _dot_nt(a, b):
    return lax.dot_general(a, b, (((1,), (1,)), ((), ())), preferred_element_type=_f32)


def _dot_tn(a, b):
    return lax.dot_general(a, b, (((0,), (0,)), ((), ())), preferred_element_type=_f32)


def _ada_kernel(c_ref, w_ref, b_ref, o_ref):
    a = _silu(c_ref[...]).astype(_bf16)
    o_ref[0] = _dot(a, w_ref[0].astype(_bf16)) + b_ref[0]


def ada_modulation(c, ada_w, ada_b):
    depth, d, n = ada_w.shape
    b = c.shape[0]
    tn = 1024
    return pl.pallas_call(
        _ada_kernel,
        grid=(depth, n // tn),
        in_specs=[
            pl.BlockSpec((b, d), lambda l, j: (0, 0)),
            pl.BlockSpec((1, d, tn), lambda l, j: (l, 0, j)),
            pl.BlockSpec((1, 1, tn), lambda l, j: (l, 0, j)),
        ],
        out_specs=pl.BlockSpec((1, b, tn), lambda l, j: (l, 0, j)),
        out_shape=jax.ShapeDtypeStruct((depth, b, n), _f32),
        compiler_params=_cparams("parallel", "parallel"),
        name="ada_modulation",
    )(c, ada_w, ada_b.reshape(depth, 1, n))


def _norm_mod(x, g, sc, sh):
    y = x * lax.rsqrt(jnp.mean(x * x, axis=-1, keepdims=True) + EPS)
    return (y * g) * (1.0 + sc) + sh


def _norm_proj_kernel(x_ref, g_ref, sc_ref, sh_ref, w_ref, o_ref, h_ref):
    @pl.when(pl.program_id(2) == 0)
    def _():
        h_ref[...] = _norm_mod(x_ref[0], g_ref[...], sc_ref[0], sh_ref[0]).astype(_bf16)

    o_ref[0] = _dot(h_ref[...], w_ref[...]).astype(o_ref.dtype)


def norm_proj(x, g, sc, sh, w):
    b, s, d = x.shape
    n = w.shape[1]
    tm = min(1024, s)
    tn = 1024
    return pl.pallas_call(
        _norm_proj_kernel,
        grid=(b, s // tm, n // tn),
        in_specs=[
            pl.BlockSpec((1, tm, d), lambda i, m, j: (i, m, 0)),
            pl.BlockSpec((1, d), lambda i, m, j: (0, 0)),
            pl.BlockSpec((1, 1, d), lambda i, m, j: (i, 0, 0)),
            pl.BlockSpec((1, 1, d), lambda i, m, j: (i, 0, 0)),
            pl.BlockSpec((d, tn), lambda i, m, j: (0, j)),
        ],
        out_specs=pl.BlockSpec((1, tm, tn), lambda i, m, j: (i, m, j)),
        out_shape=jax.ShapeDtypeStruct((b, s, n), _bf16),
        scratch_shapes=[pltpu.VMEM((tm, d), _bf16)],
        compiler_params=_cparams("parallel", "parallel", "arbitrary"),
        name="norm_proj",
    )(x, g.reshape(1, d), sc, sh, w)


def _out_proj_kernel(y_ref, w_ref, x_ref, g_ref, o_ref):
    o_ref[0] = x_ref[0] + g_ref[0] * _dot(y_ref[0], w_ref[...])


def out_proj_residual(y, w, x, gate):
    b, s, k = y.shape
    d = w.shape[1]
    tm = min(512, s)
    return pl.pallas_call(
        _out_proj_kernel,
        grid=(b, s // tm),
        in_specs=[
            pl.BlockSpec((1, tm, k), lambda i, m: (i, m, 0)),
            pl.BlockSpec((k, d), lambda i, m: (0, 0)),
            pl.BlockSpec((1, tm, d), lambda i, m: (i, m, 0)),
            pl.BlockSpec((1, 1, d), lambda i, m: (i, 0, 0)),
        ],
        out_specs=pl.BlockSpec((1, tm, d), lambda i, m: (i, m, 0)),
        out_shape=jax.ShapeDtypeStruct((b, s, d), _f32),
        compiler_params=_cparams("parallel", "parallel"),
        name="out_proj_residual",
    )(y, w, x, gate)


def _diff_attn_kernel(q_ref, k_ref, v_ref, lq1_ref, lk1_ref, lq2_ref, lk2_ref, sg_ref, o_ref,
                      bias_ref, *, tq, lam_init):
    h = pl.program_id(1)
    qi = pl.program_id(2)
    dh, dv = DA_HEAD_DIM, DA_V_DIM

    q = q_ref[0] * (dh ** -0.5)
    lane = lax.broadcasted_iota(jnp.int32, q.shape, 1)
    zero = jnp.zeros_like(q)
    q_maps = (jnp.where(lane < dh, q, zero), jnp.where(lane >= dh, q, zero))

    n_q = k_ref.shape[1] // tq

    @pl.when(qi == 0)
    def _():
        hp1 = jnp.full((1, 1), h + 1, jnp.int32).astype(_f32)
        slope = jnp.exp2(hp1 * (-8.0 / DA_HEADS))
        dist = ((n_q - 1) * tq + lax.broadcasted_iota(jnp.int32, bias_ref.shape, 0)
                - lax.broadcasted_iota(jnp.int32, bias_ref.shape, 1)).astype(_f32)
        bias_ref[...] = -slope * dist

    lam = (jnp.exp(jnp.sum(lq1_ref[...] * lk1_ref[...], axis=-1, keepdims=True))
           - jnp.exp(jnp.sum(lq2_ref[...] * lk2_ref[...], axis=-1, keepdims=True)) + lam_init)
    causal = (lax.broadcasted_iota(jnp.int32, (tq, tq), 0)
              >= lax.broadcasted_iota(jnp.int32, (tq, tq), 1))

    def attend(c):
        first = (n_q - 1 - c) * tq
        blocks = [slice(j * tq, (j + 1) * tq) for j in range(c + 1)]
        tiles = [[_dot_nt(q_m, k_ref[0, blk, :])
                  + bias_ref[:, first + blk.start:first + blk.stop] for q_m in q_maps]
                 for blk in blocks]
        tiles[c] = [jnp.where(causal, s, NEG_BIG) for s in tiles[c]]
        outs = []
        for mi in range(2):
            parts = [row[mi] for row in tiles]
            m = functools.reduce(jnp.maximum, [jnp.max(x, axis=-1, keepdims=True) for x in parts])
            ps = [jnp.exp(x - m) for x in parts]
            l = functools.reduce(jnp.add, [jnp.sum(x, axis=-1, keepdims=True) for x in ps])
            o = functools.reduce(jnp.add, [_dot(x.astype(_bf16), v_ref[0, blk, :])
                                           for x, blk in zip(ps, blocks)])
            outs.append(o / l)
        o = outs[0] - lam * outs[1]
        o = o * lax.rsqrt(jnp.mean(o * o, axis=-1, keepdims=True) + EPS)
        o_ref[0] = ((o * sg_ref[...]) * (1.0 - lam_init)).astype(o_ref.dtype)

    for c in range(n_q):
        pl.when(qi == c)(functools.partial(attend, c))


def diff_attention_core(proj, lq1, lk1, lq2, lk2, subln_g, layer_idx):
    b, s, _ = proj.shape
    nh, dh, dv = DA_HEADS, DA_HEAD_DIM, DA_V_DIM
    tq = min(512, s)
    lam_init = 0.8 - 0.6 * math.exp(-0.3 * layer_idx)
    vec = lambda n: pl.BlockSpec((1, n), lambda i, h, m: (0, 0))
    return pl.pallas_call(
        functools.partial(_diff_attn_kernel, tq=tq, lam_init=lam_init),
        grid=(b, nh, s // tq),
        in_specs=[
            pl.BlockSpec((1, tq, 2 * dh), lambda i, h, m: (i, m, h)),
            pl.BlockSpec((1, s, 2 * dh), lambda i, h, m: (i, 0, nh + h)),
            pl.BlockSpec((1, s, dv), lambda i, h, m: (i, 0, 2 * nh + h)),
            vec(dh), vec(dh), vec(dh), vec(dh), vec(dv),
        ],
        out_specs=pl.BlockSpec((1, tq, dv), lambda i, h, m: (i, m, h)),
        out_shape=jax.ShapeDtypeStruct((b, s, nh * dv), _bf16),
        scratch_shapes=[pltpu.VMEM((tq, s), _f32)],
        compiler_params=_cparams("parallel", "parallel", "arbitrary"),
        name="diff_attention",
    )(proj, proj, proj, lq1.reshape(1, dh), lk1.reshape(1, dh), lq2.reshape(1, dh),
      lk2.reshape(1, dh), subln_g.reshape(1, dv))


def _retention_kernel(q_ref, k_ref, v_ref, g_ref, gn_ref, o_ref, r_ref, *, chunk):
    h = pl.program_id(1)
    c = pl.program_id(2)
    dk = RET_DK

    @pl.when(c == 0)
    def _():
        r_ref[...] = jnp.zeros_like(r_ref)

    hf = jnp.full((1, 1), h, jnp.int32).astype(_f32)
    log_g = jnp.log(1.0 - jnp.exp2(-5.0 - hf))
    ri = lax.broadcasted_iota(jnp.int32, (chunk, chunk), 0)
    ci = lax.broadcasted_iota(jnp.int32, (chunk, chunk), 1)
    diff = (ri - ci).astype(_f32)
    decay = jnp.where(diff >= 0, jnp.exp(log_g * jnp.maximum(diff, 0.0)), 0.0)
    idx = lax.broadcasted_iota(jnp.int32, (chunk, 1), 0).astype(_f32)
    zeta = jnp.exp(log_g * (chunk - 1 - idx))
    xi = jnp.exp(log_g * (idx + 1.0))
    g_c = jnp.exp(log_g * chunk)

    decay = decay * (dk ** -0.5)
    zeta = zeta * (dk ** -0.5)
    chunks = [pl.ds(sub * chunk, chunk) for sub in range(q_ref.shape[1] // chunk)]
    intra, updates = [], []
    for rows in chunks:
        q, k, v = q_ref[0, rows, :], k_ref[0, rows, :], v_ref[0, rows, :]
        intra.append(_dot((_dot_nt(q, k) * decay).astype(_bf16), v))
        updates.append(_dot_tn((k.astype(_f32) * zeta).astype(_bf16), v))
    r = r_ref[...]
    for rows, o, upd in zip(chunks, intra, updates):
        o = o + _dot(q_ref[0, rows, :], r.astype(_bf16)) * xi
        r = g_c * r + upd
        o = o * lax.rsqrt(jnp.mean(o * o, axis=-1, keepdims=True) + EPS) * gn_ref[...]
        o_ref[0, rows, :] = (_silu(g_ref[0, rows, :].astype(_f32)) * o).astype(o_ref.dtype)
    r_ref[...] = r


def retention_core(proj, gn_g):
    b, s, _ = proj.shape
    nh, dk, dv = RET_HEADS, RET_DK, RET_DV
    chunk = RET_CHUNK
    rows = min(8 * chunk, s)
    return pl.pallas_call(
        functools.partial(_retention_kernel, chunk=chunk),
        grid=(b, nh, s // rows),
        in_specs=[
            pl.BlockSpec((1, rows, dk), lambda i, h, c: (i, c, h)),
            pl.BlockSpec((1, rows, dk), lambda i, h, c: (i, c, nh + h)),
            pl.BlockSpec((1, rows, dv), lambda i, h, c: (i, c, (2 * nh * dk) // dv + h)),
            pl.BlockSpec((1, rows, dv), lambda i, h, c: (i, c, (2 * nh * dk) // dv + nh + h)),
            pl.BlockSpec((1, dv), lambda i, h, c: (0, h)),
        ],
        out_specs=pl.BlockSpec((1, rows, dv), lambda i, h, c: (i, c, h)),
        out_shape=jax.ShapeDtypeStruct((b, s, nh * dv), _bf16),
        scratch_shapes=[pltpu.VMEM((dk, dv), _f32)],
        compiler_params=_cparams("parallel", "parallel", "arbitrary"),
        name="retention",
    )(proj, proj, proj, proj, gn_g.reshape(1, nh * dv))


def _pack_rows(h):
    n = h.shape[1] // 2
    lo = pltpu.bitcast(h[:, :n].astype(_bf16).astype(_f32), jnp.uint32)
    hi = pltpu.bitcast(h[:, n:].astype(_bf16).astype(_f32), jnp.uint32)
    return (lo >> 16) | (hi & jnp.uint32(0xFFFF0000))


def _unpack_rows(p):
    lo = pltpu.bitcast(p << 16, _f32).astype(_bf16)
    hi = pltpu.bitcast(p & jnp.uint32(0xFFFF0000), _f32).astype(_bf16)
    return lo, hi


def _first_argmax(v, iota, axis, big):
    m = jnp.max(v, axis=axis, keepdims=True)
    i = jnp.min(jnp.where(v == m, iota, big), axis=axis, keepdims=True)
    return m, i


def _router_kernel(x_ref, g_ref, sc_ref, sh_ref, rwt_ref, rb_ref, s1_ref, s3_ref, s2_ref,
                   hb_ref, idx_ref, wgt_ref, shared_ref):
    tm = x_ref.shape[1]
    h = _norm_mod(x_ref[0], g_ref[...], sc_ref[0], sh_ref[0])
    hb = h.astype(_bf16)
    hb_ref[...] = hb
    act = (_silu(_dot(hb, s1_ref[...])) * _dot(hb, s3_ref[...])).astype(_bf16)
    shared_ref[...] = _dot(act, s2_ref[...])

    scores = 1.0 / (1.0 + jnp.exp(-_dot_nt(rwt_ref[...], h)))
    choice = scores + rb_ref[...]
    c3 = choice.reshape(N_GROUPS, GROUP_SIZE, tm)
    mem = lax.broadcasted_iota(jnp.int32, c3.shape, 1)
    m1, i1 = _first_argmax(c3, mem, 1, GROUP_SIZE)
    m2 = jnp.max(jnp.where(mem == i1, -jnp.inf, c3), axis=1, keepdims=True)
    gs = m1 + m2
    gi = lax.broadcasted_iota(jnp.int32, gs.shape, 0)
    sel = jnp.zeros(gs.shape, _f32)
    for _ in range(TOPK_GROUPS):
        _, ig = _first_argmax(gs, gi, 0, N_GROUPS)
        hit = gi == ig
        sel = jnp.where(hit, 1.0, sel)
        gs = jnp.where(hit, -jnp.inf, gs)
    cm = jnp.where(sel > 0.0, c3, -jnp.inf).reshape(N_EXPERTS, tm)
    ei = lax.broadcasted_iota(jnp.int32, cm.shape, 0)
    idxs, wgts = [], []
    for _ in range(TOP_K):
        _, ie = _first_argmax(cm, ei, 0, N_EXPERTS)
        hit = ei == ie
        idxs.append(ie)
        wgts.append(jnp.sum(jnp.where(hit, scores, 0.0), axis=0, keepdims=True))
        cm = jnp.where(hit, -jnp.inf, cm)
    w = jnp.concatenate(wgts, axis=0)
    idx_ref[...] = jnp.concatenate(idxs, axis=0)
    wgt_ref[...] = w / jnp.sum(w, axis=0, keepdims=True) * ROUTED_SCALE


def moe_router(x, g, sc, sh, router_w, router_b, s_w1, s_w3, s_w2):
    b, s, d = x.shape
    t = b * s
    tm = min(256, s)
    n_s = s // tm
    sf = s_w1.shape[1]
    const = lambda shape: pl.BlockSpec(shape, lambda i, m: (0,) * len(shape))
    return pl.pallas_call(
        _router_kernel,
        grid=(b, n_s),
        in_specs=[
            pl.BlockSpec((1, tm, d), lambda i, m: (i, m, 0)),
            const((1, d)),
            pl.BlockSpec((1, 1, d), lambda i, m: (i, 0, 0)),
            pl.BlockSpec((1, 1, d), lambda i, m: (i, 0, 0)),
            const((N_EXPERTS, d)), const((N_EXPERTS, 1)),
            const((d, sf)), const((d, sf)), const((sf, d)),
        ],
        out_specs=[
            pl.BlockSpec((tm, d), lambda i, m: (i * n_s + m, 0)),
            pl.BlockSpec((TOP_K, tm), lambda i, m: (0, i * n_s + m)),
            pl.BlockSpec((TOP_K, tm), lambda i, m: (0, i * n_s + m)),
            pl.BlockSpec((tm, d), lambda i, m: (i * n_s + m, 0)),
        ],
        out_shape=[
            jax.ShapeDtypeStruct((t, d), _bf16),
            jax.ShapeDtypeStruct((TOP_K, t), jnp.int32),
            jax.ShapeDtypeStruct((TOP_K, t), _f32),
            jax.ShapeDtypeStruct((t, d), _f32),
        ],
        compiler_params=_cparams("parallel", "parallel"),
        name="moe_router",
    )(x, g.reshape(1, d), sc, sh, router_w.T, router_b.reshape(N_EXPERTS, 1),
      s_w1.astype(_bf16), s_w3.astype(_bf16), s_w2.astype(_bf16))


def _strict_lower(n):
    return jnp.where(lax.broadcasted_iota(jnp.int32, (n, n), 1)
                     < lax.broadcasted_iota(jnp.int32, (n, n), 0), 1.0, 0.0).astype(_bf16)


def _prefix_over_experts(col):
    wide = jnp.broadcast_to(col, (N_EXPERTS, 128)).astype(_bf16)
    return _dot(_strict_lower(N_EXPERTS), wide)[:, :1]


def _rank_kernel(idx_ref, piece_ref, tot_ref, base_ref):
    phase = pl.program_id(0)
    i = pl.program_id(1)
    tm = idx_ref.shape[1]

    idx = idx_ref[...]
    ei = lax.broadcasted_iota(jnp.int32, (N_EXPERTS, tm), 0)
    memf = jnp.zeros((N_EXPERTS, tm), _f32)
    for k in range(TOP_K):
        memf = jnp.where(ei == idx[k:k + 1, :], 1.0, memf)
    tile_cnt = jnp.sum(memf, axis=1, keepdims=True).astype(jnp.int32)
    seg_rows = ((tile_cnt + (SEG_ALIGN - 1)) & (-SEG_ALIGN)).astype(_f32)

    @pl.when(jnp.logical_and(phase == 0, i == 0))
    def _():
        base_ref[...] = jnp.zeros_like(base_ref)

    @pl.when(phase == 0)
    def _():
        base_ref[...] += seg_rows

    @pl.when(jnp.logical_and(phase == 1, i == 0))
    def _():
        tot = base_ref[...].astype(jnp.int32)
        tot_ref[...] = tot
        nblk = (tot + (ROW_BLOCK - 1)) >> ROW_BLOCK_LOG2
        start_blk = (32.0 * _prefix_over_experts((nblk >> 5).astype(_f32))
                     + _prefix_over_experts((nblk & 31).astype(_f32)))
        base_ref[...] = start_blk * float(ROW_BLOCK)

    @pl.when(phase == 1)
    def _():
        seg_i = seg_rows.astype(jnp.int32)
        lo = (16.0 * _prefix_over_experts((seg_i >> 4).astype(_f32))
              + _prefix_over_experts((seg_i & 15).astype(_f32)))
        lane = lax.broadcasted_iota(jnp.int32, (N_EXPERTS, PIECE_SLOTS), 1)
        row = (lane * SEG_ALIGN).astype(_f32)
        inside = jnp.logical_and(row >= lo, row < lo + seg_rows)
        dst = jnp.sum(jnp.where(inside, base_ref[...] + (row - lo), 0.0), axis=0, keepdims=True)
        total = jnp.sum(seg_rows, axis=0, keepdims=True)
        piece_ref[0] = jnp.where(lane[:1] == PIECE_SLOTS - 1, total, dst).astype(jnp.int32)
        base_ref[...] += seg_rows


def moe_rank(top_idx):
    k, t = top_idx.shape
    tm = TOKEN_TILE
    assert _tile_buffer_rows(tm) // SEG_ALIGN < PIECE_SLOTS
    return pl.pallas_call(
        _rank_kernel,
        grid=(2, t // tm),
        in_specs=[pl.BlockSpec((k, tm), lambda p, i: (0, i))],
        out_specs=[pl.BlockSpec((1, 1, PIECE_SLOTS), lambda p, i: (i * p, 0, 0)),
                   pl.BlockSpec((N_EXPERTS, 1), lambda p, i: (0, 0))],
        out_shape=[jax.ShapeDtypeStruct((t // tm, 1, PIECE_SLOTS), jnp.int32),
                   jax.ShapeDtypeStruct((N_EXPERTS, 1), jnp.int32)],
        scratch_shapes=[pltpu.VMEM((N_EXPERTS, 1), _f32)],
        compiler_params=_cparams("arbitrary", "arbitrary"),
        name="moe_rank",
    )(top_idx)


def _aligned_rows(ref, start, n):
    return ref.at[pl.ds(pl.multiple_of(start, SEG_ALIGN), n)]


def _tile_rows(piece_ref):
    return piece_ref[0, 0, PIECE_SLOTS - 1]


def _for_each_piece(piece_ref, fn):
    def piece(g, c):
        fn(g * SEG_ALIGN, piece_ref[0, 0, g])
        return c

    lax.fori_loop(0, _tile_rows(piece_ref) // SEG_ALIGN, piece, 0)


def _wait_pieces(n_pieces, make_copy):
    for bit in reversed(range(PIECE_SLOTS.bit_length())):
        @pl.when((n_pieces >> bit) & 1 == 1)
        def _(bit=bit):
            make_copy(1 << bit).wait()


def _prefix_over_lanes(row):
    n = row.shape[1]
    wide = jnp.broadcast_to(row, (SEG_ALIGN, n)).astype(_bf16)
    return _dot_nt(wide, _strict_lower(n))[:1, :]


def _tile_layout(idx_t):
    tm = idx_t.shape[0]
    lane_e = lax.broadcasted_iota(jnp.int32, (tm, N_EXPERTS), 1)
    member = jnp.zeros((tm, N_EXPERTS), _f32)
    for k in range(TOP_K):
        member = jnp.where(lane_e == idx_t[:, k:k + 1], 1.0, member)
    before = _dot(_strict_lower(tm), member.astype(_bf16))
    count = jnp.sum(member, axis=0, keepdims=True).astype(jnp.int32)
    seg_rows = (count + (SEG_ALIGN - 1)) & (-SEG_ALIGN)
    seg_lo = (16.0 * _prefix_over_lanes((seg_rows >> 4).astype(_f32))
              + _prefix_over_lanes((seg_rows & 15).astype(_f32)))
    return member, before, seg_lo, seg_lo + seg_rows.astype(_f32)


def _rows_of_chunk(r0, seg_lo, seg_hi):
    row = (lax.broadcasted_iota(jnp.int32, (PERM_ROWS, N_EXPERTS), 0) + r0).astype(_f32)
    inside = jnp.logical_and(row >= seg_lo, row < seg_hi)
    rank = jnp.sum(jnp.where(inside, row - seg_lo, 0.0), axis=1, keepdims=True)
    return jnp.where(inside, 1.0, 0.0).astype(_bf16), rank


def _dispatch_kernel(pad_from_ref, pad_n_ref, nb_ref, piece_ref, idx_ref, hb_ref, xs_ref,
                     xbuf_ref, zero_ref, sent_ref, sem, zsem):
    tm = hb_ref.shape[0]
    n_blocks = xs_ref.shape[0] // ROW_BLOCK

    @pl.when(pl.program_id(0) == 0)
    def _():
        zero_ref[...] = jnp.zeros_like(zero_ref)

        def pad_expert(e, carry):
            off = pad_from_ref[e]

            def pad_piece(r, c):
                pltpu.make_async_copy(_aligned_rows(zero_ref, 0, SEG_ALIGN),
                                      _aligned_rows(xs_ref, off + SEG_ALIGN * r, SEG_ALIGN),
                                      zsem).start()
                return c + 1

            return lax.fori_loop(0, pad_n_ref[e] // SEG_ALIGN, pad_piece, carry)

        n_pieces = lax.fori_loop(0, N_EXPERTS, pad_expert, 0)

        def pad_tail(b, carry):
            pltpu.make_async_copy(zero_ref, xs_ref.at[pl.ds(b * ROW_BLOCK, ROW_BLOCK)],
                                  zsem).start()
            return carry

        lax.fori_loop(nb_ref[0], n_blocks, pad_tail, 0)

        def wait_piece(r, c):
            pltpu.make_async_copy(_aligned_rows(zero_ref, 0, SEG_ALIGN),
                                  _aligned_rows(xs_ref, 0, SEG_ALIGN), zsem).wait()
            return c

        def wait_tail(b, c):
            pltpu.make_async_copy(zero_ref, xs_ref.at[pl.ds(0, ROW_BLOCK)], zsem).wait()
            return c

        lax.fori_loop(0, n_pieces, wait_piece, 0)
        lax.fori_loop(nb_ref[0], n_blocks, wait_tail, 0)

    step = pl.program_id(0)
    slot = step % 2
    total = _tile_rows(piece_ref)
    member, before, seg_lo, seg_hi = _tile_layout(idx_ref[...])
    place = jnp.where(member > 0.0, before + 1.0, -1.0).astype(_bf16)

    def permute(c, carry):
        r0 = c * PERM_ROWS
        expert_of_row, rank_of_row = _rows_of_chunk(r0, seg_lo, seg_hi)
        row = lax.broadcasted_iota(jnp.int32, rank_of_row.shape, 0) + r0
        want = jnp.where(row < total, rank_of_row + 1.0, -5.0)
        sel = jnp.where(_dot_nt(expert_of_row, place) == want, 1.0, 0.0)
        picked = _dot(sel.astype(_bf16), hb_ref[...])
        half = picked.shape[1] // 2
        xbuf_ref[slot, pl.ds(r0, PERM_ROWS), :] = (
            (pltpu.bitcast(picked[:, :half], jnp.uint32) >> 16)
            | pltpu.bitcast(picked[:, half:], jnp.uint32))
        return carry

    for c in range(xbuf_ref.shape[1] // PERM_ROWS):
        permute(c, 0)

    wait_sent = lambda n_pieces: _wait_pieces(n_pieces, lambda k: pltpu.make_async_copy(
        xbuf_ref.at[slot, pl.ds(0, k * SEG_ALIGN)], xs_ref.at[pl.ds(0, k * SEG_ALIGN)], sem))

    @pl.when(step > 0)
    def _():
        wait_sent(sent_ref[0])

    for s in range(2):
        @pl.when(slot == s)
        def _(s=s):
            def send(off, dst):
                pltpu.make_async_copy(_aligned_rows(xbuf_ref.at[s], off, SEG_ALIGN),
                                      _aligned_rows(xs_ref, dst, SEG_ALIGN), sem).start()

            _for_each_piece(piece_ref, send)

    sent_ref[0] = total // SEG_ALIGN

    @pl.when(step == pl.num_programs(0) - 1)
    def _():
        wait_sent(total // SEG_ALIGN)


def _tile_buffer_rows(tm):
    worst = tm * TOP_K + N_EXPERTS * (SEG_ALIGN - 1)
    return -(-worst // PERM_ROWS) * PERM_ROWS


def moe_dispatch(pieces, idx_t, hb, pad_from, pad_n, n_used, n_blocks):
    t, d = hb.shape
    tm = TOKEN_TILE
    grid_spec = pltpu.PrefetchScalarGridSpec(
        num_scalar_prefetch=3,
        grid=(t // tm,),
        in_specs=[
            pl.BlockSpec((1, 1, PIECE_SLOTS), lambda i, *_: (i, 0, 0), memory_space=pltpu.SMEM),
            pl.BlockSpec((tm, TOP_K), lambda i, *_: (i, 0)),
            pl.BlockSpec((tm, d), lambda i, *_: (i, 0)),
        ],
        out_specs=pl.BlockSpec(memory_space=pl.ANY),
        scratch_shapes=[pltpu.VMEM((2, _tile_buffer_rows(tm), d // 2), jnp.uint32),
                        pltpu.VMEM((ROW_BLOCK, d // 2), jnp.uint32),
                        pltpu.SMEM((1,), jnp.int32),
                        pltpu.SemaphoreType.DMA(()), pltpu.SemaphoreType.DMA(())],
    )
    return pl.pallas_call(
        _dispatch_kernel,
        grid_spec=grid_spec,
        out_shape=jax.ShapeDtypeStruct((n_blocks * ROW_BLOCK, d // 2), jnp.uint32),
        compiler_params=_cparams("arbitrary"),
        name="moe_dispatch",
    )(pad_from, pad_n, n_used, pieces, idx_t, hb)


def _expert_kernel(be_ref, nb_ref, xs_ref, w1_ref, w3_ref, w2_ref, y_ref, w1b, w3b, w2b):
    i = pl.program_id(0)

    @pl.when(jnp.logical_or(i == 0, be_ref[i] != be_ref[jnp.maximum(i - 1, 0)]))
    def _():
        w1b[...] = w1_ref[0, 0].astype(_bf16)
        w3b[...] = w3_ref[0, 0].astype(_bf16)
        w2b[...] = w2_ref[0, 0].astype(_bf16)

    @pl.when(i < nb_ref[0])
    def _():
        chains = [pl.ds(c * EXPERT_ROWS, EXPERT_ROWS) for c in range(ROW_BLOCK // EXPERT_ROWS)]
        xs = [jnp.concatenate(_unpack_rows(xs_ref[rows, :]), axis=1) for rows in chains]
        acts = [(_silu(_dot(x, w1b[...])) * _dot(x, w3b[...])).astype(_bf16) for x in xs]
        for rows, act in zip(chains, acts):
            y_ref[rows, :] = _pack_rows(_dot(act, w2b[...]))

    @pl.when(i >= nb_ref[0])
    def _():
        y_ref[...] = jnp.zeros_like(y_ref)


def moe_experts(xs, blk_e, n_used, w1, w3, w2, layer):
    p, half = xs.shape
    d = 2 * half
    ff = w1.shape[3]
    nb = p // ROW_BLOCK
    row = lambda i, be, nu: (jnp.minimum(i, nu[0] - 1), 0)
    wsel = lambda i, be, nu: (layer, be[i], 0, 0)
    grid_spec = pltpu.PrefetchScalarGridSpec(
        num_scalar_prefetch=2,
        grid=(nb,),
        in_specs=[
            pl.BlockSpec((ROW_BLOCK, half), row),
            pl.BlockSpec((1, 1, d, ff), wsel),
            pl.BlockSpec((1, 1, d, ff), wsel),
            pl.BlockSpec((1, 1, ff, d), wsel),
        ],
        out_specs=pl.BlockSpec((ROW_BLOCK, half), lambda i, be, nu: (i, 0)),
        scratch_shapes=[pltpu.VMEM((d, ff), _bf16), pltpu.VMEM((d, ff), _bf16),
                        pltpu.VMEM((ff, d), _bf16)],
    )
    return pl.pallas_call(
        _expert_kernel,
        grid_spec=grid_spec,
        out_shape=jax.ShapeDtypeStruct((p, half), jnp.uint32),
        compiler_params=_cparams("arbitrary"),
        name="moe_experts",
    )(blk_e, n_used, xs, w1, w3, w2)


def _combine_kernel(piece_ref, next_piece_ref, ys_ref, idx_ref, w_ref, shared_ref, x_ref, gate_ref,
                    fn_ref, o_ref, ybuf_ref, acc_ref, sem, *, final_norm):
    tm = x_ref.shape[1]
    half = ybuf_ref.shape[2]
    step = pl.program_id(0) * pl.num_programs(1) + pl.program_id(1)
    n_steps = pl.num_programs(0) * pl.num_programs(1)
    slot = step % 2

    def fetch(table_ref, s):
        def one(off, src):
            pltpu.make_async_copy(_aligned_rows(ys_ref, src, SEG_ALIGN),
                                  _aligned_rows(ybuf_ref.at[s], off, SEG_ALIGN), sem.at[s]).start()

        _for_each_piece(table_ref, one)

    @pl.when(step == 0)
    def _():
        fetch(piece_ref, 0)

    for s in range(2):
        @pl.when(jnp.logical_and(step + 1 < n_steps, slot == 1 - s))
        def _(s=s):
            fetch(next_piece_ref, s)

    total = _tile_rows(piece_ref)
    _wait_pieces(total // SEG_ALIGN, lambda k: pltpu.make_async_copy(
        ys_ref.at[pl.ds(0, k * SEG_ALIGN)], ybuf_ref.at[slot, pl.ds(0, k * SEG_ALIGN)],
        sem.at[slot]))

    idx_t = idx_ref[...]
    member, before, seg_lo, seg_hi = _tile_layout(idx_t)
    before_b = before.astype(_bf16)
    lane_e = lax.broadcasted_iota(jnp.int32, member.shape, 1)
    w_of_expert = jnp.zeros(member.shape, _f32)
    for k in range(TOP_K):
        w_of_expert = jnp.where(lane_e == idx_t[:, k:k + 1], w_ref[:, k:k + 1], w_of_expert)
    w_of_expert = w_of_expert.astype(_bf16)
    lo8 = jnp.broadcast_to(seg_lo, (SEG_ALIGN, N_EXPERTS)).astype(jnp.int32) >> 3
    lo_head, lo_rest = (lo8 >> 4).astype(_f32).astype(_bf16), (lo8 & 15).astype(_f32).astype(_bf16)
    acc_ref[...] = shared_ref[...]

    def reduce(c, carry):
        r0 = c * PERM_ROWS
        packed = ybuf_ref[slot, pl.ds(r0, PERM_ROWS), :]
        row = lax.broadcasted_iota(jnp.int32, packed.shape, 0) + r0
        packed = jnp.where(row < total, packed, jnp.uint32(0))
        y_lo, y_hi = _unpack_rows(packed)
        expert_of_row, rank_of_row = _rows_of_chunk(r0, seg_lo, seg_hi)
        del rank_of_row
        start = (128.0 * _dot_nt(lo_head, expert_of_row) + 8.0 * _dot_nt(lo_rest, expert_of_row))
        rank_row = (lax.broadcasted_iota(jnp.int32, (1, PERM_ROWS), 1) + r0).astype(_f32) - start[:1]
        hit = _dot_nt(before_b, expert_of_row) == rank_row
        wm = jnp.where(hit, _dot_nt(w_of_expert, expert_of_row), 0.0).astype(_bf16)
        acc_ref[:, :half] += _dot(wm, y_lo)
        acc_ref[:, half:] += _dot(wm, y_hi)
        return carry

    for c in range(ybuf_ref.shape[1] // PERM_ROWS):
        reduce(c, 0)

    xn = x_ref[0] + gate_ref[0] * acc_ref[...]
    if final_norm:
        xn = xn * lax.rsqrt(jnp.mean(xn * xn, axis=-1, keepdims=True) + EPS) * fn_ref[...]
    o_ref[0] = xn


def moe_combine(pieces, idx_t, ys, top_w, shared, x, gate, fn_g, final_norm):
    b, s, d = x.shape
    tm = TOKEN_TILE
    n_s = s // tm
    tile = lambda i, m: (i * n_s + m, 0)
    last_tile = b * n_s - 1
    return pl.pallas_call(
        functools.partial(_combine_kernel, final_norm=final_norm),
        grid=(b, n_s),
        in_specs=[
            pl.BlockSpec((1, 1, PIECE_SLOTS), lambda i, m: (i * n_s + m, 0, 0),
                         memory_space=pltpu.SMEM),
            pl.BlockSpec((1, 1, PIECE_SLOTS),
                         lambda i, m: (jnp.minimum(i * n_s + m + 1, last_tile), 0, 0),
                         memory_space=pltpu.SMEM),
            pl.BlockSpec(memory_space=pl.ANY),
            pl.BlockSpec((tm, TOP_K), tile),
            pl.BlockSpec((tm, TOP_K), tile),
            pl.BlockSpec((tm, d), tile),
            pl.BlockSpec((1, tm, d), lambda i, m: (i, m, 0)),
            pl.BlockSpec((1, 1, d), lambda i, m: (i, 0, 0)),
            pl.BlockSpec((1, d), lambda i, m: (0, 0)),
        ],
        out_specs=pl.BlockSpec((1, tm, d), lambda i, m: (i, m, 0)),
        out_shape=jax.ShapeDtypeStruct((b, s, d), _f32),
        scratch_shapes=[pltpu.VMEM((2, _tile_buffer_rows(tm), d // 2), jnp.uint32),
                        pltpu.VMEM((tm, d), _f32), pltpu.SemaphoreType.DMA((2,))],
        compiler_params=_cparams("arbitrary", "arbitrary"),
        name="moe_combine",
    )(pieces, pieces, ys, idx_t, top_w, shared, x, gate, fn_g.reshape(1, d))


def moe_layer(x, g, sc, sh, gate, router_w, router_b, w1, w3, w2, layer, s_w1, s_w3, s_w2,
              fn_g, final_norm):
    b, s, d = x.shape
    t = b * s
    hb, top_idx, top_w, shared = moe_router(x, g, sc, sh, router_w, router_b, s_w1, s_w3, s_w2)
    pieces, rows = moe_rank(top_idx)
    idx_t = top_idx.T
    n_tiles = pieces.shape[0]

    rows = rows.reshape(N_EXPERTS)
    nblk = (rows + ROW_BLOCK - 1) // ROW_BLOCK
    blk_ends = jnp.cumsum(nblk)
    max_rows = t * TOP_K + n_tiles * N_EXPERTS * (SEG_ALIGN - 1)
    n_blocks = -(-max_rows // ROW_BLOCK) + N_EXPERTS
    blk = jnp.arange(n_blocks, dtype=jnp.int32)
    blk_e = jnp.minimum(jnp.sum(blk[:, None] >= blk_ends[None, :], axis=1),
                        N_EXPERTS - 1).astype(jnp.int32)
    n_used = blk_ends[-1:].astype(jnp.int32)
    pad_from = ((blk_ends - nblk) * ROW_BLOCK + rows).astype(jnp.int32)
    pad_n = (nblk * ROW_BLOCK - rows).astype(jnp.int32)

    xs = moe_dispatch(pieces, idx_t, hb, pad_from, pad_n, n_used, n_blocks)
    ys = moe_experts(xs, blk_e, n_used, w1, w3, w2, layer)
    return moe_combine(pieces, idx_t, ys, top_w.T, shared, x, gate, fn_g, final_norm)


def kernel(x, c, norm_mix, norm_ffn, ada_w, ada_b, da_w_in, da_w_out, da_lq1, da_lk1, da_lq2,
           da_lk2, da_subln, ret_w_in, ret_w_out, ret_gn, router_w, router_b, exp_w1, exp_w3,
           exp_w2, sh_w1, sh_w3, sh_w2, final_norm):
    depth, d = norm_mix.shape
    b = x.shape[0]
    mod = ada_modulation(c, ada_w, ada_b).reshape(depth, b, 6, 1, d)
    for i in range(depth):
        sh_a, sc_a, g_a, sh_f, sc_f, g_f = (mod[i, :, j] for j in range(6))
        j = i // 2
        if i % 2 == 0:
            proj = norm_proj(x, norm_mix[i], sc_a, sh_a, da_w_in[j].astype(_bf16))
            y = diff_attention_core(proj, da_lq1[j], da_lk1[j], da_lq2[j], da_lk2[j],
                                    da_subln[j], i)
            x = out_proj_residual(y, da_w_out[j].astype(_bf16), x, g_a)
        else:
            proj = norm_proj(x, norm_mix[i], sc_a, sh_a, ret_w_in[j].astype(_bf16))
            y = retention_core(proj, ret_gn[j])
            x = out_proj_residual(y, ret_w_out[j].astype(_bf16), x, g_a)
        x = moe_layer(x, norm_ffn[i], sc_f, sh_f, g_f, router_w[i], router_b[i],
                      exp_w1, exp_w3, exp_w2, i,
                      sh_w1[i], sh_w3[i], sh_w2[i], final_norm, i == depth - 1)
    return x
```

```python
import functools
import math

import jax
import jax.numpy as jnp
from jax import lax
from jax.experimental import pallas as pl
from jax.experimental.pallas import tpu as pltpu

EPS = 1e-6
DA_HEADS = 8
DA_HEAD_DIM = 64
DA_V_DIM = 2 * DA_HEAD_DIM
RET_HEADS = 4
RET_DK = 256
RET_DV = 512
RET_CHUNK = 128
N_EXPERTS = 64
TOP_K = 8
N_GROUPS = 8
GROUP_SIZE = N_EXPERTS // N_GROUPS
TOPK_GROUPS = 4
ROUTED_SCALE = 2.5
ROW_BLOCK_LOG2 = 10
ROW_BLOCK = 1 << ROW_BLOCK_LOG2
EXPERT_ROWS = 256
TOKEN_TILE = 256
SEG_ALIGN = 8
PERM_ROWS = 512
PIECE_SLOTS = 384

VMEM_LIMIT = 56 * 1024 * 1024
NEG_BIG = -1e30

_f32 = jnp.float32
_bf16 = jnp.bfloat16


def _cparams(*sem):
    return pltpu.CompilerParams(dimension_semantics=sem, vmem_limit_bytes=VMEM_LIMIT)


def _silu(v):
    return v * (1.0 / (1.0 + jnp.exp(-v)))


def _dot(a, b):
    return jnp.dot(a, b, preferred_element_type=_f32)


def _dot_nt(a, b):
    return lax.dot_general(a, b, (((1,), (1,)), ((), ())), preferred_element_type=_f32)


def _dot_tn(a, b):
    return lax.dot_general(a, b, (((0,), (0,)), ((), ())), preferred_element_type=_f32)


def _ada_kernel(c_ref, w_ref, b_ref, o_ref):
    a = _silu(c_ref[...]).astype(_bf16)
    o_ref[0] = _dot(a, w_ref[0].astype(_bf16)) + b_ref[0]


def ada_modulation(c, ada_w, ada_b):
    depth, d, n = ada_w.shape
    b = c.shape[0]
    tn = 1024
    return pl.pallas_call(
        _ada_kernel,
        grid=(depth, n // tn),
        in_specs=[
            pl.BlockSpec((b, d), lambda l, j: (0, 0)),
            pl.BlockSpec((1, d, tn), lambda l, j: (l, 0, j)),
            pl.BlockSpec((1, 1, tn), lambda l, j: (l, 0, j)),
        ],
        out_specs=pl.BlockSpec((1, b, tn), lambda l, j: (l, 0, j)),
        out_shape=jax.ShapeDtypeStruct((depth, b, n), _f32),
        compiler_params=_cparams("parallel", "parallel"),
        name="ada_modulation",
    )(c, ada_w, ada_b.reshape(depth, 1, n))


def _norm_mod(x, g, sc, sh):
    y = x * lax.rsqrt(jnp.mean(x * x, axis=-1, keepdims=True) + EPS)
    return (y * g) * (1.0 + sc) + sh


def _norm_proj_kernel(x_ref, g_ref, sc_ref, sh_ref, w_ref, o_ref, h_ref):
    @pl.when(pl.program_id(2) == 0)
    def _():
        h_ref[...] = _norm_mod(x_ref[0], g_ref[...], sc_ref[0], sh_ref[0]).astype(_bf16)

    o_ref[0] = _dot(h_ref[...], w_ref[...]).astype(o_ref.dtype)


def norm_proj(x, g, sc, sh, w):
    b, s, d = x.shape
    n = w.shape[1]
    tm = min(1024, s)
    tn = 1024
    return pl.pallas_call(
        _norm_proj_kernel,
        grid=(b, s // tm, n // tn),
        in_specs=[
            pl.BlockSpec((1, tm, d), lambda i, m, j: (i, m, 0)),
            pl.BlockSpec((1, d), lambda i, m, j: (0, 0)),
            pl.BlockSpec((1, 1, d), lambda i, m, j: (i, 0, 0)),
            pl.BlockSpec((1, 1, d), lambda i, m, j: (i, 0, 0)),
            pl.BlockSpec((d, tn), lambda i, m, j: (0, j)),
        ],
        out_specs=pl.BlockSpec((1, tm, tn), lambda i, m, j: (i, m, j)),
        out_shape=jax.ShapeDtypeStruct((b, s, n), _bf16),
        scratch_shapes=[pltpu.VMEM((tm, d), _bf16)],
        compiler_params=_cparams("parallel", "parallel", "arbitrary"),
        name="norm_proj",
    )(x, g.reshape(1, d), sc, sh, w)


def _out_proj_kernel(y_ref, w_ref, x_ref, g_ref, o_ref):
    o_ref[0] = x_ref[0] + g_ref[0] * _dot(y_ref[0], w_ref[...])


def out_proj_residual(y, w, x, gate):
    b, s, k = y.shape
    d = w.shape[1]
    tm = min(512, s)
    return pl.pallas_call(
        _out_proj_kernel,
        grid=(b, s // tm),
        in_specs=[
            pl.BlockSpec((1, tm, k), lambda i, m: (i, m, 0)),
            pl.BlockSpec((k, d), lambda i, m: (0, 0)),
            pl.BlockSpec((1, tm, d), lambda i, m: (i, m, 0)),
            pl.BlockSpec((1, 1, d), lambda i, m: (i, 0, 0)),
        ],
        out_specs=pl.BlockSpec((1, tm, d), lambda i, m: (i, m, 0)),
        out_shape=jax.ShapeDtypeStruct((b, s, d), _f32),
        compiler_params=_cparams("parallel", "parallel"),
        name="out_proj_residual",
    )(y, w, x, gate)


def _diff_attn_kernel(q_ref, k_ref, v_ref, lq1_ref, lk1_ref, lq2_ref, lk2_ref, sg_ref, o_ref,
                      bias_ref, *, tq, lam_init):
    h = pl.program_id(1)
    qi = pl.program_id(2)
    dh, dv = DA_HEAD_DIM, DA_V_DIM

    q = q_ref[0] * (dh ** -0.5)
    lane = lax.broadcasted_iota(jnp.int32, q.shape, 1)
    zero = jnp.zeros_like(q)
    q_maps = (jnp.where(lane < dh, q, zero), jnp.where(lane >= dh, q, zero))

    n_q = k_ref.shape[1] // tq

    @pl.when(qi == 0)
    def _():
        hp1 = jnp.full((1, 1), h + 1, jnp.int32).astype(_f32)
        slope = jnp.exp2(hp1 * (-8.0 / DA_HEADS))
        dist = ((n_q - 1) * tq + lax.broadcasted_iota(jnp.int32, bias_ref.shape, 0)
                - lax.broadcasted_iota(jnp.int32, bias_ref.shape, 1)).astype(_f32)
        bias_ref[...] = -slope * dist

    lam = (jnp.exp(jnp.sum(lq1_ref[...] * lk1_ref[...], axis=-1, keepdims=True))
           - jnp.exp(jnp.sum(lq2_ref[...] * lk2_ref[...], axis=-1, keepdims=True)) + lam_init)
    causal = (lax.broadcasted_iota(jnp.int32, (tq, tq), 0)
              >= lax.broadcasted_iota(jnp.int32, (tq, tq), 1))

    def attend(c):
        first = (n_q - 1 - c) * tq
        blocks = [slice(j * tq, (j + 1) * tq) for j in range(c + 1)]
        tiles = [[_dot_nt(q_m, k_ref[0, blk, :])
                  + bias_ref[:, first + blk.start:first + blk.stop] for q_m in q_maps]
                 for blk in blocks]
        tiles[c] = [jnp.where(causal, s, NEG_BIG) for s in tiles[c]]
        outs = []
        for mi in range(2):
            parts = [row[mi] for row in tiles]
            m = functools.reduce(jnp.maximum, [jnp.max(x, axis=-1, keepdims=True) for x in parts])
            ps = [jnp.exp(x - m) for x in parts]
            l = functools.reduce(jnp.add, [jnp.sum(x, axis=-1, keepdims=True) for x in ps])
            o = functools.reduce(jnp.add, [_dot(x.astype(_bf16), v_ref[0, blk, :])
                                           for x, blk in zip(ps, blocks)])
            outs.append(o / l)
        o = outs[0] - lam * outs[1]
        o = o * lax.rsqrt(jnp.mean(o * o, axis=-1, keepdims=True) + EPS)
        o_ref[0] = ((o * sg_ref[...]) * (1.0 - lam_init)).astype(o_ref.dtype)

    for c in range(n_q):
        pl.when(qi == c)(functools.partial(attend, c))


def diff_attention_core(proj, lq1, lk1, lq2, lk2, subln_g, layer_idx):
    b, s, _ = proj.shape
    nh, dh, dv = DA_HEADS, DA_HEAD_DIM, DA_V_DIM
    tq = min(512, s)
    lam_init = 0.8 - 0.6 * math.exp(-0.3 * layer_idx)
    vec = lambda n: pl.BlockSpec((1, n), lambda i, h, m: (0, 0))
    return pl.pallas_call(
        functools.partial(_diff_attn_kernel, tq=tq, lam_init=lam_init),
        grid=(b, nh, s // tq),
        in_specs=[
            pl.BlockSpec((1, tq, 2 * dh), lambda i, h, m: (i, m, h)),
            pl.BlockSpec((1, s, 2 * dh), lambda i, h, m: (i, 0, nh + h)),
            pl.BlockSpec((1, s, dv), lambda i, h, m: (i, 0, 2 * nh + h)),
            vec(dh), vec(dh), vec(dh), vec(dh), vec(dv),
        ],
        out_specs=pl.BlockSpec((1, tq, dv), lambda i, h, m: (i, m, h)),
        out_shape=jax.ShapeDtypeStruct((b, s, nh * dv), _bf16),
        scratch_shapes=[pltpu.VMEM((tq, s), _f32)],
        compiler_params=_cparams("parallel", "parallel", "arbitrary"),
        name="diff_attention",
    )(proj, proj, proj, lq1.reshape(1, dh), lk1.reshape(1, dh), lq2.reshape(1, dh),
      lk2.reshape(1, dh), subln_g.reshape(1, dv))


def _retention_kernel(q_ref, k_ref, v_ref, g_ref, gn_ref, o_ref, r_ref, *, chunk):
    h = pl.program_id(1)
    c = pl.program_id(2)
    dk = RET_DK

    @pl.when(c == 0)
    def _():
        r_ref[...] = jnp.zeros_like(r_ref)

    hf = jnp.full((1, 1), h, jnp.int32).astype(_f32)
    log_g = jnp.log(1.0 - jnp.exp2(-5.0 - hf))
    ri = lax.broadcasted_iota(jnp.int32, (chunk, chunk), 0)
    ci = lax.broadcasted_iota(jnp.int32, (chunk, chunk), 1)
    diff = (ri - ci).astype(_f32)
    decay = jnp.where(diff >= 0, jnp.exp(log_g * jnp.maximum(diff, 0.0)), 0.0)
    idx = lax.broadcasted_iota(jnp.int32, (chunk, 1), 0).astype(_f32)
    zeta = jnp.exp(log_g * (chunk - 1 - idx))
    xi = jnp.exp(log_g * (idx + 1.0))
    g_c = jnp.exp(log_g * chunk)

    decay = decay * (dk ** -0.5)
    zeta = zeta * (dk ** -0.5)
    chunks = [pl.ds(sub * chunk, chunk) for sub in range(q_ref.shape[1] // chunk)]
    intra, updates = [], []
    for rows in chunks:
        q, k, v = q_ref[0, rows, :], k_ref[0, rows, :], v_ref[0, rows, :]
        intra.append(_dot((_dot_nt(q, k) * decay).astype(_bf16), v))
        updates.append(_dot_tn((k.astype(_f32) * zeta).astype(_bf16), v))
    r = r_ref[...]
    for rows, o, upd in zip(chunks, intra, updates):
        o = o + _dot(q_ref[0, rows, :], r.astype(_bf16)) * xi
        r = g_c * r + upd
        o = o * lax.rsqrt(jnp.mean(o * o, axis=-1, keepdims=True) + EPS) * gn_ref[...]
        o_ref[0, rows, :] = (_silu(g_ref[0, rows, :].astype(_f32)) * o).astype(o_ref.dtype)
    r_ref[...] = r


def retention_core(proj, gn_g):
    b, s, _ = proj.shape
    nh, dk, dv = RET_HEADS, RET_DK, RET_DV
    chunk = RET_CHUNK
    rows = min(8 * chunk, s)
    return pl.pallas_call(
        functools.partial(_retention_kernel, chunk=chunk),
        grid=(b, nh, s // rows),
        in_specs=[
            pl.BlockSpec((1, rows, dk), lambda i, h, c: (i, c, h)),
            pl.BlockSpec((1, rows, dk), lambda i, h, c: (i, c, nh + h)),
            pl.BlockSpec((1, rows, dv), lambda i, h, c: (i, c, (2 * nh * dk) // dv + h)),
            pl.BlockSpec((1, rows, dv), lambda i, h, c: (i, c, (2 * nh * dk) // dv + nh + h)),
            pl.BlockSpec((1, dv), lambda i, h, c: (0, h)),
        ],
        out_specs=pl.BlockSpec((1, rows, dv), lambda i, h, c: (i, c, h)),
        out_shape=jax.ShapeDtypeStruct((b, s, nh * dv), _bf16),
        scratch_shapes=[pltpu.VMEM((dk, dv), _f32)],
        compiler_params=_cparams("parallel", "parallel", "arbitrary"),
        name="retention",
    )(proj, proj, proj, proj, gn_g.reshape(1, nh * dv))


def _pack_rows(h):
    n = h.shape[1] // 2
    lo = pltpu.bitcast(h[:, :n].astype(_bf16).astype(_f32), jnp.uint32)
    hi = pltpu.bitcast(h[:, n:].astype(_bf16).astype(_f32), jnp.uint32)
    return (lo >> 16) | (hi & jnp.uint32(0xFFFF0000))


def _unpack_rows(p):
    lo = pltpu.bitcast(p << 16, _f32).astype(_bf16)
    hi = pltpu.bitcast(p & jnp.uint32(0xFFFF0000), _f32).astype(_bf16)
    return lo, hi


def _first_argmax(v, iota, axis, big):
    m = jnp.max(v, axis=axis, keepdims=True)
    i = jnp.min(jnp.where(v == m, iota, big), axis=axis, keepdims=True)
    return m, i


def _router_kernel(x_ref, g_ref, sc_ref, sh_ref, rwt_ref, rb_ref, s1_ref, s3_ref, s2_ref,
                   hb_ref, idx_ref, wgt_ref, shared_ref):
    tm = x_ref.shape[1]
    h = _norm_mod(x_ref[0], g_ref[...], sc_ref[0], sh_ref[0])
    hb = h.astype(_bf16)
    hb_ref[...] = hb
    act = (_silu(_dot(hb, s1_ref[...])) * _dot(hb, s3_ref[...])).astype(_bf16)
    shared_ref[...] = _dot(act, s2_ref[...])

    scores = 1.0 / (1.0 + jnp.exp(-_dot_nt(rwt_ref[...], h)))
    choice = scores + rb_ref[...]
    c3 = choice.reshape(N_GROUPS, GROUP_SIZE, tm)
    mem = lax.broadcasted_iota(jnp.int32, c3.shape, 1)
    m1, i1 = _first_argmax(c3, mem, 1, GROUP_SIZE)
    m2 = jnp.max(jnp.where(mem == i1, -jnp.inf, c3), axis=1, keepdims=True)
    gs = m1 + m2
    gi = lax.broadcasted_iota(jnp.int32, gs.shape, 0)
    sel = jnp.zeros(gs.shape, _f32)
    for _ in range(TOPK_GROUPS):
        _, ig = _first_argmax(gs, gi, 0, N_GROUPS)
        hit = gi == ig
        sel = jnp.where(hit, 1.0, sel)
        gs = jnp.where(hit, -jnp.inf, gs)
    cm = jnp.where(sel > 0.0, c3, -jnp.inf).reshape(N_EXPERTS, tm)
    ei = lax.broadcasted_iota(jnp.int32, cm.shape, 0)
    idxs, wgts = [], []
    for _ in range(TOP_K):
        _, ie = _first_argmax(cm, ei, 0, N_EXPERTS)
        hit = ei == ie
        idxs.append(ie)
        wgts.append(jnp.sum(jnp.where(hit, scores, 0.0), axis=0, keepdims=True))
        cm = jnp.where(hit, -jnp.inf, cm)
    w = jnp.concatenate(wgts, axis=0)
    idx_ref[...] = jnp.concatenate(idxs, axis=0)
    wgt_ref[...] = w / jnp.sum(w, axis=0, keepdims=True) * ROUTED_SCALE


def moe_router(x, g, sc, sh, router_w, router_b, s_w1, s_w3, s_w2):
    b, s, d = x.shape
    t = b * s
    tm = min(256, s)
    n_s = s // tm
    sf = s_w1.shape[1]
    const = lambda shape: pl.BlockSpec(shape, lambda i, m: (0,) * len(shape))
    return pl.pallas_call(
        _router_kernel,
        grid=(b, n_s),
        in_specs=[
            pl.BlockSpec((1, tm, d), lambda i, m: (i, m, 0)),
            const((1, d)),
            pl.BlockSpec((1, 1, d), lambda i, m: (i, 0, 0)),
            pl.BlockSpec((1, 1, d), lambda i, m: (i, 0, 0)),
            const((N_EXPERTS, d)), const((N_EXPERTS, 1)),
            const((d, sf)), const((d, sf)), const((sf, d)),
        ],
        out_specs=[
            pl.BlockSpec((tm, d), lambda i, m: (i * n_s + m, 0)),
            pl.BlockSpec((TOP_K, tm), lambda i, m: (0, i * n_s + m)),
            pl.BlockSpec((TOP_K, tm), lambda i, m: (0, i * n_s + m)),
            pl.BlockSpec((tm, d), lambda i, m: (i * n_s + m, 0)),
        ],
        out_shape=[
            jax.ShapeDtypeStruct((t, d), _bf16),
            jax.ShapeDtypeStruct((TOP_K, t), jnp.int32),
            jax.ShapeDtypeStruct((TOP_K, t), _f32),
            jax.ShapeDtypeStruct((t, d), _f32),
        ],
        compiler_params=_cparams("parallel", "parallel"),
        name="moe_router",
    )(x, g.reshape(1, d), sc, sh, router_w.T, router_b.reshape(N_EXPERTS, 1),
      s_w1.astype(_bf16), s_w3.astype(_bf16), s_w2.astype(_bf16))


def _strict_lower(n):
    return jnp.where(lax.broadcasted_iota(jnp.int32, (n, n), 1)
                     < lax.broadcasted_iota(jnp.int32, (n, n), 0), 1.0, 0.0).astype(_bf16)


def _prefix_over_experts(col):
    wide = jnp.broadcast_to(col, (N_EXPERTS, 128)).astype(_bf16)
    return _dot(_strict_lower(N_EXPERTS), wide)[:, :1]


def _rank_kernel(idx_ref, piece_ref, tot_ref, base_ref):
    phase = pl.program_id(0)
    i = pl.program_id(1)
    tm = idx_ref.shape[1]

    idx = idx_ref[...]
    ei = lax.broadcasted_iota(jnp.int32, (N_EXPERTS, tm), 0)
    memf = jnp.zeros((N_EXPERTS, tm), _f32)
    for k in range(TOP_K):
        memf = jnp.where(ei == idx[k:k + 1, :], 1.0, memf)
    tile_cnt = jnp.sum(memf, axis=1, keepdims=True).astype(jnp.int32)
    seg_rows = ((tile_cnt + (SEG_ALIGN - 1)) & (-SEG_ALIGN)).astype(_f32)

    @pl.when(jnp.logical_and(phase == 0, i == 0))
    def _():
        base_ref[...] = jnp.zeros_like(base_ref)

    @pl.when(phase == 0)
    def _():
        base_ref[...] += seg_rows

    @pl.when(jnp.logical_and(phase == 1, i == 0))
    def _():
        tot = base_ref[...].astype(jnp.int32)
        tot_ref[...] = tot
        nblk = (tot + (ROW_BLOCK - 1)) >> ROW_BLOCK_LOG2
        start_blk = (32.0 * _prefix_over_experts((nblk >> 5).astype(_f32))
                     + _prefix_over_experts((nblk & 31).astype(_f32)))
        base_ref[...] = start_blk * float(ROW_BLOCK)

    @pl.when(phase == 1)
    def _():
        seg_i = seg_rows.astype(jnp.int32)
        lo = (16.0 * _prefix_over_experts((seg_i >> 4).astype(_f32))
              + _prefix_over_experts((seg_i & 15).astype(_f32)))
        lane = lax.broadcasted_iota(jnp.int32, (N_EXPERTS, PIECE_SLOTS), 1)
        row = (lane * SEG_ALIGN).astype(_f32)
        inside = jnp.logical_and(row >= lo, row < lo + seg_rows)
        dst = jnp.sum(jnp.where(inside, base_ref[...] + (row - lo), 0.0), axis=0, keepdims=True)
        total = jnp.sum(seg_rows, axis=0, keepdims=True)
        piece_ref[0] = jnp.where(lane[:1] == PIECE_SLOTS - 1, total, dst).astype(jnp.int32)
        base_ref[...] += seg_rows


def moe_rank(top_idx):
    k, t = top_idx.shape
    tm = TOKEN_TILE
    assert _tile_buffer_rows(tm) // SEG_ALIGN < PIECE_SLOTS
    return pl.pallas_call(
        _rank_kernel,
        grid=(2, t // tm),
        in_specs=[pl.BlockSpec((k, tm), lambda p, i: (0, i))],
        out_specs=[pl.BlockSpec((1, 1, PIECE_SLOTS), lambda p, i: (i * p, 0, 0)),
                   pl.BlockSpec((N_EXPERTS, 1), lambda p, i: (0, 0))],
        out_shape=[jax.ShapeDtypeStruct((t // tm, 1, PIECE_SLOTS), jnp.int32),
                   jax.ShapeDtypeStruct((N_EXPERTS, 1), jnp.int32)],
        scratch_shapes=[pltpu.VMEM((N_EXPERTS, 1), _f32)],
        compiler_params=_cparams("arbitrary", "arbitrary"),
        name="moe_rank",
    )(top_idx)


def _aligned_rows(ref, start, n):
    return ref.at[pl.ds(pl.multiple_of(start, SEG_ALIGN), n)]


def _tile_rows(piece_ref):
    return piece_ref[0, 0, PIECE_SLOTS - 1]


def _for_each_piece(piece_ref, fn):
    def piece(g, c):
        fn(g * SEG_ALIGN, piece_ref[0, 0, g])
        return c

    lax.fori_loop(0, _tile_rows(piece_ref) // SEG_ALIGN, piece, 0)


def _wait_pieces(n_pieces, make_copy):
    for bit in reversed(range(PIECE_SLOTS.bit_length())):
        @pl.when((n_pieces >> bit) & 1 == 1)
        def _(bit=bit):
            make_copy(1 << bit).wait()


def _prefix_over_lanes(row):
    n = row.shape[1]
    wide = jnp.broadcast_to(row, (SEG_ALIGN, n)).astype(_bf16)
    return _dot_nt(wide, _strict_lower(n))[:1, :]


def _tile_layout(idx_t):
    tm = idx_t.shape[0]
    lane_e = lax.broadcasted_iota(jnp.int32, (tm, N_EXPERTS), 1)
    member = jnp.zeros((tm, N_EXPERTS), _f32)
    for k in range(TOP_K):
        member = jnp.where(lane_e == idx_t[:, k:k + 1], 1.0, member)
    before = _dot(_strict_lower(tm), member.astype(_bf16))
    count = jnp.sum(member, axis=0, keepdims=True).astype(jnp.int32)
    seg_rows = (count + (SEG_ALIGN - 1)) & (-SEG_ALIGN)
    seg_lo = (16.0 * _prefix_over_lanes((seg_rows >> 4).astype(_f32))
              + _prefix_over_lanes((seg_rows & 15).astype(_f32)))
    return member, before, seg_lo, seg_lo + seg_rows.astype(_f32)


def _rows_of_chunk(r0, seg_lo, seg_hi):
    row = (lax.broadcasted_iota(jnp.int32, (PERM_ROWS, N_EXPERTS), 0) + r0).astype(_f32)
    inside = jnp.logical_and(row >= seg_lo, row < seg_hi)
    rank = jnp.sum(jnp.where(inside, row - seg_lo, 0.0), axis=1, keepdims=True)
    return jnp.where(inside, 1.0, 0.0).astype(_bf16), rank


def _dispatch_kernel(pad_from_ref, pad_n_ref, nb_ref, piece_ref, idx_ref, hb_ref, xs_ref,
                     xbuf_ref, zero_ref, sent_ref, sem, zsem):
    tm = hb_ref.shape[0]
    n_blocks = xs_ref.shape[0] // ROW_BLOCK

    @pl.when(pl.program_id(0) == 0)
    def _():
        zero_ref[...] = jnp.zeros_like(zero_ref)

        def pad_expert(e, carry):
            off = pad_from_ref[e]

            def pad_piece(r, c):
                pltpu.make_async_copy(_aligned_rows(zero_ref, 0, SEG_ALIGN),
                                      _aligned_rows(xs_ref, off + SEG_ALIGN * r, SEG_ALIGN),
                                      zsem).start()
                return c + 1

            return lax.fori_loop(0, pad_n_ref[e] // SEG_ALIGN, pad_piece, carry)

        n_pieces = lax.fori_loop(0, N_EXPERTS, pad_expert, 0)

        def pad_tail(b, carry):
            pltpu.make_async_copy(zero_ref, xs_ref.at[pl.ds(b * ROW_BLOCK, ROW_BLOCK)],
                                  zsem).start()
            return carry

        lax.fori_loop(nb_ref[0], n_blocks, pad_tail, 0)

        def wait_piece(r, c):
            pltpu.make_async_copy(_aligned_rows(zero_ref, 0, SEG_ALIGN),
                                  _aligned_rows(xs_ref, 0, SEG_ALIGN), zsem).wait()
            return c

        def wait_tail(b, c):
            pltpu.make_async_copy(zero_ref, xs_ref.at[pl.ds(0, ROW_BLOCK)], zsem).wait()
            return c

        lax.fori_loop(0, n_pieces, wait_piece, 0)
        lax.fori_loop(nb_ref[0], n_blocks, wait_tail, 0)

    step = pl.program_id(0)
    slot = step % 2
    total = _tile_rows(piece_ref)
    member, before, seg_lo, seg_hi = _tile_layout(idx_ref[...])
    place = jnp.where(member > 0.0, before + 1.0, -1.0).astype(_bf16)

    def permute(c, carry):
        r0 = c * PERM_ROWS
        expert_of_row, rank_of_row = _rows_of_chunk(r0, seg_lo, seg_hi)
        row = lax.broadcasted_iota(jnp.int32, rank_of_row.shape, 0) + r0
        want = jnp.where(row < total, rank_of_row + 1.0, -5.0)
        sel = jnp.where(_dot_nt(expert_of_row, place) == want, 1.0, 0.0)
        picked = _dot(sel.astype(_bf16), hb_ref[...])
        half = picked.shape[1] // 2
        xbuf_ref[slot, pl.ds(r0, PERM_ROWS), :] = (
            (pltpu.bitcast(picked[:, :half], jnp.uint32) >> 16)
            | pltpu.bitcast(picked[:, half:], jnp.uint32))
        return carry

    for c in range(xbuf_ref.shape[1] // PERM_ROWS):
        permute(c, 0)

    wait_sent = lambda n_pieces: _wait_pieces(n_pieces, lambda k: pltpu.make_async_copy(
        xbuf_ref.at[slot, pl.ds(0, k * SEG_ALIGN)], xs_ref.at[pl.ds(0, k * SEG_ALIGN)], sem))

    @pl.when(step > 0)
    def _():
        wait_sent(sent_ref[0])

    for s in range(2):
        @pl.when(slot == s)
        def _(s=s):
            def send(off, dst):
                pltpu.make_async_copy(_aligned_rows(xbuf_ref.at[s], off, SEG_ALIGN),
                                      _aligned_rows(xs_ref, dst, SEG_ALIGN), sem).start()

            _for_each_piece(piece_ref, send)

    sent_ref[0] = total // SEG_ALIGN

    @pl.when(step == pl.num_programs(0) - 1)
    def _():
        wait_sent(total // SEG_ALIGN)


def _tile_buffer_rows(tm):
    worst = tm * TOP_K + N_EXPERTS * (SEG_ALIGN - 1)
    return -(-worst // PERM_ROWS) * PERM_ROWS


def moe_dispatch(pieces, idx_t, hb, pad_from, pad_n, n_used, n_blocks):
    t, d = hb.shape
    tm = TOKEN_TILE
    grid_spec = pltpu.PrefetchScalarGridSpec(
        num_scalar_prefetch=3,
        grid=(t // tm,),
        in_specs=[
            pl.BlockSpec((1, 1, PIECE_SLOTS), lambda i, *_: (i, 0, 0), memory_space=pltpu.SMEM),
            pl.BlockSpec((tm, TOP_K), lambda i, *_: (i, 0)),
            pl.BlockSpec((tm, d), lambda i, *_: (i, 0)),
        ],
        out_specs=pl.BlockSpec(memory_space=pl.ANY),
        scratch_shapes=[pltpu.VMEM((2, _tile_buffer_rows(tm), d // 2), jnp.uint32),
                        pltpu.VMEM((ROW_BLOCK, d // 2), jnp.uint32),
                        pltpu.SMEM((1,), jnp.int32),
                        pltpu.SemaphoreType.DMA(()), pltpu.SemaphoreType.DMA(())],
    )
    return pl.pallas_call(
        _dispatch_kernel,
        grid_spec=grid_spec,
        out_shape=jax.ShapeDtypeStruct((n_blocks * ROW_BLOCK, d // 2), jnp.uint32),
        compiler_params=_cparams("arbitrary"),
        name="moe_dispatch",
    )(pad_from, pad_n, n_used, pieces, idx_t, hb)


def _expert_kernel(be_ref, nb_ref, xs_ref, w1_ref, w3_ref, w2_ref, y_ref, w1b, w3b, w2b):
    i = pl.program_id(0)

    @pl.when(jnp.logical_or(i == 0, be_ref[i] != be_ref[jnp.maximum(i - 1, 0)]))
    def _():
        w1b[...] = w1_ref[0, 0].astype(_bf16)
        w3b[...] = w3_ref[0, 0].astype(_bf16)
        w2b[...] = w2_ref[0, 0].astype(_bf16)

    @pl.when(i < nb_ref[0])
    def _():
        chains = [pl.ds(c * EXPERT_ROWS, EXPERT_ROWS) for c in range(ROW_BLOCK // EXPERT_ROWS)]
        xs = [jnp.concatenate(_unpack_rows(xs_ref[rows, :]), axis=1) for rows in chains]
        acts = [(_silu(_dot(x, w1b[...])) * _dot(x, w3b[...])).astype(_bf16) for x in xs]
        for rows, act in zip(chains, acts):
            y_ref[rows, :] = _pack_rows(_dot(act, w2b[...]))

    @pl.when(i >= nb_ref[0])
    def _():
        y_ref[...] = jnp.zeros_like(y_ref)


def moe_experts(xs, blk_e, n_used, w1, w3, w2, layer):
    p, half = xs.shape
    d = 2 * half
    ff = w1.shape[3]
    nb = p // ROW_BLOCK
    row = lambda i, be, nu: (jnp.minimum(i, nu[0] - 1), 0)
    wsel = lambda i, be, nu: (layer, be[i], 0, 0)
    grid_spec = pltpu.PrefetchScalarGridSpec(
        num_scalar_prefetch=2,
        grid=(nb,),
        in_specs=[
            pl.BlockSpec((ROW_BLOCK, half), row),
            pl.BlockSpec((1, 1, d, ff), wsel),
            pl.BlockSpec((1, 1, d, ff), wsel),
            pl.BlockSpec((1, 1, ff, d), wsel),
        ],
        out_specs=pl.BlockSpec((ROW_BLOCK, half), lambda i, be, nu: (i, 0)),
        scratch_shapes=[pltpu.VMEM((d, ff), _bf16), pltpu.VMEM((d, ff), _bf16),
                        pltpu.VMEM((ff, d), _bf16)],
    )
    return pl.pallas_call(
        _expert_kernel,
        grid_spec=grid_spec,
        out_shape=jax.ShapeDtypeStruct((p, half), jnp.uint32),
        compiler_params=_cparams("arbitrary"),
        name="moe_experts",
    )(blk_e, n_used, xs, w1, w3, w2)


def _combine_kernel(piece_ref, next_piece_ref, ys_ref, idx_ref, w_ref, shared_ref, x_ref, gate_ref,
                    fn_ref, o_ref, ybuf_ref, sem, *, final_norm):
    tm = x_ref.shape[1]
    half = ybuf_ref.shape[2]
    step = pl.program_id(0) * pl.num_programs(1) + pl.program_id(1)
    n_steps = pl.num_programs(0) * pl.num_programs(1)
    slot = step % 2

    def fetch(table_ref, s):
        def one(off, src):
            pltpu.make_async_copy(_aligned_rows(ys_ref, src, SEG_ALIGN),
                                  _aligned_rows(ybuf_ref.at[s], off, SEG_ALIGN), sem.at[s]).start()

        _for_each_piece(table_ref, one)

    @pl.when(step == 0)
    def _():
        fetch(piece_ref, 0)

    for s in range(2):
        @pl.when(jnp.logical_and(step + 1 < n_steps, slot == 1 - s))
        def _(s=s):
            fetch(next_piece_ref, s)

    total = _tile_rows(piece_ref)
    _wait_pieces(total // SEG_ALIGN, lambda k: pltpu.make_async_copy(
        ys_ref.at[pl.ds(0, k * SEG_ALIGN)], ybuf_ref.at[slot, pl.ds(0, k * SEG_ALIGN)],
        sem.at[slot]))

    idx_t = idx_ref[...]
    member, before, seg_lo, seg_hi = _tile_layout(idx_t)
    before_b = before.astype(_bf16)
    lane_e = lax.broadcasted_iota(jnp.int32, member.shape, 1)
    w_of_expert = jnp.zeros(member.shape, _f32)
    for k in range(TOP_K):
        w_of_expert = jnp.where(lane_e == idx_t[:, k:k + 1], w_ref[:, k:k + 1], w_of_expert)
    w_of_expert = w_of_expert.astype(_bf16)
    lo8 = jnp.broadcast_to(seg_lo, (SEG_ALIGN, N_EXPERTS)).astype(jnp.int32) >> 3
    lo_head, lo_rest = (lo8 >> 4).astype(_f32).astype(_bf16), (lo8 & 15).astype(_f32).astype(_bf16)

    def weights_of_chunk(c):
        r0 = c * PERM_ROWS
        expert_of_row, rank_of_row = _rows_of_chunk(r0, seg_lo, seg_hi)
        del rank_of_row
        start = (128.0 * _dot_nt(lo_head, expert_of_row) + 8.0 * _dot_nt(lo_rest, expert_of_row))
        rank_row = (lax.broadcasted_iota(jnp.int32, (1, PERM_ROWS), 1) + r0).astype(_f32) - start[:1]
        hit = _dot_nt(before_b, expert_of_row) == rank_row
        return jnp.where(hit, _dot_nt(w_of_expert, expert_of_row), 0.0).astype(_bf16)

    wm = jnp.concatenate([weights_of_chunk(c) for c in range(ybuf_ref.shape[1] // PERM_ROWS)],
                         axis=1)
    packed = ybuf_ref[slot]
    row = lax.broadcasted_iota(jnp.int32, packed.shape, 0)
    packed = jnp.where(row < total, packed, jnp.uint32(0))
    y_lo, y_hi = _unpack_rows(packed)
    routed = jnp.concatenate([_dot(wm, y_lo), _dot(wm, y_hi)], axis=1)

    xn = x_ref[0] + gate_ref[0] * (shared_ref[...] + routed)
    if final_norm:
        xn = xn * lax.rsqrt(jnp.mean(xn * xn, axis=-1, keepdims=True) + EPS) * fn_ref[...]
    o_ref[0] = xn


def moe_combine(pieces, idx_t, ys, top_w, shared, x, gate, fn_g, final_norm):
    b, s, d = x.shape
    tm = TOKEN_TILE
    n_s = s // tm
    tile = lambda i, m: (i * n_s + m, 0)
    last_tile = b * n_s - 1
    return pl.pallas_call(
        functools.partial(_combine_kernel, final_norm=final_norm),
        grid=(b, n_s),
        in_specs=[
            pl.BlockSpec((1, 1, PIECE_SLOTS), lambda i, m: (i * n_s + m, 0, 0),
                         memory_space=pltpu.SMEM),
            pl.BlockSpec((1, 1, PIECE_SLOTS),
                         lambda i, m: (jnp.minimum(i * n_s + m + 1, last_tile), 0, 0),
                         memory_space=pltpu.SMEM),
            pl.BlockSpec(memory_space=pl.ANY),
            pl.BlockSpec((tm, TOP_K), tile),
            pl.BlockSpec((tm, TOP_K), tile),
            pl.BlockSpec((tm, d), tile),
            pl.BlockSpec((1, tm, d), lambda i, m: (i, m, 0)),
            pl.BlockSpec((1, 1, d), lambda i, m: (i, 0, 0)),
            pl.BlockSpec((1, d), lambda i, m: (0, 0)),
        ],
        out_specs=pl.BlockSpec((1, tm, d), lambda i, m: (i, m, 0)),
        out_shape=jax.ShapeDtypeStruct((b, s, d), _f32),
        scratch_shapes=[pltpu.VMEM((2, _tile_buffer_rows(tm), d // 2), jnp.uint32),
                        pltpu.SemaphoreType.DMA((2,))],
        compiler_params=_cparams("arbitrary", "arbitrary"),
        name="moe_combine",
    )(pieces, pieces, ys, idx_t, top_w, shared, x, gate, fn_g.reshape(1, d))


def moe_layer(x, g, sc, sh, gate, router_w, router_b, w1, w3, w2, layer, s_w1, s_w3, s_w2,
              fn_g, final_norm):
    b, s, d = x.shape
    t = b * s
    hb, top_idx, top_w, shared = moe_router(x, g, sc, sh, router_w, router_b, s_w1, s_w3, s_w2)
    pieces, rows = moe_rank(top_idx)
    idx_t = top_idx.T
    n_tiles = pieces.shape[0]

    rows = rows.reshape(N_EXPERTS)
    nblk = (rows + ROW_BLOCK - 1) // ROW_BLOCK
    blk_ends = jnp.cumsum(nblk)
    max_rows = t * TOP_K + n_tiles * N_EXPERTS * (SEG_ALIGN - 1)
    n_blocks = -(-max_rows // ROW_BLOCK) + N_EXPERTS
    blk = jnp.arange(n_blocks, dtype=jnp.int32)
    blk_e = jnp.minimum(jnp.sum(blk[:, None] >= blk_ends[None, :], axis=1),
                        N_EXPERTS - 1).astype(jnp.int32)
    n_used = blk_ends[-1:].astype(jnp.int32)
    pad_from = ((blk_ends - nblk) * ROW_BLOCK + rows).astype(jnp.int32)
    pad_n = (nblk * ROW_BLOCK - rows).astype(jnp.int32)

    xs = moe_dispatch(pieces, idx_t, hb, pad_from, pad_n, n_used, n_blocks)
    ys = moe_experts(xs, blk_e, n_used, w1, w3, w2, layer)
    return moe_combine(pieces, idx_t, ys, top_w.T, shared, x, gate, fn_g, final_norm)


def kernel(x, c, norm_mix, norm_ffn, ada_w, ada_b, da_w_in, da_w_out, da_lq1, da_lk1, da_lq2,
           da_lk2, da_subln, ret_w_in, ret_w_out, ret_gn, router_w, router_b, exp_w1, exp_w3,
           exp_w2, sh_w1, sh_w3, sh_w2, final_norm):
    depth, d = norm_mix.shape
    b = x.shape[0]
    mod = ada_modulation(c, ada_w, ada_b).reshape(depth, b, 6, 1, d)
    for i in range(depth):
        sh_a, sc_a, g_a, sh_f, sc_f, g_f = (mod[i, :, j] for j in range(6))
        j = i // 2
        if i % 2 == 0:
            proj = norm_proj(x, norm_mix[i], sc_a, sh_a, da_w_in[j].astype(_bf16))
            y = diff_attention_core(proj, da_lq1[j], da_lk1[j], da_lq2[j], da_lk2[j],
                                    da_subln[j], i)
            x = out_proj_residual(y, da_w_out[j].astype(_bf16), x, g_a)
        else:
            proj = norm_proj(x, norm_mix[i], sc_a, sh_a, ret_w_in[j].astype(_bf16))
            y = retention_core(proj, ret_gn[j])
            x = out_proj_residual(y, ret_w_out[j].astype(_bf16), x, g_a)
        x = moe_layer(x, norm_ffn[i], sc_f, sh_f, g_f, router_w[i], router_b[i],
                      exp_w1, exp_w3, exp_w2, i,
                      sh_w1[i], sh_w3[i], sh_w2[i], final_norm, i == depth - 1)
    return x
```

```python
import functools
import math

import jax
import jax.numpy as jnp
from jax import lax
from jax.experimental import pallas as pl
from jax.experimental.pallas import tpu as pltpu

EPS = 1e-6
DA_HEADS = 8
DA_HEAD_DIM = 64
DA_V_DIM = 2 * DA_HEAD_DIM
RET_HEADS = 4
RET_DK = 256
RET_DV = 512
RET_CHUNK = 128
N_EXPERTS = 64
TOP_K = 8
N_GROUPS = 8
GROUP_SIZE = N_EXPERTS // N_GROUPS
TOPK_GROUPS = 4
ROUTED_SCALE = 2.5
ROW_BLOCK_LOG2 = 10
ROW_BLOCK = 1 << ROW_BLOCK_LOG2
EXPERT_ROWS = 256
TOKEN_TILE = 256
SEG_ALIGN = 8
PERM_ROWS = 512
PIECE_SLOTS = 384

VMEM_LIMIT = 56 * 1024 * 1024
NEG_BIG = -1e30

_f32 = jnp.float32
_bf16 = jnp.bfloat16


def _cparams(*sem):
    return pltpu.CompilerParams(dimension_semantics=sem, vmem_limit_bytes=VMEM_LIMIT)


def _silu(v):
    return v * (1.0 / (1.0 + jnp.exp(-v)))


def _dot(a, b):
    return jnp.dot(a, b, preferred_element_type=_f32)


def _dot_nt(a, b):
    return lax.dot_general(a, b, (((1,), (1,)), ((), ())), preferred_element_type=_f32)


def _dot_tn(a, b):
    return lax.dot_general(a, b, (((0,), (0,)), ((), ())), preferred_element_type=_f32)


def _ada_kernel(c_ref, w_ref, b_ref, o_ref):
    a = _silu(c_ref[...]).astype(_bf16)
    o_ref[0] = _dot(a, w_ref[0].astype(_bf16)) + b_ref[0]


def ada_modulation(c, ada_w, ada_b):
    depth, d, n = ada_w.shape
    b = c.shape[0]
    tn = 1024
    return pl.pallas_call(
        _ada_kernel,
        grid=(depth, n // tn),
        in_specs=[
            pl.BlockSpec((b, d), lambda l, j: (0, 0)),
            pl.BlockSpec((1, d, tn), lambda l, j: (l, 0, j)),
            pl.BlockSpec((1, 1, tn), lambda l, j: (l, 0, j)),
        ],
        out_specs=pl.BlockSpec((1, b, tn), lambda l, j: (l, 0, j)),
        out_shape=jax.ShapeDtypeStruct((depth, b, n), _f32),
        compiler_params=_cparams("parallel", "parallel"),
        name="ada_modulation",
    )(c, ada_w, ada_b.reshape(depth, 1, n))


def _norm_mod(x, g, sc, sh):
    y = x * lax.rsqrt(jnp.mean(x * x, axis=-1, keepdims=True) + EPS)
    return (y * g) * (1.0 + sc) + sh


def _norm_proj_kernel(x_ref, g_ref, sc_ref, sh_ref, w_ref, o_ref, h_ref):
    @pl.when(pl.program_id(2) == 0)
    def _():
        h_ref[...] = _norm_mod(x_ref[0], g_ref[...], sc_ref[0], sh_ref[0]).astype(_bf16)

    o_ref[0] = _dot(h_ref[...], w_ref[...]).astype(o_ref.dtype)


def norm_proj(x, g, sc, sh, w):
    b, s, d = x.shape
    n = w.shape[1]
    tm = min(1024, s)
    tn = 1024
    return pl.pallas_call(
        _norm_proj_kernel,
        grid=(b, s // tm, n // tn),
        in_specs=[
            pl.BlockSpec((1, tm, d), lambda i, m, j: (i, m, 0)),
            pl.BlockSpec((1, d), lambda i, m, j: (0, 0)),
            pl.BlockSpec((1, 1, d), lambda i, m, j: (i, 0, 0)),
            pl.BlockSpec((1, 1, d), lambda i, m, j: (i, 0, 0)),
            pl.BlockSpec((d, tn), lambda i, m, j: (0, j)),
        ],
        out_specs=pl.BlockSpec((1, tm, tn), lambda i, m, j: (i, m, j)),
        out_shape=jax.ShapeDtypeStruct((b, s, n), _bf16),
        scratch_shapes=[pltpu.VMEM((tm, d), _bf16)],
        compiler_params=_cparams("parallel", "parallel", "arbitrary"),
        name="norm_proj",
    )(x, g.reshape(1, d), sc, sh, w)


def _out_proj_kernel(y_ref, w_ref, x_ref, g_ref, o_ref):
    o_ref[0] = x_ref[0] + g_ref[0] * _dot(y_ref[0], w_ref[...])


def out_proj_residual(y, w, x, gate):
    b, s, k = y.shape
    d = w.shape[1]
    tm = min(512, s)
    return pl.pallas_call(
        _out_proj_kernel,
        grid=(b, s // tm),
        in_specs=[
            pl.BlockSpec((1, tm, k), lambda i, m: (i, m, 0)),
            pl.BlockSpec((k, d), lambda i, m: (0, 0)),
            pl.BlockSpec((1, tm, d), lambda i, m: (i, m, 0)),
            pl.BlockSpec((1, 1, d), lambda i, m: (i, 0, 0)),
        ],
        out_specs=pl.BlockSpec((1, tm, d), lambda i, m: (i, m, 0)),
        out_shape=jax.ShapeDtypeStruct((b, s, d), _f32),
        compiler_params=_cparams("parallel", "parallel"),
        name="out_proj_residual",
    )(y, w, x, gate)


def _diff_attn_kernel(q_ref, k_ref, v_ref, lq1_ref, lk1_ref, lq2_ref, lk2_ref, sg_ref, o_ref,
                      bias_ref, *, tq, lam_init):
    h = pl.program_id(1)
    dh, dv = DA_HEAD_DIM, DA_V_DIM
    n_q = k_ref.shape[1] // tq

    hp1 = jnp.full((1, 1), h + 1, jnp.int32).astype(_f32)
    slope = jnp.exp2(hp1 * (-8.0 / DA_HEADS))
    dist = ((n_q - 1) * tq + lax.broadcasted_iota(jnp.int32, bias_ref.shape, 0)
            - lax.broadcasted_iota(jnp.int32, bias_ref.shape, 1)).astype(_f32)
    bias_ref[...] = -slope * dist

    lam = (jnp.exp(jnp.sum(lq1_ref[...] * lk1_ref[...], axis=-1, keepdims=True))
           - jnp.exp(jnp.sum(lq2_ref[...] * lk2_ref[...], axis=-1, keepdims=True)) + lam_init)
    causal = (lax.broadcasted_iota(jnp.int32, (tq, tq), 0)
              >= lax.broadcasted_iota(jnp.int32, (tq, tq), 1))
    lane = lax.broadcasted_iota(jnp.int32, (tq, 2 * dh), 1)

    def attend(c):
        q = q_ref[0, c * tq:(c + 1) * tq, :] * (dh ** -0.5)
        zero = jnp.zeros_like(q)
        q_maps = (jnp.where(lane < dh, q, zero), jnp.where(lane >= dh, q, zero))
        first = (n_q - 1 - c) * tq
        blocks = [slice(j * tq, (j + 1) * tq) for j in range(c + 1)]
        tiles = [[_dot_nt(q_m, k_ref[0, blk, :])
                  + bias_ref[:, first + blk.start:first + blk.stop] for q_m in q_maps]
                 for blk in blocks]
        tiles[c] = [jnp.where(causal, s, NEG_BIG) for s in tiles[c]]
        outs = []
        for mi in range(2):
            parts = [row[mi] for row in tiles]
            m = functools.reduce(jnp.maximum, [jnp.max(x, axis=-1, keepdims=True) for x in parts])
            ps = [jnp.exp(x - m) for x in parts]
            l = functools.reduce(jnp.add, [jnp.sum(x, axis=-1, keepdims=True) for x in ps])
            o = functools.reduce(jnp.add, [_dot(x.astype(_bf16), v_ref[0, blk, :])
                                           for x, blk in zip(ps, blocks)])
            outs.append(o / l)
        o = outs[0] - lam * outs[1]
        o = o * lax.rsqrt(jnp.mean(o * o, axis=-1, keepdims=True) + EPS)
        o_ref[0, c * tq:(c + 1) * tq, :] = ((o * sg_ref[...]) * (1.0 - lam_init)).astype(o_ref.dtype)

    for c in range(n_q):
        attend(c)


def diff_attention_core(proj, lq1, lk1, lq2, lk2, subln_g, layer_idx):
    b, s, _ = proj.shape
    nh, dh, dv = DA_HEADS, DA_HEAD_DIM, DA_V_DIM
    tq = min(512, s)
    lam_init = 0.8 - 0.6 * math.exp(-0.3 * layer_idx)
    vec = lambda n: pl.BlockSpec((1, n), lambda i, h: (0, 0))
    return pl.pallas_call(
        functools.partial(_diff_attn_kernel, tq=tq, lam_init=lam_init),
        grid=(b, nh),
        in_specs=[
            pl.BlockSpec((1, s, 2 * dh), lambda i, h: (i, 0, h)),
            pl.BlockSpec((1, s, 2 * dh), lambda i, h: (i, 0, nh + h)),
            pl.BlockSpec((1, s, dv), lambda i, h: (i, 0, 2 * nh + h)),
            vec(dh), vec(dh), vec(dh), vec(dh), vec(dv),
        ],
        out_specs=pl.BlockSpec((1, s, dv), lambda i, h: (i, 0, h)),
        out_shape=jax.ShapeDtypeStruct((b, s, nh * dv), _bf16),
        scratch_shapes=[pltpu.VMEM((tq, s), _f32)],
        compiler_params=_cparams("parallel", "parallel"),
        name="diff_attention",
    )(proj, proj, proj, lq1.reshape(1, dh), lk1.reshape(1, dh), lq2.reshape(1, dh),
      lk2.reshape(1, dh), subln_g.reshape(1, dv))


def _retention_kernel(q_ref, k_ref, v_ref, g_ref, gn_ref, o_ref, r_ref, *, chunk):
    h = pl.program_id(1)
    c = pl.program_id(2)
    dk = RET_DK

    @pl.when(c == 0)
    def _():
        r_ref[...] = jnp.zeros_like(r_ref)

    hf = jnp.full((1, 1), h, jnp.int32).astype(_f32)
    log_g = jnp.log(1.0 - jnp.exp2(-5.0 - hf))
    ri = lax.broadcasted_iota(jnp.int32, (chunk, chunk), 0)
    ci = lax.broadcasted_iota(jnp.int32, (chunk, chunk), 1)
    diff = (ri - ci).astype(_f32)
    decay = jnp.where(diff >= 0, jnp.exp(log_g * jnp.maximum(diff, 0.0)), 0.0)
    idx = lax.broadcasted_iota(jnp.int32, (chunk, 1), 0).astype(_f32)
    zeta = jnp.exp(log_g * (chunk - 1 - idx))
    xi = jnp.exp(log_g * (idx + 1.0))
    g_c = jnp.exp(log_g * chunk)

    decay = decay * (dk ** -0.5)
    zeta = zeta * (dk ** -0.5)
    chunks = [pl.ds(sub * chunk, chunk) for sub in range(q_ref.shape[1] // chunk)]
    intra, updates = [], []
    for rows in chunks:
        q, k, v = q_ref[0, rows, :], k_ref[0, rows, :], v_ref[0, rows, :]
        intra.append(_dot((_dot_nt(q, k) * decay).astype(_bf16), v))
        updates.append(_dot_tn((k.astype(_f32) * zeta).astype(_bf16), v))
    r = r_ref[...]
    for rows, o, upd in zip(chunks, intra, updates):
        o = o + _dot(q_ref[0, rows, :], r.astype(_bf16)) * xi
        r = g_c * r + upd
        o = o * lax.rsqrt(jnp.mean(o * o, axis=-1, keepdims=True) + EPS) * gn_ref[...]
        o_ref[0, rows, :] = (_silu(g_ref[0, rows, :].astype(_f32)) * o).astype(o_ref.dtype)
    r_ref[...] = r


def retention_core(proj, gn_g):
    b, s, _ = proj.shape
    nh, dk, dv = RET_HEADS, RET_DK, RET_DV
    chunk = RET_CHUNK
    rows = min(8 * chunk, s)
    return pl.pallas_call(
        functools.partial(_retention_kernel, chunk=chunk),
        grid=(b, nh, s // rows),
        in_specs=[
            pl.BlockSpec((1, rows, dk), lambda i, h, c: (i, c, h)),
            pl.BlockSpec((1, rows, dk), lambda i, h, c: (i, c, nh + h)),
            pl.BlockSpec((1, rows, dv), lambda i, h, c: (i, c, (2 * nh * dk) // dv + h)),
            pl.BlockSpec((1, rows, dv), lambda i, h, c: (i, c, (2 * nh * dk) // dv + nh + h)),
            pl.BlockSpec((1, dv), lambda i, h, c: (0, h)),
        ],
        out_specs=pl.BlockSpec((1, rows, dv), lambda i, h, c: (i, c, h)),
        out_shape=jax.ShapeDtypeStruct((b, s, nh * dv), _bf16),
        scratch_shapes=[pltpu.VMEM((dk, dv), _f32)],
        compiler_params=_cparams("parallel", "parallel", "arbitrary"),
        name="retention",
    )(proj, proj, proj, proj, gn_g.reshape(1, nh * dv))


def _pack_rows(h):
    n = h.shape[1] // 2
    lo = pltpu.bitcast(h[:, :n].astype(_bf16).astype(_f32), jnp.uint32)
    hi = pltpu.bitcast(h[:, n:].astype(_bf16).astype(_f32), jnp.uint32)
    return (lo >> 16) | (hi & jnp.uint32(0xFFFF0000))


def _unpack_rows(p):
    lo = pltpu.bitcast(p << 16, _f32).astype(_bf16)
    hi = pltpu.bitcast(p & jnp.uint32(0xFFFF0000), _f32).astype(_bf16)
    return lo, hi


def _first_argmax(v, iota, axis, big):
    m = jnp.max(v, axis=axis, keepdims=True)
    i = jnp.min(jnp.where(v == m, iota, big), axis=axis, keepdims=True)
    return m, i


def _router_kernel(x_ref, g_ref, sc_ref, sh_ref, rwt_ref, rb_ref, s1_ref, s3_ref, s2_ref,
                   hb_ref, idx_ref, wgt_ref, shared_ref):
    tm = x_ref.shape[1]
    h = _norm_mod(x_ref[0], g_ref[...], sc_ref[0], sh_ref[0])
    hb = h.astype(_bf16)
    hb_ref[...] = hb
    act = (_silu(_dot(hb, s1_ref[...])) * _dot(hb, s3_ref[...])).astype(_bf16)
    shared_ref[...] = _dot(act, s2_ref[...])

    scores = 1.0 / (1.0 + jnp.exp(-_dot_nt(rwt_ref[...], h)))
    choice = scores + rb_ref[...]
    c3 = choice.reshape(N_GROUPS, GROUP_SIZE, tm)
    mem = lax.broadcasted_iota(jnp.int32, c3.shape, 1)
    m1, i1 = _first_argmax(c3, mem, 1, GROUP_SIZE)
    m2 = jnp.max(jnp.where(mem == i1, -jnp.inf, c3), axis=1, keepdims=True)
    gs = m1 + m2
    gi = lax.broadcasted_iota(jnp.int32, gs.shape, 0)
    sel = jnp.zeros(gs.shape, _f32)
    for _ in range(TOPK_GROUPS):
        _, ig = _first_argmax(gs, gi, 0, N_GROUPS)
        hit = gi == ig
        sel = jnp.where(hit, 1.0, sel)
        gs = jnp.where(hit, -jnp.inf, gs)
    cm = jnp.where(sel > 0.0, c3, -jnp.inf).reshape(N_EXPERTS, tm)
    ei = lax.broadcasted_iota(jnp.int32, cm.shape, 0)
    idxs, wgts = [], []
    for _ in range(TOP_K):
        _, ie = _first_argmax(cm, ei, 0, N_EXPERTS)
        hit = ei == ie
        idxs.append(ie)
        wgts.append(jnp.sum(jnp.where(hit, scores, 0.0), axis=0, keepdims=True))
        cm = jnp.where(hit, -jnp.inf, cm)
    w = jnp.concatenate(wgts, axis=0)
    idx_ref[...] = jnp.concatenate(idxs, axis=0)
    wgt_ref[...] = w / jnp.sum(w, axis=0, keepdims=True) * ROUTED_SCALE


def moe_router(x, g, sc, sh, router_w, router_b, s_w1, s_w3, s_w2):
    b, s, d = x.shape
    t = b * s
    tm = min(256, s)
    n_s = s // tm
    sf = s_w1.shape[1]
    const = lambda shape: pl.BlockSpec(shape, lambda i, m: (0,) * len(shape))
    return pl.pallas_call(
        _router_kernel,
        grid=(b, n_s),
        in_specs=[
            pl.BlockSpec((1, tm, d), lambda i, m: (i, m, 0)),
            const((1, d)),
            pl.BlockSpec((1, 1, d), lambda i, m: (i, 0, 0)),
            pl.BlockSpec((1, 1, d), lambda i, m: (i, 0, 0)),
            const((N_EXPERTS, d)), const((N_EXPERTS, 1)),
            const((d, sf)), const((d, sf)), const((sf, d)),
        ],
        out_specs=[
            pl.BlockSpec((tm, d), lambda i, m: (i * n_s + m, 0)),
            pl.BlockSpec((TOP_K, tm), lambda i, m: (0, i * n_s + m)),
            pl.BlockSpec((TOP_K, tm), lambda i, m: (0, i * n_s + m)),
            pl.BlockSpec((tm, d), lambda i, m: (i * n_s + m, 0)),
        ],
        out_shape=[
            jax.ShapeDtypeStruct((t, d), _bf16),
            jax.ShapeDtypeStruct((TOP_K, t), jnp.int32),
            jax.ShapeDtypeStruct((TOP_K, t), _f32),
            jax.ShapeDtypeStruct((t, d), _f32),
        ],
        compiler_params=_cparams("parallel", "parallel"),
        name="moe_router",
    )(x, g.reshape(1, d), sc, sh, router_w.T, router_b.reshape(N_EXPERTS, 1),
      s_w1.astype(_bf16), s_w3.astype(_bf16), s_w2.astype(_bf16))


def _strict_lower(n):
    return jnp.where(lax.broadcasted_iota(jnp.int32, (n, n), 1)
                     < lax.broadcasted_iota(jnp.int32, (n, n), 0), 1.0, 0.0).astype(_bf16)


def _prefix_over_experts(col):
    wide = jnp.broadcast_to(col, (N_EXPERTS, 128)).astype(_bf16)
    return _dot(_strict_lower(N_EXPERTS), wide)[:, :1]


def _rank_kernel(idx_ref, piece_ref, tot_ref, base_ref):
    phase = pl.program_id(0)
    i = pl.program_id(1)
    tm = idx_ref.shape[1]

    idx = idx_ref[...]
    ei = lax.broadcasted_iota(jnp.int32, (N_EXPERTS, tm), 0)
    memf = jnp.zeros((N_EXPERTS, tm), _f32)
    for k in range(TOP_K):
        memf = jnp.where(ei == idx[k:k + 1, :], 1.0, memf)
    tile_cnt = jnp.sum(memf, axis=1, keepdims=True).astype(jnp.int32)
    seg_rows = ((tile_cnt + (SEG_ALIGN - 1)) & (-SEG_ALIGN)).astype(_f32)

    @pl.when(jnp.logical_and(phase == 0, i == 0))
    def _():
        base_ref[...] = jnp.zeros_like(base_ref)

    @pl.when(phase == 0)
    def _():
        base_ref[...] += seg_rows

    @pl.when(jnp.logical_and(phase == 1, i == 0))
    def _():
        tot = base_ref[...].astype(jnp.int32)
        tot_ref[...] = tot
        nblk = (tot + (ROW_BLOCK - 1)) >> ROW_BLOCK_LOG2
        start_blk = (32.0 * _prefix_over_experts((nblk >> 5).astype(_f32))
                     + _prefix_over_experts((nblk & 31).astype(_f32)))
        base_ref[...] = start_blk * float(ROW_BLOCK)

    @pl.when(phase == 1)
    def _():
        seg_i = seg_rows.astype(jnp.int32)
        lo = (16.0 * _prefix_over_experts((seg_i >> 4).astype(_f32))
              + _prefix_over_experts((seg_i & 15).astype(_f32)))
        lane = lax.broadcasted_iota(jnp.int32, (N_EXPERTS, PIECE_SLOTS), 1)
        row = (lane * SEG_ALIGN).astype(_f32)
        inside = jnp.logical_and(row >= lo, row < lo + seg_rows)
        dst = jnp.sum(jnp.where(inside, base_ref[...] + (row - lo), 0.0), axis=0, keepdims=True)
        total = jnp.sum(seg_rows, axis=0, keepdims=True)
        piece_ref[0] = jnp.where(lane[:1] == PIECE_SLOTS - 1, total, dst).astype(jnp.int32)
        base_ref[...] += seg_rows


def moe_rank(top_idx):
    k, t = top_idx.shape
    tm = TOKEN_TILE
    assert _tile_buffer_rows(tm) // SEG_ALIGN < PIECE_SLOTS
    return pl.pallas_call(
        _rank_kernel,
        grid=(2, t // tm),
        in_specs=[pl.BlockSpec((k, tm), lambda p, i: (0, i))],
        out_specs=[pl.BlockSpec((1, 1, PIECE_SLOTS), lambda p, i: (i * p, 0, 0)),
                   pl.BlockSpec((N_EXPERTS, 1), lambda p, i: (0, 0))],
        out_shape=[jax.ShapeDtypeStruct((t // tm, 1, PIECE_SLOTS), jnp.int32),
                   jax.ShapeDtypeStruct((N_EXPERTS, 1), jnp.int32)],
        scratch_shapes=[pltpu.VMEM((N_EXPERTS, 1), _f32)],
        compiler_params=_cparams("arbitrary", "arbitrary"),
        name="moe_rank",
    )(top_idx)


def _aligned_rows(ref, start, n):
    return ref.at[pl.ds(pl.multiple_of(start, SEG_ALIGN), n)]


def _tile_rows(piece_ref):
    return piece_ref[0, 0, PIECE_SLOTS - 1]


def _for_each_piece(piece_ref, fn):
    def piece(g, c):
        fn(g * SEG_ALIGN, piece_ref[0, 0, g])
        return c

    lax.fori_loop(0, _tile_rows(piece_ref) // SEG_ALIGN, piece, 0)


def _wait_pieces(n_pieces, make_copy):
    for bit in reversed(range(PIECE_SLOTS.bit_length())):
        @pl.when((n_pieces >> bit) & 1 == 1)
        def _(bit=bit):
            make_copy(1 << bit).wait()


def _prefix_over_lanes(row):
    n = row.shape[1]
    wide = jnp.broadcast_to(row, (SEG_ALIGN, n)).astype(_bf16)
    return _dot_nt(wide, _strict_lower(n))[:1, :]


def _tile_layout(idx_t):
    tm = idx_t.shape[0]
    lane_e = lax.broadcasted_iota(jnp.int32, (tm, N_EXPERTS), 1)
    member = jnp.zeros((tm, N_EXPERTS), _f32)
    for k in range(TOP_K):
        member = jnp.where(lane_e == idx_t[:, k:k + 1], 1.0, member)
    before = _dot(_strict_lower(tm), member.astype(_bf16))
    count = jnp.sum(member, axis=0, keepdims=True).astype(jnp.int32)
    seg_rows = (count + (SEG_ALIGN - 1)) & (-SEG_ALIGN)
    seg_lo = (16.0 * _prefix_over_lanes((seg_rows >> 4).astype(_f32))
              + _prefix_over_lanes((seg_rows & 15).astype(_f32)))
    return member, before, seg_lo, seg_lo + seg_rows.astype(_f32)


def _rows_of_chunk(r0, seg_lo, seg_hi):
    row = (lax.broadcasted_iota(jnp.int32, (PERM_ROWS, N_EXPERTS), 0) + r0).astype(_f32)
    inside = jnp.logical_and(row >= seg_lo, row < seg_hi)
    rank = jnp.sum(jnp.where(inside, row - seg_lo, 0.0), axis=1, keepdims=True)
    return jnp.where(inside, 1.0, 0.0).astype(_bf16), rank


def _dispatch_kernel(pad_from_ref, pad_n_ref, nb_ref, piece_ref, idx_ref, hb_ref, xs_ref,
                     xbuf_ref, zero_ref, sent_ref, sem, zsem):
    tm = hb_ref.shape[0]
    n_blocks = xs_ref.shape[0] // ROW_BLOCK

    @pl.when(pl.program_id(0) == 0)
    def _():
        zero_ref[...] = jnp.zeros_like(zero_ref)

        def pad_expert(e, carry):
            off = pad_from_ref[e]

            def pad_piece(r, c):
                pltpu.make_async_copy(_aligned_rows(zero_ref, 0, SEG_ALIGN),
                                      _aligned_rows(xs_ref, off + SEG_ALIGN * r, SEG_ALIGN),
                                      zsem).start()
                return c + 1

            return lax.fori_loop(0, pad_n_ref[e] // SEG_ALIGN, pad_piece, carry)

        n_pieces = lax.fori_loop(0, N_EXPERTS, pad_expert, 0)

        def pad_tail(b, carry):
            pltpu.make_async_copy(zero_ref, xs_ref.at[pl.ds(b * ROW_BLOCK, ROW_BLOCK)],
                                  zsem).start()
            return carry

        lax.fori_loop(nb_ref[0], n_blocks, pad_tail, 0)

        def wait_piece(r, c):
            pltpu.make_async_copy(_aligned_rows(zero_ref, 0, SEG_ALIGN),
                                  _aligned_rows(xs_ref, 0, SEG_ALIGN), zsem).wait()
            return c

        def wait_tail(b, c):
            pltpu.make_async_copy(zero_ref, xs_ref.at[pl.ds(0, ROW_BLOCK)], zsem).wait()
            return c

        lax.fori_loop(0, n_pieces, wait_piece, 0)
        lax.fori_loop(nb_ref[0], n_blocks, wait_tail, 0)

    step = pl.program_id(0)
    slot = step % 2
    total = _tile_rows(piece_ref)
    member, before, seg_lo, seg_hi = _tile_layout(idx_ref[...])
    place = jnp.where(member > 0.0, before + 1.0, -1.0).astype(_bf16)

    def permute(c, carry):
        r0 = c * PERM_ROWS
        expert_of_row, rank_of_row = _rows_of_chunk(r0, seg_lo, seg_hi)
        row = lax.broadcasted_iota(jnp.int32, rank_of_row.shape, 0) + r0
        want = jnp.where(row < total, rank_of_row + 1.0, -5.0)
        sel = jnp.where(_dot_nt(expert_of_row, place) == want, 1.0, 0.0)
        picked = _dot(sel.astype(_bf16), hb_ref[...])
        half = picked.shape[1] // 2
        xbuf_ref[slot, pl.ds(r0, PERM_ROWS), :] = (
            (pltpu.bitcast(picked[:, :half], jnp.uint32) >> 16)
            | pltpu.bitcast(picked[:, half:], jnp.uint32))
        return carry

    for c in range(xbuf_ref.shape[1] // PERM_ROWS):
        permute(c, 0)

    wait_sent = lambda n_pieces: _wait_pieces(n_pieces, lambda k: pltpu.make_async_copy(
        xbuf_ref.at[slot, pl.ds(0, k * SEG_ALIGN)], xs_ref.at[pl.ds(0, k * SEG_ALIGN)], sem))

    @pl.when(step > 0)
    def _():
        wait_sent(sent_ref[0])

    for s in range(2):
        @pl.when(slot == s)
        def _(s=s):
            def send(off, dst):
                pltpu.make_async_copy(_aligned_rows(xbuf_ref.at[s], off, SEG_ALIGN),
                                      _aligned_rows(xs_ref, dst, SEG_ALIGN), sem).start()

            _for_each_piece(piece_ref, send)

    sent_ref[0] = total // SEG_ALIGN

    @pl.when(step == pl.num_programs(0) - 1)
    def _():
        wait_sent(total // SEG_ALIGN)


def _tile_buffer_rows(tm):
    worst = tm * TOP_K + N_EXPERTS * (SEG_ALIGN - 1)
    return -(-worst // PERM_ROWS) * PERM_ROWS


def moe_dispatch(pieces, idx_t, hb, pad_from, pad_n, n_used, n_blocks):
    t, d = hb.shape
    tm = TOKEN_TILE
    grid_spec = pltpu.PrefetchScalarGridSpec(
        num_scalar_prefetch=3,
        grid=(t // tm,),
        in_specs=[
            pl.BlockSpec((1, 1, PIECE_SLOTS), lambda i, *_: (i, 0, 0), memory_space=pltpu.SMEM),
            pl.BlockSpec((tm, TOP_K), lambda i, *_: (i, 0)),
            pl.BlockSpec((tm, d), lambda i, *_: (i, 0)),
        ],
        out_specs=pl.BlockSpec(memory_space=pl.ANY),
        scratch_shapes=[pltpu.VMEM((2, _tile_buffer_rows(tm), d // 2), jnp.uint32),
                        pltpu.VMEM((ROW_BLOCK, d // 2), jnp.uint32),
                        pltpu.SMEM((1,), jnp.int32),
                        pltpu.SemaphoreType.DMA(()), pltpu.SemaphoreType.DMA(())],
    )
    return pl.pallas_call(
        _dispatch_kernel,
        grid_spec=grid_spec,
        out_shape=jax.ShapeDtypeStruct((n_blocks * ROW_BLOCK, d // 2), jnp.uint32),
        compiler_params=_cparams("arbitrary"),
        name="moe_dispatch",
    )(pad_from, pad_n, n_used, pieces, idx_t, hb)


def _expert_kernel(be_ref, nb_ref, xs_ref, w1_ref, w3_ref, w2_ref, y_ref, w1b, w3b, w2b):
    i = pl.program_id(0)

    @pl.when(jnp.logical_or(i == 0, be_ref[i] != be_ref[jnp.maximum(i - 1, 0)]))
    def _():
        w1b[...] = w1_ref[0, 0].astype(_bf16)
        w3b[...] = w3_ref[0, 0].astype(_bf16)
        w2b[...] = w2_ref[0, 0].astype(_bf16)

    @pl.when(i < nb_ref[0])
    def _():
        chains = [pl.ds(c * EXPERT_ROWS, EXPERT_ROWS) for c in range(ROW_BLOCK // EXPERT_ROWS)]
        xs = [jnp.concatenate(_unpack_rows(xs_ref[rows, :]), axis=1) for rows in chains]
        acts = [(_silu(_dot(x, w1b[...])) * _dot(x, w3b[...])).astype(_bf16) for x in xs]
        for rows, act in zip(chains, acts):
            y_ref[rows, :] = _pack_rows(_dot(act, w2b[...]))

    @pl.when(i >= nb_ref[0])
    def _():
        y_ref[...] = jnp.zeros_like(y_ref)


def moe_experts(xs, blk_e, n_used, w1, w3, w2, layer):
    p, half = xs.shape
    d = 2 * half
    ff = w1.shape[3]
    nb = p // ROW_BLOCK
    row = lambda i, be, nu: (jnp.minimum(i, nu[0] - 1), 0)
    wsel = lambda i, be, nu: (layer, be[i], 0, 0)
    grid_spec = pltpu.PrefetchScalarGridSpec(
        num_scalar_prefetch=2,
        grid=(nb,),
        in_specs=[
            pl.BlockSpec((ROW_BLOCK, half), row),
            pl.BlockSpec((1, 1, d, ff), wsel),
            pl.BlockSpec((1, 1, d, ff), wsel),
            pl.BlockSpec((1, 1, ff, d), wsel),
        ],
        out_specs=pl.BlockSpec((ROW_BLOCK, half), lambda i, be, nu: (i, 0)),
        scratch_shapes=[pltpu.VMEM((d, ff), _bf16), pltpu.VMEM((d, ff), _bf16),
                        pltpu.VMEM((ff, d), _bf16)],
    )
    return pl.pallas_call(
        _expert_kernel,
        grid_spec=grid_spec,
        out_shape=jax.ShapeDtypeStruct((p, half), jnp.uint32),
        compiler_params=_cparams("arbitrary"),
        name="moe_experts",
    )(blk_e, n_used, xs, w1, w3, w2)


def _combine_kernel(piece_ref, next_piece_ref, ys_ref, idx_ref, w_ref, shared_ref, x_ref, gate_ref,
                    fn_ref, o_ref, ybuf_ref, sem, *, final_norm):
    tm = x_ref.shape[1]
    half = ybuf_ref.shape[2]
    step = pl.program_id(0) * pl.num_programs(1) + pl.program_id(1)
    n_steps = pl.num_programs(0) * pl.num_programs(1)
    slot = step % 2

    def fetch(table_ref, s):
        def one(off, src):
            pltpu.make_async_copy(_aligned_rows(ys_ref, src, SEG_ALIGN),
                                  _aligned_rows(ybuf_ref.at[s], off, SEG_ALIGN), sem.at[s]).start()

        _for_each_piece(table_ref, one)

    @pl.when(step == 0)
    def _():
        fetch(piece_ref, 0)

    for s in range(2):
        @pl.when(jnp.logical_and(step + 1 < n_steps, slot == 1 - s))
        def _(s=s):
            fetch(next_piece_ref, s)

    total = _tile_rows(piece_ref)
    _wait_pieces(total // SEG_ALIGN, lambda k: pltpu.make_async_copy(
        ys_ref.at[pl.ds(0, k * SEG_ALIGN)], ybuf_ref.at[slot, pl.ds(0, k * SEG_ALIGN)],
        sem.at[slot]))

    idx_t = idx_ref[...]
    member, before, seg_lo, seg_hi = _tile_layout(idx_t)
    before_b = before.astype(_bf16)
    lane_e = lax.broadcasted_iota(jnp.int32, member.shape, 1)
    w_of_expert = jnp.zeros(member.shape, _f32)
    for k in range(TOP_K):
        w_of_expert = jnp.where(lane_e == idx_t[:, k:k + 1], w_ref[:, k:k + 1], w_of_expert)
    w_of_expert = w_of_expert.astype(_bf16)
    lo8 = jnp.broadcast_to(seg_lo, (SEG_ALIGN, N_EXPERTS)).astype(jnp.int32) >> 3
    lo_head, lo_rest = (lo8 >> 4).astype(_f32).astype(_bf16), (lo8 & 15).astype(_f32).astype(_bf16)

    def weights_of_chunk(c):
        r0 = c * PERM_ROWS
        expert_of_row, rank_of_row = _rows_of_chunk(r0, seg_lo, seg_hi)
        del rank_of_row
        start = (128.0 * _dot_nt(lo_head, expert_of_row) + 8.0 * _dot_nt(lo_rest, expert_of_row))
        rank_row = (lax.broadcasted_iota(jnp.int32, (1, PERM_ROWS), 1) + r0).astype(_f32) - start[:1]
        hit = _dot_nt(before_b, expert_of_row) == rank_row
        return jnp.where(hit, _dot_nt(w_of_expert, expert_of_row), 0.0).astype(_bf16)

    wm = jnp.concatenate([weights_of_chunk(c) for c in range(ybuf_ref.shape[1] // PERM_ROWS)],
                         axis=1)
    packed = ybuf_ref[slot]
    row = lax.broadcasted_iota(jnp.int32, packed.shape, 0)
    packed = jnp.where(row < total, packed, jnp.uint32(0))
    y_lo, y_hi = _unpack_rows(packed)
    routed = jnp.concatenate([_dot(wm, y_lo), _dot(wm, y_hi)], axis=1)

    xn = x_ref[0] + gate_ref[0] * (shared_ref[...] + routed)
    if final_norm:
        xn = xn * lax.rsqrt(jnp.mean(xn * xn, axis=-1, keepdims=True) + EPS) * fn_ref[...]
    o_ref[0] = xn


def moe_combine(pieces, idx_t, ys, top_w, shared, x, gate, fn_g, final_norm):
    b, s, d = x.shape
    tm = TOKEN_TILE
    n_s = s // tm
    tile = lambda i, m: (i * n_s + m, 0)
    last_tile = b * n_s - 1
    return pl.pallas_call(
        functools.partial(_combine_kernel, final_norm=final_norm),
        grid=(b, n_s),
        in_specs=[
            pl.BlockSpec((1, 1, PIECE_SLOTS), lambda i, m: (i * n_s + m, 0, 0),
                         memory_space=pltpu.SMEM),
            pl.BlockSpec((1, 1, PIECE_SLOTS),
                         lambda i, m: (jnp.minimum(i * n_s + m + 1, last_tile), 0, 0),
                         memory_space=pltpu.SMEM),
            pl.BlockSpec(memory_space=pl.ANY),
            pl.BlockSpec((tm, TOP_K), tile),
            pl.BlockSpec((tm, TOP_K), tile),
            pl.BlockSpec((tm, d), tile),
            pl.BlockSpec((1, tm, d), lambda i, m: (i, m, 0)),
            pl.BlockSpec((1, 1, d), lambda i, m: (i, 0, 0)),
            pl.BlockSpec((1, d), lambda i, m: (0, 0)),
        ],
        out_specs=pl.BlockSpec((1, tm, d), lambda i, m: (i, m, 0)),
        out_shape=jax.ShapeDtypeStruct((b, s, d), _f32),
        scratch_shapes=[pltpu.VMEM((2, _tile_buffer_rows(tm), d // 2), jnp.uint32),
                        pltpu.SemaphoreType.DMA((2,))],
        compiler_params=_cparams("arbitrary", "arbitrary"),
        name="moe_combine",
    )(pieces, pieces, ys, idx_t, top_w, shared, x, gate, fn_g.reshape(1, d))


def moe_layer(x, g, sc, sh, gate, router_w, router_b, w1, w3, w2, layer, s_w1, s_w3, s_w2,
              fn_g, final_norm):
    b, s, d = x.shape
    t = b * s
    hb, top_idx, top_w, shared = moe_router(x, g, sc, sh, router_w, router_b, s_w1, s_w3, s_w2)
    pieces, rows = moe_rank(top_idx)
    idx_t = top_idx.T
    n_tiles = pieces.shape[0]

    rows = rows.reshape(N_EXPERTS)
    nblk = (rows + ROW_BLOCK - 1) // ROW_BLOCK
    blk_ends = jnp.cumsum(nblk)
    max_rows = t * TOP_K + n_tiles * N_EXPERTS * (SEG_ALIGN - 1)
    n_blocks = -(-max_rows // ROW_BLOCK) + N_EXPERTS
    blk = jnp.arange(n_blocks, dtype=jnp.int32)
    blk_e = jnp.minimum(jnp.sum(blk[:, None] >= blk_ends[None, :], axis=1),
                        N_EXPERTS - 1).astype(jnp.int32)
    n_used = blk_ends[-1:].astype(jnp.int32)
    pad_from = ((blk_ends - nblk) * ROW_BLOCK + rows).astype(jnp.int32)
    pad_n = (nblk * ROW_BLOCK - rows).astype(jnp.int32)

    xs = moe_dispatch(pieces, idx_t, hb, pad_from, pad_n, n_used, n_blocks)
    ys = moe_experts(xs, blk_e, n_used, w1, w3, w2, layer)
    return moe_combine(pieces, idx_t, ys, top_w.T, shared, x, gate, fn_g, final_norm)


def kernel(x, c, norm_mix, norm_ffn, ada_w, ada_b, da_w_in, da_w_out, da_lq1, da_lk1, da_lq2,
           da_lk2, da_subln, ret_w_in, ret_w_out, ret_gn, router_w, router_b, exp_w1, exp_w3,
           exp_w2, sh_w1, sh_w3, sh_w2, final_norm):
    depth, d = norm_mix.shape
    b = x.shape[0]
    mod = ada_modulation(c, ada_w, ada_b).reshape(depth, b, 6, 1, d)
    for i in range(depth):
        sh_a, sc_a, g_a, sh_f, sc_f, g_f = (mod[i, :, j] for j in range(6))
        j = i // 2
        if i % 2 == 0:
            proj = norm_proj(x, norm_mix[i], sc_a, sh_a, da_w_in[j].astype(_bf16))
            y = diff_attention_core(proj, da_lq1[j], da_lk1[j], da_lq2[j], da_lk2[j],
                                    da_subln[j], i)
            x = out_proj_residual(y, da_w_out[j].astype(_bf16), x, g_a)
        else:
            proj = norm_proj(x, norm_mix[i], sc_a, sh_a, ret_w_in[j].astype(_bf16))
            y = retention_core(proj, ret_gn[j])
            x = out_proj_residual(y, ret_w_out[j].astype(_bf16), x, g_a)
        x = moe_layer(x, norm_ffn[i], sc_f, sh_f, g_f, router_w[i], router_b[i],
                      exp_w1, exp_w3, exp_w2, i,
                      sh_w1[i], sh_w3[i], sh_w2[i], final_norm, i == depth - 1)
    return x
```

```python
import functools
import math

import jax
import jax.numpy as jnp
from jax import lax
from jax.experimental import pallas as pl
from jax.experimental.pallas import tpu as pltpu

EPS = 1e-6
DA_HEADS = 8
DA_HEAD_DIM = 64
DA_V_DIM = 2 * DA_HEAD_DIM
RET_HEADS = 4
RET_DK = 256
RET_DV = 512
RET_CHUNK = 128
N_EXPERTS = 64
TOP_K = 8
N_GROUPS = 8
GROUP_SIZE = N_EXPERTS // N_GROUPS
TOPK_GROUPS = 4
ROUTED_SCALE = 2.5
ROW_BLOCK_LOG2 = 10
ROW_BLOCK = 1 << ROW_BLOCK_LOG2
EXPERT_ROWS = 256
TOKEN_TILE = 256
SEG_ALIGN = 8
PERM_ROWS = 512
PIECE_SLOTS = 384

VMEM_LIMIT = 56 * 1024 * 1024
NEG_BIG = -1e30

_f32 = jnp.float32
_bf16 = jnp.bfloat16


def _cparams(*sem):
    return pltpu.CompilerParams(dimension_semantics=sem, vmem_limit_bytes=VMEM_LIMIT)


def _silu(v):
    return v * (1.0 / (1.0 + jnp.exp(-v)))


def _dot(a, b):
    return jnp.dot(a, b, preferred_element_type=_f32)


def _dot_nt(a, b):
    return lax.dot_general(a, b, (((1,), (1,)), ((), ())), preferred_element_type=_f32)


def _dot_tn(a, b):
    return lax.dot_general(a, b, (((0,), (0,)), ((), ())), preferred_element_type=_f32)


def _ada_kernel(c_ref, w_ref, b_ref, o_ref):
    a = _silu(c_ref[...]).astype(_bf16)
    o_ref[0] = _dot(a, w_ref[0].astype(_bf16)) + b_ref[0]


def ada_modulation(c, ada_w, ada_b):
    depth, d, n = ada_w.shape
    b = c.shape[0]
    tn = 1024
    return pl.pallas_call(
        _ada_kernel,
        grid=(depth, n // tn),
        in_specs=[
            pl.BlockSpec((b, d), lambda l, j: (0, 0)),
            pl.BlockSpec((1, d, tn), lambda l, j: (l, 0, j)),
            pl.BlockSpec((1, 1, tn), lambda l, j: (l, 0, j)),
        ],
        out_specs=pl.BlockSpec((1, b, tn), lambda l, j: (l, 0, j)),
        out_shape=jax.ShapeDtypeStruct((depth, b, n), _f32),
        compiler_params=_cparams("parallel", "parallel"),
        name="ada_modulation",
    )(c, ada_w, ada_b.reshape(depth, 1, n))


def _norm_mod(x, g, sc, sh):
    y = x * lax.rsqrt(jnp.mean(x * x, axis=-1, keepdims=True) + EPS)
    return (y * g) * (1.0 + sc) + sh


def _norm_proj_kernel(x_ref, g_ref, sc_ref, sh_ref, w_ref, o_ref, h_ref):
    @pl.when(pl.program_id(2) == 0)
    def _():
        h_ref[...] = _norm_mod(x_ref[0], g_ref[...], sc_ref[0], sh_ref[0]).astype(_bf16)

    o_ref[0] = _dot(h_ref[...], w_ref[...]).astype(o_ref.dtype)


def norm_proj(x, g, sc, sh, w):
    b, s, d = x.shape
    n = w.shape[1]
    tm = min(1024, s)
    tn = 1024
    return pl.pallas_call(
        _norm_proj_kernel,
        grid=(b, s // tm, n // tn),
        in_specs=[
            pl.BlockSpec((1, tm, d), lambda i, m, j: (i, m, 0)),
            pl.BlockSpec((1, d), lambda i, m, j: (0, 0)),
            pl.BlockSpec((1, 1, d), lambda i, m, j: (i, 0, 0)),
            pl.BlockSpec((1, 1, d), lambda i, m, j: (i, 0, 0)),
            pl.BlockSpec((d, tn), lambda i, m, j: (0, j)),
        ],
        out_specs=pl.BlockSpec((1, tm, tn), lambda i, m, j: (i, m, j)),
        out_shape=jax.ShapeDtypeStruct((b, s, n), _bf16),
        scratch_shapes=[pltpu.VMEM((tm, d), _bf16)],
        compiler_params=_cparams("parallel", "parallel", "arbitrary"),
        name="norm_proj",
    )(x, g.reshape(1, d), sc, sh, w)


def _out_proj_kernel(y_ref, w_ref, x_ref, g_ref, o_ref):
    o_ref[0] = x_ref[0] + g_ref[0] * _dot(y_ref[0], w_ref[...])


def out_proj_residual(y, w, x, gate):
    b, s, k = y.shape
    d = w.shape[1]
    tm = min(512, s)
    return pl.pallas_call(
        _out_proj_kernel,
        grid=(b, s // tm),
        in_specs=[
            pl.BlockSpec((1, tm, k), lambda i, m: (i, m, 0)),
            pl.BlockSpec((k, d), lambda i, m: (0, 0)),
            pl.BlockSpec((1, tm, d), lambda i, m: (i, m, 0)),
            pl.BlockSpec((1, 1, d), lambda i, m: (i, 0, 0)),
        ],
        out_specs=pl.BlockSpec((1, tm, d), lambda i, m: (i, m, 0)),
        out_shape=jax.ShapeDtypeStruct((b, s, d), _f32),
        compiler_params=_cparams("parallel", "parallel"),
        name="out_proj_residual",
    )(y, w, x, gate)


def _diff_attn_kernel(q_ref, k_ref, v_ref, lq1_ref, lk1_ref, lq2_ref, lk2_ref, sg_ref, o_ref,
                      bias_ref, *, tq, lam_init):
    h = pl.program_id(1)
    dh, dv = DA_HEAD_DIM, DA_V_DIM
    n_q = k_ref.shape[1] // tq

    hp1 = jnp.full((1, 1), h + 1, jnp.int32).astype(_f32)
    slope = jnp.exp2(hp1 * (-8.0 / DA_HEADS))
    dist = ((n_q - 1) * tq + lax.broadcasted_iota(jnp.int32, bias_ref.shape, 0)
            - lax.broadcasted_iota(jnp.int32, bias_ref.shape, 1)).astype(_f32)
    bias_ref[...] = -slope * dist

    lam = (jnp.exp(jnp.sum(lq1_ref[...] * lk1_ref[...], axis=-1, keepdims=True))
           - jnp.exp(jnp.sum(lq2_ref[...] * lk2_ref[...], axis=-1, keepdims=True)) + lam_init)
    causal = (lax.broadcasted_iota(jnp.int32, (tq, tq), 0)
              >= lax.broadcasted_iota(jnp.int32, (tq, tq), 1))
    lane = lax.broadcasted_iota(jnp.int32, (tq, 2 * dh), 1)

    def attend(c):
        q = q_ref[0, c * tq:(c + 1) * tq, :] * (dh ** -0.5)
        zero = jnp.zeros_like(q)
        q_maps = (jnp.where(lane < dh, q, zero), jnp.where(lane >= dh, q, zero))
        first = (n_q - 1 - c) * tq
        blocks = [slice(j * tq, (j + 1) * tq) for j in range(c + 1)]
        tiles = [[_dot_nt(q_m, k_ref[0, blk, :])
                  + bias_ref[:, first + blk.start:first + blk.stop] for q_m in q_maps]
                 for blk in blocks]
        tiles[c] = [jnp.where(causal, s, NEG_BIG) for s in tiles[c]]
        outs = []
        for mi in range(2):
            parts = [row[mi] for row in tiles]
            m = functools.reduce(jnp.maximum, [jnp.max(x, axis=-1, keepdims=True) for x in parts])
            ps = [jnp.exp(x - m) for x in parts]
            l = functools.reduce(jnp.add, [jnp.sum(x, axis=-1, keepdims=True) for x in ps])
            o = functools.reduce(jnp.add, [_dot(x.astype(_bf16), v_ref[0, blk, :])
                                           for x, blk in zip(ps, blocks)])
            outs.append(o / l)
        o = outs[0] - lam * outs[1]
        o = o * lax.rsqrt(jnp.mean(o * o, axis=-1, keepdims=True) + EPS)
        o_ref[0, c * tq:(c + 1) * tq, :] = ((o * sg_ref[...]) * (1.0 - lam_init)).astype(o_ref.dtype)

    for c in range(n_q):
        attend(c)


def diff_attention_core(proj, lq1, lk1, lq2, lk2, subln_g, layer_idx):
    b, s, _ = proj.shape
    nh, dh, dv = DA_HEADS, DA_HEAD_DIM, DA_V_DIM
    tq = min(512, s)
    lam_init = 0.8 - 0.6 * math.exp(-0.3 * layer_idx)
    vec = lambda n: pl.BlockSpec((1, n), lambda i, h: (0, 0))
    return pl.pallas_call(
        functools.partial(_diff_attn_kernel, tq=tq, lam_init=lam_init),
        grid=(b, nh),
        in_specs=[
            pl.BlockSpec((1, s, 2 * dh), lambda i, h: (i, 0, h)),
            pl.BlockSpec((1, s, 2 * dh), lambda i, h: (i, 0, nh + h)),
            pl.BlockSpec((1, s, dv), lambda i, h: (i, 0, 2 * nh + h)),
            vec(dh), vec(dh), vec(dh), vec(dh), vec(dv),
        ],
        out_specs=pl.BlockSpec((1, s, dv), lambda i, h: (i, 0, h)),
        out_shape=jax.ShapeDtypeStruct((b, s, nh * dv), _bf16),
        scratch_shapes=[pltpu.VMEM((tq, s), _f32)],
        compiler_params=_cparams("parallel", "parallel"),
        name="diff_attention",
    )(proj, proj, proj, lq1.reshape(1, dh), lk1.reshape(1, dh), lq2.reshape(1, dh),
      lk2.reshape(1, dh), subln_g.reshape(1, dv))


def _retention_kernel(q_ref, k_ref, v_ref, g_ref, gn_ref, o_ref, r_ref, *, chunk):
    h = pl.program_id(1)
    c = pl.program_id(2)
    dk = RET_DK

    @pl.when(c == 0)
    def _():
        r_ref[...] = jnp.zeros_like(r_ref)

    hf = jnp.full((1, 1), h, jnp.int32).astype(_f32)
    log_g = jnp.log(1.0 - jnp.exp2(-5.0 - hf))
    ri = lax.broadcasted_iota(jnp.int32, (chunk, chunk), 0)
    ci = lax.broadcasted_iota(jnp.int32, (chunk, chunk), 1)
    diff = (ri - ci).astype(_f32)
    decay = jnp.where(diff >= 0, jnp.exp(log_g * jnp.maximum(diff, 0.0)), 0.0)
    idx = lax.broadcasted_iota(jnp.int32, (chunk, 1), 0).astype(_f32)
    zeta = jnp.exp(log_g * (chunk - 1 - idx))
    xi = jnp.exp(log_g * (idx + 1.0))
    g_c = jnp.exp(log_g * chunk)

    decay = decay * (dk ** -0.5)
    zeta = zeta * (dk ** -0.5)
    chunks = [pl.ds(sub * chunk, chunk) for sub in range(q_ref.shape[1] // chunk)]
    intra, updates = [], []
    for rows in chunks:
        q, k, v = q_ref[0, rows, :], k_ref[0, rows, :], v_ref[0, rows, :]
        intra.append(_dot((_dot_nt(q, k) * decay).astype(_bf16), v))
        updates.append(_dot_tn((k.astype(_f32) * zeta).astype(_bf16), v))
    r = r_ref[...]
    for rows, o, upd in zip(chunks, intra, updates):
        o = o + _dot(q_ref[0, rows, :], r.astype(_bf16)) * xi
        r = g_c * r + upd
        o = o * lax.rsqrt(jnp.mean(o * o, axis=-1, keepdims=True) + EPS) * gn_ref[...]
        o_ref[0, rows, :] = (_silu(g_ref[0, rows, :].astype(_f32)) * o).astype(o_ref.dtype)
    r_ref[...] = r


def retention_core(proj, gn_g):
    b, s, _ = proj.shape
    nh, dk, dv = RET_HEADS, RET_DK, RET_DV
    chunk = RET_CHUNK
    rows = min(8 * chunk, s)
    return pl.pallas_call(
        functools.partial(_retention_kernel, chunk=chunk),
        grid=(b, nh, s // rows),
        in_specs=[
            pl.BlockSpec((1, rows, dk), lambda i, h, c: (i, c, h)),
            pl.BlockSpec((1, rows, dk), lambda i, h, c: (i, c, nh + h)),
            pl.BlockSpec((1, rows, dv), lambda i, h, c: (i, c, (2 * nh * dk) // dv + h)),
            pl.BlockSpec((1, rows, dv), lambda i, h, c: (i, c, (2 * nh * dk) // dv + nh + h)),
            pl.BlockSpec((1, dv), lambda i, h, c: (0, h)),
        ],
        out_specs=pl.BlockSpec((1, rows, dv), lambda i, h, c: (i, c, h)),
        out_shape=jax.ShapeDtypeStruct((b, s, nh * dv), _bf16),
        scratch_shapes=[pltpu.VMEM((dk, dv), _f32)],
        compiler_params=_cparams("parallel", "parallel", "arbitrary"),
        name="retention",
    )(proj, proj, proj, proj, gn_g.reshape(1, nh * dv))


def _pack_rows(h):
    n = h.shape[1] // 2
    lo = pltpu.bitcast(h[:, :n].astype(_bf16).astype(_f32), jnp.uint32)
    hi = pltpu.bitcast(h[:, n:].astype(_bf16).astype(_f32), jnp.uint32)
    return (lo >> 16) | (hi & jnp.uint32(0xFFFF0000))


def _unpack_rows(p):
    lo = pltpu.bitcast(p << 16, _f32).astype(_bf16)
    hi = pltpu.bitcast(p & jnp.uint32(0xFFFF0000), _f32).astype(_bf16)
    return lo, hi


def _first_argmax(v, iota, axis, big):
    m = jnp.max(v, axis=axis, keepdims=True)
    i = jnp.min(jnp.where(v == m, iota, big), axis=axis, keepdims=True)
    return m, i


def _router_kernel(x_ref, g_ref, sc_ref, sh_ref, rwt_ref, rb_ref, s1_ref, s3_ref, s2_ref,
                   hb_ref, idx_ref, wgt_ref, shared_ref):
    tm = x_ref.shape[1]
    h = _norm_mod(x_ref[0], g_ref[...], sc_ref[0], sh_ref[0])
    hb = h.astype(_bf16)
    hb_ref[...] = hb
    act = (_silu(_dot(hb, s1_ref[...])) * _dot(hb, s3_ref[...])).astype(_bf16)
    shared_ref[...] = _dot(act, s2_ref[...])

    scores = 1.0 / (1.0 + jnp.exp(-_dot_nt(rwt_ref[...], h)))
    choice = scores + rb_ref[...]
    c3 = choice.reshape(N_GROUPS, GROUP_SIZE, tm)
    mem = lax.broadcasted_iota(jnp.int32, c3.shape, 1)
    m1, i1 = _first_argmax(c3, mem, 1, GROUP_SIZE)
    m2 = jnp.max(jnp.where(mem == i1, -jnp.inf, c3), axis=1, keepdims=True)
    gs = m1 + m2
    gi = lax.broadcasted_iota(jnp.int32, gs.shape, 0)
    sel = jnp.zeros(gs.shape, _f32)
    for _ in range(TOPK_GROUPS):
        _, ig = _first_argmax(gs, gi, 0, N_GROUPS)
        hit = gi == ig
        sel = jnp.where(hit, 1.0, sel)
        gs = jnp.where(hit, -jnp.inf, gs)
    cm = jnp.where(sel > 0.0, c3, -jnp.inf).reshape(N_EXPERTS, tm)
    ei = lax.broadcasted_iota(jnp.int32, cm.shape, 0)
    idxs, wgts = [], []
    for _ in range(TOP_K):
        _, ie = _first_argmax(cm, ei, 0, N_EXPERTS)
        hit = ei == ie
        idxs.append(ie)
        wgts.append(jnp.sum(jnp.where(hit, scores, 0.0), axis=0, keepdims=True))
        cm = jnp.where(hit, -jnp.inf, cm)
    w = jnp.concatenate(wgts, axis=0)
    idx_ref[...] = jnp.concatenate(idxs, axis=0)
    wgt_ref[...] = w / jnp.sum(w, axis=0, keepdims=True) * ROUTED_SCALE


def moe_router(x, g, sc, sh, router_w, router_b, s_w1, s_w3, s_w2):
    b, s, d = x.shape
    t = b * s
    tm = min(512, s)
    n_s = s // tm
    sf = s_w1.shape[1]
    const = lambda shape: pl.BlockSpec(shape, lambda i, m: (0,) * len(shape))
    return pl.pallas_call(
        _router_kernel,
        grid=(b, n_s),
        in_specs=[
            pl.BlockSpec((1, tm, d), lambda i, m: (i, m, 0)),
            const((1, d)),
            pl.BlockSpec((1, 1, d), lambda i, m: (i, 0, 0)),
            pl.BlockSpec((1, 1, d), lambda i, m: (i, 0, 0)),
            const((N_EXPERTS, d)), const((N_EXPERTS, 1)),
            const((d, sf)), const((d, sf)), const((sf, d)),
        ],
        out_specs=[
            pl.BlockSpec((tm, d), lambda i, m: (i * n_s + m, 0)),
            pl.BlockSpec((TOP_K, tm), lambda i, m: (0, i * n_s + m)),
            pl.BlockSpec((TOP_K, tm), lambda i, m: (0, i * n_s + m)),
            pl.BlockSpec((tm, d), lambda i, m: (i * n_s + m, 0)),
        ],
        out_shape=[
            jax.ShapeDtypeStruct((t, d), _bf16),
            jax.ShapeDtypeStruct((TOP_K, t), jnp.int32),
            jax.ShapeDtypeStruct((TOP_K, t), _f32),
            jax.ShapeDtypeStruct((t, d), _f32),
        ],
        compiler_params=_cparams("parallel", "parallel"),
        name="moe_router",
    )(x, g.reshape(1, d), sc, sh, router_w.T, router_b.reshape(N_EXPERTS, 1),
      s_w1.astype(_bf16), s_w3.astype(_bf16), s_w2.astype(_bf16))


def _strict_lower(n):
    return jnp.where(lax.broadcasted_iota(jnp.int32, (n, n), 1)
                     < lax.broadcasted_iota(jnp.int32, (n, n), 0), 1.0, 0.0).astype(_bf16)


def _prefix_over_experts(col):
    wide = jnp.broadcast_to(col, (N_EXPERTS, 128)).astype(_bf16)
    return _dot(_strict_lower(N_EXPERTS), wide)[:, :1]


def _rank_kernel(idx_ref, piece_ref, tot_ref, base_ref):
    phase = pl.program_id(0)
    i = pl.program_id(1)
    tm = TOKEN_TILE

    @pl.when(jnp.logical_and(phase == 0, i == 0))
    def _():
        base_ref[...] = jnp.zeros_like(base_ref)

    @pl.when(jnp.logical_and(phase == 1, i == 0))
    def _():
        tot = base_ref[...].astype(jnp.int32)
        tot_ref[...] = tot
        nblk = (tot + (ROW_BLOCK - 1)) >> ROW_BLOCK_LOG2
        start_blk = (32.0 * _prefix_over_experts((nblk >> 5).astype(_f32))
                     + _prefix_over_experts((nblk & 31).astype(_f32)))
        base_ref[...] = start_blk * float(ROW_BLOCK)

    ei = lax.broadcasted_iota(jnp.int32, (N_EXPERTS, tm), 0)
    for sub in range(idx_ref.shape[1] // tm):
        idx = idx_ref[:, sub * tm:(sub + 1) * tm]
        memf = jnp.zeros((N_EXPERTS, tm), _f32)
        for k in range(TOP_K):
            memf = jnp.where(ei == idx[k:k + 1, :], 1.0, memf)
        tile_cnt = jnp.sum(memf, axis=1, keepdims=True).astype(jnp.int32)
        seg_rows = ((tile_cnt + (SEG_ALIGN - 1)) & (-SEG_ALIGN)).astype(_f32)

        @pl.when(phase == 0)
        def _(seg_rows=seg_rows):
            base_ref[...] += seg_rows

        @pl.when(phase == 1)
        def _(seg_rows=seg_rows, sub=sub):
            seg_i = seg_rows.astype(jnp.int32)
            lo = (16.0 * _prefix_over_experts((seg_i >> 4).astype(_f32))
                  + _prefix_over_experts((seg_i & 15).astype(_f32)))
            lane = lax.broadcasted_iota(jnp.int32, (N_EXPERTS, PIECE_SLOTS), 1)
            row = (lane * SEG_ALIGN).astype(_f32)
            inside = jnp.logical_and(row >= lo, row < lo + seg_rows)
            dst = jnp.sum(jnp.where(inside, base_ref[...] + (row - lo), 0.0), axis=0,
                          keepdims=True)
            total = jnp.sum(seg_rows, axis=0, keepdims=True)
            piece_ref[sub] = jnp.where(lane[:1] == PIECE_SLOTS - 1, total, dst).astype(jnp.int32)
            base_ref[...] += seg_rows


def moe_rank(top_idx):
    k, t = top_idx.shape
    tm = TOKEN_TILE
    assert _tile_buffer_rows(tm) // SEG_ALIGN < PIECE_SLOTS
    per_step = 4 if t % (4 * tm) == 0 else 1
    return pl.pallas_call(
        _rank_kernel,
        grid=(2, t // (per_step * tm)),
        in_specs=[pl.BlockSpec((k, per_step * tm), lambda p, i: (0, i))],
        out_specs=[pl.BlockSpec((per_step, 1, PIECE_SLOTS), lambda p, i: (i * p, 0, 0)),
                   pl.BlockSpec((N_EXPERTS, 1), lambda p, i: (0, 0))],
        out_shape=[jax.ShapeDtypeStruct((t // tm, 1, PIECE_SLOTS), jnp.int32),
                   jax.ShapeDtypeStruct((N_EXPERTS, 1), jnp.int32)],
        scratch_shapes=[pltpu.VMEM((N_EXPERTS, 1), _f32)],
        compiler_params=_cparams("arbitrary", "arbitrary"),
        name="moe_rank",
    )(top_idx)


def _aligned_rows(ref, start, n):
    return ref.at[pl.ds(pl.multiple_of(start, SEG_ALIGN), n)]


def _tile_rows(piece_ref):
    return piece_ref[0, 0, PIECE_SLOTS - 1]


def _for_each_piece(piece_ref, fn):
    def piece(g, c):
        fn(g * SEG_ALIGN, piece_ref[0, 0, g])
        return c

    lax.fori_loop(0, _tile_rows(piece_ref) // SEG_ALIGN, piece, 0)


def _wait_pieces(n_pieces, make_copy):
    for bit in reversed(range(PIECE_SLOTS.bit_length())):
        @pl.when((n_pieces >> bit) & 1 == 1)
        def _(bit=bit):
            make_copy(1 << bit).wait()


def _prefix_over_lanes(row):
    n = row.shape[1]
    wide = jnp.broadcast_to(row, (SEG_ALIGN, n)).astype(_bf16)
    return _dot_nt(wide, _strict_lower(n))[:1, :]


def _tile_layout(idx_t):
    tm = idx_t.shape[0]
    lane_e = lax.broadcasted_iota(jnp.int32, (tm, N_EXPERTS), 1)
    member = jnp.zeros((tm, N_EXPERTS), _f32)
    for k in range(TOP_K):
        member = jnp.where(lane_e == idx_t[:, k:k + 1], 1.0, member)
    before = _dot(_strict_lower(tm), member.astype(_bf16))
    count = jnp.sum(member, axis=0, keepdims=True).astype(jnp.int32)
    seg_rows = (count + (SEG_ALIGN - 1)) & (-SEG_ALIGN)
    seg_lo = (16.0 * _prefix_over_lanes((seg_rows >> 4).astype(_f32))
              + _prefix_over_lanes((seg_rows & 15).astype(_f32)))
    return member, before, seg_lo, seg_lo + seg_rows.astype(_f32)


def _rows_of_chunk(r0, seg_lo, seg_hi):
    row = (lax.broadcasted_iota(jnp.int32, (PERM_ROWS, N_EXPERTS), 0) + r0).astype(_f32)
    inside = jnp.logical_and(row >= seg_lo, row < seg_hi)
    rank = jnp.sum(jnp.where(inside, row - seg_lo, 0.0), axis=1, keepdims=True)
    return jnp.where(inside, 1.0, 0.0).astype(_bf16), rank


def _dispatch_kernel(pad_from_ref, pad_n_ref, nb_ref, piece_ref, idx_ref, hb_ref, xs_ref,
                     xbuf_ref, zero_ref, sent_ref, sem, zsem):
    tm = hb_ref.shape[0]
    n_blocks = xs_ref.shape[0] // ROW_BLOCK

    @pl.when(pl.program_id(0) == 0)
    def _():
        zero_ref[...] = jnp.zeros_like(zero_ref)

        def pad_expert(e, carry):
            off = pad_from_ref[e]

            def pad_piece(r, c):
                pltpu.make_async_copy(_aligned_rows(zero_ref, 0, SEG_ALIGN),
                                      _aligned_rows(xs_ref, off + SEG_ALIGN * r, SEG_ALIGN),
                                      zsem).start()
                return c + 1

            return lax.fori_loop(0, pad_n_ref[e] // SEG_ALIGN, pad_piece, carry)

        n_pieces = lax.fori_loop(0, N_EXPERTS, pad_expert, 0)

        def pad_tail(b, carry):
            pltpu.make_async_copy(zero_ref, xs_ref.at[pl.ds(b * ROW_BLOCK, ROW_BLOCK)],
                                  zsem).start()
            return carry

        lax.fori_loop(nb_ref[0], n_blocks, pad_tail, 0)

        def wait_piece(r, c):
            pltpu.make_async_copy(_aligned_rows(zero_ref, 0, SEG_ALIGN),
                                  _aligned_rows(xs_ref, 0, SEG_ALIGN), zsem).wait()
            return c

        def wait_tail(b, c):
            pltpu.make_async_copy(zero_ref, xs_ref.at[pl.ds(0, ROW_BLOCK)], zsem).wait()
            return c

        lax.fori_loop(0, n_pieces, wait_piece, 0)
        lax.fori_loop(nb_ref[0], n_blocks, wait_tail, 0)

    step = pl.program_id(0)
    slot = step % 2
    total = _tile_rows(piece_ref)
    member, before, seg_lo, seg_hi = _tile_layout(idx_ref[...])
    place = jnp.where(member > 0.0, before + 1.0, -1.0).astype(_bf16)

    def permute(c, carry):
        r0 = c * PERM_ROWS
        expert_of_row, rank_of_row = _rows_of_chunk(r0, seg_lo, seg_hi)
        row = lax.broadcasted_iota(jnp.int32, rank_of_row.shape, 0) + r0
        want = jnp.where(row < total, rank_of_row + 1.0, -5.0)
        sel = jnp.where(_dot_nt(expert_of_row, place) == want, 1.0, 0.0)
        picked = _dot(sel.astype(_bf16), hb_ref[...])
        half = picked.shape[1] // 2
        xbuf_ref[slot, pl.ds(r0, PERM_ROWS), :] = (
            (pltpu.bitcast(picked[:, :half], jnp.uint32) >> 16)
            | pltpu.bitcast(picked[:, half:], jnp.uint32))
        return carry

    for c in range(xbuf_ref.shape[1] // PERM_ROWS):
        permute(c, 0)

    wait_sent = lambda n_pieces: _wait_pieces(n_pieces, lambda k: pltpu.make_async_copy(
        xbuf_ref.at[slot, pl.ds(0, k * SEG_ALIGN)], xs_ref.at[pl.ds(0, k * SEG_ALIGN)], sem))

    @pl.when(step > 0)
    def _():
        wait_sent(sent_ref[0])

    for s in range(2):
        @pl.when(slot == s)
        def _(s=s):
            def send(off, dst):
                pltpu.make_async_copy(_aligned_rows(xbuf_ref.at[s], off, SEG_ALIGN),
                                      _aligned_rows(xs_ref, dst, SEG_ALIGN), sem).start()

            _for_each_piece(piece_ref, send)

    sent_ref[0] = total // SEG_ALIGN

    @pl.when(step == pl.num_programs(0) - 1)
    def _():
        wait_sent(total // SEG_ALIGN)


def _tile_buffer_rows(tm):
    worst = tm * TOP_K + N_EXPERTS * (SEG_ALIGN - 1)
    return -(-worst // PERM_ROWS) * PERM_ROWS


def moe_dispatch(pieces, idx_t, hb, pad_from, pad_n, n_used, n_blocks):
    t, d = hb.shape
    tm = TOKEN_TILE
    grid_spec = pltpu.PrefetchScalarGridSpec(
        num_scalar_prefetch=3,
        grid=(t // tm,),
        in_specs=[
            pl.BlockSpec((1, 1, PIECE_SLOTS), lambda i, *_: (i, 0, 0), memory_space=pltpu.SMEM),
            pl.BlockSpec((tm, TOP_K), lambda i, *_: (i, 0)),
            pl.BlockSpec((tm, d), lambda i, *_: (i, 0)),
        ],
        out_specs=pl.BlockSpec(memory_space=pl.ANY),
        scratch_shapes=[pltpu.VMEM((2, _tile_buffer_rows(tm), d // 2), jnp.uint32),
                        pltpu.VMEM((ROW_BLOCK, d // 2), jnp.uint32),
                        pltpu.SMEM((1,), jnp.int32),
                        pltpu.SemaphoreType.DMA(()), pltpu.SemaphoreType.DMA(())],
    )
    return pl.pallas_call(
        _dispatch_kernel,
        grid_spec=grid_spec,
        out_shape=jax.ShapeDtypeStruct((n_blocks * ROW_BLOCK, d // 2), jnp.uint32),
        compiler_params=_cparams("arbitrary"),
        name="moe_dispatch",
    )(pad_from, pad_n, n_used, pieces, idx_t, hb)


def _expert_kernel(be_ref, nb_ref, xs_ref, w1_ref, w3_ref, w2_ref, y_ref, w1b, w3b, w2b):
    i = pl.program_id(0)

    @pl.when(jnp.logical_or(i == 0, be_ref[i] != be_ref[jnp.maximum(i - 1, 0)]))
    def _():
        w1b[...] = w1_ref[0, 0].astype(_bf16)
        w3b[...] = w3_ref[0, 0].astype(_bf16)
        w2b[...] = w2_ref[0, 0].astype(_bf16)

    @pl.when(i < nb_ref[0])
    def _():
        chains = [pl.ds(c * EXPERT_ROWS, EXPERT_ROWS) for c in range(ROW_BLOCK // EXPERT_ROWS)]
        xs = [jnp.concatenate(_unpack_rows(xs_ref[rows, :]), axis=1) for rows in chains]
        acts = [(_silu(_dot(x, w1b[...])) * _dot(x, w3b[...])).astype(_bf16) for x in xs]
        for rows, act in zip(chains, acts):
            y_ref[rows, :] = _pack_rows(_dot(act, w2b[...]))

    @pl.when(i >= nb_ref[0])
    def _():
        y_ref[...] = jnp.zeros_like(y_ref)


def moe_experts(xs, blk_e, n_used, w1, w3, w2, layer):
    p, half = xs.shape
    d = 2 * half
    ff = w1.shape[3]
    nb = p // ROW_BLOCK
    row = lambda i, be, nu: (jnp.minimum(i, nu[0] - 1), 0)
    wsel = lambda i, be, nu: (layer, be[i], 0, 0)
    grid_spec = pltpu.PrefetchScalarGridSpec(
        num_scalar_prefetch=2,
        grid=(nb,),
        in_specs=[
            pl.BlockSpec((ROW_BLOCK, half), row),
            pl.BlockSpec((1, 1, d, ff), wsel),
            pl.BlockSpec((1, 1, d, ff), wsel),
            pl.BlockSpec((1, 1, ff, d), wsel),
        ],
        out_specs=pl.BlockSpec((ROW_BLOCK, half), lambda i, be, nu: (i, 0)),
        scratch_shapes=[pltpu.VMEM((d, ff), _bf16), pltpu.VMEM((d, ff), _bf16),
                        pltpu.VMEM((ff, d), _bf16)],
    )
    return pl.pallas_call(
        _expert_kernel,
        grid_spec=grid_spec,
        out_shape=jax.ShapeDtypeStruct((p, half), jnp.uint32),
        compiler_params=_cparams("arbitrary"),
        name="moe_experts",
    )(blk_e, n_used, xs, w1, w3, w2)


def _combine_kernel(piece_ref, next_piece_ref, ys_ref, idx_ref, w_ref, shared_ref, x_ref, gate_ref,
                    fn_ref, o_ref, ybuf_ref, sem, *, final_norm):
    tm = x_ref.shape[1]
    half = ybuf_ref.shape[2]
    step = pl.program_id(0) * pl.num_programs(1) + pl.program_id(1)
    n_steps = pl.num_programs(0) * pl.num_programs(1)
    slot = step % 2

    def fetch(table_ref, s):
        def one(off, src):
            pltpu.make_async_copy(_aligned_rows(ys_ref, src, SEG_ALIGN),
                                  _aligned_rows(ybuf_ref.at[s], off, SEG_ALIGN), sem.at[s]).start()

        _for_each_piece(table_ref, one)

    @pl.when(step == 0)
    def _():
        fetch(piece_ref, 0)

    for s in range(2):
        @pl.when(jnp.logical_and(step + 1 < n_steps, slot == 1 - s))
        def _(s=s):
            fetch(next_piece_ref, s)

    total = _tile_rows(piece_ref)
    _wait_pieces(total // SEG_ALIGN, lambda k: pltpu.make_async_copy(
        ys_ref.at[pl.ds(0, k * SEG_ALIGN)], ybuf_ref.at[slot, pl.ds(0, k * SEG_ALIGN)],
        sem.at[slot]))

    idx_t = idx_ref[...]
    member, before, seg_lo, seg_hi = _tile_layout(idx_t)
    before_b = before.astype(_bf16)
    lane_e = lax.broadcasted_iota(jnp.int32, member.shape, 1)
    w_of_expert = jnp.zeros(member.shape, _f32)
    for k in range(TOP_K):
        w_of_expert = jnp.where(lane_e == idx_t[:, k:k + 1], w_ref[:, k:k + 1], w_of_expert)
    w_of_expert = w_of_expert.astype(_bf16)
    lo8 = jnp.broadcast_to(seg_lo, (SEG_ALIGN, N_EXPERTS)).astype(jnp.int32) >> 3
    lo_head, lo_rest = (lo8 >> 4).astype(_f32).astype(_bf16), (lo8 & 15).astype(_f32).astype(_bf16)

    def weights_of_chunk(c):
        r0 = c * PERM_ROWS
        expert_of_row, rank_of_row = _rows_of_chunk(r0, seg_lo, seg_hi)
        del rank_of_row
        start = (128.0 * _dot_nt(lo_head, expert_of_row) + 8.0 * _dot_nt(lo_rest, expert_of_row))
        rank_row = (lax.broadcasted_iota(jnp.int32, (1, PERM_ROWS), 1) + r0).astype(_f32) - start[:1]
        hit = _dot_nt(before_b, expert_of_row) == rank_row
        return jnp.where(hit, _dot_nt(w_of_expert, expert_of_row), 0.0).astype(_bf16)

    wm = jnp.concatenate([weights_of_chunk(c) for c in range(ybuf_ref.shape[1] // PERM_ROWS)],
                         axis=1)
    packed = ybuf_ref[slot]
    row = lax.broadcasted_iota(jnp.int32, packed.shape, 0)
    packed = jnp.where(row < total, packed, jnp.uint32(0))
    y_lo, y_hi = _unpack_rows(packed)
    routed = jnp.concatenate([_dot(wm, y_lo), _dot(wm, y_hi)], axis=1)

    xn = x_ref[0] + gate_ref[0] * (shared_ref[...] + routed)
    if final_norm:
        xn = xn * lax.rsqrt(jnp.mean(xn * xn, axis=-1, keepdims=True) + EPS) * fn_ref[...]
    o_ref[0] = xn


def moe_combine(pieces, idx_t, ys, top_w, shared, x, gate, fn_g, final_norm):
    b, s, d = x.shape
    tm = TOKEN_TILE
    n_s = s // tm
    tile = lambda i, m: (i * n_s + m, 0)
    last_tile = b * n_s - 1
    return pl.pallas_call(
        functools.partial(_combine_kernel, final_norm=final_norm),
        grid=(b, n_s),
        in_specs=[
            pl.BlockSpec((1, 1, PIECE_SLOTS), lambda i, m: (i * n_s + m, 0, 0),
                         memory_space=pltpu.SMEM),
            pl.BlockSpec((1, 1, PIECE_SLOTS),
                         lambda i, m: (jnp.minimum(i * n_s + m + 1, last_tile), 0, 0),
                         memory_space=pltpu.SMEM),
            pl.BlockSpec(memory_space=pl.ANY),
            pl.BlockSpec((tm, TOP_K), tile),
            pl.BlockSpec((tm, TOP_K), tile),
            pl.BlockSpec((tm, d), tile),
            pl.BlockSpec((1, tm, d), lambda i, m: (i, m, 0)),
            pl.BlockSpec((1, 1, d), lambda i, m: (i, 0, 0)),
            pl.BlockSpec((1, d), lambda i, m: (0, 0)),
        ],
        out_specs=pl.BlockSpec((1, tm, d), lambda i, m: (i, m, 0)),
        out_shape=jax.ShapeDtypeStruct((b, s, d), _f32),
        scratch_shapes=[pltpu.VMEM((2, _tile_buffer_rows(tm), d // 2), jnp.uint32),
                        pltpu.SemaphoreType.DMA((2,))],
        compiler_params=_cparams("arbitrary", "arbitrary"),
        name="moe_combine",
    )(pieces, pieces, ys, idx_t, top_w, shared, x, gate, fn_g.reshape(1, d))


def moe_layer(x, g, sc, sh, gate, router_w, router_b, w1, w3, w2, layer, s_w1, s_w3, s_w2,
              fn_g, final_norm):
    b, s, d = x.shape
    t = b * s
    hb, top_idx, top_w, shared = moe_router(x, g, sc, sh, router_w, router_b, s_w1, s_w3, s_w2)
    pieces, rows = moe_rank(top_idx)
    idx_t = top_idx.T
    n_tiles = pieces.shape[0]

    rows = rows.reshape(N_EXPERTS)
    nblk = (rows + ROW_BLOCK - 1) // ROW_BLOCK
    blk_ends = jnp.cumsum(nblk)
    max_rows = t * TOP_K + n_tiles * N_EXPERTS * (SEG_ALIGN - 1)
    n_blocks = -(-max_rows // ROW_BLOCK) + N_EXPERTS
    blk = jnp.arange(n_blocks, dtype=jnp.int32)
    blk_e = jnp.minimum(jnp.sum(blk[:, None] >= blk_ends[None, :], axis=1),
                        N_EXPERTS - 1).astype(jnp.int32)
    n_used = blk_ends[-1:].astype(jnp.int32)
    pad_from = ((blk_ends - nblk) * ROW_BLOCK + rows).astype(jnp.int32)
    pad_n = (nblk * ROW_BLOCK - rows).astype(jnp.int32)

    xs = moe_dispatch(pieces, idx_t, hb, pad_from, pad_n, n_used, n_blocks)
    ys = moe_experts(xs, blk_e, n_used, w1, w3, w2, layer)
    return moe_combine(pieces, idx_t, ys, top_w.T, shared, x, gate, fn_g, final_norm)


def kernel(x, c, norm_mix, norm_ffn, ada_w, ada_b, da_w_in, da_w_out, da_lq1, da_lk1, da_lq2,
           da_lk2, da_subln, ret_w_in, ret_w_out, ret_gn, router_w, router_b, exp_w1, exp_w3,
           exp_w2, sh_w1, sh_w3, sh_w2, final_norm):
    depth, d = norm_mix.shape
    b = x.shape[0]
    mod = ada_modulation(c, ada_w, ada_b).reshape(depth, b, 6, 1, d)
    for i in range(depth):
        sh_a, sc_a, g_a, sh_f, sc_f, g_f = (mod[i, :, j] for j in range(6))
        j = i // 2
        if i % 2 == 0:
            proj = norm_proj(x, norm_mix[i], sc_a, sh_a, da_w_in[j].astype(_bf16))
            y = diff_attention_core(proj, da_lq1[j], da_lk1[j], da_lq2[j], da_lk2[j],
                                    da_subln[j], i)
            x = out_proj_residual(y, da_w_out[j].astype(_bf16), x, g_a)
        else:
            proj = norm_proj(x, norm_mix[i], sc_a, sh_a, ret_w_in[j].astype(_bf16))
            y = retention_core(proj, ret_gn[j])
            x = out_proj_residual(y, ret_w_out[j].astype(_bf16), x, g_a)
        x = moe_layer(x, norm_ffn[i], sc_f, sh_f, g_f, router_w[i], router_b[i],
                      exp_w1, exp_w3, exp_w2, i,
                      sh_w1[i], sh_w3[i], sh_w2[i], final_norm, i == depth - 1)
    return x
```

```python
import functools
import math

import jax
import jax.numpy as jnp
from jax import lax
from jax.experimental import pallas as pl
from jax.experimental.pallas import tpu as pltpu

EPS = 1e-6
DA_HEADS = 8
DA_HEAD_DIM = 64
DA_V_DIM = 2 * DA_HEAD_DIM
RET_HEADS = 4
RET_DK = 256
RET_DV = 512
RET_CHUNK = 128
N_EXPERTS = 64
TOP_K = 8
N_GROUPS = 8
GROUP_SIZE = N_EXPERTS // N_GROUPS
TOPK_GROUPS = 4
ROUTED_SCALE = 2.5
ROW_BLOCK_LOG2 = 10
ROW_BLOCK = 1 << ROW_BLOCK_LOG2
EXPERT_ROWS = 256
TOKEN_TILE = 256
SEG_ALIGN = 8
PERM_ROWS = 512
PIECE_SLOTS = 384

VMEM_LIMIT = 56 * 1024 * 1024
NEG_BIG = -1e30

_f32 = jnp.float32
_bf16 = jnp.bfloat16


def _cparams(*sem):
    return pltpu.CompilerParams(dimension_semantics=sem, vmem_limit_bytes=VMEM_LIMIT)


def _silu(v):
    return v * (1.0 / (1.0 + jnp.exp(-v)))


def _dot(a, b):
    return jnp.dot(a, b, preferred_element_type=_f32)


def _dot_nt(a, b):
    return lax.dot_general(a, b, (((1,), (1,)), ((), ())), preferred_element_type=_f32)


def _dot_tn(a, b):
    return lax.dot_general(a, b, (((0,), (0,)), ((), ())), preferred_element_type=_f32)


def _ada_kernel(c_ref, w_ref, b_ref, o_ref):
    a = _silu(c_ref[...]).astype(_bf16)
    o_ref[0] = _dot(a, w_ref[0].astype(_bf16)) + b_ref[0]


def ada_modulation(c, ada_w, ada_b):
    depth, d, n = ada_w.shape
    b = c.shape[0]
    tn = 1024
    return pl.pallas_call(
        _ada_kernel,
        grid=(depth, n // tn),
        in_specs=[
            pl.BlockSpec((b, d), lambda l, j: (0, 0)),
            pl.BlockSpec((1, d, tn), lambda l, j: (l, 0, j)),
            pl.BlockSpec((1, 1, tn), lambda l, j: (l, 0, j)),
        ],
        out_specs=pl.BlockSpec((1, b, tn), lambda l, j: (l, 0, j)),
        out_shape=jax.ShapeDtypeStruct((depth, b, n), _f32),
        compiler_params=_cparams("parallel", "parallel"),
        name="ada_modulation",
    )(c, ada_w, ada_b.reshape(depth, 1, n))


def _norm_mod(x, g, sc, sh):
    y = x * lax.rsqrt(jnp.mean(x * x, axis=-1, keepdims=True) + EPS)
    return (y * g) * (1.0 + sc) + sh


def _norm_proj_kernel(x_ref, g_ref, sc_ref, sh_ref, w_ref, o_ref, h_ref):
    @pl.when(pl.program_id(2) == 0)
    def _():
        h_ref[...] = _norm_mod(x_ref[0], g_ref[...], sc_ref[0], sh_ref[0]).astype(_bf16)

    o_ref[0] = _dot(h_ref[...], w_ref[...]).astype(o_ref.dtype)


def norm_proj(x, g, sc, sh, w):
    b, s, d = x.shape
    n = w.shape[1]
    tm = min(1024, s)
    tn = 1024
    return pl.pallas_call(
        _norm_proj_kernel,
        grid=(b, s // tm, n // tn),
        in_specs=[
            pl.BlockSpec((1, tm, d), lambda i, m, j: (i, m, 0)),
            pl.BlockSpec((1, d), lambda i, m, j: (0, 0)),
            pl.BlockSpec((1, 1, d), lambda i, m, j: (i, 0, 0)),
            pl.BlockSpec((1, 1, d), lambda i, m, j: (i, 0, 0)),
            pl.BlockSpec((d, tn), lambda i, m, j: (0, j)),
        ],
        out_specs=pl.BlockSpec((1, tm, tn), lambda i, m, j: (i, m, j)),
        out_shape=jax.ShapeDtypeStruct((b, s, n), _bf16),
        scratch_shapes=[pltpu.VMEM((tm, d), _bf16)],
        compiler_params=_cparams("parallel", "parallel", "arbitrary"),
        name="norm_proj",
    )(x, g.reshape(1, d), sc, sh, w)


def _out_proj_kernel(y_ref, w_ref, x_ref, g_ref, o_ref):
    o_ref[0] = x_ref[0] + g_ref[0] * _dot(y_ref[0], w_ref[...])


def out_proj_residual(y, w, x, gate):
    b, s, k = y.shape
    d = w.shape[1]
    tm = min(512, s)
    return pl.pallas_call(
        _out_proj_kernel,
        grid=(b, s // tm),
        in_specs=[
            pl.BlockSpec((1, tm, k), lambda i, m: (i, m, 0)),
            pl.BlockSpec((k, d), lambda i, m: (0, 0)),
            pl.BlockSpec((1, tm, d), lambda i, m: (i, m, 0)),
            pl.BlockSpec((1, 1, d), lambda i, m: (i, 0, 0)),
        ],
        out_specs=pl.BlockSpec((1, tm, d), lambda i, m: (i, m, 0)),
        out_shape=jax.ShapeDtypeStruct((b, s, d), _f32),
        compiler_params=_cparams("parallel", "parallel"),
        name="out_proj_residual",
    )(y, w, x, gate)


def _diff_attn_kernel(q_ref, k_ref, v_ref, lq1_ref, lk1_ref, lq2_ref, lk2_ref, sg_ref, o_ref,
                      bias_ref, *, tq, lam_init):
    h = pl.program_id(1)
    dh, dv = DA_HEAD_DIM, DA_V_DIM
    n_q = k_ref.shape[1] // tq

    hp1 = jnp.full((1, 1), h + 1, jnp.int32).astype(_f32)
    slope = jnp.exp2(hp1 * (-8.0 / DA_HEADS))
    dist = ((n_q - 1) * tq + lax.broadcasted_iota(jnp.int32, bias_ref.shape, 0)
            - lax.broadcasted_iota(jnp.int32, bias_ref.shape, 1)).astype(_f32)
    bias_ref[...] = -slope * dist

    lam = (jnp.exp(jnp.sum(lq1_ref[...] * lk1_ref[...], axis=-1, keepdims=True))
           - jnp.exp(jnp.sum(lq2_ref[...] * lk2_ref[...], axis=-1, keepdims=True)) + lam_init)
    causal = (lax.broadcasted_iota(jnp.int32, (tq, tq), 0)
              >= lax.broadcasted_iota(jnp.int32, (tq, tq), 1))
    lane = lax.broadcasted_iota(jnp.int32, (tq, 2 * dh), 1)

    def attend(c):
        q = q_ref[0, c * tq:(c + 1) * tq, :] * (dh ** -0.5)
        zero = jnp.zeros_like(q)
        q_maps = (jnp.where(lane < dh, q, zero), jnp.where(lane >= dh, q, zero))
        first = (n_q - 1 - c) * tq
        blocks = [slice(j * tq, (j + 1) * tq) for j in range(c + 1)]
        tiles = [[_dot_nt(q_m, k_ref[0, blk, :])
                  + bias_ref[:, first + blk.start:first + blk.stop] for q_m in q_maps]
                 for blk in blocks]
        tiles[c] = [jnp.where(causal, s, NEG_BIG) for s in tiles[c]]
        outs = []
        for mi in range(2):
            parts = [row[mi] for row in tiles]
            m = functools.reduce(jnp.maximum, [jnp.max(x, axis=-1, keepdims=True) for x in parts])
            ps = [jnp.exp(x - m) for x in parts]
            l = functools.reduce(jnp.add, [jnp.sum(x, axis=-1, keepdims=True) for x in ps])
            o = functools.reduce(jnp.add, [_dot(x.astype(_bf16), v_ref[0, blk, :])
                                           for x, blk in zip(ps, blocks)])
            outs.append(o / l)
        o = outs[0] - lam * outs[1]
        o = o * lax.rsqrt(jnp.mean(o * o, axis=-1, keepdims=True) + EPS)
        o_ref[0, c * tq:(c + 1) * tq, :] = ((o * sg_ref[...]) * (1.0 - lam_init)).astype(o_ref.dtype)

    for c in range(n_q):
        attend(c)


def diff_attention_core(proj, lq1, lk1, lq2, lk2, subln_g, layer_idx):
    b, s, _ = proj.shape
    nh, dh, dv = DA_HEADS, DA_HEAD_DIM, DA_V_DIM
    tq = min(512, s)
    lam_init = 0.8 - 0.6 * math.exp(-0.3 * layer_idx)
    vec = lambda n: pl.BlockSpec((1, n), lambda i, h: (0, 0))
    return pl.pallas_call(
        functools.partial(_diff_attn_kernel, tq=tq, lam_init=lam_init),
        grid=(b, nh),
        in_specs=[
            pl.BlockSpec((1, s, 2 * dh), lambda i, h: (i, 0, h)),
            pl.BlockSpec((1, s, 2 * dh), lambda i, h: (i, 0, nh + h)),
            pl.BlockSpec((1, s, dv), lambda i, h: (i, 0, 2 * nh + h)),
            vec(dh), vec(dh), vec(dh), vec(dh), vec(dv),
        ],
        out_specs=pl.BlockSpec((1, s, dv), lambda i, h: (i, 0, h)),
        out_shape=jax.ShapeDtypeStruct((b, s, nh * dv), _bf16),
        scratch_shapes=[pltpu.VMEM((tq, s), _f32)],
        compiler_params=_cparams("parallel", "parallel"),
        name="diff_attention",
    )(proj, proj, proj, lq1.reshape(1, dh), lk1.reshape(1, dh), lq2.reshape(1, dh),
      lk2.reshape(1, dh), subln_g.reshape(1, dv))


def _retention_kernel(q_ref, k_ref, v_ref, g_ref, gn_ref, o_ref, r_ref, *, chunk):
    h = pl.program_id(1)
    c = pl.program_id(2)
    dk = RET_DK

    @pl.when(c == 0)
    def _():
        r_ref[...] = jnp.zeros_like(r_ref)

    hf = jnp.full((1, 1), h, jnp.int32).astype(_f32)
    log_g = jnp.log(1.0 - jnp.exp2(-5.0 - hf))
    ri = lax.broadcasted_iota(jnp.int32, (chunk, chunk), 0)
    ci = lax.broadcasted_iota(jnp.int32, (chunk, chunk), 1)
    diff = (ri - ci).astype(_f32)
    decay = jnp.where(diff >= 0, jnp.exp(log_g * jnp.maximum(diff, 0.0)), 0.0)
    idx = lax.broadcasted_iota(jnp.int32, (chunk, 1), 0).astype(_f32)
    zeta = jnp.exp(log_g * (chunk - 1 - idx))
    xi = jnp.exp(log_g * (idx + 1.0))
    g_c = jnp.exp(log_g * chunk)

    decay = decay * (dk ** -0.5)
    zeta = zeta * (dk ** -0.5)
    chunks = [pl.ds(sub * chunk, chunk) for sub in range(q_ref.shape[1] // chunk)]
    intra, updates = [], []
    for rows in chunks:
        q, k, v = q_ref[0, rows, :], k_ref[0, rows, :], v_ref[0, rows, :]
        intra.append(_dot((_dot_nt(q, k) * decay).astype(_bf16), v))
        updates.append(_dot_tn((k.astype(_f32) * zeta).astype(_bf16), v))
    r = r_ref[...]
    for rows, o, upd in zip(chunks, intra, updates):
        o = o + _dot(q_ref[0, rows, :], r.astype(_bf16)) * xi
        r = g_c * r + upd
        o = o * lax.rsqrt(jnp.mean(o * o, axis=-1, keepdims=True) + EPS) * gn_ref[...]
        o_ref[0, rows, :] = (_silu(g_ref[0, rows, :].astype(_f32)) * o).astype(o_ref.dtype)
    r_ref[...] = r


def retention_core(proj, gn_g):
    b, s, _ = proj.shape
    nh, dk, dv = RET_HEADS, RET_DK, RET_DV
    chunk = RET_CHUNK
    rows = min(16 * chunk, s)
    return pl.pallas_call(
        functools.partial(_retention_kernel, chunk=chunk),
        grid=(b, nh, s // rows),
        in_specs=[
            pl.BlockSpec((1, rows, dk), lambda i, h, c: (i, c, h)),
            pl.BlockSpec((1, rows, dk), lambda i, h, c: (i, c, nh + h)),
            pl.BlockSpec((1, rows, dv), lambda i, h, c: (i, c, (2 * nh * dk) // dv + h)),
            pl.BlockSpec((1, rows, dv), lambda i, h, c: (i, c, (2 * nh * dk) // dv + nh + h)),
            pl.BlockSpec((1, dv), lambda i, h, c: (0, h)),
        ],
        out_specs=pl.BlockSpec((1, rows, dv), lambda i, h, c: (i, c, h)),
        out_shape=jax.ShapeDtypeStruct((b, s, nh * dv), _bf16),
        scratch_shapes=[pltpu.VMEM((dk, dv), _f32)],
        compiler_params=_cparams("parallel", "parallel", "arbitrary"),
        name="retention",
    )(proj, proj, proj, proj, gn_g.reshape(1, nh * dv))


def _pack_rows(h):
    n = h.shape[1] // 2
    lo = pltpu.bitcast(h[:, :n].astype(_bf16).astype(_f32), jnp.uint32)
    hi = pltpu.bitcast(h[:, n:].astype(_bf16).astype(_f32), jnp.uint32)
    return (lo >> 16) | (hi & jnp.uint32(0xFFFF0000))


def _unpack_rows(p):
    lo = pltpu.bitcast(p << 16, _f32).astype(_bf16)
    hi = pltpu.bitcast(p & jnp.uint32(0xFFFF0000), _f32).astype(_bf16)
    return lo, hi


def _first_argmax(v, iota, axis, big):
    m = jnp.max(v, axis=axis, keepdims=True)
    i = jnp.min(jnp.where(v == m, iota, big), axis=axis, keepdims=True)
    return m, i


def _router_kernel(x_ref, g_ref, sc_ref, sh_ref, rwt_ref, rb_ref, s1_ref, s3_ref, s2_ref,
                   hb_ref, idx_ref, wgt_ref, shared_ref):
    tm = x_ref.shape[1]
    h = _norm_mod(x_ref[0], g_ref[...], sc_ref[0], sh_ref[0])
    hb = h.astype(_bf16)
    hb_ref[...] = hb
    act = (_silu(_dot(hb, s1_ref[...])) * _dot(hb, s3_ref[...])).astype(_bf16)
    shared_ref[...] = _dot(act, s2_ref[...])

    scores = 1.0 / (1.0 + jnp.exp(-_dot_nt(rwt_ref[...], h)))
    choice = scores + rb_ref[...]
    c3 = choice.reshape(N_GROUPS, GROUP_SIZE, tm)
    mem = lax.broadcasted_iota(jnp.int32, c3.shape, 1)
    m1, i1 = _first_argmax(c3, mem, 1, GROUP_SIZE)
    m2 = jnp.max(jnp.where(mem == i1, -jnp.inf, c3), axis=1, keepdims=True)
    gs = m1 + m2
    gi = lax.broadcasted_iota(jnp.int32, gs.shape, 0)
    sel = jnp.zeros(gs.shape, _f32)
    for _ in range(TOPK_GROUPS):
        _, ig = _first_argmax(gs, gi, 0, N_GROUPS)
        hit = gi == ig
        sel = jnp.where(hit, 1.0, sel)
        gs = jnp.where(hit, -jnp.inf, gs)
    cm = jnp.where(sel > 0.0, c3, -jnp.inf).reshape(N_EXPERTS, tm)
    ei = lax.broadcasted_iota(jnp.int32, cm.shape, 0)
    idxs, wgts = [], []
    for _ in range(TOP_K):
        _, ie = _first_argmax(cm, ei, 0, N_EXPERTS)
        hit = ei == ie
        idxs.append(ie)
        wgts.append(jnp.sum(jnp.where(hit, scores, 0.0), axis=0, keepdims=True))
        cm = jnp.where(hit, -jnp.inf, cm)
    w = jnp.concatenate(wgts, axis=0)
    idx_ref[...] = jnp.concatenate(idxs, axis=0)
    wgt_ref[...] = w / jnp.sum(w, axis=0, keepdims=True) * ROUTED_SCALE


def moe_router(x, g, sc, sh, router_w, router_b, s_w1, s_w3, s_w2):
    b, s, d = x.shape
    t = b * s
    tm = min(512, s)
    n_s = s // tm
    sf = s_w1.shape[1]
    const = lambda shape: pl.BlockSpec(shape, lambda i, m: (0,) * len(shape))
    return pl.pallas_call(
        _router_kernel,
        grid=(b, n_s),
        in_specs=[
            pl.BlockSpec((1, tm, d), lambda i, m: (i, m, 0)),
            const((1, d)),
            pl.BlockSpec((1, 1, d), lambda i, m: (i, 0, 0)),
            pl.BlockSpec((1, 1, d), lambda i, m: (i, 0, 0)),
            const((N_EXPERTS, d)), const((N_EXPERTS, 1)),
            const((d, sf)), const((d, sf)), const((sf, d)),
        ],
        out_specs=[
            pl.BlockSpec((tm, d), lambda i, m: (i * n_s + m, 0)),
            pl.BlockSpec((TOP_K, tm), lambda i, m: (0, i * n_s + m)),
            pl.BlockSpec((TOP_K, tm), lambda i, m: (0, i * n_s + m)),
            pl.BlockSpec((tm, d), lambda i, m: (i * n_s + m, 0)),
        ],
        out_shape=[
            jax.ShapeDtypeStruct((t, d), _bf16),
            jax.ShapeDtypeStruct((TOP_K, t), jnp.int32),
            jax.ShapeDtypeStruct((TOP_K, t), _f32),
            jax.ShapeDtypeStruct((t, d), _f32),
        ],
        compiler_params=_cparams("parallel", "parallel"),
        name="moe_router",
    )(x, g.reshape(1, d), sc, sh, router_w.T, router_b.reshape(N_EXPERTS, 1),
      s_w1.astype(_bf16), s_w3.astype(_bf16), s_w2.astype(_bf16))


def _strict_lower(n):
    return jnp.where(lax.broadcasted_iota(jnp.int32, (n, n), 1)
                     < lax.broadcasted_iota(jnp.int32, (n, n), 0), 1.0, 0.0).astype(_bf16)


def _prefix_over_experts(col):
    wide = jnp.broadcast_to(col, (N_EXPERTS, 128)).astype(_bf16)
    return _dot(_strict_lower(N_EXPERTS), wide)[:, :1]


def _rank_kernel(idx_ref, piece_ref, tot_ref, base_ref):
    phase = pl.program_id(0)
    i = pl.program_id(1)
    tm = TOKEN_TILE

    @pl.when(jnp.logical_and(phase == 0, i == 0))
    def _():
        base_ref[...] = jnp.zeros_like(base_ref)

    @pl.when(jnp.logical_and(phase == 1, i == 0))
    def _():
        tot = base_ref[...].astype(jnp.int32)
        tot_ref[...] = tot
        nblk = (tot + (ROW_BLOCK - 1)) >> ROW_BLOCK_LOG2
        start_blk = (32.0 * _prefix_over_experts((nblk >> 5).astype(_f32))
                     + _prefix_over_experts((nblk & 31).astype(_f32)))
        base_ref[...] = start_blk * float(ROW_BLOCK)

    ei = lax.broadcasted_iota(jnp.int32, (N_EXPERTS, tm), 0)
    for sub in range(idx_ref.shape[1] // tm):
        idx = idx_ref[:, sub * tm:(sub + 1) * tm]
        memf = jnp.zeros((N_EXPERTS, tm), _f32)
        for k in range(TOP_K):
            memf = jnp.where(ei == idx[k:k + 1, :], 1.0, memf)
        tile_cnt = jnp.sum(memf, axis=1, keepdims=True).astype(jnp.int32)
        seg_rows = ((tile_cnt + (SEG_ALIGN - 1)) & (-SEG_ALIGN)).astype(_f32)

        @pl.when(phase == 0)
        def _(seg_rows=seg_rows):
            base_ref[...] += seg_rows

        @pl.when(phase == 1)
        def _(seg_rows=seg_rows, sub=sub):
            seg_i = seg_rows.astype(jnp.int32)
            lo = (16.0 * _prefix_over_experts((seg_i >> 4).astype(_f32))
                  + _prefix_over_experts((seg_i & 15).astype(_f32)))
            lane = lax.broadcasted_iota(jnp.int32, (N_EXPERTS, PIECE_SLOTS), 1)
            row = (lane * SEG_ALIGN).astype(_f32)
            inside = jnp.logical_and(row >= lo, row < lo + seg_rows)
            dst = jnp.sum(jnp.where(inside, base_ref[...] + (row - lo), 0.0), axis=0,
                          keepdims=True)
            total = jnp.sum(seg_rows, axis=0, keepdims=True)
            piece_ref[sub] = jnp.where(lane[:1] == PIECE_SLOTS - 1, total, dst).astype(jnp.int32)
            base_ref[...] += seg_rows


def moe_rank(top_idx):
    k, t = top_idx.shape
    tm = TOKEN_TILE
    assert _tile_buffer_rows(tm) // SEG_ALIGN < PIECE_SLOTS
    per_step = 4 if t % (4 * tm) == 0 else 1
    return pl.pallas_call(
        _rank_kernel,
        grid=(2, t // (per_step * tm)),
        in_specs=[pl.BlockSpec((k, per_step * tm), lambda p, i: (0, i))],
        out_specs=[pl.BlockSpec((per_step, 1, PIECE_SLOTS), lambda p, i: (i * p, 0, 0)),
                   pl.BlockSpec((N_EXPERTS, 1), lambda p, i: (0, 0))],
        out_shape=[jax.ShapeDtypeStruct((t // tm, 1, PIECE_SLOTS), jnp.int32),
                   jax.ShapeDtypeStruct((N_EXPERTS, 1), jnp.int32)],
        scratch_shapes=[pltpu.VMEM((N_EXPERTS, 1), _f32)],
        compiler_params=_cparams("arbitrary", "arbitrary"),
        name="moe_rank",
    )(top_idx)


def _aligned_rows(ref, start, n):
    return ref.at[pl.ds(pl.multiple_of(start, SEG_ALIGN), n)]


def _tile_rows(piece_ref):
    return piece_ref[0, 0, PIECE_SLOTS - 1]


def _for_each_piece(piece_ref, fn):
    def piece(g, c):
        fn(g * SEG_ALIGN, piece_ref[0, 0, g])
        return c

    lax.fori_loop(0, _tile_rows(piece_ref) // SEG_ALIGN, piece, 0)


def _wait_pieces(n_pieces, make_copy):
    for bit in reversed(range(PIECE_SLOTS.bit_length())):
        @pl.when((n_pieces >> bit) & 1 == 1)
        def _(bit=bit):
            make_copy(1 << bit).wait()


def _prefix_over_lanes(row):
    n = row.shape[1]
    wide = jnp.broadcast_to(row, (SEG_ALIGN, n)).astype(_bf16)
    return _dot_nt(wide, _strict_lower(n))[:1, :]


def _member_by_token(idx_t):
    lane_e = lax.broadcasted_iota(jnp.int32, (idx_t.shape[0], N_EXPERTS), 1)
    member = jnp.zeros(lane_e.shape, _f32)
    for k in range(TOP_K):
        member = jnp.where(lane_e == idx_t[:, k:k + 1], 1.0, member)
    return member


def _member_by_expert(idx):
    sub_e = lax.broadcasted_iota(jnp.int32, (N_EXPERTS, idx.shape[1]), 0)
    member = jnp.zeros(sub_e.shape, _f32)
    for k in range(TOP_K):
        member = jnp.where(sub_e == idx[k:k + 1, :], 1.0, member)
    return member


def _segment_bounds(count, prefix):
    seg_rows = (count.astype(jnp.int32) + (SEG_ALIGN - 1)) & (-SEG_ALIGN)
    lo = 16.0 * prefix((seg_rows >> 4).astype(_f32)) + prefix((seg_rows & 15).astype(_f32))
    return lo, lo + seg_rows.astype(_f32)


def _rows_of_chunk(r0, seg_lo, seg_hi):
    row = (lax.broadcasted_iota(jnp.int32, (PERM_ROWS, N_EXPERTS), 0) + r0).astype(_f32)
    inside = jnp.logical_and(row >= seg_lo, row < seg_hi)
    rank = jnp.sum(jnp.where(inside, row - seg_lo, 0.0), axis=1, keepdims=True)
    return jnp.where(inside, 1.0, 0.0).astype(_bf16), rank


def _experts_of_chunk(r0, seg_lo, seg_hi):
    row = (lax.broadcasted_iota(jnp.int32, (N_EXPERTS, PERM_ROWS), 1) + r0).astype(_f32)
    inside = jnp.logical_and(row >= seg_lo, row < seg_hi)
    rank = jnp.sum(jnp.where(inside, row - seg_lo, 0.0), axis=0, keepdims=True)
    return jnp.where(inside, 1.0, 0.0).astype(_bf16), rank


def _dispatch_kernel(pad_from_ref, pad_n_ref, nb_ref, piece_ref, idx_ref, idx_t_ref, hb_ref, xs_ref,
                     xbuf_ref, zero_ref, sent_ref, sem, zsem):
    tm = hb_ref.shape[0]
    n_blocks = xs_ref.shape[0] // ROW_BLOCK

    @pl.when(pl.program_id(0) == 0)
    def _():
        zero_ref[...] = jnp.zeros_like(zero_ref)

        def pad_expert(e, carry):
            off = pad_from_ref[e]

            def pad_piece(r, c):
                pltpu.make_async_copy(_aligned_rows(zero_ref, 0, SEG_ALIGN),
                                      _aligned_rows(xs_ref, off + SEG_ALIGN * r, SEG_ALIGN),
                                      zsem).start()
                return c + 1

            return lax.fori_loop(0, pad_n_ref[e] // SEG_ALIGN, pad_piece, carry)

        n_pieces = lax.fori_loop(0, N_EXPERTS, pad_expert, 0)

        def pad_tail(b, carry):
            pltpu.make_async_copy(zero_ref, xs_ref.at[pl.ds(b * ROW_BLOCK, ROW_BLOCK)],
                                  zsem).start()
            return carry

        lax.fori_loop(nb_ref[0], n_blocks, pad_tail, 0)

        def wait_piece(r, c):
            pltpu.make_async_copy(_aligned_rows(zero_ref, 0, SEG_ALIGN),
                                  _aligned_rows(xs_ref, 0, SEG_ALIGN), zsem).wait()
            return c

        def wait_tail(b, c):
            pltpu.make_async_copy(zero_ref, xs_ref.at[pl.ds(0, ROW_BLOCK)], zsem).wait()
            return c

        lax.fori_loop(0, n_pieces, wait_piece, 0)
        lax.fori_loop(nb_ref[0], n_blocks, wait_tail, 0)

    step = pl.program_id(0)
    slot = step % 2
    total = _tile_rows(piece_ref)
    count = jnp.sum(_member_by_token(idx_t_ref[...]), axis=0, keepdims=True)
    seg_lo, seg_hi = _segment_bounds(count, _prefix_over_lanes)
    member = _member_by_expert(idx_ref[...])
    before = _dot_nt(member.astype(_bf16), _strict_lower(tm))
    place = jnp.where(member > 0.0, before + 1.0, -1.0).astype(_bf16)

    def select(c):
        r0 = c * PERM_ROWS
        expert_of_row, rank_of_row = _rows_of_chunk(r0, seg_lo, seg_hi)
        row = lax.broadcasted_iota(jnp.int32, rank_of_row.shape, 0) + r0
        want = jnp.where(row < total, rank_of_row + 1.0, -5.0)
        return jnp.where(_dot(expert_of_row, place) == want, 1.0, 0.0).astype(_bf16)

    sel = jnp.concatenate([select(c) for c in range(xbuf_ref.shape[1] // PERM_ROWS)], axis=0)
    picked = _dot(sel, hb_ref[...])
    half = picked.shape[1] // 2
    xbuf_ref[slot] = ((pltpu.bitcast(picked[:, :half], jnp.uint32) >> 16)
                      | pltpu.bitcast(picked[:, half:], jnp.uint32))

    wait_sent = lambda n_pieces: _wait_pieces(n_pieces, lambda k: pltpu.make_async_copy(
        xbuf_ref.at[slot, pl.ds(0, k * SEG_ALIGN)], xs_ref.at[pl.ds(0, k * SEG_ALIGN)], sem))

    @pl.when(step > 0)
    def _():
        wait_sent(sent_ref[0])

    for s in range(2):
        @pl.when(slot == s)
        def _(s=s):
            def send(off, dst):
                pltpu.make_async_copy(_aligned_rows(xbuf_ref.at[s], off, SEG_ALIGN),
                                      _aligned_rows(xs_ref, dst, SEG_ALIGN), sem).start()

            _for_each_piece(piece_ref, send)

    sent_ref[0] = total // SEG_ALIGN

    @pl.when(step == pl.num_programs(0) - 1)
    def _():
        wait_sent(total // SEG_ALIGN)


def _tile_buffer_rows(tm):
    worst = tm * TOP_K + N_EXPERTS * (SEG_ALIGN - 1)
    return -(-worst // PERM_ROWS) * PERM_ROWS


def moe_dispatch(pieces, idx, idx_t, hb, pad_from, pad_n, n_used, n_blocks):
    t, d = hb.shape
    tm = TOKEN_TILE
    grid_spec = pltpu.PrefetchScalarGridSpec(
        num_scalar_prefetch=3,
        grid=(t // tm,),
        in_specs=[
            pl.BlockSpec((1, 1, PIECE_SLOTS), lambda i, *_: (i, 0, 0), memory_space=pltpu.SMEM),
            pl.BlockSpec((TOP_K, tm), lambda i, *_: (0, i)),
            pl.BlockSpec((tm, TOP_K), lambda i, *_: (i, 0)),
            pl.BlockSpec((tm, d), lambda i, *_: (i, 0)),
        ],
        out_specs=pl.BlockSpec(memory_space=pl.ANY),
        scratch_shapes=[pltpu.VMEM((2, _tile_buffer_rows(tm), d // 2), jnp.uint32),
                        pltpu.VMEM((ROW_BLOCK, d // 2), jnp.uint32),
                        pltpu.SMEM((1,), jnp.int32),
                        pltpu.SemaphoreType.DMA(()), pltpu.SemaphoreType.DMA(())],
    )
    return pl.pallas_call(
        _dispatch_kernel,
        grid_spec=grid_spec,
        out_shape=jax.ShapeDtypeStruct((n_blocks * ROW_BLOCK, d // 2), jnp.uint32),
        compiler_params=_cparams("arbitrary"),
        name="moe_dispatch",
    )(pad_from, pad_n, n_used, pieces, idx, idx_t, hb)


def _expert_kernel(be_ref, nb_ref, xs_ref, w1_ref, w3_ref, w2_ref, y_ref, w1b, w3b, w2b):
    i = pl.program_id(0)

    @pl.when(jnp.logical_or(i == 0, be_ref[i] != be_ref[jnp.maximum(i - 1, 0)]))
    def _():
        w1b[...] = w1_ref[0, 0].astype(_bf16)
        w3b[...] = w3_ref[0, 0].astype(_bf16)
        w2b[...] = w2_ref[0, 0].astype(_bf16)

    @pl.when(i < nb_ref[0])
    def _():
        chains = [pl.ds(c * EXPERT_ROWS, EXPERT_ROWS) for c in range(ROW_BLOCK // EXPERT_ROWS)]
        xs = [jnp.concatenate(_unpack_rows(xs_ref[rows, :]), axis=1) for rows in chains]
        acts = [(_silu(_dot(x, w1b[...])) * _dot(x, w3b[...])).astype(_bf16) for x in xs]
        for rows, act in zip(chains, acts):
            y_ref[rows, :] = _pack_rows(_dot(act, w2b[...]))

    @pl.when(i >= nb_ref[0])
    def _():
        y_ref[...] = jnp.zeros_like(y_ref)


def moe_experts(xs, blk_e, n_used, w1, w3, w2, layer):
    p, half = xs.shape
    d = 2 * half
    ff = w1.shape[3]
    nb = p // ROW_BLOCK
    row = lambda i, be, nu: (jnp.minimum(i, nu[0] - 1), 0)
    wsel = lambda i, be, nu: (layer, be[i], 0, 0)
    grid_spec = pltpu.PrefetchScalarGridSpec(
        num_scalar_prefetch=2,
        grid=(nb,),
        in_specs=[
            pl.BlockSpec((ROW_BLOCK, half), row),
            pl.BlockSpec((1, 1, d, ff), wsel),
            pl.BlockSpec((1, 1, d, ff), wsel),
            pl.BlockSpec((1, 1, ff, d), wsel),
        ],
        out_specs=pl.BlockSpec((ROW_BLOCK, half), lambda i, be, nu: (i, 0)),
        scratch_shapes=[pltpu.VMEM((d, ff), _bf16), pltpu.VMEM((d, ff), _bf16),
                        pltpu.VMEM((ff, d), _bf16)],
    )
    return pl.pallas_call(
        _expert_kernel,
        grid_spec=grid_spec,
        out_shape=jax.ShapeDtypeStruct((p, half), jnp.uint32),
        compiler_params=_cparams("arbitrary"),
        name="moe_experts",
    )(blk_e, n_used, xs, w1, w3, w2)


def _combine_kernel(piece_ref, next_piece_ref, ys_ref, idx_ref, idx_t_ref, w_ref, shared_ref, x_ref, gate_ref,
                    fn_ref, o_ref, ybuf_ref, sem, *, final_norm):
    tm = x_ref.shape[1]
    half = ybuf_ref.shape[2]
    step = pl.program_id(0) * pl.num_programs(1) + pl.program_id(1)
    n_steps = pl.num_programs(0) * pl.num_programs(1)
    slot = step % 2

    def fetch(table_ref, s):
        def one(off, src):
            pltpu.make_async_copy(_aligned_rows(ys_ref, src, SEG_ALIGN),
                                  _aligned_rows(ybuf_ref.at[s], off, SEG_ALIGN), sem.at[s]).start()

        _for_each_piece(table_ref, one)

    @pl.when(step == 0)
    def _():
        fetch(piece_ref, 0)

    for s in range(2):
        @pl.when(jnp.logical_and(step + 1 < n_steps, slot == 1 - s))
        def _(s=s):
            fetch(next_piece_ref, s)

    total = _tile_rows(piece_ref)
    _wait_pieces(total // SEG_ALIGN, lambda k: pltpu.make_async_copy(
        ys_ref.at[pl.ds(0, k * SEG_ALIGN)], ybuf_ref.at[slot, pl.ds(0, k * SEG_ALIGN)],
        sem.at[slot]))

    idx_t = idx_t_ref[...]
    member = _member_by_token(idx_t)
    before_b = _dot(_strict_lower(tm), member.astype(_bf16)).astype(_bf16)
    lane_e = lax.broadcasted_iota(jnp.int32, member.shape, 1)
    w_of_expert = jnp.zeros(member.shape, _f32)
    for k in range(TOP_K):
        w_of_expert = jnp.where(lane_e == idx_t[:, k:k + 1], w_ref[:, k:k + 1], w_of_expert)
    w_of_expert = w_of_expert.astype(_bf16)
    count = jnp.sum(_member_by_expert(idx_ref[...]), axis=1, keepdims=True)
    seg_lo, seg_hi = _segment_bounds(count, _prefix_over_experts)

    def weights_of_chunk(c):
        expert_of_row, rank_of_row = _experts_of_chunk(c * PERM_ROWS, seg_lo, seg_hi)
        hit = _dot(before_b, expert_of_row) == rank_of_row
        return jnp.where(hit, _dot(w_of_expert, expert_of_row), 0.0).astype(_bf16)

    wm = jnp.concatenate([weights_of_chunk(c) for c in range(ybuf_ref.shape[1] // PERM_ROWS)],
                         axis=1)
    packed = ybuf_ref[slot]
    row = lax.broadcasted_iota(jnp.int32, packed.shape, 0)
    packed = jnp.where(row < total, packed, jnp.uint32(0))
    y_lo, y_hi = _unpack_rows(packed)
    routed = jnp.concatenate([_dot(wm, y_lo), _dot(wm, y_hi)], axis=1)

    xn = x_ref[0] + gate_ref[0] * (shared_ref[...] + routed)
    if final_norm:
        xn = xn * lax.rsqrt(jnp.mean(xn * xn, axis=-1, keepdims=True) + EPS) * fn_ref[...]
    o_ref[0] = xn


def moe_combine(pieces, idx, idx_t, ys, top_w, shared, x, gate, fn_g, final_norm):
    b, s, d = x.shape
    tm = TOKEN_TILE
    n_s = s // tm
    tile = lambda i, m: (i * n_s + m, 0)
    last_tile = b * n_s - 1
    return pl.pallas_call(
        functools.partial(_combine_kernel, final_norm=final_norm),
        grid=(b, n_s),
        in_specs=[
            pl.BlockSpec((1, 1, PIECE_SLOTS), lambda i, m: (i * n_s + m, 0, 0),
                         memory_space=pltpu.SMEM),
            pl.BlockSpec((1, 1, PIECE_SLOTS),
                         lambda i, m: (jnp.minimum(i * n_s + m + 1, last_tile), 0, 0),
                         memory_space=pltpu.SMEM),
            pl.BlockSpec(memory_space=pl.ANY),
            pl.BlockSpec((TOP_K, tm), lambda i, m: (0, i * n_s + m)),
            pl.BlockSpec((tm, TOP_K), tile),
            pl.BlockSpec((tm, TOP_K), tile),
            pl.BlockSpec((tm, d), tile),
            pl.BlockSpec((1, tm, d), lambda i, m: (i, m, 0)),
            pl.BlockSpec((1, 1, d), lambda i, m: (i, 0, 0)),
            pl.BlockSpec((1, d), lambda i, m: (0, 0)),
        ],
        out_specs=pl.BlockSpec((1, tm, d), lambda i, m: (i, m, 0)),
        out_shape=jax.ShapeDtypeStruct((b, s, d), _f32),
        scratch_shapes=[pltpu.VMEM((2, _tile_buffer_rows(tm), d // 2), jnp.uint32),
                        pltpu.SemaphoreType.DMA((2,))],
        compiler_params=_cparams("arbitrary", "arbitrary"),
        name="moe_combine",
    )(pieces, pieces, ys, idx, idx_t, top_w, shared, x, gate, fn_g.reshape(1, d))


def moe_layer(x, g, sc, sh, gate, router_w, router_b, w1, w3, w2, layer, s_w1, s_w3, s_w2,
              fn_g, final_norm):
    b, s, d = x.shape
    t = b * s
    hb, top_idx, top_w, shared = moe_router(x, g, sc, sh, router_w, router_b, s_w1, s_w3, s_w2)
    pieces, rows = moe_rank(top_idx)
    idx_t = top_idx.T
    n_tiles = pieces.shape[0]

    rows = rows.reshape(N_EXPERTS)
    nblk = (rows + ROW_BLOCK - 1) // ROW_BLOCK
    blk_ends = jnp.cumsum(nblk)
    max_rows = t * TOP_K + n_tiles * N_EXPERTS * (SEG_ALIGN - 1)
    n_blocks = -(-max_rows // ROW_BLOCK) + N_EXPERTS
    blk = jnp.arange(n_blocks, dtype=jnp.int32)
    blk_e = jnp.minimum(jnp.sum(blk[:, None] >= blk_ends[None, :], axis=1),
                        N_EXPERTS - 1).astype(jnp.int32)
    n_used = blk_ends[-1:].astype(jnp.int32)
    pad_from = ((blk_ends - nblk) * ROW_BLOCK + rows).astype(jnp.int32)
    pad_n = (nblk * ROW_BLOCK - rows).astype(jnp.int32)

    xs = moe_dispatch(pieces, top_idx, idx_t, hb, pad_from, pad_n, n_used, n_blocks)
    ys = moe_experts(xs, blk_e, n_used, w1, w3, w2, layer)
    return moe_combine(pieces, top_idx, idx_t, ys, top_w.T, shared, x, gate, fn_g, final_norm)


def kernel(x, c, norm_mix, norm_ffn, ada_w, ada_b, da_w_in, da_w_out, da_lq1, da_lk1, da_lq2,
           da_lk2, da_subln, ret_w_in, ret_w_out, ret_gn, router_w, router_b, exp_w1, exp_w3,
           exp_w2, sh_w1, sh_w3, sh_w2, final_norm):
    depth, d = norm_mix.shape
    b = x.shape[0]
    mod = ada_modulation(c, ada_w, ada_b).reshape(depth, b, 6, 1, d)
    for i in range(depth):
        sh_a, sc_a, g_a, sh_f, sc_f, g_f = (mod[i, :, j] for j in range(6))
        j = i // 2
        if i % 2 == 0:
            proj = norm_proj(x, norm_mix[i], sc_a, sh_a, da_w_in[j].astype(_bf16))
            y = diff_attention_core(proj, da_lq1[j], da_lk1[j], da_lq2[j], da_lk2[j],
                                    da_subln[j], i)
            x = out_proj_residual(y, da_w_out[j].astype(_bf16), x, g_a)
        else:
            proj = norm_proj(x, norm_mix[i], sc_a, sh_a, ret_w_in[j].astype(_bf16))
            y = retention_core(proj, ret_gn[j])
            x = out_proj_residual(y, ret_w_out[j].astype(_bf16), x, g_a)
        x = moe_layer(x, norm_ffn[i], sc_f, sh_f, g_f, router_w[i], router_b[i],
                      exp_w1, exp_w3, exp_w2, i,
                      sh_w1[i], sh_w3[i], sh_w2[i], final_norm, i == depth - 1)
    return x
```

```python
import functools
import math

import jax
import jax.numpy as jnp
from jax import lax
from jax.experimental import pallas as pl
from jax.experimental.pallas import tpu as pltpu

EPS = 1e-6
DA_HEADS = 8
DA_HEAD_DIM = 64
DA_V_DIM = 2 * DA_HEAD_DIM
RET_HEADS = 4
RET_DK = 256
RET_DV = 512
RET_CHUNK = 128
N_EXPERTS = 64
TOP_K = 8
N_GROUPS = 8
GROUP_SIZE = N_EXPERTS // N_GROUPS
TOPK_GROUPS = 4
ROUTED_SCALE = 2.5
ROW_BLOCK_LOG2 = 10
ROW_BLOCK = 1 << ROW_BLOCK_LOG2
EXPERT_ROWS = 256
TOKEN_TILE = 256
SEG_ALIGN = 8
PERM_ROWS = 512
PIECE_SLOTS = 384
PIECE_UNROLL = 8

VMEM_LIMIT = 56 * 1024 * 1024
NEG_BIG = -1e30

_f32 = jnp.float32
_bf16 = jnp.bfloat16


def _cparams(*sem):
    return pltpu.CompilerParams(dimension_semantics=sem, vmem_limit_bytes=VMEM_LIMIT)


def _silu(v):
    return v * (1.0 / (1.0 + jnp.exp(-v)))


def _dot(a, b):
    return jnp.dot(a, b, preferred_element_type=_f32)


def _dot_nt(a, b):
    return lax.dot_general(a, b, (((1,), (1,)), ((), ())), preferred_element_type=_f32)


def _dot_tn(a, b):
    return lax.dot_general(a, b, (((0,), (0,)), ((), ())), preferred_element_type=_f32)


def _ada_kernel(c_ref, w_ref, b_ref, o_ref):
    a = _silu(c_ref[...]).astype(_bf16)
    o_ref[0] = _dot(a, w_ref[0].astype(_bf16)) + b_ref[0]


def ada_modulation(c, ada_w, ada_b):
    depth, d, n = ada_w.shape
    b = c.shape[0]
    tn = 1024
    return pl.pallas_call(
        _ada_kernel,
        grid=(depth, n // tn),
        in_specs=[
            pl.BlockSpec((b, d), lambda l, j: (0, 0)),
            pl.BlockSpec((1, d, tn), lambda l, j: (l, 0, j)),
            pl.BlockSpec((1, 1, tn), lambda l, j: (l, 0, j)),
        ],
        out_specs=pl.BlockSpec((1, b, tn), lambda l, j: (l, 0, j)),
        out_shape=jax.ShapeDtypeStruct((depth, b, n), _f32),
        compiler_params=_cparams("parallel", "parallel"),
        name="ada_modulation",
    )(c, ada_w, ada_b.reshape(depth, 1, n))


def _norm_mod(x, g, sc, sh):
    y = x * lax.rsqrt(jnp.mean(x * x, axis=-1, keepdims=True) + EPS)
    return (y * g) * (1.0 + sc) + sh


def _norm_proj_kernel(x_ref, g_ref, sc_ref, sh_ref, w_ref, o_ref, h_ref):
    @pl.when(pl.program_id(2) == 0)
    def _():
        h_ref[...] = _norm_mod(x_ref[0], g_ref[...], sc_ref[0], sh_ref[0]).astype(_bf16)

    o_ref[0] = _dot(h_ref[...], w_ref[...]).astype(o_ref.dtype)


def norm_proj(x, g, sc, sh, w):
    b, s, d = x.shape
    n = w.shape[1]
    tm = min(1024, s)
    tn = 1024
    return pl.pallas_call(
        _norm_proj_kernel,
        grid=(b, s // tm, n // tn),
        in_specs=[
            pl.BlockSpec((1, tm, d), lambda i, m, j: (i, m, 0)),
            pl.BlockSpec((1, d), lambda i, m, j: (0, 0)),
            pl.BlockSpec((1, 1, d), lambda i, m, j: (i, 0, 0)),
            pl.BlockSpec((1, 1, d), lambda i, m, j: (i, 0, 0)),
            pl.BlockSpec((d, tn), lambda i, m, j: (0, j)),
        ],
        out_specs=pl.BlockSpec((1, tm, tn), lambda i, m, j: (i, m, j)),
        out_shape=jax.ShapeDtypeStruct((b, s, n), _bf16),
        scratch_shapes=[pltpu.VMEM((tm, d), _bf16)],
        compiler_params=_cparams("parallel", "parallel", "arbitrary"),
        name="norm_proj",
    )(x, g.reshape(1, d), sc, sh, w)


def _out_proj_kernel(y_ref, w_ref, x_ref, g_ref, o_ref):
    o_ref[0] = x_ref[0] + g_ref[0] * _dot(y_ref[0], w_ref[...])


def out_proj_residual(y, w, x, gate):
    b, s, k = y.shape
    d = w.shape[1]
    tm = min(512, s)
    return pl.pallas_call(
        _out_proj_kernel,
        grid=(b, s // tm),
        in_specs=[
            pl.BlockSpec((1, tm, k), lambda i, m: (i, m, 0)),
            pl.BlockSpec((k, d), lambda i, m: (0, 0)),
            pl.BlockSpec((1, tm, d), lambda i, m: (i, m, 0)),
            pl.BlockSpec((1, 1, d), lambda i, m: (i, 0, 0)),
        ],
        out_specs=pl.BlockSpec((1, tm, d), lambda i, m: (i, m, 0)),
        out_shape=jax.ShapeDtypeStruct((b, s, d), _f32),
        compiler_params=_cparams("parallel", "parallel"),
        name="out_proj_residual",
    )(y, w, x, gate)


def _diff_attn_kernel(q_ref, k_ref, v_ref, lq1_ref, lk1_ref, lq2_ref, lk2_ref, sg_ref, o_ref,
                      bias_ref, *, tq, lam_init):
    h = pl.program_id(1)
    dh, dv = DA_HEAD_DIM, DA_V_DIM
    n_q = k_ref.shape[1] // tq

    hp1 = jnp.full((1, 1), h + 1, jnp.int32).astype(_f32)
    slope = jnp.exp2(hp1 * (-8.0 / DA_HEADS))
    dist = ((n_q - 1) * tq + lax.broadcasted_iota(jnp.int32, bias_ref.shape, 0)
            - lax.broadcasted_iota(jnp.int32, bias_ref.shape, 1)).astype(_f32)
    bias_ref[...] = -slope * dist

    lam = (jnp.exp(jnp.sum(lq1_ref[...] * lk1_ref[...], axis=-1, keepdims=True))
           - jnp.exp(jnp.sum(lq2_ref[...] * lk2_ref[...], axis=-1, keepdims=True)) + lam_init)
    causal = (lax.broadcasted_iota(jnp.int32, (tq, tq), 0)
              >= lax.broadcasted_iota(jnp.int32, (tq, tq), 1))
    lane = lax.broadcasted_iota(jnp.int32, (tq, 2 * dh), 1)

    def attend(c):
        q = q_ref[0, c * tq:(c + 1) * tq, :] * (dh ** -0.5)
        zero = jnp.zeros_like(q)
        q_maps = (jnp.where(lane < dh, q, zero), jnp.where(lane >= dh, q, zero))
        first = (n_q - 1 - c) * tq
        blocks = [slice(j * tq, (j + 1) * tq) for j in range(c + 1)]
        tiles = [[_dot_nt(q_m, k_ref[0, blk, :])
                  + bias_ref[:, first + blk.start:first + blk.stop] for q_m in q_maps]
                 for blk in blocks]
        tiles[c] = [jnp.where(causal, s, NEG_BIG) for s in tiles[c]]
        outs = []
        for mi in range(2):
            parts = [row[mi] for row in tiles]
            m = functools.reduce(jnp.maximum, [jnp.max(x, axis=-1, keepdims=True) for x in parts])
            ps = [jnp.exp(x - m) for x in parts]
            l = functools.reduce(jnp.add, [jnp.sum(x, axis=-1, keepdims=True) for x in ps])
            o = functools.reduce(jnp.add, [_dot(x.astype(_bf16), v_ref[0, blk, :])
                                           for x, blk in zip(ps, blocks)])
            outs.append(o / l)
        o = outs[0] - lam * outs[1]
        o = o * lax.rsqrt(jnp.mean(o * o, axis=-1, keepdims=True) + EPS)
        o_ref[0, c * tq:(c + 1) * tq, :] = ((o * sg_ref[...]) * (1.0 - lam_init)).astype(o_ref.dtype)

    for c in range(n_q):
        attend(c)


def diff_attention_core(proj, lq1, lk1, lq2, lk2, subln_g, layer_idx):
    b, s, _ = proj.shape
    nh, dh, dv = DA_HEADS, DA_HEAD_DIM, DA_V_DIM
    tq = min(512, s)
    lam_init = 0.8 - 0.6 * math.exp(-0.3 * layer_idx)
    vec = lambda n: pl.BlockSpec((1, n), lambda i, h: (0, 0))
    return pl.pallas_call(
        functools.partial(_diff_attn_kernel, tq=tq, lam_init=lam_init),
        grid=(b, nh),
        in_specs=[
            pl.BlockSpec((1, s, 2 * dh), lambda i, h: (i, 0, h)),
            pl.BlockSpec((1, s, 2 * dh), lambda i, h: (i, 0, nh + h)),
            pl.BlockSpec((1, s, dv), lambda i, h: (i, 0, 2 * nh + h)),
            vec(dh), vec(dh), vec(dh), vec(dh), vec(dv),
        ],
        out_specs=pl.BlockSpec((1, s, dv), lambda i, h: (i, 0, h)),
        out_shape=jax.ShapeDtypeStruct((b, s, nh * dv), _bf16),
        scratch_shapes=[pltpu.VMEM((tq, s), _f32)],
        compiler_params=_cparams("parallel", "parallel"),
        name="diff_attention",
    )(proj, proj, proj, lq1.reshape(1, dh), lk1.reshape(1, dh), lq2.reshape(1, dh),
      lk2.reshape(1, dh), subln_g.reshape(1, dv))


def _retention_kernel(q_ref, k_ref, v_ref, g_ref, gn_ref, o_ref, r_ref, *, chunk):
    h = pl.program_id(1)
    c = pl.program_id(2)
    dk = RET_DK

    @pl.when(c == 0)
    def _():
        r_ref[...] = jnp.zeros_like(r_ref)

    hf = jnp.full((1, 1), h, jnp.int32).astype(_f32)
    log_g = jnp.log(1.0 - jnp.exp2(-5.0 - hf))
    ri = lax.broadcasted_iota(jnp.int32, (chunk, chunk), 0)
    ci = lax.broadcasted_iota(jnp.int32, (chunk, chunk), 1)
    diff = (ri - ci).astype(_f32)
    decay = jnp.where(diff >= 0, jnp.exp(log_g * jnp.maximum(diff, 0.0)), 0.0)
    idx = lax.broadcasted_iota(jnp.int32, (chunk, 1), 0).astype(_f32)
    zeta = jnp.exp(log_g * (chunk - 1 - idx))
    xi = jnp.exp(log_g * (idx + 1.0))
    g_c = jnp.exp(log_g * chunk)

    decay = decay * (dk ** -0.5)
    zeta = zeta * (dk ** -0.5)
    chunks = [pl.ds(sub * chunk, chunk) for sub in range(q_ref.shape[1] // chunk)]
    intra, updates = [], []
    for rows in chunks:
        q, k, v = q_ref[0, rows, :], k_ref[0, rows, :], v_ref[0, rows, :]
        intra.append(_dot((_dot_nt(q, k) * decay).astype(_bf16), v))
        updates.append(_dot_tn((k.astype(_f32) * zeta).astype(_bf16), v))
    r = r_ref[...]
    for rows, o, upd in zip(chunks, intra, updates):
        o = o + _dot(q_ref[0, rows, :], r.astype(_bf16)) * xi
        r = g_c * r + upd
        o = o * lax.rsqrt(jnp.mean(o * o, axis=-1, keepdims=True) + EPS) * gn_ref[...]
        o_ref[0, rows, :] = (_silu(g_ref[0, rows, :].astype(_f32)) * o).astype(o_ref.dtype)
    r_ref[...] = r


def retention_core(proj, gn_g):
    b, s, _ = proj.shape
    nh, dk, dv = RET_HEADS, RET_DK, RET_DV
    chunk = RET_CHUNK
    rows = min(16 * chunk, s)
    return pl.pallas_call(
        functools.partial(_retention_kernel, chunk=chunk),
        grid=(b, nh, s // rows),
        in_specs=[
            pl.BlockSpec((1, rows, dk), lambda i, h, c: (i, c, h)),
            pl.BlockSpec((1, rows, dk), lambda i, h, c: (i, c, nh + h)),
            pl.BlockSpec((1, rows, dv), lambda i, h, c: (i, c, (2 * nh * dk) // dv + h)),
            pl.BlockSpec((1, rows, dv), lambda i, h, c: (i, c, (2 * nh * dk) // dv + nh + h)),
            pl.BlockSpec((1, dv), lambda i, h, c: (0, h)),
        ],
        out_specs=pl.BlockSpec((1, rows, dv), lambda i, h, c: (i, c, h)),
        out_shape=jax.ShapeDtypeStruct((b, s, nh * dv), _bf16),
        scratch_shapes=[pltpu.VMEM((dk, dv), _f32)],
        compiler_params=_cparams("parallel", "parallel", "arbitrary"),
        name="retention",
    )(proj, proj, proj, proj, gn_g.reshape(1, nh * dv))


def _pack_rows(h):
    n = h.shape[1] // 2
    lo = pltpu.bitcast(h[:, :n].astype(_bf16).astype(_f32), jnp.uint32)
    hi = pltpu.bitcast(h[:, n:].astype(_bf16).astype(_f32), jnp.uint32)
    return (lo >> 16) | (hi & jnp.uint32(0xFFFF0000))


def _unpack_rows(p):
    lo = pltpu.bitcast(p << 16, _f32).astype(_bf16)
    hi = pltpu.bitcast(p & jnp.uint32(0xFFFF0000), _f32).astype(_bf16)
    return lo, hi


def _first_argmax(v, iota, axis, big):
    m = jnp.max(v, axis=axis, keepdims=True)
    i = jnp.min(jnp.where(v == m, iota, big), axis=axis, keepdims=True)
    return m, i


def _router_kernel(x_ref, g_ref, sc_ref, sh_ref, rwt_ref, rb_ref, s1_ref, s3_ref, s2_ref,
                   hb_ref, idx_ref, wgt_ref, shared_ref):
    tm = x_ref.shape[1]
    h = _norm_mod(x_ref[0], g_ref[...], sc_ref[0], sh_ref[0])
    hb = h.astype(_bf16)
    hb_ref[...] = hb
    act = (_silu(_dot(hb, s1_ref[...])) * _dot(hb, s3_ref[...])).astype(_bf16)
    shared_ref[...] = _dot(act, s2_ref[...])

    scores = 1.0 / (1.0 + jnp.exp(-_dot_nt(rwt_ref[...], h)))
    choice = scores + rb_ref[...]
    c3 = choice.reshape(N_GROUPS, GROUP_SIZE, tm)
    mem = lax.broadcasted_iota(jnp.int32, c3.shape, 1)
    m1, i1 = _first_argmax(c3, mem, 1, GROUP_SIZE)
    m2 = jnp.max(jnp.where(mem == i1, -jnp.inf, c3), axis=1, keepdims=True)
    gs = m1 + m2
    gi = lax.broadcasted_iota(jnp.int32, gs.shape, 0)
    sel = jnp.zeros(gs.shape, _f32)
    for _ in range(TOPK_GROUPS):
        _, ig = _first_argmax(gs, gi, 0, N_GROUPS)
        hit = gi == ig
        sel = jnp.where(hit, 1.0, sel)
        gs = jnp.where(hit, -jnp.inf, gs)
    cm = jnp.where(sel > 0.0, c3, -jnp.inf).reshape(N_EXPERTS, tm)
    ei = lax.broadcasted_iota(jnp.int32, cm.shape, 0)
    idxs, wgts = [], []
    for _ in range(TOP_K):
        _, ie = _first_argmax(cm, ei, 0, N_EXPERTS)
        hit = ei == ie
        idxs.append(ie)
        wgts.append(jnp.sum(jnp.where(hit, scores, 0.0), axis=0, keepdims=True))
        cm = jnp.where(hit, -jnp.inf, cm)
    w = jnp.concatenate(wgts, axis=0)
    idx_ref[...] = jnp.concatenate(idxs, axis=0)
    wgt_ref[...] = w / jnp.sum(w, axis=0, keepdims=True) * ROUTED_SCALE


def moe_router(x, g, sc, sh, router_w, router_b, s_w1, s_w3, s_w2):
    b, s, d = x.shape
    t = b * s
    tm = min(512, s)
    n_s = s // tm
    sf = s_w1.shape[1]
    const = lambda shape: pl.BlockSpec(shape, lambda i, m: (0,) * len(shape))
    return pl.pallas_call(
        _router_kernel,
        grid=(b, n_s),
        in_specs=[
            pl.BlockSpec((1, tm, d), lambda i, m: (i, m, 0)),
            const((1, d)),
            pl.BlockSpec((1, 1, d), lambda i, m: (i, 0, 0)),
            pl.BlockSpec((1, 1, d), lambda i, m: (i, 0, 0)),
            const((N_EXPERTS, d)), const((N_EXPERTS, 1)),
            const((d, sf)), const((d, sf)), const((sf, d)),
        ],
        out_specs=[
            pl.BlockSpec((tm, d), lambda i, m: (i * n_s + m, 0)),
            pl.BlockSpec((TOP_K, tm), lambda i, m: (0, i * n_s + m)),
            pl.BlockSpec((TOP_K, tm), lambda i, m: (0, i * n_s + m)),
            pl.BlockSpec((tm, d), lambda i, m: (i * n_s + m, 0)),
        ],
        out_shape=[
            jax.ShapeDtypeStruct((t, d), _bf16),
            jax.ShapeDtypeStruct((TOP_K, t), jnp.int32),
            jax.ShapeDtypeStruct((TOP_K, t), _f32),
            jax.ShapeDtypeStruct((t, d), _f32),
        ],
        compiler_params=_cparams("parallel", "parallel"),
        name="moe_router",
    )(x, g.reshape(1, d), sc, sh, router_w.T, router_b.reshape(N_EXPERTS, 1),
      s_w1.astype(_bf16), s_w3.astype(_bf16), s_w2.astype(_bf16))


def _strict_lower(n):
    return jnp.where(lax.broadcasted_iota(jnp.int32, (n, n), 1)
                     < lax.broadcasted_iota(jnp.int32, (n, n), 0), 1.0, 0.0).astype(_bf16)


def _prefix_over_experts(col):
    wide = jnp.broadcast_to(col, (N_EXPERTS, 128)).astype(_bf16)
    return _dot(_strict_lower(N_EXPERTS), wide)[:, :1]


def _rank_kernel(idx_ref, piece_ref, tot_ref, base_ref):
    phase = pl.program_id(0)
    i = pl.program_id(1)
    tm = TOKEN_TILE

    @pl.when(jnp.logical_and(phase == 0, i == 0))
    def _():
        base_ref[...] = jnp.zeros_like(base_ref)

    @pl.when(jnp.logical_and(phase == 1, i == 0))
    def _():
        tot = base_ref[...].astype(jnp.int32)
        tot_ref[...] = tot
        nblk = (tot + (ROW_BLOCK - 1)) >> ROW_BLOCK_LOG2
        start_blk = (32.0 * _prefix_over_experts((nblk >> 5).astype(_f32))
                     + _prefix_over_experts((nblk & 31).astype(_f32)))
        base_ref[...] = start_blk * float(ROW_BLOCK)

    ei = lax.broadcasted_iota(jnp.int32, (N_EXPERTS, tm), 0)
    for sub in range(idx_ref.shape[1] // tm):
        idx = idx_ref[:, sub * tm:(sub + 1) * tm]
        memf = jnp.zeros((N_EXPERTS, tm), _f32)
        for k in range(TOP_K):
            memf = jnp.where(ei == idx[k:k + 1, :], 1.0, memf)
        tile_cnt = jnp.sum(memf, axis=1, keepdims=True).astype(jnp.int32)
        seg_rows = ((tile_cnt + (SEG_ALIGN - 1)) & (-SEG_ALIGN)).astype(_f32)

        @pl.when(phase == 0)
        def _(seg_rows=seg_rows):
            base_ref[...] += seg_rows

        @pl.when(phase == 1)
        def _(seg_rows=seg_rows, sub=sub):
            seg_i = seg_rows.astype(jnp.int32)
            lo = (16.0 * _prefix_over_experts((seg_i >> 4).astype(_f32))
                  + _prefix_over_experts((seg_i & 15).astype(_f32)))
            lane = lax.broadcasted_iota(jnp.int32, (N_EXPERTS, PIECE_SLOTS), 1)
            row = (lane * SEG_ALIGN).astype(_f32)
            inside = jnp.logical_and(row >= lo, row < lo + seg_rows)
            dst = jnp.sum(jnp.where(inside, base_ref[...] + (row - lo), 0.0), axis=0,
                          keepdims=True) * (1.0 / SEG_ALIGN)
            total = jnp.sum(seg_rows, axis=0, keepdims=True)
            piece_ref[sub] = jnp.where(lane[:1] == PIECE_SLOTS - 1, total, dst).astype(jnp.int32)
            base_ref[...] += seg_rows


def moe_rank(top_idx):
    k, t = top_idx.shape
    tm = TOKEN_TILE
    assert _tile_buffer_rows(tm) // SEG_ALIGN < PIECE_SLOTS
    per_step = 4 if t % (4 * tm) == 0 else 1
    return pl.pallas_call(
        _rank_kernel,
        grid=(2, t // (per_step * tm)),
        in_specs=[pl.BlockSpec((k, per_step * tm), lambda p, i: (0, i))],
        out_specs=[pl.BlockSpec((per_step, 1, PIECE_SLOTS), lambda p, i: (i * p, 0, 0)),
                   pl.BlockSpec((N_EXPERTS, 1), lambda p, i: (0, 0))],
        out_shape=[jax.ShapeDtypeStruct((t // tm, 1, PIECE_SLOTS), jnp.int32),
                   jax.ShapeDtypeStruct((N_EXPERTS, 1), jnp.int32)],
        scratch_shapes=[pltpu.VMEM((N_EXPERTS, 1), _f32)],
        compiler_params=_cparams("arbitrary", "arbitrary"),
        name="moe_rank",
    )(top_idx)


def _as_groups(a):
    return a.reshape(a.shape[0] // SEG_ALIGN, SEG_ALIGN, a.shape[1])


def _tile_rows(piece_ref):
    return piece_ref[0, 0, PIECE_SLOTS - 1]


def _for_each_piece(piece_ref, fn):
    def piece(g, c):
        fn(g, piece_ref[0, 0, g])
        return c

    def group(i, c):
        for u in range(PIECE_UNROLL):
            piece(i * PIECE_UNROLL + u, c)
        return c

    n = _tile_rows(piece_ref) // SEG_ALIGN
    lax.fori_loop(0, n // PIECE_UNROLL, group, 0)
    lax.fori_loop(n // PIECE_UNROLL * PIECE_UNROLL, n, piece, 0)


def _wait_pieces(n_pieces, make_copy):
    for bit in reversed(range(PIECE_SLOTS.bit_length())):
        @pl.when((n_pieces >> bit) & 1 == 1)
        def _(bit=bit):
            make_copy(1 << bit).wait()


def _prefix_over_lanes(row):
    n = row.shape[1]
    wide = jnp.broadcast_to(row, (SEG_ALIGN, n)).astype(_bf16)
    return _dot_nt(wide, _strict_lower(n))[:1, :]


def _member_by_token(idx_t):
    lane_e = lax.broadcasted_iota(jnp.int32, (idx_t.shape[0], N_EXPERTS), 1)
    member = jnp.zeros(lane_e.shape, _f32)
    for k in range(TOP_K):
        member = jnp.where(lane_e == idx_t[:, k:k + 1], 1.0, member)
    return member


def _member_by_expert(idx):
    sub_e = lax.broadcasted_iota(jnp.int32, (N_EXPERTS, idx.shape[1]), 0)
    member = jnp.zeros(sub_e.shape, _f32)
    for k in range(TOP_K):
        member = jnp.where(sub_e == idx[k:k + 1, :], 1.0, member)
    return member


def _segment_bounds(count, prefix):
    seg_rows = (count.astype(jnp.int32) + (SEG_ALIGN - 1)) & (-SEG_ALIGN)
    lo = 16.0 * prefix((seg_rows >> 4).astype(_f32)) + prefix((seg_rows & 15).astype(_f32))
    return lo, lo + seg_rows.astype(_f32)


def _rows_of_chunk(r0, seg_lo, seg_hi):
    row = (lax.broadcasted_iota(jnp.int32, (PERM_ROWS, N_EXPERTS), 0) + r0).astype(_f32)
    inside = jnp.logical_and(row >= seg_lo, row < seg_hi)
    rank = jnp.sum(jnp.where(inside, row - seg_lo, 0.0), axis=1, keepdims=True)
    return jnp.where(inside, 1.0, 0.0).astype(_bf16), rank


def _experts_of_chunk(r0, seg_lo, seg_hi):
    row = (lax.broadcasted_iota(jnp.int32, (N_EXPERTS, PERM_ROWS), 1) + r0).astype(_f32)
    inside = jnp.logical_and(row >= seg_lo, row < seg_hi)
    rank = jnp.sum(jnp.where(inside, row - seg_lo, 0.0), axis=0, keepdims=True)
    return jnp.where(inside, 1.0, 0.0).astype(_bf16), rank


def _dispatch_kernel(pad_from_ref, pad_n_ref, nb_ref, piece_ref, idx_ref, idx_t_ref, hb_ref, xs_ref,
                     xbuf_ref, zero_ref, sent_ref, sem, zsem):
    tm = hb_ref.shape[0]
    block_groups = zero_ref.shape[0]
    n_blocks = xs_ref.shape[0] // block_groups

    @pl.when(pl.program_id(0) == 0)
    def _():
        zero_ref[...] = jnp.zeros_like(zero_ref)

        def pad_expert(e, carry):
            off = pad_from_ref[e]

            def pad_piece(r, c):
                pltpu.make_async_copy(zero_ref.at[0], xs_ref.at[off + r], zsem).start()
                return c + 1

            return lax.fori_loop(0, pad_n_ref[e], pad_piece, carry)

        n_pieces = lax.fori_loop(0, N_EXPERTS, pad_expert, 0)

        def pad_tail(b, carry):
            pltpu.make_async_copy(zero_ref, xs_ref.at[pl.ds(b * block_groups, block_groups)],
                                  zsem).start()
            return carry

        lax.fori_loop(nb_ref[0], n_blocks, pad_tail, 0)

        def wait_piece(r, c):
            pltpu.make_async_copy(zero_ref.at[0], xs_ref.at[0], zsem).wait()
            return c

        def wait_tail(b, c):
            pltpu.make_async_copy(zero_ref, xs_ref.at[pl.ds(0, block_groups)], zsem).wait()
            return c

        lax.fori_loop(0, n_pieces, wait_piece, 0)
        lax.fori_loop(nb_ref[0], n_blocks, wait_tail, 0)

    step = pl.program_id(0)
    slot = step % 2
    total = _tile_rows(piece_ref)
    count = jnp.sum(_member_by_token(idx_t_ref[...]), axis=0, keepdims=True)
    seg_lo, seg_hi = _segment_bounds(count, _prefix_over_lanes)
    member = _member_by_expert(idx_ref[...])
    before = _dot_nt(member.astype(_bf16), _strict_lower(tm))
    place = jnp.where(member > 0.0, before + 1.0, -1.0).astype(_bf16)

    def select(c):
        r0 = c * PERM_ROWS
        expert_of_row, rank_of_row = _rows_of_chunk(r0, seg_lo, seg_hi)
        row = lax.broadcasted_iota(jnp.int32, rank_of_row.shape, 0) + r0
        want = jnp.where(row < total, rank_of_row + 1.0, -5.0)
        return jnp.where(_dot(expert_of_row, place) == want, 1.0, 0.0).astype(_bf16)

    buffer_rows = xbuf_ref.shape[1] * SEG_ALIGN
    sel = jnp.concatenate([select(c) for c in range(buffer_rows // PERM_ROWS)], axis=0)
    picked = _dot(sel, hb_ref[...])
    half = picked.shape[1] // 2
    xbuf_ref[slot] = _as_groups((pltpu.bitcast(picked[:, :half], jnp.uint32) >> 16)
                                | pltpu.bitcast(picked[:, half:], jnp.uint32))

    wait_sent = lambda n_pieces: _wait_pieces(n_pieces, lambda k: pltpu.make_async_copy(
        xbuf_ref.at[slot, pl.ds(0, k)], xs_ref.at[pl.ds(0, k)], sem))

    @pl.when(step > 0)
    def _():
        wait_sent(sent_ref[0])

    for s in range(2):
        @pl.when(slot == s)
        def _(s=s):
            def send(g, dst):
                pltpu.make_async_copy(xbuf_ref.at[s, g], xs_ref.at[dst], sem).start()

            _for_each_piece(piece_ref, send)

    sent_ref[0] = total // SEG_ALIGN

    @pl.when(step == pl.num_programs(0) - 1)
    def _():
        wait_sent(total // SEG_ALIGN)


def _tile_buffer_rows(tm):
    worst = tm * TOP_K + N_EXPERTS * (SEG_ALIGN - 1)
    return -(-worst // PERM_ROWS) * PERM_ROWS


def moe_dispatch(pieces, idx, idx_t, hb, pad_from, pad_n, n_used, n_blocks):
    t, d = hb.shape
    tm = TOKEN_TILE
    grid_spec = pltpu.PrefetchScalarGridSpec(
        num_scalar_prefetch=3,
        grid=(t // tm,),
        in_specs=[
            pl.BlockSpec((1, 1, PIECE_SLOTS), lambda i, *_: (i, 0, 0), memory_space=pltpu.SMEM),
            pl.BlockSpec((TOP_K, tm), lambda i, *_: (0, i)),
            pl.BlockSpec((tm, TOP_K), lambda i, *_: (i, 0)),
            pl.BlockSpec((tm, d), lambda i, *_: (i, 0)),
        ],
        out_specs=pl.BlockSpec(memory_space=pl.ANY),
        scratch_shapes=[pltpu.VMEM((2, _tile_buffer_rows(tm) // SEG_ALIGN, SEG_ALIGN, d // 2),
                                   jnp.uint32),
                        pltpu.VMEM((ROW_BLOCK // SEG_ALIGN, SEG_ALIGN, d // 2), jnp.uint32),
                        pltpu.SMEM((1,), jnp.int32),
                        pltpu.SemaphoreType.DMA(()), pltpu.SemaphoreType.DMA(())],
    )
    xs = pl.pallas_call(
        _dispatch_kernel,
        grid_spec=grid_spec,
        out_shape=jax.ShapeDtypeStruct((n_blocks * ROW_BLOCK // SEG_ALIGN, SEG_ALIGN, d // 2),
                                       jnp.uint32),
        compiler_params=_cparams("arbitrary"),
        name="moe_dispatch",
    )(pad_from // SEG_ALIGN, pad_n // SEG_ALIGN, n_used, pieces, idx, idx_t, hb)
    return xs.reshape(n_blocks * ROW_BLOCK, d // 2)


def _expert_kernel(be_ref, nb_ref, xs_ref, w1_ref, w3_ref, w2_ref, y_ref, w1b, w3b, w2b):
    i = pl.program_id(0)

    @pl.when(jnp.logical_or(i == 0, be_ref[i] != be_ref[jnp.maximum(i - 1, 0)]))
    def _():
        w1b[...] = w1_ref[0, 0].astype(_bf16)
        w3b[...] = w3_ref[0, 0].astype(_bf16)
        w2b[...] = w2_ref[0, 0].astype(_bf16)

    @pl.when(i < nb_ref[0])
    def _():
        chains = [pl.ds(c * EXPERT_ROWS, EXPERT_ROWS) for c in range(ROW_BLOCK // EXPERT_ROWS)]
        xs = [jnp.concatenate(_unpack_rows(xs_ref[rows, :]), axis=1) for rows in chains]
        acts = [(_silu(_dot(x, w1b[...])) * _dot(x, w3b[...])).astype(_bf16) for x in xs]
        for rows, act in zip(chains, acts):
            y_ref[rows, :] = _pack_rows(_dot(act, w2b[...]))

    @pl.when(i >= nb_ref[0])
    def _():
        y_ref[...] = jnp.zeros_like(y_ref)


def moe_experts(xs, blk_e, n_used, w1, w3, w2, layer):
    p, half = xs.shape
    d = 2 * half
    ff = w1.shape[3]
    nb = p // ROW_BLOCK
    row = lambda i, be, nu: (jnp.minimum(i, nu[0] - 1), 0)
    wsel = lambda i, be, nu: (layer, be[i], 0, 0)
    grid_spec = pltpu.PrefetchScalarGridSpec(
        num_scalar_prefetch=2,
        grid=(nb,),
        in_specs=[
            pl.BlockSpec((ROW_BLOCK, half), row),
            pl.BlockSpec((1, 1, d, ff), wsel),
            pl.BlockSpec((1, 1, d, ff), wsel),
            pl.BlockSpec((1, 1, ff, d), wsel),
        ],
        out_specs=pl.BlockSpec((ROW_BLOCK, half), lambda i, be, nu: (i, 0)),
        scratch_shapes=[pltpu.VMEM((d, ff), _bf16), pltpu.VMEM((d, ff), _bf16),
                        pltpu.VMEM((ff, d), _bf16)],
    )
    return pl.pallas_call(
        _expert_kernel,
        grid_spec=grid_spec,
        out_shape=jax.ShapeDtypeStruct((p, half), jnp.uint32),
        compiler_params=_cparams("arbitrary"),
        name="moe_experts",
    )(blk_e, n_used, xs, w1, w3, w2)


def _combine_kernel(piece_ref, next_piece_ref, ys_ref, idx_ref, idx_t_ref, w_ref, shared_ref, x_ref, gate_ref,
                    fn_ref, o_ref, ybuf_ref, sem, *, final_norm):
    tm = x_ref.shape[1]
    buffer_rows = ybuf_ref.shape[1] * SEG_ALIGN
    step = pl.program_id(0) * pl.num_programs(1) + pl.program_id(1)
    n_steps = pl.num_programs(0) * pl.num_programs(1)
    slot = step % 2

    def fetch(table_ref, s):
        def one(g, src):
            pltpu.make_async_copy(ys_ref.at[src], ybuf_ref.at[s, g], sem.at[s]).start()

        _for_each_piece(table_ref, one)

    @pl.when(step == 0)
    def _():
        fetch(piece_ref, 0)

    for s in range(2):
        @pl.when(jnp.logical_and(step + 1 < n_steps, slot == 1 - s))
        def _(s=s):
            fetch(next_piece_ref, s)

    total = _tile_rows(piece_ref)
    _wait_pieces(total // SEG_ALIGN, lambda k: pltpu.make_async_copy(
        ys_ref.at[pl.ds(0, k)], ybuf_ref.at[slot, pl.ds(0, k)], sem.at[slot]))

    idx_t = idx_t_ref[...]
    member = _member_by_token(idx_t)
    before_b = _dot(_strict_lower(tm), member.astype(_bf16)).astype(_bf16)
    lane_e = lax.broadcasted_iota(jnp.int32, member.shape, 1)
    w_of_expert = jnp.zeros(member.shape, _f32)
    for k in range(TOP_K):
        w_of_expert = jnp.where(lane_e == idx_t[:, k:k + 1], w_ref[:, k:k + 1], w_of_expert)
    w_of_expert = w_of_expert.astype(_bf16)
    count = jnp.sum(_member_by_expert(idx_ref[...]), axis=1, keepdims=True)
    seg_lo, seg_hi = _segment_bounds(count, _prefix_over_experts)

    def weights_of_chunk(c):
        expert_of_row, rank_of_row = _experts_of_chunk(c * PERM_ROWS, seg_lo, seg_hi)
        hit = _dot(before_b, expert_of_row) == rank_of_row
        return jnp.where(hit, _dot(w_of_expert, expert_of_row), 0.0).astype(_bf16)

    wm = jnp.concatenate([weights_of_chunk(c) for c in range(buffer_rows // PERM_ROWS)], axis=1)
    packed = ybuf_ref[slot].reshape(buffer_rows, ybuf_ref.shape[3])
    row = lax.broadcasted_iota(jnp.int32, packed.shape, 0)
    packed = jnp.where(row < total, packed, jnp.uint32(0))
    y_lo, y_hi = _unpack_rows(packed)
    routed = jnp.concatenate([_dot(wm, y_lo), _dot(wm, y_hi)], axis=1)

    xn = x_ref[0] + gate_ref[0] * (shared_ref[...] + routed)
    if final_norm:
        xn = xn * lax.rsqrt(jnp.mean(xn * xn, axis=-1, keepdims=True) + EPS) * fn_ref[...]
    o_ref[0] = xn


def moe_combine(pieces, idx, idx_t, ys, top_w, shared, x, gate, fn_g, final_norm):
    b, s, d = x.shape
    tm = TOKEN_TILE
    n_s = s // tm
    tile = lambda i, m: (i * n_s + m, 0)
    last_tile = b * n_s - 1
    return pl.pallas_call(
        functools.partial(_combine_kernel, final_norm=final_norm),
        grid=(b, n_s),
        in_specs=[
            pl.BlockSpec((1, 1, PIECE_SLOTS), lambda i, m: (i * n_s + m, 0, 0),
                         memory_space=pltpu.SMEM),
            pl.BlockSpec((1, 1, PIECE_SLOTS),
                         lambda i, m: (jnp.minimum(i * n_s + m + 1, last_tile), 0, 0),
                         memory_space=pltpu.SMEM),
            pl.BlockSpec(memory_space=pl.ANY),
            pl.BlockSpec((TOP_K, tm), lambda i, m: (0, i * n_s + m)),
            pl.BlockSpec((tm, TOP_K), tile),
            pl.BlockSpec((tm, TOP_K), tile),
            pl.BlockSpec((tm, d), tile),
            pl.BlockSpec((1, tm, d), lambda i, m: (i, m, 0)),
            pl.BlockSpec((1, 1, d), lambda i, m: (i, 0, 0)),
            pl.BlockSpec((1, d), lambda i, m: (0, 0)),
        ],
        out_specs=pl.BlockSpec((1, tm, d), lambda i, m: (i, m, 0)),
        out_shape=jax.ShapeDtypeStruct((b, s, d), _f32),
        scratch_shapes=[pltpu.VMEM((2, _tile_buffer_rows(tm) // SEG_ALIGN, SEG_ALIGN, d // 2),
                                   jnp.uint32),
                        pltpu.SemaphoreType.DMA((2,))],
        compiler_params=_cparams("arbitrary", "arbitrary"),
        name="moe_combine",
    )(pieces, pieces, _as_groups(ys), idx, idx_t, top_w, shared, x, gate, fn_g.reshape(1, d))


def moe_layer(x, g, sc, sh, gate, router_w, router_b, w1, w3, w2, layer, s_w1, s_w3, s_w2,
              fn_g, final_norm):
    b, s, d = x.shape
    t = b * s
    hb, top_idx, top_w, shared = moe_router(x, g, sc, sh, router_w, router_b, s_w1, s_w3, s_w2)
    pieces, rows = moe_rank(top_idx)
    idx_t = top_idx.T
    n_tiles = pieces.shape[0]

    rows = rows.reshape(N_EXPERTS)
    nblk = (rows + ROW_BLOCK - 1) // ROW_BLOCK
    blk_ends = jnp.cumsum(nblk)
    max_rows = t * TOP_K + n_tiles * N_EXPERTS * (SEG_ALIGN - 1)
    n_blocks = -(-max_rows // ROW_BLOCK) + N_EXPERTS
    blk = jnp.arange(n_blocks, dtype=jnp.int32)
    blk_e = jnp.minimum(jnp.sum(blk[:, None] >= blk_ends[None, :], axis=1),
                        N_EXPERTS - 1).astype(jnp.int32)
    n_used = blk_ends[-1:].astype(jnp.int32)
    pad_from = ((blk_ends - nblk) * ROW_BLOCK + rows).astype(jnp.int32)
    pad_n = (nblk * ROW_BLOCK - rows).astype(jnp.int32)

    xs = moe_dispatch(pieces, top_idx, idx_t, hb, pad_from, pad_n, n_used, n_blocks)
    ys = moe_experts(xs, blk_e, n_used, w1, w3, w2, layer)
    return moe_combine(pieces, top_idx, idx_t, ys, top_w.T, shared, x, gate, fn_g, final_norm)


def kernel(x, c, norm_mix, norm_ffn, ada_w, ada_b, da_w_in, da_w_out, da_lq1, da_lk1, da_lq2,
           da_lk2, da_subln, ret_w_in, ret_w_out, ret_gn, router_w, router_b, exp_w1, exp_w3,
           exp_w2, sh_w1, sh_w3, sh_w2, final_norm):
    depth, d = norm_mix.shape
    b = x.shape[0]
    mod = ada_modulation(c, ada_w, ada_b).reshape(depth, b, 6, 1, d)
    for i in range(depth):
        sh_a, sc_a, g_a, sh_f, sc_f, g_f = (mod[i, :, j] for j in range(6))
        j = i // 2
        if i % 2 == 0:
            proj = norm_proj(x, norm_mix[i], sc_a, sh_a, da_w_in[j].astype(_bf16))
            y = diff_attention_core(proj, da_lq1[j], da_lk1[j], da_lq2[j], da_lk2[j],
                                    da_subln[j], i)
            x = out_proj_residual(y, da_w_out[j].astype(_bf16), x, g_a)
        else:
            proj = norm_proj(x, norm_mix[i], sc_a, sh_a, ret_w_in[j].astype(_bf16))
            y = retention_core(proj, ret_gn[j])
            x = out_proj_residual(y, ret_w_out[j].astype(_bf16), x, g_a)
        x = moe_layer(x, norm_ffn[i], sc_f, sh_f, g_f, router_w[i], router_b[i],
                      exp_w1, exp_w3, exp_w2, i,
                      sh_w1[i], sh_w3[i], sh_w2[i], final_norm, i == depth - 1)
    return x
```

```python
import functools
import math

import jax
import jax.numpy as jnp
from jax import lax
from jax.experimental import pallas as pl
from jax.experimental.pallas import tpu as pltpu

EPS = 1e-6
DA_HEADS = 8
DA_HEAD_DIM = 64
DA_V_DIM = 2 * DA_HEAD_DIM
RET_HEADS = 4
RET_DK = 256
RET_DV = 512
RET_CHUNK = 128
N_EXPERTS = 64
TOP_K = 8
N_GROUPS = 8
GROUP_SIZE = N_EXPERTS // N_GROUPS
TOPK_GROUPS = 4
ROUTED_SCALE = 2.5
ROW_BLOCK_LOG2 = 10
ROW_BLOCK = 1 << ROW_BLOCK_LOG2
EXPERT_ROWS = 256
TOKEN_TILE = 256
SEG_ALIGN = 8
PERM_ROWS = 512
PIECE_SLOTS = 384
PIECE_UNROLL = 8

VMEM_LIMIT = 56 * 1024 * 1024
NEG_BIG = -1e30

_f32 = jnp.float32
_bf16 = jnp.bfloat16


def _cparams(*sem):
    return pltpu.CompilerParams(dimension_semantics=sem, vmem_limit_bytes=VMEM_LIMIT)


def _silu(v):
    return v * (1.0 / (1.0 + jnp.exp(-v)))


def _dot(a, b):
    return jnp.dot(a, b, preferred_element_type=_f32)


def _dot_nt(a, b):
    return lax.dot_general(a, b, (((1,), (1,)), ((), ())), preferred_element_type=_f32)


def _dot_tn(a, b):
    return lax.dot_general(a, b, (((0,), (0,)), ((), ())), preferred_element_type=_f32)


def _ada_kernel(c_ref, w_ref, b_ref, o_ref):
    a = _silu(c_ref[...]).astype(_bf16)
    o_ref[0] = _dot(a, w_ref[0].astype(_bf16)) + b_ref[0]


def ada_modulation(c, ada_w, ada_b):
    depth, d, n = ada_w.shape
    b = c.shape[0]
    tn = 1024
    return pl.pallas_call(
        _ada_kernel,
        grid=(depth, n // tn),
        in_specs=[
            pl.BlockSpec((b, d), lambda l, j: (0, 0)),
            pl.BlockSpec((1, d, tn), lambda l, j: (l, 0, j)),
            pl.BlockSpec((1, 1, tn), lambda l, j: (l, 0, j)),
        ],
        out_specs=pl.BlockSpec((1, b, tn), lambda l, j: (l, 0, j)),
        out_shape=jax.ShapeDtypeStruct((depth, b, n), _f32),
        compiler_params=_cparams("parallel", "parallel"),
        name="ada_modulation",
    )(c, ada_w, ada_b.reshape(depth, 1, n))


def _norm_mod(x, g, sc, sh):
    y = x * lax.rsqrt(jnp.mean(x * x, axis=-1, keepdims=True) + EPS)
    return (y * g) * (1.0 + sc) + sh


def _norm_proj_kernel(x_ref, g_ref, sc_ref, sh_ref, w_ref, o_ref, h_ref):
    @pl.when(pl.program_id(2) == 0)
    def _():
        h_ref[...] = _norm_mod(x_ref[0], g_ref[...], sc_ref[0], sh_ref[0]).astype(_bf16)

    o_ref[0] = _dot(h_ref[...], w_ref[...]).astype(o_ref.dtype)


def norm_proj(x, g, sc, sh, w):
    b, s, d = x.shape
    n = w.shape[1]
    tm = min(1024, s)
    tn = 1024
    return pl.pallas_call(
        _norm_proj_kernel,
        grid=(b, s // tm, n // tn),
        in_specs=[
            pl.BlockSpec((1, tm, d), lambda i, m, j: (i, m, 0)),
            pl.BlockSpec((1, d), lambda i, m, j: (0, 0)),
            pl.BlockSpec((1, 1, d), lambda i, m, j: (i, 0, 0)),
            pl.BlockSpec((1, 1, d), lambda i, m, j: (i, 0, 0)),
            pl.BlockSpec((d, tn), lambda i, m, j: (0, j)),
        ],
        out_specs=pl.BlockSpec((1, tm, tn), lambda i, m, j: (i, m, j)),
        out_shape=jax.ShapeDtypeStruct((b, s, n), _bf16),
        scratch_shapes=[pltpu.VMEM((tm, d), _bf16)],
        compiler_params=_cparams("parallel", "parallel", "arbitrary"),
        name="norm_proj",
    )(x, g.reshape(1, d), sc, sh, w)


def _out_proj_kernel(y_ref, w_ref, x_ref, g_ref, o_ref):
    o_ref[0] = x_ref[0] + g_ref[0] * _dot(y_ref[0], w_ref[...])


def out_proj_residual(y, w, x, gate):
    b, s, k = y.shape
    d = w.shape[1]
    tm = min(512, s)
    return pl.pallas_call(
        _out_proj_kernel,
        grid=(b, s // tm),
        in_specs=[
            pl.BlockSpec((1, tm, k), lambda i, m: (i, m, 0)),
            pl.BlockSpec((k, d), lambda i, m: (0, 0)),
            pl.BlockSpec((1, tm, d), lambda i, m: (i, m, 0)),
            pl.BlockSpec((1, 1, d), lambda i, m: (i, 0, 0)),
        ],
        out_specs=pl.BlockSpec((1, tm, d), lambda i, m: (i, m, 0)),
        out_shape=jax.ShapeDtypeStruct((b, s, d), _f32),
        compiler_params=_cparams("parallel", "parallel"),
        name="out_proj_residual",
    )(y, w, x, gate)


def _diff_attn_kernel(q_ref, k_ref, v_ref, lq1_ref, lk1_ref, lq2_ref, lk2_ref, sg_ref, o_ref,
                      bias_ref, *, tq, lam_init):
    h = pl.program_id(0)
    dh, dv = DA_HEAD_DIM, DA_V_DIM
    n_q = k_ref.shape[1] // tq

    @pl.when(pl.program_id(1) == 0)
    def _():
        hp1 = jnp.full((1, 1), h + 1, jnp.int32).astype(_f32)
        slope = jnp.exp2(hp1 * (-8.0 / DA_HEADS))
        dist = ((n_q - 1) * tq + lax.broadcasted_iota(jnp.int32, bias_ref.shape, 0)
                - lax.broadcasted_iota(jnp.int32, bias_ref.shape, 1)).astype(_f32)
        bias_ref[...] = -slope * dist

    lam = (jnp.exp(jnp.sum(lq1_ref[...] * lk1_ref[...], axis=-1, keepdims=True))
           - jnp.exp(jnp.sum(lq2_ref[...] * lk2_ref[...], axis=-1, keepdims=True)) + lam_init)
    causal = (lax.broadcasted_iota(jnp.int32, (tq, tq), 0)
              >= lax.broadcasted_iota(jnp.int32, (tq, tq), 1))
    lane = lax.broadcasted_iota(jnp.int32, (tq, 2 * dh), 1)

    def attend(c):
        q = q_ref[0, c * tq:(c + 1) * tq, :] * (dh ** -0.5)
        zero = jnp.zeros_like(q)
        q_maps = (jnp.where(lane < dh, q, zero), jnp.where(lane >= dh, q, zero))
        first = (n_q - 1 - c) * tq
        blocks = [slice(j * tq, (j + 1) * tq) for j in range(c + 1)]
        tiles = [[_dot_nt(q_m, k_ref[0, blk, :])
                  + bias_ref[:, first + blk.start:first + blk.stop] for q_m in q_maps]
                 for blk in blocks]
        tiles[c] = [jnp.where(causal, s, NEG_BIG) for s in tiles[c]]
        outs = []
        for mi in range(2):
            parts = [row[mi] for row in tiles]
            m = functools.reduce(jnp.maximum, [jnp.max(x, axis=-1, keepdims=True) for x in parts])
            ps = [jnp.exp(x - m) for x in parts]
            l = functools.reduce(jnp.add, [jnp.sum(x, axis=-1, keepdims=True) for x in ps])
            o = functools.reduce(jnp.add, [_dot(x.astype(_bf16), v_ref[0, blk, :])
                                           for x, blk in zip(ps, blocks)])
            outs.append(o / l)
        o = outs[0] - lam * outs[1]
        o = o * lax.rsqrt(jnp.mean(o * o, axis=-1, keepdims=True) + EPS)
        o = (o * sg_ref[...]) * (1.0 - lam_init)
        o_ref[0, c * tq:(c + 1) * tq, :] = o.astype(o_ref.dtype)

    for c in range(n_q):
        attend(c)


def diff_attention_core(proj, lq1, lk1, lq2, lk2, subln_g, layer_idx):
    b, s, _ = proj.shape
    nh, dh, dv = DA_HEADS, DA_HEAD_DIM, DA_V_DIM
    tq = min(512, s)
    lam_init = 0.8 - 0.6 * math.exp(-0.3 * layer_idx)
    vec = lambda n: pl.BlockSpec((1, n), lambda h, i: (0, 0))
    return pl.pallas_call(
        functools.partial(_diff_attn_kernel, tq=tq, lam_init=lam_init),
        grid=(nh, b),
        in_specs=[
            pl.BlockSpec((1, s, 2 * dh), lambda h, i: (i, 0, h)),
            pl.BlockSpec((1, s, 2 * dh), lambda h, i: (i, 0, nh + h)),
            pl.BlockSpec((1, s, dv), lambda h, i: (i, 0, 2 * nh + h)),
            vec(dh), vec(dh), vec(dh), vec(dh), vec(dv),
        ],
        out_specs=pl.BlockSpec((1, s, dv), lambda h, i: (i, 0, h)),
        out_shape=jax.ShapeDtypeStruct((b, s, nh * dv), _bf16),
        scratch_shapes=[pltpu.VMEM((tq, s), _f32)],
        compiler_params=_cparams("parallel", "arbitrary"),
        name="diff_attention",
    )(proj, proj, proj, lq1.reshape(1, dh), lk1.reshape(1, dh), lq2.reshape(1, dh),
      lk2.reshape(1, dh), subln_g.reshape(1, dv))


def _retention_kernel(q_ref, k_ref, v_ref, g_ref, gn_ref, o_ref, r_ref, *, chunk):
    h = pl.program_id(1)
    c = pl.program_id(2)
    dk = RET_DK

    @pl.when(c == 0)
    def _():
        r_ref[...] = jnp.zeros_like(r_ref)

    hf = jnp.full((1, 1), h, jnp.int32).astype(_f32)
    log_g = jnp.log(1.0 - jnp.exp2(-5.0 - hf))
    ri = lax.broadcasted_iota(jnp.int32, (chunk, chunk), 0)
    ci = lax.broadcasted_iota(jnp.int32, (chunk, chunk), 1)
    diff = (ri - ci).astype(_f32)
    decay = jnp.where(diff >= 0, jnp.exp(log_g * jnp.maximum(diff, 0.0)), 0.0)
    idx = lax.broadcasted_iota(jnp.int32, (chunk, 1), 0).astype(_f32)
    zeta = jnp.exp(log_g * (chunk - 1 - idx))
    xi = jnp.exp(log_g * (idx + 1.0))
    g_c = jnp.exp(log_g * chunk)

    decay = decay * (dk ** -0.5)
    zeta = zeta * (dk ** -0.5)
    chunks = [pl.ds(sub * chunk, chunk) for sub in range(q_ref.shape[1] // chunk)]
    intra, updates = [], []
    for rows in chunks:
        q, k, v = q_ref[0, rows, :], k_ref[0, rows, :], v_ref[0, rows, :]
        intra.append(_dot((_dot_nt(q, k) * decay).astype(_bf16), v))
        updates.append(_dot_tn((k.astype(_f32) * zeta).astype(_bf16), v))
    r = r_ref[...]
    for rows, o, upd in zip(chunks, intra, updates):
        o = o + _dot(q_ref[0, rows, :], r.astype(_bf16)) * xi
        r = g_c * r + upd
        o = o * lax.rsqrt(jnp.mean(o * o, axis=-1, keepdims=True) + EPS) * gn_ref[...]
        o_ref[0, rows, :] = (_silu(g_ref[0, rows, :].astype(_f32)) * o).astype(o_ref.dtype)
    r_ref[...] = r


def retention_core(proj, gn_g):
    b, s, _ = proj.shape
    nh, dk, dv = RET_HEADS, RET_DK, RET_DV
    chunk = RET_CHUNK
    rows = min(16 * chunk, s)
    return pl.pallas_call(
        functools.partial(_retention_kernel, chunk=chunk),
        grid=(b, nh, s // rows),
        in_specs=[
            pl.BlockSpec((1, rows, dk), lambda i, h, c: (i, c, h)),
            pl.BlockSpec((1, rows, dk), lambda i, h, c: (i, c, nh + h)),
            pl.BlockSpec((1, rows, dv), lambda i, h, c: (i, c, (2 * nh * dk) // dv + h)),
            pl.BlockSpec((1, rows, dv), lambda i, h, c: (i, c, (2 * nh * dk) // dv + nh + h)),
            pl.BlockSpec((1, dv), lambda i, h, c: (0, h)),
        ],
        out_specs=pl.BlockSpec((1, rows, dv), lambda i, h, c: (i, c, h)),
        out_shape=jax.ShapeDtypeStruct((b, s, nh * dv), _bf16),
        scratch_shapes=[pltpu.VMEM((dk, dv), _f32)],
        compiler_params=_cparams("parallel", "parallel", "arbitrary"),
        name="retention",
    )(proj, proj, proj, proj, gn_g.reshape(1, nh * dv))


def _pack_rows(h):
    n = h.shape[1] // 2
    lo = pltpu.bitcast(h[:, :n].astype(_bf16).astype(_f32), jnp.uint32)
    hi = pltpu.bitcast(h[:, n:].astype(_bf16).astype(_f32), jnp.uint32)
    return (lo >> 16) | (hi & jnp.uint32(0xFFFF0000))


def _unpack_rows(p):
    lo = pltpu.bitcast(p << 16, _f32).astype(_bf16)
    hi = pltpu.bitcast(p & jnp.uint32(0xFFFF0000), _f32).astype(_bf16)
    return lo, hi


def _first_argmax(v, iota, axis, big):
    m = jnp.max(v, axis=axis, keepdims=True)
    i = jnp.min(jnp.where(v == m, iota, big), axis=axis, keepdims=True)
    return m, i


def _router_kernel(x_ref, g_ref, sc_ref, sh_ref, rwt_ref, rb_ref, s1_ref, s3_ref, s2_ref,
                   hb_ref, idx_ref, wgt_ref, shared_ref):
    tm = x_ref.shape[1]
    h = _norm_mod(x_ref[0], g_ref[...], sc_ref[0], sh_ref[0])
    hb = h.astype(_bf16)
    hb_ref[...] = hb
    act = (_silu(_dot(hb, s1_ref[...])) * _dot(hb, s3_ref[...])).astype(_bf16)
    shared_ref[...] = _dot(act, s2_ref[...])

    scores = 1.0 / (1.0 + jnp.exp(-_dot_nt(rwt_ref[...], h)))
    choice = scores + rb_ref[...]
    c3 = choice.reshape(N_GROUPS, GROUP_SIZE, tm)
    mem = lax.broadcasted_iota(jnp.int32, c3.shape, 1)
    m1, i1 = _first_argmax(c3, mem, 1, GROUP_SIZE)
    m2 = jnp.max(jnp.where(mem == i1, -jnp.inf, c3), axis=1, keepdims=True)
    gs = m1 + m2
    gi = lax.broadcasted_iota(jnp.int32, gs.shape, 0)
    sel = jnp.zeros(gs.shape, _f32)
    for _ in range(TOPK_GROUPS):
        _, ig = _first_argmax(gs, gi, 0, N_GROUPS)
        hit = gi == ig
        sel = jnp.where(hit, 1.0, sel)
        gs = jnp.where(hit, -jnp.inf, gs)
    cm = jnp.where(sel > 0.0, c3, -jnp.inf).reshape(N_EXPERTS, tm)
    ei = lax.broadcasted_iota(jnp.int32, cm.shape, 0)
    idxs, wgts = [], []
    for _ in range(TOP_K):
        _, ie = _first_argmax(cm, ei, 0, N_EXPERTS)
        hit = ei == ie
        idxs.append(ie)
        wgts.append(jnp.sum(jnp.where(hit, scores, 0.0), axis=0, keepdims=True))
        cm = jnp.where(hit, -jnp.inf, cm)
    w = jnp.concatenate(wgts, axis=0)
    idx_ref[...] = jnp.concatenate(idxs, axis=0)
    wgt_ref[...] = w / jnp.sum(w, axis=0, keepdims=True) * ROUTED_SCALE


def moe_router(x, g, sc, sh, router_w, router_b, s_w1, s_w3, s_w2):
    b, s, d = x.shape
    t = b * s
    tm = min(512, s)
    n_s = s // tm
    sf = s_w1.shape[1]
    const = lambda shape: pl.BlockSpec(shape, lambda i, m: (0,) * len(shape))
    return pl.pallas_call(
        _router_kernel,
        grid=(b, n_s),
        in_specs=[
            pl.BlockSpec((1, tm, d), lambda i, m: (i, m, 0)),
            const((1, d)),
            pl.BlockSpec((1, 1, d), lambda i, m: (i, 0, 0)),
            pl.BlockSpec((1, 1, d), lambda i, m: (i, 0, 0)),
            const((N_EXPERTS, d)), const((N_EXPERTS, 1)),
            const((d, sf)), const((d, sf)), const((sf, d)),
        ],
        out_specs=[
            pl.BlockSpec((tm, d), lambda i, m: (i * n_s + m, 0)),
            pl.BlockSpec((TOP_K, tm), lambda i, m: (0, i * n_s + m)),
            pl.BlockSpec((TOP_K, tm), lambda i, m: (0, i * n_s + m)),
            pl.BlockSpec((tm, d), lambda i, m: (i * n_s + m, 0)),
        ],
        out_shape=[
            jax.ShapeDtypeStruct((t, d), _bf16),
            jax.ShapeDtypeStruct((TOP_K, t), jnp.int32),
            jax.ShapeDtypeStruct((TOP_K, t), _f32),
            jax.ShapeDtypeStruct((t, d), _f32),
        ],
        compiler_params=_cparams("parallel", "parallel"),
        name="moe_router",
    )(x, g.reshape(1, d), sc, sh, router_w.T, router_b.reshape(N_EXPERTS, 1),
      s_w1.astype(_bf16), s_w3.astype(_bf16), s_w2.astype(_bf16))


def _strict_lower(n):
    return jnp.where(lax.broadcasted_iota(jnp.int32, (n, n), 1)
                     < lax.broadcasted_iota(jnp.int32, (n, n), 0), 1.0, 0.0).astype(_bf16)


def _prefix_over_experts(col):
    wide = jnp.broadcast_to(col, (N_EXPERTS, 128)).astype(_bf16)
    return _dot(_strict_lower(N_EXPERTS), wide)[:, :1]


def _rank_kernel(idx_ref, piece_ref, tot_ref, base_ref):
    phase = pl.program_id(0)
    i = pl.program_id(1)
    tm = TOKEN_TILE

    @pl.when(jnp.logical_and(phase == 0, i == 0))
    def _():
        base_ref[...] = jnp.zeros_like(base_ref)

    @pl.when(jnp.logical_and(phase == 1, i == 0))
    def _():
        tot = base_ref[...].astype(jnp.int32)
        tot_ref[...] = tot
        nblk = (tot + (ROW_BLOCK - 1)) >> ROW_BLOCK_LOG2
        start_blk = (32.0 * _prefix_over_experts((nblk >> 5).astype(_f32))
                     + _prefix_over_experts((nblk & 31).astype(_f32)))
        base_ref[...] = start_blk * float(ROW_BLOCK)

    ei = lax.broadcasted_iota(jnp.int32, (N_EXPERTS, tm), 0)
    for sub in range(idx_ref.shape[1] // tm):
        idx = idx_ref[:, sub * tm:(sub + 1) * tm]
        memf = jnp.zeros((N_EXPERTS, tm), _f32)
        for k in range(TOP_K):
            memf = jnp.where(ei == idx[k:k + 1, :], 1.0, memf)
        tile_cnt = jnp.sum(memf, axis=1, keepdims=True).astype(jnp.int32)
        seg_rows = ((tile_cnt + (SEG_ALIGN - 1)) & (-SEG_ALIGN)).astype(_f32)

        @pl.when(phase == 0)
        def _(seg_rows=seg_rows):
            base_ref[...] += seg_rows

        @pl.when(phase == 1)
        def _(seg_rows=seg_rows, sub=sub):
            seg_i = seg_rows.astype(jnp.int32)
            lo = (16.0 * _prefix_over_experts((seg_i >> 4).astype(_f32))
                  + _prefix_over_experts((seg_i & 15).astype(_f32)))
            lane = lax.broadcasted_iota(jnp.int32, (N_EXPERTS, PIECE_SLOTS), 1)
            row = (lane * SEG_ALIGN).astype(_f32)
            inside = jnp.logical_and(row >= lo, row < lo + seg_rows)
            dst = jnp.sum(jnp.where(inside, base_ref[...] + (row - lo), 0.0), axis=0,
                          keepdims=True) * (1.0 / SEG_ALIGN)
            total = jnp.sum(seg_rows, axis=0, keepdims=True)
            piece_ref[sub] = jnp.where(lane[:1] == PIECE_SLOTS - 1, total, dst).astype(jnp.int32)
            base_ref[...] += seg_rows


def moe_rank(top_idx):
    k, t = top_idx.shape
    tm = TOKEN_TILE
    assert _tile_buffer_rows(tm) // SEG_ALIGN < PIECE_SLOTS
    per_step = 4 if t % (4 * tm) == 0 else 1
    return pl.pallas_call(
        _rank_kernel,
        grid=(2, t // (per_step * tm)),
        in_specs=[pl.BlockSpec((k, per_step * tm), lambda p, i: (0, i))],
        out_specs=[pl.BlockSpec((per_step, 1, PIECE_SLOTS), lambda p, i: (i * p, 0, 0)),
                   pl.BlockSpec((N_EXPERTS, 1), lambda p, i: (0, 0))],
        out_shape=[jax.ShapeDtypeStruct((t // tm, 1, PIECE_SLOTS), jnp.int32),
                   jax.ShapeDtypeStruct((N_EXPERTS, 1), jnp.int32)],
        scratch_shapes=[pltpu.VMEM((N_EXPERTS, 1), _f32)],
        compiler_params=_cparams("arbitrary", "arbitrary"),
        name="moe_rank",
    )(top_idx)


def _as_groups(a):
    return a.reshape(a.shape[0] // SEG_ALIGN, SEG_ALIGN, a.shape[1])


def _tile_rows(piece_ref):
    return piece_ref[0, 0, PIECE_SLOTS - 1]


def _for_each_piece(piece_ref, fn):
    def piece(g, c):
        fn(g, piece_ref[0, 0, g])
        return c

    def group(i, c):
        for u in range(PIECE_UNROLL):
            piece(i * PIECE_UNROLL + u, c)
        return c

    n = _tile_rows(piece_ref) // SEG_ALIGN
    lax.fori_loop(0, n // PIECE_UNROLL, group, 0)
    lax.fori_loop(n // PIECE_UNROLL * PIECE_UNROLL, n, piece, 0)


def _wait_pieces(n_pieces, make_copy):
    for bit in reversed(range(PIECE_SLOTS.bit_length())):
        @pl.when((n_pieces >> bit) & 1 == 1)
        def _(bit=bit):
            make_copy(1 << bit).wait()


def _prefix_over_lanes(row):
    n = row.shape[1]
    wide = jnp.broadcast_to(row, (SEG_ALIGN, n)).astype(_bf16)
    return _dot_nt(wide, _strict_lower(n))[:1, :]


def _member_by_token(idx_t):
    lane_e = lax.broadcasted_iota(jnp.int32, (idx_t.shape[0], N_EXPERTS), 1)
    member = jnp.zeros(lane_e.shape, _f32)
    for k in range(TOP_K):
        member = jnp.where(lane_e == idx_t[:, k:k + 1], 1.0, member)
    return member


def _member_by_expert(idx):
    sub_e = lax.broadcasted_iota(jnp.int32, (N_EXPERTS, idx.shape[1]), 0)
    member = jnp.zeros(sub_e.shape, _f32)
    for k in range(TOP_K):
        member = jnp.where(sub_e == idx[k:k + 1, :], 1.0, member)
    return member


def _segment_bounds(count, prefix):
    seg_rows = (count.astype(jnp.int32) + (SEG_ALIGN - 1)) & (-SEG_ALIGN)
    lo = 16.0 * prefix((seg_rows >> 4).astype(_f32)) + prefix((seg_rows & 15).astype(_f32))
    return lo, lo + seg_rows.astype(_f32)


def _rows_of_chunk(r0, seg_lo, seg_hi):
    row = (lax.broadcasted_iota(jnp.int32, (PERM_ROWS, N_EXPERTS), 0) + r0).astype(_f32)
    inside = jnp.logical_and(row >= seg_lo, row < seg_hi)
    rank = jnp.sum(jnp.where(inside, row - seg_lo, 0.0), axis=1, keepdims=True)
    return jnp.where(inside, 1.0, 0.0).astype(_bf16), rank


def _experts_of_chunk(r0, seg_lo, seg_hi):
    row = (lax.broadcasted_iota(jnp.int32, (N_EXPERTS, PERM_ROWS), 1) + r0).astype(_f32)
    inside = jnp.logical_and(row >= seg_lo, row < seg_hi)
    rank = jnp.sum(jnp.where(inside, row - seg_lo, 0.0), axis=0, keepdims=True)
    return jnp.where(inside, 1.0, 0.0).astype(_bf16), rank


def _dispatch_kernel(pad_from_ref, pad_n_ref, nb_ref, piece_ref, idx_ref, idx_t_ref, hb_ref,
                     xs_ref, xbuf_ref, zero_ref, sent_ref, sem, zsem):
    tm = hb_ref.shape[0]
    block_groups = zero_ref.shape[0]
    n_blocks = xs_ref.shape[0] // block_groups

    @pl.when(pl.program_id(0) == 0)
    def _():
        zero_ref[...] = jnp.zeros_like(zero_ref)

        def pad_expert(e, carry):
            off = pad_from_ref[e]

            def pad_piece(r, c):
                pltpu.make_async_copy(zero_ref.at[0], xs_ref.at[off + r], zsem).start()
                return c + 1

            return lax.fori_loop(0, pad_n_ref[e], pad_piece, carry)

        n_pieces = lax.fori_loop(0, N_EXPERTS, pad_expert, 0)

        def pad_tail(b, carry):
            pltpu.make_async_copy(zero_ref, xs_ref.at[pl.ds(b * block_groups, block_groups)],
                                  zsem).start()
            return carry

        lax.fori_loop(nb_ref[0], n_blocks, pad_tail, 0)

        def wait_piece(r, c):
            pltpu.make_async_copy(zero_ref.at[0], xs_ref.at[0], zsem).wait()
            return c

        def wait_tail(b, c):
            pltpu.make_async_copy(zero_ref, xs_ref.at[pl.ds(0, block_groups)], zsem).wait()
            return c

        lax.fori_loop(0, n_pieces, wait_piece, 0)
        lax.fori_loop(nb_ref[0], n_blocks, wait_tail, 0)

    step = pl.program_id(0)
    slot = step % 2
    total = _tile_rows(piece_ref)
    count = jnp.sum(_member_by_token(idx_t_ref[...]), axis=0, keepdims=True)
    seg_lo, seg_hi = _segment_bounds(count, _prefix_over_lanes)
    member = _member_by_expert(idx_ref[...])
    before = _dot_nt(member.astype(_bf16), _strict_lower(tm))
    place = jnp.where(member > 0.0, before + 1.0, -1.0).astype(_bf16)

    def select(c):
        r0 = c * PERM_ROWS
        expert_of_row, rank_of_row = _rows_of_chunk(r0, seg_lo, seg_hi)
        row = lax.broadcasted_iota(jnp.int32, rank_of_row.shape, 0) + r0
        want = jnp.where(row < total, rank_of_row + 1.0, -5.0)
        return jnp.where(_dot(expert_of_row, place) == want, 1.0, 0.0).astype(_bf16)

    buffer_rows = xbuf_ref.shape[1] * SEG_ALIGN
    sel = jnp.concatenate([select(c) for c in range(buffer_rows // PERM_ROWS)], axis=0)
    picked = _dot(sel, hb_ref[...])
    half = picked.shape[1] // 2
    xbuf_ref[slot] = _as_groups((pltpu.bitcast(picked[:, :half], jnp.uint32) >> 16)
                                | pltpu.bitcast(picked[:, half:], jnp.uint32))

    wait_sent = lambda n_pieces: _wait_pieces(n_pieces, lambda k: pltpu.make_async_copy(
        xbuf_ref.at[slot, pl.ds(0, k)], xs_ref.at[pl.ds(0, k)], sem))

    @pl.when(step > 0)
    def _():
        wait_sent(sent_ref[0])

    for s in range(2):
        @pl.when(slot == s)
        def _(s=s):
            def send(g, dst):
                pltpu.make_async_copy(xbuf_ref.at[s, g], xs_ref.at[dst], sem).start()

            _for_each_piece(piece_ref, send)

    sent_ref[0] = total // SEG_ALIGN

    @pl.when(step == pl.num_programs(0) - 1)
    def _():
        wait_sent(total // SEG_ALIGN)


def _tile_buffer_rows(tm):
    worst = tm * TOP_K + N_EXPERTS * (SEG_ALIGN - 1)
    return -(-worst // PERM_ROWS) * PERM_ROWS


def moe_dispatch(pieces, idx, idx_t, hb, pad_from, pad_n, n_used, n_blocks):
    t, d = hb.shape
    tm = TOKEN_TILE
    grid_spec = pltpu.PrefetchScalarGridSpec(
        num_scalar_prefetch=3,
        grid=(t // tm,),
        in_specs=[
            pl.BlockSpec((1, 1, PIECE_SLOTS), lambda i, *_: (i, 0, 0), memory_space=pltpu.SMEM),
            pl.BlockSpec((TOP_K, tm), lambda i, *_: (0, i)),
            pl.BlockSpec((tm, TOP_K), lambda i, *_: (i, 0)),
            pl.BlockSpec((tm, d), lambda i, *_: (i, 0)),
        ],
        out_specs=pl.BlockSpec(memory_space=pl.ANY),
        scratch_shapes=[pltpu.VMEM((2, _tile_buffer_rows(tm) // SEG_ALIGN, SEG_ALIGN, d // 2),
                                   jnp.uint32),
                        pltpu.VMEM((ROW_BLOCK // SEG_ALIGN, SEG_ALIGN, d // 2), jnp.uint32),
                        pltpu.SMEM((1,), jnp.int32),
                        pltpu.SemaphoreType.DMA(()), pltpu.SemaphoreType.DMA(())],
    )
    xs = pl.pallas_call(
        _dispatch_kernel,
        grid_spec=grid_spec,
        out_shape=jax.ShapeDtypeStruct((n_blocks * ROW_BLOCK // SEG_ALIGN, SEG_ALIGN, d // 2),
                                       jnp.uint32),
        compiler_params=_cparams("arbitrary"),
        name="moe_dispatch",
    )(pad_from // SEG_ALIGN, pad_n // SEG_ALIGN, n_used, pieces, idx, idx_t, hb)
    return xs.reshape(n_blocks * ROW_BLOCK, d // 2)


def _expert_kernel(be_ref, nb_ref, xs_ref, w1_ref, w3_ref, w2_ref, y_ref, w1b, w3b, w2b):
    i = pl.program_id(0)

    @pl.when(jnp.logical_or(i == 0, be_ref[i] != be_ref[jnp.maximum(i - 1, 0)]))
    def _():
        w1b[...] = w1_ref[0, 0].astype(_bf16)
        w3b[...] = w3_ref[0, 0].astype(_bf16)
        w2b[...] = w2_ref[0, 0].astype(_bf16)

    @pl.when(i < nb_ref[0])
    def _():
        chains = [pl.ds(c * EXPERT_ROWS, EXPERT_ROWS) for c in range(ROW_BLOCK // EXPERT_ROWS)]
        xs = [jnp.concatenate(_unpack_rows(xs_ref[rows, :]), axis=1) for rows in chains]
        acts = [(_silu(_dot(x, w1b[...])) * _dot(x, w3b[...])).astype(_bf16) for x in xs]
        for rows, act in zip(chains, acts):
            y_ref[rows, :] = _pack_rows(_dot(act, w2b[...]))

    @pl.when(i >= nb_ref[0])
    def _():
        y_ref[...] = jnp.zeros_like(y_ref)


def moe_experts(xs, blk_e, n_used, w1, w3, w2, layer):
    p, half = xs.shape
    d = 2 * half
    ff = w1.shape[3]
    nb = p // ROW_BLOCK
    row = lambda i, be, nu: (jnp.minimum(i, nu[0] - 1), 0)
    wsel = lambda i, be, nu: (layer, be[i], 0, 0)
    grid_spec = pltpu.PrefetchScalarGridSpec(
        num_scalar_prefetch=2,
        grid=(nb,),
        in_specs=[
            pl.BlockSpec((ROW_BLOCK, half), row),
            pl.BlockSpec((1, 1, d, ff), wsel),
            pl.BlockSpec((1, 1, d, ff), wsel),
            pl.BlockSpec((1, 1, ff, d), wsel),
        ],
        out_specs=pl.BlockSpec((ROW_BLOCK, half), lambda i, be, nu: (i, 0)),
        scratch_shapes=[pltpu.VMEM((d, ff), _bf16), pltpu.VMEM((d, ff), _bf16),
                        pltpu.VMEM((ff, d), _bf16)],
    )
    return pl.pallas_call(
        _expert_kernel,
        grid_spec=grid_spec,
        out_shape=jax.ShapeDtypeStruct((p, half), jnp.uint32),
        compiler_params=_cparams("arbitrary"),
        name="moe_experts",
    )(blk_e, n_used, xs, w1, w3, w2)


def _combine_kernel(piece_ref, next_piece_ref, ys_ref, idx_ref, idx_t_ref, w_ref, shared_ref, x_ref,
                    gate_ref, fn_ref, o_ref, ybuf_ref, sem, *, final_norm):
    tm = x_ref.shape[1]
    buffer_rows = ybuf_ref.shape[1] * SEG_ALIGN
    step = pl.program_id(0) * pl.num_programs(1) + pl.program_id(1)
    n_steps = pl.num_programs(0) * pl.num_programs(1)
    slot = step % 2

    def fetch(table_ref, s):
        def one(g, src):
            pltpu.make_async_copy(ys_ref.at[src], ybuf_ref.at[s, g], sem.at[s]).start()

        _for_each_piece(table_ref, one)

    @pl.when(step == 0)
    def _():
        fetch(piece_ref, 0)

    for s in range(2):
        @pl.when(jnp.logical_and(step + 1 < n_steps, slot == 1 - s))
        def _(s=s):
            fetch(next_piece_ref, s)

    total = _tile_rows(piece_ref)
    _wait_pieces(total // SEG_ALIGN, lambda k: pltpu.make_async_copy(
        ys_ref.at[pl.ds(0, k)], ybuf_ref.at[slot, pl.ds(0, k)], sem.at[slot]))

    idx_t = idx_t_ref[...]
    member = _member_by_token(idx_t)
    before_b = _dot(_strict_lower(tm), member.astype(_bf16)).astype(_bf16)
    lane_e = lax.broadcasted_iota(jnp.int32, member.shape, 1)
    w_of_expert = jnp.zeros(member.shape, _f32)
    for k in range(TOP_K):
        w_of_expert = jnp.where(lane_e == idx_t[:, k:k + 1], w_ref[:, k:k + 1], w_of_expert)
    w_of_expert = w_of_expert.astype(_bf16)
    count = jnp.sum(_member_by_expert(idx_ref[...]), axis=1, keepdims=True)
    seg_lo, seg_hi = _segment_bounds(count, _prefix_over_experts)

    def weights_of_chunk(c):
        expert_of_row, rank_of_row = _experts_of_chunk(c * PERM_ROWS, seg_lo, seg_hi)
        hit = _dot(before_b, expert_of_row) == rank_of_row
        return jnp.where(hit, _dot(w_of_expert, expert_of_row), 0.0).astype(_bf16)

    wm = jnp.concatenate([weights_of_chunk(c) for c in range(buffer_rows // PERM_ROWS)], axis=1)
    packed = ybuf_ref[slot].reshape(buffer_rows, ybuf_ref.shape[3])
    row = lax.broadcasted_iota(jnp.int32, packed.shape, 0)
    packed = jnp.where(row < total, packed, jnp.uint32(0))
    y_lo, y_hi = _unpack_rows(packed)
    routed = jnp.concatenate([_dot(wm, y_lo), _dot(wm, y_hi)], axis=1)

    xn = x_ref[0] + gate_ref[0] * (shared_ref[...] + routed)
    if final_norm:
        xn = xn * lax.rsqrt(jnp.mean(xn * xn, axis=-1, keepdims=True) + EPS) * fn_ref[...]
    o_ref[0] = xn


def moe_combine(pieces, idx, idx_t, ys, top_w, shared, x, gate, fn_g, final_norm):
    b, s, d = x.shape
    tm = TOKEN_TILE
    n_s = s // tm
    tile = lambda i, m: (i * n_s + m, 0)
    last_tile = b * n_s - 1
    return pl.pallas_call(
        functools.partial(_combine_kernel, final_norm=final_norm),
        grid=(b, n_s),
        in_specs=[
            pl.BlockSpec((1, 1, PIECE_SLOTS), lambda i, m: (i * n_s + m, 0, 0),
                         memory_space=pltpu.SMEM),
            pl.BlockSpec((1, 1, PIECE_SLOTS),
                         lambda i, m: (jnp.minimum(i * n_s + m + 1, last_tile), 0, 0),
                         memory_space=pltpu.SMEM),
            pl.BlockSpec(memory_space=pl.ANY),
            pl.BlockSpec((TOP_K, tm), lambda i, m: (0, i * n_s + m)),
            pl.BlockSpec((tm, TOP_K), tile),
            pl.BlockSpec((tm, TOP_K), tile),
            pl.BlockSpec((tm, d), tile),
            pl.BlockSpec((1, tm, d), lambda i, m: (i, m, 0)),
            pl.BlockSpec((1, 1, d), lambda i, m: (i, 0, 0)),
            pl.BlockSpec((1, d), lambda i, m: (0, 0)),
        ],
        out_specs=pl.BlockSpec((1, tm, d), lambda i, m: (i, m, 0)),
        out_shape=jax.ShapeDtypeStruct((b, s, d), _f32),
        scratch_shapes=[pltpu.VMEM((2, _tile_buffer_rows(tm) // SEG_ALIGN, SEG_ALIGN, d // 2),
                                   jnp.uint32),
                        pltpu.SemaphoreType.DMA((2,))],
        compiler_params=_cparams("arbitrary", "arbitrary"),
        name="moe_combine",
    )(pieces, pieces, _as_groups(ys), idx, idx_t, top_w, shared, x, gate, fn_g.reshape(1, d))


def moe_layer(x, g, sc, sh, gate, router_w, router_b, w1, w3, w2, layer, s_w1, s_w3, s_w2,
              fn_g, final_norm):
    b, s, d = x.shape
    t = b * s
    hb, top_idx, top_w, shared = moe_router(x, g, sc, sh, router_w, router_b, s_w1, s_w3, s_w2)
    pieces, rows = moe_rank(top_idx)
    idx_t = top_idx.T
    n_tiles = pieces.shape[0]

    rows = rows.reshape(N_EXPERTS)
    nblk = (rows + ROW_BLOCK - 1) // ROW_BLOCK
    blk_ends = jnp.cumsum(nblk)
    max_rows = t * TOP_K + n_tiles * N_EXPERTS * (SEG_ALIGN - 1)
    n_blocks = -(-max_rows // ROW_BLOCK) + N_EXPERTS
    blk = jnp.arange(n_blocks, dtype=jnp.int32)
    blk_e = jnp.minimum(jnp.sum(blk[:, None] >= blk_ends[None, :], axis=1),
                        N_EXPERTS - 1).astype(jnp.int32)
    n_used = blk_ends[-1:].astype(jnp.int32)
    pad_from = ((blk_ends - nblk) * ROW_BLOCK + rows).astype(jnp.int32)
    pad_n = (nblk * ROW_BLOCK - rows).astype(jnp.int32)

    xs = moe_dispatch(pieces, top_idx, idx_t, hb, pad_from, pad_n, n_used, n_blocks)
    ys = moe_experts(xs, blk_e, n_used, w1, w3, w2, layer)
    return moe_combine(pieces, top_idx, idx_t, ys, top_w.T, shared, x, gate, fn_g, final_norm)


def kernel(x, c, norm_mix, norm_ffn, ada_w, ada_b, da_w_in, da_w_out, da_lq1, da_lk1, da_lq2,
           da_lk2, da_subln, ret_w_in, ret_w_out, ret_gn, router_w, router_b, exp_w1, exp_w3,
           exp_w2, sh_w1, sh_w3, sh_w2, final_norm):
    depth, d = norm_mix.shape
    b = x.shape[0]
    mod = ada_modulation(c, ada_w, ada_b).reshape(depth, b, 6, 1, d)
    for i in range(depth):
        sh_a, sc_a, g_a, sh_f, sc_f, g_f = (mod[i, :, j] for j in range(6))
        j = i // 2
        if i % 2 == 0:
            proj = norm_proj(x, norm_mix[i], sc_a, sh_a, da_w_in[j].astype(_bf16))
            y = diff_attention_core(proj, da_lq1[j], da_lk1[j], da_lq2[j], da_lk2[j],
                                    da_subln[j], i)
            x = out_proj_residual(y, da_w_out[j].astype(_bf16), x, g_a)
        else:
            proj = norm_proj(x, norm_mix[i], sc_a, sh_a, ret_w_in[j].astype(_bf16))
            y = retention_core(proj, ret_gn[j])
            x = out_proj_residual(y, ret_w_out[j].astype(_bf16), x, g_a)
        x = moe_layer(x, norm_ffn[i], sc_f, sh_f, g_f, router_w[i], router_b[i],
                      exp_w1, exp_w3, exp_w2, i,
                      sh_w1[i], sh_w3[i], sh_w2[i], final_norm, i == depth - 1)
    return x
```

```python
import functools
import math

import jax
import jax.numpy as jnp
from jax import lax
from jax.experimental import pallas as pl
from jax.experimental.pallas import tpu as pltpu

EPS = 1e-6
DA_HEADS = 8
DA_HEAD_DIM = 64
DA_V_DIM = 2 * DA_HEAD_DIM
RET_HEADS = 4
RET_DK = 256
RET_DV = 512
RET_CHUNK = 128
N_EXPERTS = 64
TOP_K = 8
N_GROUPS = 8
GROUP_SIZE = N_EXPERTS // N_GROUPS
TOPK_GROUPS = 4
ROUTED_SCALE = 2.5
ROW_BLOCK_LOG2 = 10
ROW_BLOCK = 1 << ROW_BLOCK_LOG2
EXPERT_ROWS = 256
TOKEN_TILE = 256
SEG_ALIGN = 8
PERM_ROWS = 512
PIECE_SLOTS = 384
PIECE_UNROLL = 8

VMEM_LIMIT = 56 * 1024 * 1024
NEG_BIG = -1e30

_f32 = jnp.float32
_bf16 = jnp.bfloat16


def _cparams(*sem):
    return pltpu.CompilerParams(dimension_semantics=sem, vmem_limit_bytes=VMEM_LIMIT)


def _silu(v):
    return v * (1.0 / (1.0 + jnp.exp(-v)))


def _dot(a, b):
    return jnp.dot(a, b, preferred_element_type=_f32)


def _dot_nt(a, b):
    return lax.dot_general(a, b, (((1,), (1,)), ((), ())), preferred_element_type=_f32)


def _dot_tn(a, b):
    return lax.dot_general(a, b, (((0,), (0,)), ((), ())), preferred_element_type=_f32)


def _ada_kernel(c_ref, w_ref, b_ref, o_ref):
    a = _silu(c_ref[...]).astype(_bf16)
    o_ref[0] = _dot(a, w_ref[0].astype(_bf16)) + b_ref[0]


def ada_modulation(c, ada_w, ada_b):
    depth, d, n = ada_w.shape
    b = c.shape[0]
    tn = 1024
    return pl.pallas_call(
        _ada_kernel,
        grid=(depth, n // tn),
        in_specs=[
            pl.BlockSpec((b, d), lambda l, j: (0, 0)),
            pl.BlockSpec((1, d, tn), lambda l, j: (l, 0, j)),
            pl.BlockSpec((1, 1, tn), lambda l, j: (l, 0, j)),
        ],
        out_specs=pl.BlockSpec((1, b, tn), lambda l, j: (l, 0, j)),
        out_shape=jax.ShapeDtypeStruct((depth, b, n), _f32),
        compiler_params=_cparams("parallel", "parallel"),
        name="ada_modulation",
    )(c, ada_w, ada_b.reshape(depth, 1, n))


def _norm_mod(x, g, sc, sh):
    y = x * lax.rsqrt(jnp.mean(x * x, axis=-1, keepdims=True) + EPS)
    return (y * g) * (1.0 + sc) + sh


def _norm_proj_kernel(x_ref, g_ref, sc_ref, sh_ref, w_ref, o_ref, h_ref):
    @pl.when(pl.program_id(2) == 0)
    def _():
        h_ref[...] = _norm_mod(x_ref[0], g_ref[...], sc_ref[0], sh_ref[0]).astype(_bf16)

    o_ref[0] = _dot(h_ref[...], w_ref[...]).astype(o_ref.dtype)


def norm_proj(x, g, sc, sh, w):
    b, s, d = x.shape
    n = w.shape[1]
    tm = min(1024, s)
    tn = 1024
    return pl.pallas_call(
        _norm_proj_kernel,
        grid=(b, s // tm, n // tn),
        in_specs=[
            pl.BlockSpec((1, tm, d), lambda i, m, j: (i, m, 0)),
            pl.BlockSpec((1, d), lambda i, m, j: (0, 0)),
            pl.BlockSpec((1, 1, d), lambda i, m, j: (i, 0, 0)),
            pl.BlockSpec((1, 1, d), lambda i, m, j: (i, 0, 0)),
            pl.BlockSpec((d, tn), lambda i, m, j: (0, j)),
        ],
        out_specs=pl.BlockSpec((1, tm, tn), lambda i, m, j: (i, m, j)),
        out_shape=jax.ShapeDtypeStruct((b, s, n), _bf16),
        scratch_shapes=[pltpu.VMEM((tm, d), _bf16)],
        compiler_params=_cparams("parallel", "parallel", "arbitrary"),
        name="norm_proj",
    )(x, g.reshape(1, d), sc, sh, w)


def _diff_attn_kernel(q_ref, k_ref, v_ref, lq1_ref, lk1_ref, lq2_ref, lk2_ref, sg_ref, o_ref,
                      bias_ref, *, tq, lam_init):
    h = pl.program_id(0)
    dh, dv = DA_HEAD_DIM, DA_V_DIM
    n_q = k_ref.shape[1] // tq

    @pl.when(pl.program_id(1) == 0)
    def _():
        hp1 = jnp.full((1, 1), h + 1, jnp.int32).astype(_f32)
        slope = jnp.exp2(hp1 * (-8.0 / DA_HEADS))
        dist = ((n_q - 1) * tq + lax.broadcasted_iota(jnp.int32, bias_ref.shape, 0)
                - lax.broadcasted_iota(jnp.int32, bias_ref.shape, 1)).astype(_f32)
        bias_ref[...] = -slope * dist

    lam = (jnp.exp(jnp.sum(lq1_ref[...] * lk1_ref[...], axis=-1, keepdims=True))
           - jnp.exp(jnp.sum(lq2_ref[...] * lk2_ref[...], axis=-1, keepdims=True)) + lam_init)
    causal = (lax.broadcasted_iota(jnp.int32, (tq, tq), 0)
              >= lax.broadcasted_iota(jnp.int32, (tq, tq), 1))
    lane = lax.broadcasted_iota(jnp.int32, (tq, 2 * dh), 1)

    def attend(c):
        q = q_ref[0, c * tq:(c + 1) * tq, :] * (dh ** -0.5)
        zero = jnp.zeros_like(q)
        q_maps = (jnp.where(lane < dh, q, zero), jnp.where(lane >= dh, q, zero))
        first = (n_q - 1 - c) * tq
        blocks = [slice(j * tq, (j + 1) * tq) for j in range(c + 1)]
        tiles = [[_dot_nt(q_m, k_ref[0, blk, :])
                  + bias_ref[:, first + blk.start:first + blk.stop] for q_m in q_maps]
                 for blk in blocks]
        tiles[c] = [jnp.where(causal, s, NEG_BIG) for s in tiles[c]]
        outs = []
        for mi in range(2):
            parts = [row[mi] for row in tiles]
            m = functools.reduce(jnp.maximum, [jnp.max(x, axis=-1, keepdims=True) for x in parts])
            ps = [jnp.exp(x - m) for x in parts]
            l = functools.reduce(jnp.add, [jnp.sum(x, axis=-1, keepdims=True) for x in ps])
            o = functools.reduce(jnp.add, [_dot(x.astype(_bf16), v_ref[0, blk, :])
                                           for x, blk in zip(ps, blocks)])
            outs.append(o / l)
        o = outs[0] - lam * outs[1]
        o = o * lax.rsqrt(jnp.mean(o * o, axis=-1, keepdims=True) + EPS)
        o = (o * sg_ref[...]) * (1.0 - lam_init)
        o_ref[0, c * tq:(c + 1) * tq, :] = o.astype(o_ref.dtype)

    for c in range(n_q):
        attend(c)


def diff_attention_core(proj, lq1, lk1, lq2, lk2, subln_g, layer_idx):
    b, s, _ = proj.shape
    nh, dh, dv = DA_HEADS, DA_HEAD_DIM, DA_V_DIM
    tq = min(512, s)
    lam_init = 0.8 - 0.6 * math.exp(-0.3 * layer_idx)
    vec = lambda n: pl.BlockSpec((1, n), lambda h, i: (0, 0))
    return pl.pallas_call(
        functools.partial(_diff_attn_kernel, tq=tq, lam_init=lam_init),
        grid=(nh, b),
        in_specs=[
            pl.BlockSpec((1, s, 2 * dh), lambda h, i: (i, 0, h)),
            pl.BlockSpec((1, s, 2 * dh), lambda h, i: (i, 0, nh + h)),
            pl.BlockSpec((1, s, dv), lambda h, i: (i, 0, 2 * nh + h)),
            vec(dh), vec(dh), vec(dh), vec(dh), vec(dv),
        ],
        out_specs=pl.BlockSpec((1, s, dv), lambda h, i: (i, 0, h)),
        out_shape=jax.ShapeDtypeStruct((b, s, nh * dv), _bf16),
        scratch_shapes=[pltpu.VMEM((tq, s), _f32)],
        compiler_params=_cparams("parallel", "arbitrary"),
        name="diff_attention",
    )(proj, proj, proj, lq1.reshape(1, dh), lk1.reshape(1, dh), lq2.reshape(1, dh),
      lk2.reshape(1, dh), subln_g.reshape(1, dv))


def _retention_kernel(q_ref, k_ref, v_ref, g_ref, gn_ref, o_ref, r_ref, *, chunk):
    h = pl.program_id(1)
    c = pl.program_id(2)
    dk = RET_DK

    @pl.when(c == 0)
    def _():
        r_ref[...] = jnp.zeros_like(r_ref)

    hf = jnp.full((1, 1), h, jnp.int32).astype(_f32)
    log_g = jnp.log(1.0 - jnp.exp2(-5.0 - hf))
    ri = lax.broadcasted_iota(jnp.int32, (chunk, chunk), 0)
    ci = lax.broadcasted_iota(jnp.int32, (chunk, chunk), 1)
    diff = (ri - ci).astype(_f32)
    decay = jnp.where(diff >= 0, jnp.exp(log_g * jnp.maximum(diff, 0.0)), 0.0)
    idx = lax.broadcasted_iota(jnp.int32, (chunk, 1), 0).astype(_f32)
    zeta = jnp.exp(log_g * (chunk - 1 - idx))
    xi = jnp.exp(log_g * (idx + 1.0))
    g_c = jnp.exp(log_g * chunk)

    decay = decay * (dk ** -0.5)
    zeta = zeta * (dk ** -0.5)
    chunks = [pl.ds(sub * chunk, chunk) for sub in range(q_ref.shape[1] // chunk)]
    intra, updates = [], []
    for rows in chunks:
        q, k, v = q_ref[0, rows, :], k_ref[0, rows, :], v_ref[0, rows, :]
        intra.append(_dot((_dot_nt(q, k) * decay).astype(_bf16), v))
        updates.append(_dot_tn((k.astype(_f32) * zeta).astype(_bf16), v))
    r = r_ref[...]
    for rows, o, upd in zip(chunks, intra, updates):
        o = o + _dot(q_ref[0, rows, :], r.astype(_bf16)) * xi
        r = g_c * r + upd
        o = o * lax.rsqrt(jnp.mean(o * o, axis=-1, keepdims=True) + EPS) * gn_ref[...]
        o_ref[0, rows, :] = (_silu(g_ref[0, rows, :].astype(_f32)) * o).astype(o_ref.dtype)
    r_ref[...] = r


def retention_core(proj, gn_g):
    b, s, _ = proj.shape
    nh, dk, dv = RET_HEADS, RET_DK, RET_DV
    chunk = RET_CHUNK
    rows = min(16 * chunk, s)
    return pl.pallas_call(
        functools.partial(_retention_kernel, chunk=chunk),
        grid=(b, nh, s // rows),
        in_specs=[
            pl.BlockSpec((1, rows, dk), lambda i, h, c: (i, c, h)),
            pl.BlockSpec((1, rows, dk), lambda i, h, c: (i, c, nh + h)),
            pl.BlockSpec((1, rows, dv), lambda i, h, c: (i, c, (2 * nh * dk) // dv + h)),
            pl.BlockSpec((1, rows, dv), lambda i, h, c: (i, c, (2 * nh * dk) // dv + nh + h)),
            pl.BlockSpec((1, dv), lambda i, h, c: (0, h)),
        ],
        out_specs=pl.BlockSpec((1, rows, dv), lambda i, h, c: (i, c, h)),
        out_shape=jax.ShapeDtypeStruct((b, s, nh * dv), _bf16),
        scratch_shapes=[pltpu.VMEM((dk, dv), _f32)],
        compiler_params=_cparams("parallel", "parallel", "arbitrary"),
        name="retention",
    )(proj, proj, proj, proj, gn_g.reshape(1, nh * dv))


def _pack_rows(h):
    n = h.shape[1] // 2
    lo = pltpu.bitcast(h[:, :n].astype(_bf16).astype(_f32), jnp.uint32)
    hi = pltpu.bitcast(h[:, n:].astype(_bf16).astype(_f32), jnp.uint32)
    return (lo >> 16) | (hi & jnp.uint32(0xFFFF0000))


def _unpack_rows(p):
    lo = pltpu.bitcast(p << 16, _f32).astype(_bf16)
    hi = pltpu.bitcast(p & jnp.uint32(0xFFFF0000), _f32).astype(_bf16)
    return lo, hi


def _first_argmax(v, iota, axis, big):
    m = jnp.max(v, axis=axis, keepdims=True)
    i = jnp.min(jnp.where(v == m, iota, big), axis=axis, keepdims=True)
    return m, i


def _router_kernel(y_ref, wo_ref, x_ref, ga_ref, g_ref, sc_ref, sh_ref, rwt_ref, rb_ref, s1_ref,
                   s3_ref, s2_ref, xo_ref, hb_ref, idx_ref, wgt_ref, shared_ref):
    tm = x_ref.shape[1]
    x = x_ref[0] + ga_ref[0] * _dot(y_ref[0], wo_ref[...])
    xo_ref[0] = x
    h = _norm_mod(x, g_ref[...], sc_ref[0], sh_ref[0])
    hb = h.astype(_bf16)
    hb_ref[...] = hb
    act = (_silu(_dot(hb, s1_ref[...])) * _dot(hb, s3_ref[...])).astype(_bf16)
    shared_ref[...] = _dot(act, s2_ref[...])

    scores = 1.0 / (1.0 + jnp.exp(-_dot_nt(rwt_ref[...], h)))
    choice = scores + rb_ref[...]
    c3 = choice.reshape(N_GROUPS, GROUP_SIZE, tm)
    mem = lax.broadcasted_iota(jnp.int32, c3.shape, 1)
    m1, i1 = _first_argmax(c3, mem, 1, GROUP_SIZE)
    m2 = jnp.max(jnp.where(mem == i1, -jnp.inf, c3), axis=1, keepdims=True)
    gs = m1 + m2
    gi = lax.broadcasted_iota(jnp.int32, gs.shape, 0)
    sel = jnp.zeros(gs.shape, _f32)
    for _ in range(TOPK_GROUPS):
        _, ig = _first_argmax(gs, gi, 0, N_GROUPS)
        hit = gi == ig
        sel = jnp.where(hit, 1.0, sel)
        gs = jnp.where(hit, -jnp.inf, gs)
    cm = jnp.where(sel > 0.0, c3, -jnp.inf).reshape(N_EXPERTS, tm)
    ei = lax.broadcasted_iota(jnp.int32, cm.shape, 0)
    idxs, wgts = [], []
    for _ in range(TOP_K):
        _, ie = _first_argmax(cm, ei, 0, N_EXPERTS)
        hit = ei == ie
        idxs.append(ie)
        wgts.append(jnp.sum(jnp.where(hit, scores, 0.0), axis=0, keepdims=True))
        cm = jnp.where(hit, -jnp.inf, cm)
    w = jnp.concatenate(wgts, axis=0)
    idx_ref[...] = jnp.concatenate(idxs, axis=0)
    wgt_ref[...] = w / jnp.sum(w, axis=0, keepdims=True) * ROUTED_SCALE


def moe_router(y, w_out, x, gate_a, g, sc, sh, router_w, router_b, s_w1, s_w3, s_w2):
    b, s, d = x.shape
    t = b * s
    kdim = y.shape[2]
    tm = min(512, s)
    n_s = s // tm
    sf = s_w1.shape[1]
    const = lambda shape: pl.BlockSpec(shape, lambda i, m: (0,) * len(shape))
    return pl.pallas_call(
        _router_kernel,
        grid=(b, n_s),
        in_specs=[
            pl.BlockSpec((1, tm, kdim), lambda i, m: (i, m, 0)),
            const((kdim, d)),
            pl.BlockSpec((1, tm, d), lambda i, m: (i, m, 0)),
            pl.BlockSpec((1, 1, d), lambda i, m: (i, 0, 0)),
            const((1, d)),
            pl.BlockSpec((1, 1, d), lambda i, m: (i, 0, 0)),
            pl.BlockSpec((1, 1, d), lambda i, m: (i, 0, 0)),
            const((N_EXPERTS, d)), const((N_EXPERTS, 1)),
            const((d, sf)), const((d, sf)), const((sf, d)),
        ],
        out_specs=[
            pl.BlockSpec((1, tm, d), lambda i, m: (i, m, 0)),
            pl.BlockSpec((tm, d), lambda i, m: (i * n_s + m, 0)),
            pl.BlockSpec((TOP_K, tm), lambda i, m: (0, i * n_s + m)),
            pl.BlockSpec((TOP_K, tm), lambda i, m: (0, i * n_s + m)),
            pl.BlockSpec((tm, d), lambda i, m: (i * n_s + m, 0)),
        ],
        out_shape=[
            jax.ShapeDtypeStruct((b, s, d), _f32),
            jax.ShapeDtypeStruct((t, d), _bf16),
            jax.ShapeDtypeStruct((TOP_K, t), jnp.int32),
            jax.ShapeDtypeStruct((TOP_K, t), _f32),
            jax.ShapeDtypeStruct((t, d), _f32),
        ],
        compiler_params=_cparams("parallel", "parallel"),
        name="moe_router",
    )(y, w_out, x, gate_a, g.reshape(1, d), sc, sh, router_w.T, router_b.reshape(N_EXPERTS, 1),
      s_w1.astype(_bf16), s_w3.astype(_bf16), s_w2.astype(_bf16))


def _strict_lower(n):
    return jnp.where(lax.broadcasted_iota(jnp.int32, (n, n), 1)
                     < lax.broadcasted_iota(jnp.int32, (n, n), 0), 1.0, 0.0).astype(_bf16)


def _prefix_over_experts(col):
    wide = jnp.broadcast_to(col, (N_EXPERTS, 128)).astype(_bf16)
    return _dot(_strict_lower(N_EXPERTS), wide)[:, :1]


def _rank_kernel(idx_ref, piece_ref, tot_ref, base_ref):
    phase = pl.program_id(0)
    i = pl.program_id(1)
    tm = TOKEN_TILE

    @pl.when(jnp.logical_and(phase == 0, i == 0))
    def _():
        base_ref[...] = jnp.zeros_like(base_ref)

    @pl.when(jnp.logical_and(phase == 1, i == 0))
    def _():
        tot = base_ref[...].astype(jnp.int32)
        tot_ref[...] = tot
        nblk = (tot + (ROW_BLOCK - 1)) >> ROW_BLOCK_LOG2
        start_blk = (32.0 * _prefix_over_experts((nblk >> 5).astype(_f32))
                     + _prefix_over_experts((nblk & 31).astype(_f32)))
        base_ref[...] = start_blk * float(ROW_BLOCK)

    ei = lax.broadcasted_iota(jnp.int32, (N_EXPERTS, tm), 0)
    for sub in range(idx_ref.shape[1] // tm):
        idx = idx_ref[:, sub * tm:(sub + 1) * tm]
        memf = jnp.zeros((N_EXPERTS, tm), _f32)
        for k in range(TOP_K):
            memf = jnp.where(ei == idx[k:k + 1, :], 1.0, memf)
        tile_cnt = jnp.sum(memf, axis=1, keepdims=True).astype(jnp.int32)
        seg_rows = ((tile_cnt + (SEG_ALIGN - 1)) & (-SEG_ALIGN)).astype(_f32)

        @pl.when(phase == 0)
        def _(seg_rows=seg_rows):
            base_ref[...] += seg_rows

        @pl.when(phase == 1)
        def _(seg_rows=seg_rows, sub=sub):
            seg_i = seg_rows.astype(jnp.int32)
            lo = (16.0 * _prefix_over_experts((seg_i >> 4).astype(_f32))
                  + _prefix_over_experts((seg_i & 15).astype(_f32)))
            lane = lax.broadcasted_iota(jnp.int32, (N_EXPERTS, PIECE_SLOTS), 1)
            row = (lane * SEG_ALIGN).astype(_f32)
            inside = jnp.logical_and(row >= lo, row < lo + seg_rows)
            dst = jnp.sum(jnp.where(inside, base_ref[...] + (row - lo), 0.0), axis=0,
                          keepdims=True) * (1.0 / SEG_ALIGN)
            total = jnp.sum(seg_rows, axis=0, keepdims=True)
            piece_ref[sub] = jnp.where(lane[:1] == PIECE_SLOTS - 1, total, dst).astype(jnp.int32)
            base_ref[...] += seg_rows


def moe_rank(top_idx):
    k, t = top_idx.shape
    tm = TOKEN_TILE
    assert _tile_buffer_rows(tm) // SEG_ALIGN < PIECE_SLOTS
    per_step = 4 if t % (4 * tm) == 0 else 1
    return pl.pallas_call(
        _rank_kernel,
        grid=(2, t // (per_step * tm)),
        in_specs=[pl.BlockSpec((k, per_step * tm), lambda p, i: (0, i))],
        out_specs=[pl.BlockSpec((per_step, 1, PIECE_SLOTS), lambda p, i: (i * p, 0, 0)),
                   pl.BlockSpec((N_EXPERTS, 1), lambda p, i: (0, 0))],
        out_shape=[jax.ShapeDtypeStruct((t // tm, 1, PIECE_SLOTS), jnp.int32),
                   jax.ShapeDtypeStruct((N_EXPERTS, 1), jnp.int32)],
        scratch_shapes=[pltpu.VMEM((N_EXPERTS, 1), _f32)],
        compiler_params=_cparams("arbitrary", "arbitrary"),
        name="moe_rank",
    )(top_idx)


def _as_groups(a):
    return a.reshape(a.shape[0] // SEG_ALIGN, SEG_ALIGN, a.shape[1])


def _tile_rows(piece_ref):
    return piece_ref[0, 0, PIECE_SLOTS - 1]


def _for_each_piece(piece_ref, fn):
    def piece(g, c):
        fn(g, piece_ref[0, 0, g])
        return c

    def group(i, c):
        for u in range(PIECE_UNROLL):
            piece(i * PIECE_UNROLL + u, c)
        return c

    n = _tile_rows(piece_ref) // SEG_ALIGN
    lax.fori_loop(0, n // PIECE_UNROLL, group, 0)
    lax.fori_loop(n // PIECE_UNROLL * PIECE_UNROLL, n, piece, 0)


def _wait_pieces(n_pieces, make_copy):
    for bit in reversed(range(PIECE_SLOTS.bit_length())):
        @pl.when((n_pieces >> bit) & 1 == 1)
        def _(bit=bit):
            make_copy(1 << bit).wait()


def _prefix_over_lanes(row):
    n = row.shape[1]
    wide = jnp.broadcast_to(row, (SEG_ALIGN, n)).astype(_bf16)
    return _dot_nt(wide, _strict_lower(n))[:1, :]


def _member_by_token(idx_t):
    lane_e = lax.broadcasted_iota(jnp.int32, (idx_t.shape[0], N_EXPERTS), 1)
    member = jnp.zeros(lane_e.shape, _f32)
    for k in range(TOP_K):
        member = jnp.where(lane_e == idx_t[:, k:k + 1], 1.0, member)
    return member


def _member_by_expert(idx):
    sub_e = lax.broadcasted_iota(jnp.int32, (N_EXPERTS, idx.shape[1]), 0)
    member = jnp.zeros(sub_e.shape, _f32)
    for k in range(TOP_K):
        member = jnp.where(sub_e == idx[k:k + 1, :], 1.0, member)
    return member


def _segment_bounds(count, prefix):
    seg_rows = (count.astype(jnp.int32) + (SEG_ALIGN - 1)) & (-SEG_ALIGN)
    lo = 16.0 * prefix((seg_rows >> 4).astype(_f32)) + prefix((seg_rows & 15).astype(_f32))
    return lo, lo + seg_rows.astype(_f32)


def _rows_of_chunk(r0, seg_lo, seg_hi):
    row = (lax.broadcasted_iota(jnp.int32, (PERM_ROWS, N_EXPERTS), 0) + r0).astype(_f32)
    inside = jnp.logical_and(row >= seg_lo, row < seg_hi)
    rank = jnp.sum(jnp.where(inside, row - seg_lo, 0.0), axis=1, keepdims=True)
    return jnp.where(inside, 1.0, 0.0).astype(_bf16), rank


def _experts_of_chunk(r0, seg_lo, seg_hi):
    row = (lax.broadcasted_iota(jnp.int32, (N_EXPERTS, PERM_ROWS), 1) + r0).astype(_f32)
    inside = jnp.logical_and(row >= seg_lo, row < seg_hi)
    rank = jnp.sum(jnp.where(inside, row - seg_lo, 0.0), axis=0, keepdims=True)
    return jnp.where(inside, 1.0, 0.0).astype(_bf16), rank


def _dispatch_kernel(pad_from_ref, pad_n_ref, nb_ref, piece_ref, idx_ref, idx_t_ref, hb_ref,
                     xs_ref, xbuf_ref, zero_ref, sent_ref, sem, zsem):
    tm = hb_ref.shape[0]
    block_groups = zero_ref.shape[0]
    n_blocks = xs_ref.shape[0] // block_groups

    @pl.when(pl.program_id(0) == 0)
    def _():
        zero_ref[...] = jnp.zeros_like(zero_ref)

        def pad_expert(e, carry):
            off = pad_from_ref[e]

            def pad_piece(r, c):
                pltpu.make_async_copy(zero_ref.at[0], xs_ref.at[off + r], zsem).start()
                return c + 1

            return lax.fori_loop(0, pad_n_ref[e], pad_piece, carry)

        n_pieces = lax.fori_loop(0, N_EXPERTS, pad_expert, 0)

        def pad_tail(b, carry):
            pltpu.make_async_copy(zero_ref, xs_ref.at[pl.ds(b * block_groups, block_groups)],
                                  zsem).start()
            return carry

        lax.fori_loop(nb_ref[0], n_blocks, pad_tail, 0)

        def wait_piece(r, c):
            pltpu.make_async_copy(zero_ref.at[0], xs_ref.at[0], zsem).wait()
            return c

        def wait_tail(b, c):
            pltpu.make_async_copy(zero_ref, xs_ref.at[pl.ds(0, block_groups)], zsem).wait()
            return c

        lax.fori_loop(0, n_pieces, wait_piece, 0)
        lax.fori_loop(nb_ref[0], n_blocks, wait_tail, 0)

    step = pl.program_id(0)
    slot = step % 2
    total = _tile_rows(piece_ref)
    count = jnp.sum(_member_by_token(idx_t_ref[...]), axis=0, keepdims=True)
    seg_lo, seg_hi = _segment_bounds(count, _prefix_over_lanes)
    member = _member_by_expert(idx_ref[...])
    before = _dot_nt(member.astype(_bf16), _strict_lower(tm))
    place = jnp.where(member > 0.0, before + 1.0, -1.0).astype(_bf16)

    def select(c):
        r0 = c * PERM_ROWS
        expert_of_row, rank_of_row = _rows_of_chunk(r0, seg_lo, seg_hi)
        row = lax.broadcasted_iota(jnp.int32, rank_of_row.shape, 0) + r0
        want = jnp.where(row < total, rank_of_row + 1.0, -5.0)
        return jnp.where(_dot(expert_of_row, place) == want, 1.0, 0.0).astype(_bf16)

    buffer_rows = xbuf_ref.shape[1] * SEG_ALIGN
    sel = jnp.concatenate([select(c) for c in range(buffer_rows // PERM_ROWS)], axis=0)
    picked = _dot(sel, hb_ref[...])
    half = picked.shape[1] // 2
    xbuf_ref[slot] = _as_groups((pltpu.bitcast(picked[:, :half], jnp.uint32) >> 16)
                                | pltpu.bitcast(picked[:, half:], jnp.uint32))

    wait_sent = lambda n_pieces: _wait_pieces(n_pieces, lambda k: pltpu.make_async_copy(
        xbuf_ref.at[slot, pl.ds(0, k)], xs_ref.at[pl.ds(0, k)], sem))

    @pl.when(step > 0)
    def _():
        wait_sent(sent_ref[0])

    for s in range(2):
        @pl.when(slot == s)
        def _(s=s):
            def send(g, dst):
                pltpu.make_async_copy(xbuf_ref.at[s, g], xs_ref.at[dst], sem).start()

            _for_each_piece(piece_ref, send)

    sent_ref[0] = total // SEG_ALIGN

    @pl.when(step == pl.num_programs(0) - 1)
    def _():
        wait_sent(total // SEG_ALIGN)


def _tile_buffer_rows(tm):
    worst = tm * TOP_K + N_EXPERTS * (SEG_ALIGN - 1)
    return -(-worst // PERM_ROWS) * PERM_ROWS


def moe_dispatch(pieces, idx, idx_t, hb, pad_from, pad_n, n_used, n_blocks):
    t, d = hb.shape
    tm = TOKEN_TILE
    grid_spec = pltpu.PrefetchScalarGridSpec(
        num_scalar_prefetch=3,
        grid=(t // tm,),
        in_specs=[
            pl.BlockSpec((1, 1, PIECE_SLOTS), lambda i, *_: (i, 0, 0), memory_space=pltpu.SMEM),
            pl.BlockSpec((TOP_K, tm), lambda i, *_: (0, i)),
            pl.BlockSpec((tm, TOP_K), lambda i, *_: (i, 0)),
            pl.BlockSpec((tm, d), lambda i, *_: (i, 0)),
        ],
        out_specs=pl.BlockSpec(memory_space=pl.ANY),
        scratch_shapes=[pltpu.VMEM((2, _tile_buffer_rows(tm) // SEG_ALIGN, SEG_ALIGN, d // 2),
                                   jnp.uint32),
                        pltpu.VMEM((ROW_BLOCK // SEG_ALIGN, SEG_ALIGN, d // 2), jnp.uint32),
                        pltpu.SMEM((1,), jnp.int32),
                        pltpu.SemaphoreType.DMA(()), pltpu.SemaphoreType.DMA(())],
    )
    xs = pl.pallas_call(
        _dispatch_kernel,
        grid_spec=grid_spec,
        out_shape=jax.ShapeDtypeStruct((n_blocks * ROW_BLOCK // SEG_ALIGN, SEG_ALIGN, d // 2),
                                       jnp.uint32),
        compiler_params=_cparams("arbitrary"),
        name="moe_dispatch",
    )(pad_from // SEG_ALIGN, pad_n // SEG_ALIGN, n_used, pieces, idx, idx_t, hb)
    return xs.reshape(n_blocks * ROW_BLOCK, d // 2)


def _expert_kernel(be_ref, nb_ref, xs_ref, w1_ref, w3_ref, w2_ref, y_ref, w1b, w3b, w2b):
    i = pl.program_id(0)

    @pl.when(jnp.logical_or(i == 0, be_ref[i] != be_ref[jnp.maximum(i - 1, 0)]))
    def _():
        w1b[...] = w1_ref[0, 0].astype(_bf16)
        w3b[...] = w3_ref[0, 0].astype(_bf16)
        w2b[...] = w2_ref[0, 0].astype(_bf16)

    @pl.when(i < nb_ref[0])
    def _():
        chains = [pl.ds(c * EXPERT_ROWS, EXPERT_ROWS) for c in range(ROW_BLOCK // EXPERT_ROWS)]
        xs = [jnp.concatenate(_unpack_rows(xs_ref[rows, :]), axis=1) for rows in chains]
        acts = [(_silu(_dot(x, w1b[...])) * _dot(x, w3b[...])).astype(_bf16) for x in xs]
        for rows, act in zip(chains, acts):
            y_ref[rows, :] = _pack_rows(_dot(act, w2b[...]))

    @pl.when(i >= nb_ref[0])
    def _():
        y_ref[...] = jnp.zeros_like(y_ref)


def moe_experts(xs, blk_e, n_used, w1, w3, w2, layer):
    p, half = xs.shape
    d = 2 * half
    ff = w1.shape[3]
    nb = p // ROW_BLOCK
    row = lambda i, be, nu: (jnp.minimum(i, nu[0] - 1), 0)
    wsel = lambda i, be, nu: (layer, be[i], 0, 0)
    grid_spec = pltpu.PrefetchScalarGridSpec(
        num_scalar_prefetch=2,
        grid=(nb,),
        in_specs=[
            pl.BlockSpec((ROW_BLOCK, half), row),
            pl.BlockSpec((1, 1, d, ff), wsel),
            pl.BlockSpec((1, 1, d, ff), wsel),
            pl.BlockSpec((1, 1, ff, d), wsel),
        ],
        out_specs=pl.BlockSpec((ROW_BLOCK, half), lambda i, be, nu: (i, 0)),
        scratch_shapes=[pltpu.VMEM((d, ff), _bf16), pltpu.VMEM((d, ff), _bf16),
                        pltpu.VMEM((ff, d), _bf16)],
    )
    return pl.pallas_call(
        _expert_kernel,
        grid_spec=grid_spec,
        out_shape=jax.ShapeDtypeStruct((p, half), jnp.uint32),
        compiler_params=_cparams("arbitrary"),
        name="moe_experts",
    )(blk_e, n_used, xs, w1, w3, w2)


def _combine_kernel(piece_ref, next_piece_ref, ys_ref, idx_ref, idx_t_ref, w_ref, shared_ref, x_ref,
                    gate_ref, fn_ref, o_ref, ybuf_ref, sem, *, final_norm):
    tm = x_ref.shape[1]
    buffer_rows = ybuf_ref.shape[1] * SEG_ALIGN
    step = pl.program_id(0) * pl.num_programs(1) + pl.program_id(1)
    n_steps = pl.num_programs(0) * pl.num_programs(1)
    slot = step % 2

    def fetch(table_ref, s):
        def one(g, src):
            pltpu.make_async_copy(ys_ref.at[src], ybuf_ref.at[s, g], sem.at[s]).start()

        _for_each_piece(table_ref, one)

    @pl.when(step == 0)
    def _():
        fetch(piece_ref, 0)

    for s in range(2):
        @pl.when(jnp.logical_and(step + 1 < n_steps, slot == 1 - s))
        def _(s=s):
            fetch(next_piece_ref, s)

    total = _tile_rows(piece_ref)
    _wait_pieces(total // SEG_ALIGN, lambda k: pltpu.make_async_copy(
        ys_ref.at[pl.ds(0, k)], ybuf_ref.at[slot, pl.ds(0, k)], sem.at[slot]))

    idx_t = idx_t_ref[...]
    member = _member_by_token(idx_t)
    before_b = _dot(_strict_lower(tm), member.astype(_bf16)).astype(_bf16)
    lane_e = lax.broadcasted_iota(jnp.int32, member.shape, 1)
    w_of_expert = jnp.zeros(member.shape, _f32)
    for k in range(TOP_K):
        w_of_expert = jnp.where(lane_e == idx_t[:, k:k + 1], w_ref[:, k:k + 1], w_of_expert)
    w_of_expert = w_of_expert.astype(_bf16)
    count = jnp.sum(_member_by_expert(idx_ref[...]), axis=1, keepdims=True)
    seg_lo, seg_hi = _segment_bounds(count, _prefix_over_experts)

    def weights_of_chunk(c):
        expert_of_row, rank_of_row = _experts_of_chunk(c * PERM_ROWS, seg_lo, seg_hi)
        hit = _dot(before_b, expert_of_row) == rank_of_row
        return jnp.where(hit, _dot(w_of_expert, expert_of_row), 0.0).astype(_bf16)

    wm = jnp.concatenate([weights_of_chunk(c) for c in range(buffer_rows // PERM_ROWS)], axis=1)
    packed = ybuf_ref[slot].reshape(buffer_rows, ybuf_ref.shape[3])
    row = lax.broadcasted_iota(jnp.int32, packed.shape, 0)
    packed = jnp.where(row < total, packed, jnp.uint32(0))
    y_lo, y_hi = _unpack_rows(packed)
    routed = jnp.concatenate([_dot(wm, y_lo), _dot(wm, y_hi)], axis=1)

    xn = x_ref[0] + gate_ref[0] * (shared_ref[...] + routed)
    if final_norm:
        xn = xn * lax.rsqrt(jnp.mean(xn * xn, axis=-1, keepdims=True) + EPS) * fn_ref[...]
    o_ref[0] = xn


def moe_combine(pieces, idx, idx_t, ys, top_w, shared, x, gate, fn_g, final_norm):
    b, s, d = x.shape
    tm = TOKEN_TILE
    n_s = s // tm
    tile = lambda i, m: (i * n_s + m, 0)
    last_tile = b * n_s - 1
    return pl.pallas_call(
        functools.partial(_combine_kernel, final_norm=final_norm),
        grid=(b, n_s),
        in_specs=[
            pl.BlockSpec((1, 1, PIECE_SLOTS), lambda i, m: (i * n_s + m, 0, 0),
                         memory_space=pltpu.SMEM),
            pl.BlockSpec((1, 1, PIECE_SLOTS),
                         lambda i, m: (jnp.minimum(i * n_s + m + 1, last_tile), 0, 0),
                         memory_space=pltpu.SMEM),
            pl.BlockSpec(memory_space=pl.ANY),
            pl.BlockSpec((TOP_K, tm), lambda i, m: (0, i * n_s + m)),
            pl.BlockSpec((tm, TOP_K), tile),
            pl.BlockSpec((tm, TOP_K), tile),
            pl.BlockSpec((tm, d), tile),
            pl.BlockSpec((1, tm, d), lambda i, m: (i, m, 0)),
            pl.BlockSpec((1, 1, d), lambda i, m: (i, 0, 0)),
            pl.BlockSpec((1, d), lambda i, m: (0, 0)),
        ],
        out_specs=pl.BlockSpec((1, tm, d), lambda i, m: (i, m, 0)),
        out_shape=jax.ShapeDtypeStruct((b, s, d), _f32),
        scratch_shapes=[pltpu.VMEM((2, _tile_buffer_rows(tm) // SEG_ALIGN, SEG_ALIGN, d // 2),
                                   jnp.uint32),
                        pltpu.SemaphoreType.DMA((2,))],
        compiler_params=_cparams("arbitrary", "arbitrary"),
        name="moe_combine",
    )(pieces, pieces, _as_groups(ys), idx, idx_t, top_w, shared, x, gate, fn_g.reshape(1, d))


def moe_layer(y, w_out, x, gate_a, g, sc, sh, gate, router_w, router_b, w1, w3, w2, layer,
              s_w1, s_w3, s_w2, fn_g, final_norm):
    b, s, d = x.shape
    t = b * s
    x, hb, top_idx, top_w, shared = moe_router(y, w_out, x, gate_a, g, sc, sh, router_w, router_b,
                                               s_w1, s_w3, s_w2)
    pieces, rows = moe_rank(top_idx)
    idx_t = top_idx.T
    n_tiles = pieces.shape[0]

    rows = rows.reshape(N_EXPERTS)
    nblk = (rows + ROW_BLOCK - 1) // ROW_BLOCK
    blk_ends = jnp.cumsum(nblk)
    max_rows = t * TOP_K + n_tiles * N_EXPERTS * (SEG_ALIGN - 1)
    n_blocks = -(-max_rows // ROW_BLOCK) + N_EXPERTS
    blk = jnp.arange(n_blocks, dtype=jnp.int32)
    blk_e = jnp.minimum(jnp.sum(blk[:, None] >= blk_ends[None, :], axis=1),
                        N_EXPERTS - 1).astype(jnp.int32)
    n_used = blk_ends[-1:].astype(jnp.int32)
    pad_from = ((blk_ends - nblk) * ROW_BLOCK + rows).astype(jnp.int32)
    pad_n = (nblk * ROW_BLOCK - rows).astype(jnp.int32)

    xs = moe_dispatch(pieces, top_idx, idx_t, hb, pad_from, pad_n, n_used, n_blocks)
    ys = moe_experts(xs, blk_e, n_used, w1, w3, w2, layer)
    return moe_combine(pieces, top_idx, idx_t, ys, top_w.T, shared, x, gate, fn_g, final_norm)


def kernel(x, c, norm_mix, norm_ffn, ada_w, ada_b, da_w_in, da_w_out, da_lq1, da_lk1, da_lq2,
           da_lk2, da_subln, ret_w_in, ret_w_out, ret_gn, router_w, router_b, exp_w1, exp_w3,
           exp_w2, sh_w1, sh_w3, sh_w2, final_norm):
    depth, d = norm_mix.shape
    b = x.shape[0]
    mod = ada_modulation(c, ada_w, ada_b).reshape(depth, b, 6, 1, d)
    for i in range(depth):
        sh_a, sc_a, g_a, sh_f, sc_f, g_f = (mod[i, :, j] for j in range(6))
        j = i // 2
        if i % 2 == 0:
            proj = norm_proj(x, norm_mix[i], sc_a, sh_a, da_w_in[j].astype(_bf16))
            y = diff_attention_core(proj, da_lq1[j], da_lk1[j], da_lq2[j], da_lk2[j],
                                    da_subln[j], i)
            w_out = da_w_out[j].astype(_bf16)
        else:
            proj = norm_proj(x, norm_mix[i], sc_a, sh_a, ret_w_in[j].astype(_bf16))
            y = retention_core(proj, ret_gn[j])
            w_out = ret_w_out[j].astype(_bf16)
        x = moe_layer(y, w_out, x, g_a, norm_ffn[i], sc_f, sh_f, g_f, router_w[i], router_b[i],
                      exp_w1, exp_w3, exp_w2, i,
                      sh_w1[i], sh_w3[i], sh_w2[i], final_norm, i == depth - 1)
    return x
```

```python
import functools
import math

import jax
import jax.numpy as jnp
from jax import lax
from jax.experimental import pallas as pl
from jax.experimental.pallas import tpu as pltpu

EPS = 1e-6
DA_HEADS = 8
DA_HEAD_DIM = 64
DA_V_DIM = 2 * DA_HEAD_DIM
RET_HEADS = 4
RET_DK = 256
RET_DV = 512
RET_CHUNK = 128
N_EXPERTS = 64
TOP_K = 8
N_GROUPS = 8
GROUP_SIZE = N_EXPERTS // N_GROUPS
TOPK_GROUPS = 4
ROUTED_SCALE = 2.5
ROW_BLOCK_LOG2 = 10
ROW_BLOCK = 1 << ROW_BLOCK_LOG2
EXPERT_ROWS = 256
TOKEN_TILE = 256
SEG_ALIGN = 8
PERM_ROWS = 512
PIECE_SLOTS = 384
PIECE_UNROLL = 8

VMEM_LIMIT = 56 * 1024 * 1024
NEG_BIG = -1e30

_f32 = jnp.float32
_bf16 = jnp.bfloat16


def _cparams(*sem):
    return pltpu.CompilerParams(dimension_semantics=sem, vmem_limit_bytes=VMEM_LIMIT)


def _silu(v):
    return v * (1.0 / (1.0 + jnp.exp(-v)))


def _dot(a, b):
    return jnp.dot(a, b, preferred_element_type=_f32)


def _dot_nt(a, b):
    return lax.dot_general(a, b, (((1,), (1,)), ((), ())), preferred_element_type=_f32)


def _dot_tn(a, b):
    return lax.dot_general(a, b, (((0,), (0,)), ((), ())), preferred_element_type=_f32)


def _ada_kernel(c_ref, w_ref, b_ref, o_ref):
    a = _silu(c_ref[...]).astype(_bf16)
    o_ref[0] = _dot(a, w_ref[0].astype(_bf16)) + b_ref[0]


def ada_modulation(c, ada_w, ada_b):
    depth, d, n = ada_w.shape
    b = c.shape[0]
    tn = 1024
    return pl.pallas_call(
        _ada_kernel,
        grid=(depth, n // tn),
        in_specs=[
            pl.BlockSpec((b, d), lambda l, j: (0, 0)),
            pl.BlockSpec((1, d, tn), lambda l, j: (l, 0, j)),
            pl.BlockSpec((1, 1, tn), lambda l, j: (l, 0, j)),
        ],
        out_specs=pl.BlockSpec((1, b, tn), lambda l, j: (l, 0, j)),
        out_shape=jax.ShapeDtypeStruct((depth, b, n), _f32),
        compiler_params=_cparams("parallel", "parallel"),
        name="ada_modulation",
    )(c, ada_w, ada_b.reshape(depth, 1, n))


def _norm_mod(x, g, sc, sh):
    y = x * lax.rsqrt(jnp.mean(x * x, axis=-1, keepdims=True) + EPS)
    return (y * g) * (1.0 + sc) + sh


def _norm_proj_kernel(x_ref, g_ref, sc_ref, sh_ref, w_ref, o_ref, h_ref):
    @pl.when(pl.program_id(2) == 0)
    def _():
        h_ref[...] = _norm_mod(x_ref[0], g_ref[...], sc_ref[0], sh_ref[0]).astype(_bf16)

    o_ref[0] = _dot(h_ref[...], w_ref[...]).astype(o_ref.dtype)


def norm_proj(x, g, sc, sh, w):
    b, s, d = x.shape
    n = w.shape[1]
    tm = min(2048, s)
    tn = 1024
    return pl.pallas_call(
        _norm_proj_kernel,
        grid=(b, s // tm, n // tn),
        in_specs=[
            pl.BlockSpec((1, tm, d), lambda i, m, j: (i, m, 0)),
            pl.BlockSpec((1, d), lambda i, m, j: (0, 0)),
            pl.BlockSpec((1, 1, d), lambda i, m, j: (i, 0, 0)),
            pl.BlockSpec((1, 1, d), lambda i, m, j: (i, 0, 0)),
            pl.BlockSpec((d, tn), lambda i, m, j: (0, j)),
        ],
        out_specs=pl.BlockSpec((1, tm, tn), lambda i, m, j: (i, m, j)),
        out_shape=jax.ShapeDtypeStruct((b, s, n), _bf16),
        scratch_shapes=[pltpu.VMEM((tm, d), _bf16)],
        compiler_params=_cparams("parallel", "parallel", "arbitrary"),
        name="norm_proj",
    )(x, g.reshape(1, d), sc, sh, w)


def _diff_attn_kernel(q_ref, k_ref, v_ref, lq1_ref, lk1_ref, lq2_ref, lk2_ref, sg_ref, o_ref,
                      bias_ref, *, tq, lam_init):
    h = pl.program_id(0)
    dh, dv = DA_HEAD_DIM, DA_V_DIM
    n_q = k_ref.shape[1] // tq

    @pl.when(pl.program_id(1) == 0)
    def _():
        hp1 = jnp.full((1, 1), h + 1, jnp.int32).astype(_f32)
        slope = jnp.exp2(hp1 * (-8.0 / DA_HEADS))
        dist = ((n_q - 1) * tq + lax.broadcasted_iota(jnp.int32, bias_ref.shape, 0)
                - lax.broadcasted_iota(jnp.int32, bias_ref.shape, 1)).astype(_f32)
        bias_ref[...] = -slope * dist

    lam = (jnp.exp(jnp.sum(lq1_ref[...] * lk1_ref[...], axis=-1, keepdims=True))
           - jnp.exp(jnp.sum(lq2_ref[...] * lk2_ref[...], axis=-1, keepdims=True)) + lam_init)
    causal = (lax.broadcasted_iota(jnp.int32, (tq, tq), 0)
              >= lax.broadcasted_iota(jnp.int32, (tq, tq), 1))
    lane = lax.broadcasted_iota(jnp.int32, (tq, 2 * dh), 1)

    def attend(c):
        q = q_ref[0, c * tq:(c + 1) * tq, :] * (dh ** -0.5)
        zero = jnp.zeros_like(q)
        q_maps = (jnp.where(lane < dh, q, zero), jnp.where(lane >= dh, q, zero))
        first = (n_q - 1 - c) * tq
        blocks = [slice(j * tq, (j + 1) * tq) for j in range(c + 1)]
        tiles = [[_dot_nt(q_m, k_ref[0, blk, :])
                  + bias_ref[:, first + blk.start:first + blk.stop] for q_m in q_maps]
                 for blk in blocks]
        tiles[c] = [jnp.where(causal, s, NEG_BIG) for s in tiles[c]]
        outs = []
        for mi in range(2):
            parts = [row[mi] for row in tiles]
            m = functools.reduce(jnp.maximum, [jnp.max(x, axis=-1, keepdims=True) for x in parts])
            ps = [jnp.exp(x - m) for x in parts]
            l = functools.reduce(jnp.add, [jnp.sum(x, axis=-1, keepdims=True) for x in ps])
            o = functools.reduce(jnp.add, [_dot(x.astype(_bf16), v_ref[0, blk, :])
                                           for x, blk in zip(ps, blocks)])
            outs.append(o / l)
        o = outs[0] - lam * outs[1]
        o = o * lax.rsqrt(jnp.mean(o * o, axis=-1, keepdims=True) + EPS)
        o = (o * sg_ref[...]) * (1.0 - lam_init)
        o_ref[0, c * tq:(c + 1) * tq, :] = o.astype(o_ref.dtype)

    for c in range(n_q):
        attend(c)


def diff_attention_core(proj, lq1, lk1, lq2, lk2, subln_g, layer_idx):
    b, s, _ = proj.shape
    nh, dh, dv = DA_HEADS, DA_HEAD_DIM, DA_V_DIM
    tq = min(512, s)
    lam_init = 0.8 - 0.6 * math.exp(-0.3 * layer_idx)
    vec = lambda n: pl.BlockSpec((1, n), lambda h, i: (0, 0))
    return pl.pallas_call(
        functools.partial(_diff_attn_kernel, tq=tq, lam_init=lam_init),
        grid=(nh, b),
        in_specs=[
            pl.BlockSpec((1, s, 2 * dh), lambda h, i: (i, 0, h)),
            pl.BlockSpec((1, s, 2 * dh), lambda h, i: (i, 0, nh + h)),
            pl.BlockSpec((1, s, dv), lambda h, i: (i, 0, 2 * nh + h)),
            vec(dh), vec(dh), vec(dh), vec(dh), vec(dv),
        ],
        out_specs=pl.BlockSpec((1, s, dv), lambda h, i: (i, 0, h)),
        out_shape=jax.ShapeDtypeStruct((b, s, nh * dv), _bf16),
        scratch_shapes=[pltpu.VMEM((tq, s), _f32)],
        compiler_params=_cparams("parallel", "arbitrary"),
        name="diff_attention",
    )(proj, proj, proj, lq1.reshape(1, dh), lk1.reshape(1, dh), lq2.reshape(1, dh),
      lk2.reshape(1, dh), subln_g.reshape(1, dv))


def _retention_kernel(q_ref, k_ref, v_ref, g_ref, gn_ref, o_ref, r_ref, *, chunk):
    h = pl.program_id(1)
    c = pl.program_id(2)
    dk = RET_DK

    @pl.when(c == 0)
    def _():
        r_ref[...] = jnp.zeros_like(r_ref)

    hf = jnp.full((1, 1), h, jnp.int32).astype(_f32)
    log_g = jnp.log(1.0 - jnp.exp2(-5.0 - hf))
    ri = lax.broadcasted_iota(jnp.int32, (chunk, chunk), 0)
    ci = lax.broadcasted_iota(jnp.int32, (chunk, chunk), 1)
    diff = (ri - ci).astype(_f32)
    decay = jnp.where(diff >= 0, jnp.exp(log_g * jnp.maximum(diff, 0.0)), 0.0)
    idx = lax.broadcasted_iota(jnp.int32, (chunk, 1), 0).astype(_f32)
    zeta = jnp.exp(log_g * (chunk - 1 - idx))
    xi = jnp.exp(log_g * (idx + 1.0))
    g_c = jnp.exp(log_g * chunk)

    decay = decay * (dk ** -0.5)
    zeta = zeta * (dk ** -0.5)
    chunks = [pl.ds(sub * chunk, chunk) for sub in range(q_ref.shape[1] // chunk)]
    intra, updates = [], []
    for rows in chunks:
        q, k, v = q_ref[0, rows, :], k_ref[0, rows, :], v_ref[0, rows, :]
        intra.append(_dot((_dot_nt(q, k) * decay).astype(_bf16), v))
        updates.append(_dot_tn((k.astype(_f32) * zeta).astype(_bf16), v))
    r = r_ref[...]
    for rows, o, upd in zip(chunks, intra, updates):
        o = o + _dot(q_ref[0, rows, :], r.astype(_bf16)) * xi
        r = g_c * r + upd
        o = o * lax.rsqrt(jnp.mean(o * o, axis=-1, keepdims=True) + EPS) * gn_ref[...]
        o_ref[0, rows, :] = (_silu(g_ref[0, rows, :].astype(_f32)) * o).astype(o_ref.dtype)
    r_ref[...] = r


def retention_core(proj, gn_g):
    b, s, _ = proj.shape
    nh, dk, dv = RET_HEADS, RET_DK, RET_DV
    chunk = RET_CHUNK
    rows = min(16 * chunk, s)
    return pl.pallas_call(
        functools.partial(_retention_kernel, chunk=chunk),
        grid=(b, nh, s // rows),
        in_specs=[
            pl.BlockSpec((1, rows, dk), lambda i, h, c: (i, c, h)),
            pl.BlockSpec((1, rows, dk), lambda i, h, c: (i, c, nh + h)),
            pl.BlockSpec((1, rows, dv), lambda i, h, c: (i, c, (2 * nh * dk) // dv + h)),
            pl.BlockSpec((1, rows, dv), lambda i, h, c: (i, c, (2 * nh * dk) // dv + nh + h)),
            pl.BlockSpec((1, dv), lambda i, h, c: (0, h)),
        ],
        out_specs=pl.BlockSpec((1, rows, dv), lambda i, h, c: (i, c, h)),
        out_shape=jax.ShapeDtypeStruct((b, s, nh * dv), _bf16),
        scratch_shapes=[pltpu.VMEM((dk, dv), _f32)],
        compiler_params=_cparams("parallel", "parallel", "arbitrary"),
        name="retention",
    )(proj, proj, proj, proj, gn_g.reshape(1, nh * dv))


def _pack_rows(h):
    n = h.shape[1] // 2
    lo = pltpu.bitcast(h[:, :n].astype(_bf16).astype(_f32), jnp.uint32)
    hi = pltpu.bitcast(h[:, n:].astype(_bf16).astype(_f32), jnp.uint32)
    return (lo >> 16) | (hi & jnp.uint32(0xFFFF0000))


def _unpack_rows(p):
    lo = pltpu.bitcast(p << 16, _f32).astype(_bf16)
    hi = pltpu.bitcast(p & jnp.uint32(0xFFFF0000), _f32).astype(_bf16)
    return lo, hi


def _first_argmax(v, iota, axis, big):
    m = jnp.max(v, axis=axis, keepdims=True)
    i = jnp.min(jnp.where(v == m, iota, big), axis=axis, keepdims=True)
    return m, i


def _router_kernel(y_ref, wo_ref, x_ref, ga_ref, g_ref, sc_ref, sh_ref, rwt_ref, rb_ref, s1_ref,
                   s3_ref, s2_ref, xo_ref, hb_ref, idx_ref, wgt_ref, shared_ref):
    tm = x_ref.shape[1]
    x = x_ref[0] + ga_ref[0] * _dot(y_ref[0], wo_ref[...])
    xo_ref[0] = x
    h = _norm_mod(x, g_ref[...], sc_ref[0], sh_ref[0])
    hb = h.astype(_bf16)
    hb_ref[...] = hb
    act = (_silu(_dot(hb, s1_ref[...])) * _dot(hb, s3_ref[...])).astype(_bf16)
    shared_ref[...] = _dot(act, s2_ref[...])

    scores = 1.0 / (1.0 + jnp.exp(-_dot_nt(rwt_ref[...], h)))
    choice = scores + rb_ref[...]
    c3 = choice.reshape(N_GROUPS, GROUP_SIZE, tm)
    mem = lax.broadcasted_iota(jnp.int32, c3.shape, 1)
    m1, i1 = _first_argmax(c3, mem, 1, GROUP_SIZE)
    m2 = jnp.max(jnp.where(mem == i1, -jnp.inf, c3), axis=1, keepdims=True)
    gs = m1 + m2
    gi = lax.broadcasted_iota(jnp.int32, gs.shape, 0)
    sel = jnp.zeros(gs.shape, _f32)
    for _ in range(TOPK_GROUPS):
        _, ig = _first_argmax(gs, gi, 0, N_GROUPS)
        hit = gi == ig
        sel = jnp.where(hit, 1.0, sel)
        gs = jnp.where(hit, -jnp.inf, gs)
    cm = jnp.where(sel > 0.0, c3, -jnp.inf).reshape(N_EXPERTS, tm)
    ei = lax.broadcasted_iota(jnp.int32, cm.shape, 0)
    idxs, wgts = [], []
    for _ in range(TOP_K):
        _, ie = _first_argmax(cm, ei, 0, N_EXPERTS)
        hit = ei == ie
        idxs.append(ie)
        wgts.append(jnp.sum(jnp.where(hit, scores, 0.0), axis=0, keepdims=True))
        cm = jnp.where(hit, -jnp.inf, cm)
    w = jnp.concatenate(wgts, axis=0)
    idx_ref[...] = jnp.concatenate(idxs, axis=0)
    wgt_ref[...] = w / jnp.sum(w, axis=0, keepdims=True) * ROUTED_SCALE


def moe_router(y, w_out, x, gate_a, g, sc, sh, router_w, router_b, s_w1, s_w3, s_w2):
    b, s, d = x.shape
    t = b * s
    kdim = y.shape[2]
    tm = min(512, s)
    n_s = s // tm
    sf = s_w1.shape[1]
    const = lambda shape: pl.BlockSpec(shape, lambda i, m: (0,) * len(shape))
    return pl.pallas_call(
        _router_kernel,
        grid=(b, n_s),
        in_specs=[
            pl.BlockSpec((1, tm, kdim), lambda i, m: (i, m, 0)),
            const((kdim, d)),
            pl.BlockSpec((1, tm, d), lambda i, m: (i, m, 0)),
            pl.BlockSpec((1, 1, d), lambda i, m: (i, 0, 0)),
            const((1, d)),
            pl.BlockSpec((1, 1, d), lambda i, m: (i, 0, 0)),
            pl.BlockSpec((1, 1, d), lambda i, m: (i, 0, 0)),
            const((N_EXPERTS, d)), const((N_EXPERTS, 1)),
            const((d, sf)), const((d, sf)), const((sf, d)),
        ],
        out_specs=[
            pl.BlockSpec((1, tm, d), lambda i, m: (i, m, 0)),
            pl.BlockSpec((tm, d), lambda i, m: (i * n_s + m, 0)),
            pl.BlockSpec((TOP_K, tm), lambda i, m: (0, i * n_s + m)),
            pl.BlockSpec((TOP_K, tm), lambda i, m: (0, i * n_s + m)),
            pl.BlockSpec((tm, d), lambda i, m: (i * n_s + m, 0)),
        ],
        out_shape=[
            jax.ShapeDtypeStruct((b, s, d), _f32),
            jax.ShapeDtypeStruct((t, d), _bf16),
            jax.ShapeDtypeStruct((TOP_K, t), jnp.int32),
            jax.ShapeDtypeStruct((TOP_K, t), _f32),
            jax.ShapeDtypeStruct((t, d), _f32),
        ],
        compiler_params=_cparams("parallel", "parallel"),
        name="moe_router",
    )(y, w_out, x, gate_a, g.reshape(1, d), sc, sh, router_w.T, router_b.reshape(N_EXPERTS, 1),
      s_w1.astype(_bf16), s_w3.astype(_bf16), s_w2.astype(_bf16))


def _strict_lower(n):
    return jnp.where(lax.broadcasted_iota(jnp.int32, (n, n), 1)
                     < lax.broadcasted_iota(jnp.int32, (n, n), 0), 1.0, 0.0).astype(_bf16)


def _prefix_over_experts(col):
    wide = jnp.broadcast_to(col, (N_EXPERTS, 128)).astype(_bf16)
    return _dot(_strict_lower(N_EXPERTS), wide)[:, :1]


def _rank_kernel(idx_ref, piece_ref, tot_ref, base_ref):
    phase = pl.program_id(0)
    i = pl.program_id(1)
    tm = TOKEN_TILE

    @pl.when(jnp.logical_and(phase == 0, i == 0))
    def _():
        base_ref[...] = jnp.zeros_like(base_ref)

    @pl.when(jnp.logical_and(phase == 1, i == 0))
    def _():
        tot = base_ref[...].astype(jnp.int32)
        tot_ref[...] = tot
        nblk = (tot + (ROW_BLOCK - 1)) >> ROW_BLOCK_LOG2
        start_blk = (32.0 * _prefix_over_experts((nblk >> 5).astype(_f32))
                     + _prefix_over_experts((nblk & 31).astype(_f32)))
        base_ref[...] = start_blk * float(ROW_BLOCK)

    ei = lax.broadcasted_iota(jnp.int32, (N_EXPERTS, tm), 0)
    for sub in range(idx_ref.shape[1] // tm):
        idx = idx_ref[:, sub * tm:(sub + 1) * tm]
        memf = jnp.zeros((N_EXPERTS, tm), _f32)
        for k in range(TOP_K):
            memf = jnp.where(ei == idx[k:k + 1, :], 1.0, memf)
        tile_cnt = jnp.sum(memf, axis=1, keepdims=True).astype(jnp.int32)
        seg_rows = ((tile_cnt + (SEG_ALIGN - 1)) & (-SEG_ALIGN)).astype(_f32)

        @pl.when(phase == 0)
        def _(seg_rows=seg_rows):
            base_ref[...] += seg_rows

        @pl.when(phase == 1)
        def _(seg_rows=seg_rows, sub=sub):
            seg_i = seg_rows.astype(jnp.int32)
            lo = (16.0 * _prefix_over_experts((seg_i >> 4).astype(_f32))
                  + _prefix_over_experts((seg_i & 15).astype(_f32)))
            lane = lax.broadcasted_iota(jnp.int32, (N_EXPERTS, PIECE_SLOTS), 1)
            row = (lane * SEG_ALIGN).astype(_f32)
            inside = jnp.logical_and(row >= lo, row < lo + seg_rows)
            dst = jnp.sum(jnp.where(inside, base_ref[...] + (row - lo), 0.0), axis=0,
                          keepdims=True) * (1.0 / SEG_ALIGN)
            total = jnp.sum(seg_rows, axis=0, keepdims=True)
            piece_ref[sub] = jnp.where(lane[:1] == PIECE_SLOTS - 1, total, dst).astype(jnp.int32)
            base_ref[...] += seg_rows


def moe_rank(top_idx):
    k, t = top_idx.shape
    tm = TOKEN_TILE
    assert _tile_buffer_rows(tm) // SEG_ALIGN < PIECE_SLOTS
    per_step = 4 if t % (4 * tm) == 0 else 1
    return pl.pallas_call(
        _rank_kernel,
        grid=(2, t // (per_step * tm)),
        in_specs=[pl.BlockSpec((k, per_step * tm), lambda p, i: (0, i))],
        out_specs=[pl.BlockSpec((per_step, 1, PIECE_SLOTS), lambda p, i: (i * p, 0, 0)),
                   pl.BlockSpec((N_EXPERTS, 1), lambda p, i: (0, 0))],
        out_shape=[jax.ShapeDtypeStruct((t // tm, 1, PIECE_SLOTS), jnp.int32),
                   jax.ShapeDtypeStruct((N_EXPERTS, 1), jnp.int32)],
        scratch_shapes=[pltpu.VMEM((N_EXPERTS, 1), _f32)],
        compiler_params=_cparams("arbitrary", "arbitrary"),
        name="moe_rank",
    )(top_idx)


def _as_groups(a):
    return a.reshape(a.shape[0] // SEG_ALIGN, SEG_ALIGN, a.shape[1])


def _tile_rows(piece_ref):
    return piece_ref[0, 0, PIECE_SLOTS - 1]


def _for_each_piece(piece_ref, fn):
    def piece(g, c):
        fn(g, piece_ref[0, 0, g])
        return c

    def group(i, c):
        for u in range(PIECE_UNROLL):
            piece(i * PIECE_UNROLL + u, c)
        return c

    n = _tile_rows(piece_ref) // SEG_ALIGN
    lax.fori_loop(0, n // PIECE_UNROLL, group, 0)
    lax.fori_loop(n // PIECE_UNROLL * PIECE_UNROLL, n, piece, 0)


def _wait_pieces(n_pieces, make_copy):
    for bit in reversed(range(PIECE_SLOTS.bit_length())):
        @pl.when((n_pieces >> bit) & 1 == 1)
        def _(bit=bit):
            make_copy(1 << bit).wait()


def _prefix_over_lanes(row):
    n = row.shape[1]
    wide = jnp.broadcast_to(row, (SEG_ALIGN, n)).astype(_bf16)
    return _dot_nt(wide, _strict_lower(n))[:1, :]


def _member_by_token(idx_t):
    lane_e = lax.broadcasted_iota(jnp.int32, (idx_t.shape[0], N_EXPERTS), 1)
    member = jnp.zeros(lane_e.shape, _f32)
    for k in range(TOP_K):
        member = jnp.where(lane_e == idx_t[:, k:k + 1], 1.0, member)
    return member


def _member_by_expert(idx):
    sub_e = lax.broadcasted_iota(jnp.int32, (N_EXPERTS, idx.shape[1]), 0)
    member = jnp.zeros(sub_e.shape, _f32)
    for k in range(TOP_K):
        member = jnp.where(sub_e == idx[k:k + 1, :], 1.0, member)
    return member


def _segment_bounds(count, prefix):
    seg_rows = (count.astype(jnp.int32) + (SEG_ALIGN - 1)) & (-SEG_ALIGN)
    lo = 16.0 * prefix((seg_rows >> 4).astype(_f32)) + prefix((seg_rows & 15).astype(_f32))
    return lo, lo + seg_rows.astype(_f32)


def _rows_of_chunk(r0, seg_lo, seg_hi):
    row = (lax.broadcasted_iota(jnp.int32, (PERM_ROWS, N_EXPERTS), 0) + r0).astype(_f32)
    inside = jnp.logical_and(row >= seg_lo, row < seg_hi)
    rank = jnp.sum(jnp.where(inside, row - seg_lo, 0.0), axis=1, keepdims=True)
    return jnp.where(inside, 1.0, 0.0).astype(_bf16), rank


def _experts_of_chunk(r0, seg_lo, seg_hi):
    row = (lax.broadcasted_iota(jnp.int32, (N_EXPERTS, PERM_ROWS), 1) + r0).astype(_f32)
    inside = jnp.logical_and(row >= seg_lo, row < seg_hi)
    rank = jnp.sum(jnp.where(inside, row - seg_lo, 0.0), axis=0, keepdims=True)
    return jnp.where(inside, 1.0, 0.0).astype(_bf16), rank


def _dispatch_kernel(pad_from_ref, pad_n_ref, nb_ref, piece_ref, idx_ref, idx_t_ref, hb_ref,
                     xs_ref, xbuf_ref, zero_ref, sent_ref, sem, zsem):
    tm = hb_ref.shape[0]
    block_groups = zero_ref.shape[0]
    n_blocks = xs_ref.shape[0] // block_groups

    @pl.when(pl.program_id(0) == 0)
    def _():
        zero_ref[...] = jnp.zeros_like(zero_ref)

        def pad_expert(e, carry):
            off = pad_from_ref[e]

            def pad_piece(r, c):
                pltpu.make_async_copy(zero_ref.at[0], xs_ref.at[off + r], zsem).start()
                return c + 1

            return lax.fori_loop(0, pad_n_ref[e], pad_piece, carry)

        n_pieces = lax.fori_loop(0, N_EXPERTS, pad_expert, 0)

        def pad_tail(b, carry):
            pltpu.make_async_copy(zero_ref, xs_ref.at[pl.ds(b * block_groups, block_groups)],
                                  zsem).start()
            return carry

        lax.fori_loop(nb_ref[0], n_blocks, pad_tail, 0)

        def wait_piece(r, c):
            pltpu.make_async_copy(zero_ref.at[0], xs_ref.at[0], zsem).wait()
            return c

        def wait_tail(b, c):
            pltpu.make_async_copy(zero_ref, xs_ref.at[pl.ds(0, block_groups)], zsem).wait()
            return c

        lax.fori_loop(0, n_pieces, wait_piece, 0)
        lax.fori_loop(nb_ref[0], n_blocks, wait_tail, 0)

    step = pl.program_id(0)
    slot = step % 2
    total = _tile_rows(piece_ref)
    count = jnp.sum(_member_by_token(idx_t_ref[...]), axis=0, keepdims=True)
    seg_lo, seg_hi = _segment_bounds(count, _prefix_over_lanes)
    member = _member_by_expert(idx_ref[...])
    before = _dot_nt(member.astype(_bf16), _strict_lower(tm))
    place = jnp.where(member > 0.0, before + 1.0, -1.0).astype(_bf16)

    def select(c):
        r0 = c * PERM_ROWS
        expert_of_row, rank_of_row = _rows_of_chunk(r0, seg_lo, seg_hi)
        row = lax.broadcasted_iota(jnp.int32, rank_of_row.shape, 0) + r0
        want = jnp.where(row < total, rank_of_row + 1.0, -5.0)
        return jnp.where(_dot(expert_of_row, place) == want, 1.0, 0.0).astype(_bf16)

    buffer_rows = xbuf_ref.shape[1] * SEG_ALIGN
    sel = jnp.concatenate([select(c) for c in range(buffer_rows // PERM_ROWS)], axis=0)
    picked = _dot(sel, hb_ref[...])
    half = picked.shape[1] // 2
    xbuf_ref[slot] = _as_groups((pltpu.bitcast(picked[:, :half], jnp.uint32) >> 16)
                                | pltpu.bitcast(picked[:, half:], jnp.uint32))

    wait_sent = lambda n_pieces: _wait_pieces(n_pieces, lambda k: pltpu.make_async_copy(
        xbuf_ref.at[slot, pl.ds(0, k)], xs_ref.at[pl.ds(0, k)], sem))

    @pl.when(step > 0)
    def _():
        wait_sent(sent_ref[0])

    for s in range(2):
        @pl.when(slot == s)
        def _(s=s):
            def send(g, dst):
                pltpu.make_async_copy(xbuf_ref.at[s, g], xs_ref.at[dst], sem).start()

            _for_each_piece(piece_ref, send)

    sent_ref[0] = total // SEG_ALIGN

    @pl.when(step == pl.num_programs(0) - 1)
    def _():
        wait_sent(total // SEG_ALIGN)


def _tile_buffer_rows(tm):
    worst = tm * TOP_K + N_EXPERTS * (SEG_ALIGN - 1)
    return -(-worst // PERM_ROWS) * PERM_ROWS


def moe_dispatch(pieces, idx, idx_t, hb, pad_from, pad_n, n_used, n_blocks):
    t, d = hb.shape
    tm = TOKEN_TILE
    grid_spec = pltpu.PrefetchScalarGridSpec(
        num_scalar_prefetch=3,
        grid=(t // tm,),
        in_specs=[
            pl.BlockSpec((1, 1, PIECE_SLOTS), lambda i, *_: (i, 0, 0), memory_space=pltpu.SMEM),
            pl.BlockSpec((TOP_K, tm), lambda i, *_: (0, i)),
            pl.BlockSpec((tm, TOP_K), lambda i, *_: (i, 0)),
            pl.BlockSpec((tm, d), lambda i, *_: (i, 0)),
        ],
        out_specs=pl.BlockSpec(memory_space=pl.ANY),
        scratch_shapes=[pltpu.VMEM((2, _tile_buffer_rows(tm) // SEG_ALIGN, SEG_ALIGN, d // 2),
                                   jnp.uint32),
                        pltpu.VMEM((ROW_BLOCK // SEG_ALIGN, SEG_ALIGN, d // 2), jnp.uint32),
                        pltpu.SMEM((1,), jnp.int32),
                        pltpu.SemaphoreType.DMA(()), pltpu.SemaphoreType.DMA(())],
    )
    xs = pl.pallas_call(
        _dispatch_kernel,
        grid_spec=grid_spec,
        out_shape=jax.ShapeDtypeStruct((n_blocks * ROW_BLOCK // SEG_ALIGN, SEG_ALIGN, d // 2),
                                       jnp.uint32),
        compiler_params=_cparams("arbitrary"),
        name="moe_dispatch",
    )(pad_from // SEG_ALIGN, pad_n // SEG_ALIGN, n_used, pieces, idx, idx_t, hb)
    return xs.reshape(n_blocks * ROW_BLOCK, d // 2)


def _expert_kernel(be_ref, nb_ref, xs_ref, w1_ref, w3_ref, w2_ref, y_ref, w1b, w3b, w2b):
    i = pl.program_id(0)

    @pl.when(jnp.logical_or(i == 0, be_ref[i] != be_ref[jnp.maximum(i - 1, 0)]))
    def _():
        w1b[...] = w1_ref[0, 0].astype(_bf16)
        w3b[...] = w3_ref[0, 0].astype(_bf16)
        w2b[...] = w2_ref[0, 0].astype(_bf16)

    @pl.when(i < nb_ref[0])
    def _():
        chains = [pl.ds(c * EXPERT_ROWS, EXPERT_ROWS) for c in range(ROW_BLOCK // EXPERT_ROWS)]
        xs = [jnp.concatenate(_unpack_rows(xs_ref[rows, :]), axis=1) for rows in chains]
        acts = [(_silu(_dot(x, w1b[...])) * _dot(x, w3b[...])).astype(_bf16) for x in xs]
        for rows, act in zip(chains, acts):
            y_ref[rows, :] = _pack_rows(_dot(act, w2b[...]))

    @pl.when(i >= nb_ref[0])
    def _():
        y_ref[...] = jnp.zeros_like(y_ref)


def moe_experts(xs, blk_e, n_used, w1, w3, w2, layer):
    p, half = xs.shape
    d = 2 * half
    ff = w1.shape[3]
    nb = p // ROW_BLOCK
    row = lambda i, be, nu: (jnp.minimum(i, nu[0] - 1), 0)
    wsel = lambda i, be, nu: (layer, be[i], 0, 0)
    grid_spec = pltpu.PrefetchScalarGridSpec(
        num_scalar_prefetch=2,
        grid=(nb,),
        in_specs=[
            pl.BlockSpec((ROW_BLOCK, half), row),
            pl.BlockSpec((1, 1, d, ff), wsel),
            pl.BlockSpec((1, 1, d, ff), wsel),
            pl.BlockSpec((1, 1, ff, d), wsel),
        ],
        out_specs=pl.BlockSpec((ROW_BLOCK, half), lambda i, be, nu: (i, 0)),
        scratch_shapes=[pltpu.VMEM((d, ff), _bf16), pltpu.VMEM((d, ff), _bf16),
                        pltpu.VMEM((ff, d), _bf16)],
    )
    return pl.pallas_call(
        _expert_kernel,
        grid_spec=grid_spec,
        out_shape=jax.ShapeDtypeStruct((p, half), jnp.uint32),
        compiler_params=_cparams("arbitrary"),
        name="moe_experts",
    )(blk_e, n_used, xs, w1, w3, w2)


def _combine_kernel(piece_ref, next_piece_ref, ys_ref, idx_ref, idx_t_ref, w_ref, shared_ref, x_ref,
                    gate_ref, fn_ref, o_ref, ybuf_ref, sem, *, final_norm):
    tm = x_ref.shape[1]
    buffer_rows = ybuf_ref.shape[1] * SEG_ALIGN
    step = pl.program_id(0) * pl.num_programs(1) + pl.program_id(1)
    n_steps = pl.num_programs(0) * pl.num_programs(1)
    slot = step % 2

    def fetch(table_ref, s):
        def one(g, src):
            pltpu.make_async_copy(ys_ref.at[src], ybuf_ref.at[s, g], sem.at[s]).start()

        _for_each_piece(table_ref, one)

    @pl.when(step == 0)
    def _():
        fetch(piece_ref, 0)

    for s in range(2):
        @pl.when(jnp.logical_and(step + 1 < n_steps, slot == 1 - s))
        def _(s=s):
            fetch(next_piece_ref, s)

    total = _tile_rows(piece_ref)
    _wait_pieces(total // SEG_ALIGN, lambda k: pltpu.make_async_copy(
        ys_ref.at[pl.ds(0, k)], ybuf_ref.at[slot, pl.ds(0, k)], sem.at[slot]))

    idx_t = idx_t_ref[...]
    member = _member_by_token(idx_t)
    before_b = _dot(_strict_lower(tm), member.astype(_bf16)).astype(_bf16)
    lane_e = lax.broadcasted_iota(jnp.int32, member.shape, 1)
    w_of_expert = jnp.zeros(member.shape, _f32)
    for k in range(TOP_K):
        w_of_expert = jnp.where(lane_e == idx_t[:, k:k + 1], w_ref[:, k:k + 1], w_of_expert)
    w_of_expert = w_of_expert.astype(_bf16)
    count = jnp.sum(_member_by_expert(idx_ref[...]), axis=1, keepdims=True)
    seg_lo, seg_hi = _segment_bounds(count, _prefix_over_experts)

    def weights_of_chunk(c):
        expert_of_row, rank_of_row = _experts_of_chunk(c * PERM_ROWS, seg_lo, seg_hi)
        hit = _dot(before_b, expert_of_row) == rank_of_row
        return jnp.where(hit, _dot(w_of_expert, expert_of_row), 0.0).astype(_bf16)

    wm = jnp.concatenate([weights_of_chunk(c) for c in range(buffer_rows // PERM_ROWS)], axis=1)
    packed = ybuf_ref[slot].reshape(buffer_rows, ybuf_ref.shape[3])
    row = lax.broadcasted_iota(jnp.int32, packed.shape, 0)
    packed = jnp.where(row < total, packed, jnp.uint32(0))
    y_lo, y_hi = _unpack_rows(packed)
    routed = jnp.concatenate([_dot(wm, y_lo), _dot(wm, y_hi)], axis=1)

    xn = x_ref[0] + gate_ref[0] * (shared_ref[...] + routed)
    if final_norm:
        xn = xn * lax.rsqrt(jnp.mean(xn * xn, axis=-1, keepdims=True) + EPS) * fn_ref[...]
    o_ref[0] = xn


def moe_combine(pieces, idx, idx_t, ys, top_w, shared, x, gate, fn_g, final_norm):
    b, s, d = x.shape
    tm = TOKEN_TILE
    n_s = s // tm
    tile = lambda i, m: (i * n_s + m, 0)
    last_tile = b * n_s - 1
    return pl.pallas_call(
        functools.partial(_combine_kernel, final_norm=final_norm),
        grid=(b, n_s),
        in_specs=[
            pl.BlockSpec((1, 1, PIECE_SLOTS), lambda i, m: (i * n_s + m, 0, 0),
                         memory_space=pltpu.SMEM),
            pl.BlockSpec((1, 1, PIECE_SLOTS),
                         lambda i, m: (jnp.minimum(i * n_s + m + 1, last_tile), 0, 0),
                         memory_space=pltpu.SMEM),
            pl.BlockSpec(memory_space=pl.ANY),
            pl.BlockSpec((TOP_K, tm), lambda i, m: (0, i * n_s + m)),
            pl.BlockSpec((tm, TOP_K), tile),
            pl.BlockSpec((tm, TOP_K), tile),
            pl.BlockSpec((tm, d), tile),
            pl.BlockSpec((1, tm, d), lambda i, m: (i, m, 0)),
            pl.BlockSpec((1, 1, d), lambda i, m: (i, 0, 0)),
            pl.BlockSpec((1, d), lambda i, m: (0, 0)),
        ],
        out_specs=pl.BlockSpec((1, tm, d), lambda i, m: (i, m, 0)),
        out_shape=jax.ShapeDtypeStruct((b, s, d), _f32),
        scratch_shapes=[pltpu.VMEM((2, _tile_buffer_rows(tm) // SEG_ALIGN, SEG_ALIGN, d // 2),
                                   jnp.uint32),
                        pltpu.SemaphoreType.DMA((2,))],
        compiler_params=_cparams("arbitrary", "arbitrary"),
        name="moe_combine",
    )(pieces, pieces, _as_groups(ys), idx, idx_t, top_w, shared, x, gate, fn_g.reshape(1, d))


def moe_layer(y, w_out, x, gate_a, g, sc, sh, gate, router_w, router_b, w1, w3, w2, layer,
              s_w1, s_w3, s_w2, fn_g, final_norm):
    b, s, d = x.shape
    t = b * s
    x, hb, top_idx, top_w, shared = moe_router(y, w_out, x, gate_a, g, sc, sh, router_w, router_b,
                                               s_w1, s_w3, s_w2)
    pieces, rows = moe_rank(top_idx)
    idx_t = top_idx.T
    n_tiles = pieces.shape[0]

    rows = rows.reshape(N_EXPERTS)
    nblk = (rows + ROW_BLOCK - 1) // ROW_BLOCK
    blk_ends = jnp.cumsum(nblk)
    max_rows = t * TOP_K + n_tiles * N_EXPERTS * (SEG_ALIGN - 1)
    n_blocks = -(-max_rows // ROW_BLOCK) + N_EXPERTS
    blk = jnp.arange(n_blocks, dtype=jnp.int32)
    blk_e = jnp.minimum(jnp.sum(blk[:, None] >= blk_ends[None, :], axis=1),
                        N_EXPERTS - 1).astype(jnp.int32)
    n_used = blk_ends[-1:].astype(jnp.int32)
    pad_from = ((blk_ends - nblk) * ROW_BLOCK + rows).astype(jnp.int32)
    pad_n = (nblk * ROW_BLOCK - rows).astype(jnp.int32)

    xs = moe_dispatch(pieces, top_idx, idx_t, hb, pad_from, pad_n, n_used, n_blocks)
    ys = moe_experts(xs, blk_e, n_used, w1, w3, w2, layer)
    return moe_combine(pieces, top_idx, idx_t, ys, top_w.T, shared, x, gate, fn_g, final_norm)


def kernel(x, c, norm_mix, norm_ffn, ada_w, ada_b, da_w_in, da_w_out, da_lq1, da_lk1, da_lq2,
           da_lk2, da_subln, ret_w_in, ret_w_out, ret_gn, router_w, router_b, exp_w1, exp_w3,
           exp_w2, sh_w1, sh_w3, sh_w2, final_norm):
    depth, d = norm_mix.shape
    b = x.shape[0]
    mod = ada_modulation(c, ada_w, ada_b).reshape(depth, b, 6, 1, d)
    for i in range(depth):
        sh_a, sc_a, g_a, sh_f, sc_f, g_f = (mod[i, :, j] for j in range(6))
        j = i // 2
        if i % 2 == 0:
            proj = norm_proj(x, norm_mix[i], sc_a, sh_a, da_w_in[j].astype(_bf16))
            y = diff_attention_core(proj, da_lq1[j], da_lk1[j], da_lq2[j], da_lk2[j],
                                    da_subln[j], i)
            w_out = da_w_out[j].astype(_bf16)
        else:
            proj = norm_proj(x, norm_mix[i], sc_a, sh_a, ret_w_in[j].astype(_bf16))
            y = retention_core(proj, ret_gn[j])
            w_out = ret_w_out[j].astype(_bf16)
        x = moe_layer(y, w_out, x, g_a, norm_ffn[i], sc_f, sh_f, g_f, router_w[i], router_b[i],
                      exp_w1, exp_w3, exp_w2, i,
                      sh_w1[i], sh_w3[i], sh_w2[i], final_norm, i == depth - 1)
    return x
```

```python
import functools
import math

import jax
import jax.numpy as jnp
from jax import lax
from jax.experimental import pallas as pl
from jax.experimental.pallas import tpu as pltpu

EPS = 1e-6
DA_HEADS = 8
DA_HEAD_DIM = 64
DA_V_DIM = 2 * DA_HEAD_DIM
RET_HEADS = 4
RET_DK = 256
RET_DV = 512
RET_CHUNK = 128
N_EXPERTS = 64
TOP_K = 8
N_GROUPS = 8
GROUP_SIZE = N_EXPERTS // N_GROUPS
TOPK_GROUPS = 4
ROUTED_SCALE = 2.5
ROW_BLOCK_LOG2 = 10
ROW_BLOCK = 1 << ROW_BLOCK_LOG2
EXPERT_ROWS = 256
TOKEN_TILE = 256
SEG_ALIGN = 8
PERM_ROWS = 512
PIECE_SLOTS = 384
PIECE_UNROLL = 8

VMEM_LIMIT = 56 * 1024 * 1024
NEG_BIG = -1e30

_f32 = jnp.float32
_bf16 = jnp.bfloat16


def _cparams(*sem):
    return pltpu.CompilerParams(dimension_semantics=sem, vmem_limit_bytes=VMEM_LIMIT)


def _silu(v):
    return v * (1.0 / (1.0 + jnp.exp(-v)))


def _dot(a, b):
    return jnp.dot(a, b, preferred_element_type=_f32)


def _dot_nt(a, b):
    return lax.dot_general(a, b, (((1,), (1,)), ((), ())), preferred_element_type=_f32)


def _dot_tn(a, b):
    return lax.dot_general(a, b, (((0,), (0,)), ((), ())), preferred_element_type=_f32)


def _ada_kernel(c_ref, w_ref, b_ref, o_ref):
    a = _silu(c_ref[...]).astype(_bf16)
    o_ref[0] = _dot(a, w_ref[0].astype(_bf16)) + b_ref[0]


def ada_modulation(c, ada_w, ada_b):
    depth, d, n = ada_w.shape
    b = c.shape[0]
    tn = 1024
    return pl.pallas_call(
        _ada_kernel,
        grid=(depth, n // tn),
        in_specs=[
            pl.BlockSpec((b, d), lambda l, j: (0, 0)),
            pl.BlockSpec((1, d, tn), lambda l, j: (l, 0, j)),
            pl.BlockSpec((1, 1, tn), lambda l, j: (l, 0, j)),
        ],
        out_specs=pl.BlockSpec((1, b, tn), lambda l, j: (l, 0, j)),
        out_shape=jax.ShapeDtypeStruct((depth, b, n), _f32),
        compiler_params=_cparams("parallel", "parallel"),
        name="ada_modulation",
    )(c, ada_w, ada_b.reshape(depth, 1, n))


def _norm_mod(x, g, sc, sh):
    y = x * lax.rsqrt(jnp.mean(x * x, axis=-1, keepdims=True) + EPS)
    return (y * g) * (1.0 + sc) + sh


def _norm_proj_kernel(x_ref, g_ref, sc_ref, sh_ref, w_ref, o_ref, h_ref):
    @pl.when(pl.program_id(2) == 0)
    def _():
        h_ref[...] = _norm_mod(x_ref[0], g_ref[...], sc_ref[0], sh_ref[0]).astype(_bf16)

    o_ref[0] = _dot(h_ref[...], w_ref[...]).astype(o_ref.dtype)


def norm_proj(x, g, sc, sh, w):
    b, s, d = x.shape
    n = w.shape[1]
    tm = min(2048, s)
    tn = 1024
    return pl.pallas_call(
        _norm_proj_kernel,
        grid=(b, s // tm, n // tn),
        in_specs=[
            pl.BlockSpec((1, tm, d), lambda i, m, j: (i, m, 0)),
            pl.BlockSpec((1, d), lambda i, m, j: (0, 0)),
            pl.BlockSpec((1, 1, d), lambda i, m, j: (i, 0, 0)),
            pl.BlockSpec((1, 1, d), lambda i, m, j: (i, 0, 0)),
            pl.BlockSpec((d, tn), lambda i, m, j: (0, j)),
        ],
        out_specs=pl.BlockSpec((1, tm, tn), lambda i, m, j: (i, m, j)),
        out_shape=jax.ShapeDtypeStruct((b, s, n), _bf16),
        scratch_shapes=[pltpu.VMEM((tm, d), _bf16)],
        compiler_params=_cparams("parallel", "parallel", "arbitrary"),
        name="norm_proj",
    )(x, g.reshape(1, d), sc, sh, w)


def _diff_attn_kernel(q_ref, k_ref, v_ref, lq1_ref, lk1_ref, lq2_ref, lk2_ref, sg_ref, o_ref,
                      bias_ref, *, tq, lam_init):
    h = pl.program_id(0)
    dh, dv = DA_HEAD_DIM, DA_V_DIM
    n_q = k_ref.shape[1] // tq

    @pl.when(pl.program_id(1) == 0)
    def _():
        hp1 = jnp.full((1, 1), h + 1, jnp.int32).astype(_f32)
        slope = jnp.exp2(hp1 * (-8.0 / DA_HEADS))
        dist = ((n_q - 1) * tq + lax.broadcasted_iota(jnp.int32, bias_ref.shape, 0)
                - lax.broadcasted_iota(jnp.int32, bias_ref.shape, 1)).astype(_f32)
        bias_ref[...] = -slope * dist

    lam = (jnp.exp(jnp.sum(lq1_ref[...] * lk1_ref[...], axis=-1, keepdims=True))
           - jnp.exp(jnp.sum(lq2_ref[...] * lk2_ref[...], axis=-1, keepdims=True)) + lam_init)
    causal = (lax.broadcasted_iota(jnp.int32, (tq, tq), 0)
              >= lax.broadcasted_iota(jnp.int32, (tq, tq), 1))
    lane = lax.broadcasted_iota(jnp.int32, (tq, 2 * dh), 1)

    def attend(c):
        q = q_ref[0, c * tq:(c + 1) * tq, :] * (dh ** -0.5)
        zero = jnp.zeros_like(q)
        q_maps = (jnp.where(lane < dh, q, zero), jnp.where(lane >= dh, q, zero))
        first = (n_q - 1 - c) * tq
        blocks = [slice(j * tq, (j + 1) * tq) for j in range(c + 1)]
        tiles = [[_dot_nt(q_m, k_ref[0, blk, :])
                  + bias_ref[:, first + blk.start:first + blk.stop] for q_m in q_maps]
                 for blk in blocks]
        tiles[c] = [jnp.where(causal, s, NEG_BIG) for s in tiles[c]]
        outs = []
        for mi in range(2):
            parts = [row[mi] for row in tiles]
            m = functools.reduce(jnp.maximum, [jnp.max(x, axis=-1, keepdims=True) for x in parts])
            ps = [jnp.exp(x - m) for x in parts]
            l = functools.reduce(jnp.add, [jnp.sum(x, axis=-1, keepdims=True) for x in ps])
            o = functools.reduce(jnp.add, [_dot(x.astype(_bf16), v_ref[0, blk, :])
                                           for x, blk in zip(ps, blocks)])
            outs.append(o / l)
        o = outs[0] - lam * outs[1]
        o = o * lax.rsqrt(jnp.mean(o * o, axis=-1, keepdims=True) + EPS)
        o = (o * sg_ref[...]) * (1.0 - lam_init)
        o_ref[0, c * tq:(c + 1) * tq, :] = o.astype(o_ref.dtype)

    for c in range(n_q):
        attend(c)


def diff_attention_core(proj, lq1, lk1, lq2, lk2, subln_g, layer_idx):
    b, s, _ = proj.shape
    nh, dh, dv = DA_HEADS, DA_HEAD_DIM, DA_V_DIM
    tq = min(512, s)
    lam_init = 0.8 - 0.6 * math.exp(-0.3 * layer_idx)
    vec = lambda n: pl.BlockSpec((1, n), lambda h, i: (0, 0))
    return pl.pallas_call(
        functools.partial(_diff_attn_kernel, tq=tq, lam_init=lam_init),
        grid=(nh, b),
        in_specs=[
            pl.BlockSpec((1, s, 2 * dh), lambda h, i: (i, 0, h)),
            pl.BlockSpec((1, s, 2 * dh), lambda h, i: (i, 0, nh + h)),
            pl.BlockSpec((1, s, dv), lambda h, i: (i, 0, 2 * nh + h)),
            vec(dh), vec(dh), vec(dh), vec(dh), vec(dv),
        ],
        out_specs=pl.BlockSpec((1, s, dv), lambda h, i: (i, 0, h)),
        out_shape=jax.ShapeDtypeStruct((b, s, nh * dv), _bf16),
        scratch_shapes=[pltpu.VMEM((tq, s), _f32)],
        compiler_params=_cparams("parallel", "arbitrary"),
        name="diff_attention",
    )(proj, proj, proj, lq1.reshape(1, dh), lk1.reshape(1, dh), lq2.reshape(1, dh),
      lk2.reshape(1, dh), subln_g.reshape(1, dv))


def _retention_kernel(q_ref, k_ref, v_ref, g_ref, gn_ref, o_ref, r_ref, *, chunk):
    h = pl.program_id(1)
    c = pl.program_id(2)
    dk = RET_DK

    @pl.when(c == 0)
    def _():
        r_ref[...] = jnp.zeros_like(r_ref)

    hf = jnp.full((1, 1), h, jnp.int32).astype(_f32)
    log_g = jnp.log(1.0 - jnp.exp2(-5.0 - hf))
    ri = lax.broadcasted_iota(jnp.int32, (chunk, chunk), 0)
    ci = lax.broadcasted_iota(jnp.int32, (chunk, chunk), 1)
    diff = (ri - ci).astype(_f32)
    decay = jnp.where(diff >= 0, jnp.exp(log_g * jnp.maximum(diff, 0.0)), 0.0)
    idx = lax.broadcasted_iota(jnp.int32, (chunk, 1), 0).astype(_f32)
    zeta = jnp.exp(log_g * (chunk - 1 - idx))
    xi = jnp.exp(log_g * (idx + 1.0))
    g_c = jnp.exp(log_g * chunk)

    decay = decay * (dk ** -0.5)
    zeta = zeta * (dk ** -0.5)
    chunks = [pl.ds(sub * chunk, chunk) for sub in range(q_ref.shape[1] // chunk)]
    intra, updates = [], []
    for rows in chunks:
        q, k, v = q_ref[0, rows, :], k_ref[0, rows, :], v_ref[0, rows, :]
        intra.append(_dot((_dot_nt(q, k) * decay).astype(_bf16), v))
        updates.append(_dot_tn((k.astype(_f32) * zeta).astype(_bf16), v))
    r = r_ref[...]
    for rows, o, upd in zip(chunks, intra, updates):
        o = o + _dot(q_ref[0, rows, :], r.astype(_bf16)) * xi
        r = g_c * r + upd
        o = o * lax.rsqrt(jnp.mean(o * o, axis=-1, keepdims=True) + EPS) * gn_ref[...]
        o_ref[0, rows, :] = (_silu(g_ref[0, rows, :].astype(_f32)) * o).astype(o_ref.dtype)
    r_ref[...] = r


def retention_core(proj, gn_g):
    b, s, _ = proj.shape
    nh, dk, dv = RET_HEADS, RET_DK, RET_DV
    chunk = RET_CHUNK
    rows = min(16 * chunk, s)
    return pl.pallas_call(
        functools.partial(_retention_kernel, chunk=chunk),
        grid=(b, nh, s // rows),
        in_specs=[
            pl.BlockSpec((1, rows, dk), lambda i, h, c: (i, c, h)),
            pl.BlockSpec((1, rows, dk), lambda i, h, c: (i, c, nh + h)),
            pl.BlockSpec((1, rows, dv), lambda i, h, c: (i, c, (2 * nh * dk) // dv + h)),
            pl.BlockSpec((1, rows, dv), lambda i, h, c: (i, c, (2 * nh * dk) // dv + nh + h)),
            pl.BlockSpec((1, dv), lambda i, h, c: (0, h)),
        ],
        out_specs=pl.BlockSpec((1, rows, dv), lambda i, h, c: (i, c, h)),
        out_shape=jax.ShapeDtypeStruct((b, s, nh * dv), _bf16),
        scratch_shapes=[pltpu.VMEM((dk, dv), _f32)],
        compiler_params=_cparams("parallel", "parallel", "arbitrary"),
        name="retention",
    )(proj, proj, proj, proj, gn_g.reshape(1, nh * dv))


def _pack_rows(h):
    n = h.shape[1] // 2
    lo = pltpu.bitcast(h[:, :n].astype(_bf16).astype(_f32), jnp.uint32)
    hi = pltpu.bitcast(h[:, n:].astype(_bf16).astype(_f32), jnp.uint32)
    return (lo >> 16) | (hi & jnp.uint32(0xFFFF0000))


def _unpack_rows(p):
    lo = pltpu.bitcast(p << 16, _f32).astype(_bf16)
    hi = pltpu.bitcast(p & jnp.uint32(0xFFFF0000), _f32).astype(_bf16)
    return lo, hi


def _first_argmax(v, iota, axis, big):
    m = jnp.max(v, axis=axis, keepdims=True)
    i = jnp.min(jnp.where(v == m, iota, big), axis=axis, keepdims=True)
    return m, i


def _router_kernel(y_ref, wo_ref, x_ref, ga_ref, g_ref, sc_ref, sh_ref, rwt_ref, rb_ref, s1_ref,
                   s3_ref, s2_ref, xo_ref, hb_ref, idx_ref, wgt_ref, shared_ref):
    tm = x_ref.shape[1]
    x = x_ref[0] + ga_ref[0] * _dot(y_ref[0], wo_ref[...])
    xo_ref[0] = x
    h = _norm_mod(x, g_ref[...], sc_ref[0], sh_ref[0])
    hb = h.astype(_bf16)
    hb_ref[...] = hb
    act = (_silu(_dot(hb, s1_ref[...])) * _dot(hb, s3_ref[...])).astype(_bf16)
    shared_ref[...] = _dot(act, s2_ref[...])

    scores = 1.0 / (1.0 + jnp.exp(-_dot_nt(rwt_ref[...], h)))
    choice = scores + rb_ref[...]
    c3 = choice.reshape(N_GROUPS, GROUP_SIZE, tm)
    mem = lax.broadcasted_iota(jnp.int32, c3.shape, 1)
    m1, i1 = _first_argmax(c3, mem, 1, GROUP_SIZE)
    m2 = jnp.max(jnp.where(mem == i1, -jnp.inf, c3), axis=1, keepdims=True)
    gs = m1 + m2
    gi = lax.broadcasted_iota(jnp.int32, gs.shape, 0)
    sel = jnp.zeros(gs.shape, _f32)
    for _ in range(TOPK_GROUPS):
        _, ig = _first_argmax(gs, gi, 0, N_GROUPS)
        hit = gi == ig
        sel = jnp.where(hit, 1.0, sel)
        gs = jnp.where(hit, -jnp.inf, gs)
    cm = jnp.where(sel > 0.0, c3, -jnp.inf).reshape(N_EXPERTS, tm)
    ei = lax.broadcasted_iota(jnp.int32, cm.shape, 0)
    idxs, wgts = [], []
    for _ in range(TOP_K):
        _, ie = _first_argmax(cm, ei, 0, N_EXPERTS)
        hit = ei == ie
        idxs.append(ie)
        wgts.append(jnp.sum(jnp.where(hit, scores, 0.0), axis=0, keepdims=True))
        cm = jnp.where(hit, -jnp.inf, cm)
    w = jnp.concatenate(wgts, axis=0)
    idx_ref[...] = jnp.concatenate(idxs, axis=0)
    wgt_ref[...] = w / jnp.sum(w, axis=0, keepdims=True) * ROUTED_SCALE


def moe_router(y, w_out, x, gate_a, g, sc, sh, router_w, router_b, s_w1, s_w3, s_w2):
    b, s, d = x.shape
    t = b * s
    kdim = y.shape[2]
    tm = min(512, s)
    n_s = s // tm
    sf = s_w1.shape[1]
    const = lambda shape: pl.BlockSpec(shape, lambda i, m: (0,) * len(shape))
    return pl.pallas_call(
        _router_kernel,
        grid=(b, n_s),
        in_specs=[
            pl.BlockSpec((1, tm, kdim), lambda i, m: (i, m, 0)),
            const((kdim, d)),
            pl.BlockSpec((1, tm, d), lambda i, m: (i, m, 0)),
            pl.BlockSpec((1, 1, d), lambda i, m: (i, 0, 0)),
            const((1, d)),
            pl.BlockSpec((1, 1, d), lambda i, m: (i, 0, 0)),
            pl.BlockSpec((1, 1, d), lambda i, m: (i, 0, 0)),
            const((N_EXPERTS, d)), const((N_EXPERTS, 1)),
            const((d, sf)), const((d, sf)), const((sf, d)),
        ],
        out_specs=[
            pl.BlockSpec((1, tm, d), lambda i, m: (i, m, 0)),
            pl.BlockSpec((tm, d), lambda i, m: (i * n_s + m, 0)),
            pl.BlockSpec((TOP_K, tm), lambda i, m: (0, i * n_s + m)),
            pl.BlockSpec((TOP_K, tm), lambda i, m: (0, i * n_s + m)),
            pl.BlockSpec((tm, d), lambda i, m: (i * n_s + m, 0)),
        ],
        out_shape=[
            jax.ShapeDtypeStruct((b, s, d), _f32),
            jax.ShapeDtypeStruct((t, d), _bf16),
            jax.ShapeDtypeStruct((TOP_K, t), jnp.int32),
            jax.ShapeDtypeStruct((TOP_K, t), _f32),
            jax.ShapeDtypeStruct((t, d), _f32),
        ],
        compiler_params=_cparams("parallel", "parallel"),
        name="moe_router",
    )(y, w_out, x, gate_a, g.reshape(1, d), sc, sh, router_w.T, router_b.reshape(N_EXPERTS, 1),
      s_w1.astype(_bf16), s_w3.astype(_bf16), s_w2.astype(_bf16))


def _strict_lower(n):
    return jnp.where(lax.broadcasted_iota(jnp.int32, (n, n), 1)
                     < lax.broadcasted_iota(jnp.int32, (n, n), 0), 1.0, 0.0).astype(_bf16)


def _prefix_over_experts(col):
    wide = jnp.broadcast_to(col, (N_EXPERTS, 128)).astype(_bf16)
    return _dot(_strict_lower(N_EXPERTS), wide)[:, :1]


def _rank_kernel(idx_ref, piece_ref, tot_ref, base_ref):
    phase = pl.program_id(0)
    i = pl.program_id(1)
    tm = TOKEN_TILE

    @pl.when(jnp.logical_and(phase == 0, i == 0))
    def _():
        base_ref[...] = jnp.zeros_like(base_ref)

    @pl.when(jnp.logical_and(phase == 1, i == 0))
    def _():
        tot = base_ref[...].astype(jnp.int32)
        tot_ref[...] = tot
        nblk = (tot + (ROW_BLOCK - 1)) >> ROW_BLOCK_LOG2
        start_blk = (32.0 * _prefix_over_experts((nblk >> 5).astype(_f32))
                     + _prefix_over_experts((nblk & 31).astype(_f32)))
        base_ref[...] = start_blk * float(ROW_BLOCK)

    ei = lax.broadcasted_iota(jnp.int32, (N_EXPERTS, tm), 0)
    for sub in range(idx_ref.shape[1] // tm):
        idx = idx_ref[:, sub * tm:(sub + 1) * tm]
        memf = jnp.zeros((N_EXPERTS, tm), _f32)
        for k in range(TOP_K):
            memf = jnp.where(ei == idx[k:k + 1, :], 1.0, memf)
        tile_cnt = jnp.sum(memf, axis=1, keepdims=True).astype(jnp.int32)
        seg_rows = ((tile_cnt + (SEG_ALIGN - 1)) & (-SEG_ALIGN)).astype(_f32)

        @pl.when(phase == 0)
        def _(seg_rows=seg_rows):
            base_ref[...] += seg_rows

        @pl.when(phase == 1)
        def _(seg_rows=seg_rows, sub=sub):
            seg_i = seg_rows.astype(jnp.int32)
            lo = (16.0 * _prefix_over_experts((seg_i >> 4).astype(_f32))
                  + _prefix_over_experts((seg_i & 15).astype(_f32)))
            lane = lax.broadcasted_iota(jnp.int32, (N_EXPERTS, PIECE_SLOTS), 1)
            row = (lane * SEG_ALIGN).astype(_f32)
            inside = jnp.logical_and(row >= lo, row < lo + seg_rows)
            dst = jnp.sum(jnp.where(inside, base_ref[...] + (row - lo), 0.0), axis=0,
                          keepdims=True) * (1.0 / SEG_ALIGN)
            total = jnp.sum(seg_rows, axis=0, keepdims=True)
            piece_ref[sub] = jnp.where(lane[:1] == PIECE_SLOTS - 1, total, dst).astype(jnp.int32)
            base_ref[...] += seg_rows


def moe_rank(top_idx):
    k, t = top_idx.shape
    tm = TOKEN_TILE
    assert _tile_buffer_rows(tm) // SEG_ALIGN < PIECE_SLOTS
    per_step = 8 if t % (8 * tm) == 0 else 1
    return pl.pallas_call(
        _rank_kernel,
        grid=(2, t // (per_step * tm)),
        in_specs=[pl.BlockSpec((k, per_step * tm), lambda p, i: (0, i))],
        out_specs=[pl.BlockSpec((per_step, 1, PIECE_SLOTS), lambda p, i: (i * p, 0, 0)),
                   pl.BlockSpec((N_EXPERTS, 1), lambda p, i: (0, 0))],
        out_shape=[jax.ShapeDtypeStruct((t // tm, 1, PIECE_SLOTS), jnp.int32),
                   jax.ShapeDtypeStruct((N_EXPERTS, 1), jnp.int32)],
        scratch_shapes=[pltpu.VMEM((N_EXPERTS, 1), _f32)],
        compiler_params=_cparams("arbitrary", "arbitrary"),
        name="moe_rank",
    )(top_idx)


def _as_groups(a):
    return a.reshape(a.shape[0] // SEG_ALIGN, SEG_ALIGN, a.shape[1])


def _tile_rows(piece_ref):
    return piece_ref[0, 0, PIECE_SLOTS - 1]


def _for_each_piece(piece_ref, fn):
    def piece(g, c):
        fn(g, piece_ref[0, 0, g])
        return c

    def group(i, c):
        for u in range(PIECE_UNROLL):
            piece(i * PIECE_UNROLL + u, c)
        return c

    n = _tile_rows(piece_ref) // SEG_ALIGN
    lax.fori_loop(0, n // PIECE_UNROLL, group, 0)
    lax.fori_loop(n // PIECE_UNROLL * PIECE_UNROLL, n, piece, 0)


def _wait_pieces(n_pieces, make_copy):
    for bit in reversed(range(PIECE_SLOTS.bit_length())):
        @pl.when((n_pieces >> bit) & 1 == 1)
        def _(bit=bit):
            make_copy(1 << bit).wait()


def _prefix_over_lanes(row):
    n = row.shape[1]
    wide = jnp.broadcast_to(row, (SEG_ALIGN, n)).astype(_bf16)
    return _dot_nt(wide, _strict_lower(n))[:1, :]


def _member_by_token(idx_t):
    lane_e = lax.broadcasted_iota(jnp.int32, (idx_t.shape[0], N_EXPERTS), 1)
    member = jnp.zeros(lane_e.shape, _f32)
    for k in range(TOP_K):
        member = jnp.where(lane_e == idx_t[:, k:k + 1], 1.0, member)
    return member


def _member_by_expert(idx):
    sub_e = lax.broadcasted_iota(jnp.int32, (N_EXPERTS, idx.shape[1]), 0)
    member = jnp.zeros(sub_e.shape, _f32)
    for k in range(TOP_K):
        member = jnp.where(sub_e == idx[k:k + 1, :], 1.0, member)
    return member


def _segment_bounds(count, prefix):
    seg_rows = (count.astype(jnp.int32) + (SEG_ALIGN - 1)) & (-SEG_ALIGN)
    lo = 16.0 * prefix((seg_rows >> 4).astype(_f32)) + prefix((seg_rows & 15).astype(_f32))
    return lo, lo + seg_rows.astype(_f32)


def _rows_of_chunk(r0, seg_lo, seg_hi):
    row = (lax.broadcasted_iota(jnp.int32, (PERM_ROWS, N_EXPERTS), 0) + r0).astype(_f32)
    inside = jnp.logical_and(row >= seg_lo, row < seg_hi)
    rank = jnp.sum(jnp.where(inside, row - seg_lo, 0.0), axis=1, keepdims=True)
    return jnp.where(inside, 1.0, 0.0).astype(_bf16), rank


def _experts_of_chunk(r0, seg_lo, seg_hi):
    row = (lax.broadcasted_iota(jnp.int32, (N_EXPERTS, PERM_ROWS), 1) + r0).astype(_f32)
    inside = jnp.logical_and(row >= seg_lo, row < seg_hi)
    rank = jnp.sum(jnp.where(inside, row - seg_lo, 0.0), axis=0, keepdims=True)
    return jnp.where(inside, 1.0, 0.0).astype(_bf16), rank


def _dispatch_kernel(pad_from_ref, pad_n_ref, nb_ref, piece_ref, idx_ref, idx_t_ref, hb_ref,
                     xs_ref, xbuf_ref, zero_ref, sent_ref, sem, zsem):
    tm = hb_ref.shape[0]
    block_groups = zero_ref.shape[0]
    n_blocks = xs_ref.shape[0] // block_groups

    @pl.when(pl.program_id(0) == 0)
    def _():
        zero_ref[...] = jnp.zeros_like(zero_ref)

        def pad_expert(e, carry):
            off = pad_from_ref[e]

            def pad_piece(r, c):
                pltpu.make_async_copy(zero_ref.at[0], xs_ref.at[off + r], zsem).start()
                return c + 1

            return lax.fori_loop(0, pad_n_ref[e], pad_piece, carry)

        n_pieces = lax.fori_loop(0, N_EXPERTS, pad_expert, 0)

        def pad_tail(b, carry):
            pltpu.make_async_copy(zero_ref, xs_ref.at[pl.ds(b * block_groups, block_groups)],
                                  zsem).start()
            return carry

        lax.fori_loop(nb_ref[0], n_blocks, pad_tail, 0)

        def wait_piece(r, c):
            pltpu.make_async_copy(zero_ref.at[0], xs_ref.at[0], zsem).wait()
            return c

        def wait_tail(b, c):
            pltpu.make_async_copy(zero_ref, xs_ref.at[pl.ds(0, block_groups)], zsem).wait()
            return c

        lax.fori_loop(0, n_pieces, wait_piece, 0)
        lax.fori_loop(nb_ref[0], n_blocks, wait_tail, 0)

    step = pl.program_id(0)
    slot = step % 2
    total = _tile_rows(piece_ref)
    count = jnp.sum(_member_by_token(idx_t_ref[...]), axis=0, keepdims=True)
    seg_lo, seg_hi = _segment_bounds(count, _prefix_over_lanes)
    member = _member_by_expert(idx_ref[...])
    before = _dot_nt(member.astype(_bf16), _strict_lower(tm))
    place = jnp.where(member > 0.0, before + 1.0, -1.0).astype(_bf16)

    def select(c):
        r0 = c * PERM_ROWS
        expert_of_row, rank_of_row = _rows_of_chunk(r0, seg_lo, seg_hi)
        row = lax.broadcasted_iota(jnp.int32, rank_of_row.shape, 0) + r0
        want = jnp.where(row < total, rank_of_row + 1.0, -5.0)
        return jnp.where(_dot(expert_of_row, place) == want, 1.0, 0.0).astype(_bf16)

    buffer_rows = xbuf_ref.shape[1] * SEG_ALIGN
    sel = jnp.concatenate([select(c) for c in range(buffer_rows // PERM_ROWS)], axis=0)
    picked = _dot(sel, hb_ref[...])
    half = picked.shape[1] // 2
    xbuf_ref[slot] = _as_groups((pltpu.bitcast(picked[:, :half], jnp.uint32) >> 16)
                                | pltpu.bitcast(picked[:, half:], jnp.uint32))

    wait_sent = lambda n_pieces: _wait_pieces(n_pieces, lambda k: pltpu.make_async_copy(
        xbuf_ref.at[slot, pl.ds(0, k)], xs_ref.at[pl.ds(0, k)], sem))

    @pl.when(step > 0)
    def _():
        wait_sent(sent_ref[0])

    for s in range(2):
        @pl.when(slot == s)
        def _(s=s):
            def send(g, dst):
                pltpu.make_async_copy(xbuf_ref.at[s, g], xs_ref.at[dst], sem).start()

            _for_each_piece(piece_ref, send)

    sent_ref[0] = total // SEG_ALIGN

    @pl.when(step == pl.num_programs(0) - 1)
    def _():
        wait_sent(total // SEG_ALIGN)


def _tile_buffer_rows(tm):
    worst = tm * TOP_K + N_EXPERTS * (SEG_ALIGN - 1)
    return -(-worst // PERM_ROWS) * PERM_ROWS


def moe_dispatch(pieces, idx, idx_t, hb, pad_from, pad_n, n_used, n_blocks):
    t, d = hb.shape
    tm = TOKEN_TILE
    grid_spec = pltpu.PrefetchScalarGridSpec(
        num_scalar_prefetch=3,
        grid=(t // tm,),
        in_specs=[
            pl.BlockSpec((1, 1, PIECE_SLOTS), lambda i, *_: (i, 0, 0), memory_space=pltpu.SMEM),
            pl.BlockSpec((TOP_K, tm), lambda i, *_: (0, i)),
            pl.BlockSpec((tm, TOP_K), lambda i, *_: (i, 0)),
            pl.BlockSpec((tm, d), lambda i, *_: (i, 0)),
        ],
        out_specs=pl.BlockSpec(memory_space=pl.ANY),
        scratch_shapes=[pltpu.VMEM((2, _tile_buffer_rows(tm) // SEG_ALIGN, SEG_ALIGN, d // 2),
                                   jnp.uint32),
                        pltpu.VMEM((ROW_BLOCK // SEG_ALIGN, SEG_ALIGN, d // 2), jnp.uint32),
                        pltpu.SMEM((1,), jnp.int32),
                        pltpu.SemaphoreType.DMA(()), pltpu.SemaphoreType.DMA(())],
    )
    xs = pl.pallas_call(
        _dispatch_kernel,
        grid_spec=grid_spec,
        out_shape=jax.ShapeDtypeStruct((n_blocks * ROW_BLOCK // SEG_ALIGN, SEG_ALIGN, d // 2),
                                       jnp.uint32),
        compiler_params=_cparams("arbitrary"),
        name="moe_dispatch",
    )(pad_from // SEG_ALIGN, pad_n // SEG_ALIGN, n_used, pieces, idx, idx_t, hb)
    return xs.reshape(n_blocks * ROW_BLOCK, d // 2)


def _expert_kernel(be_ref, nb_ref, xs_ref, w1_ref, w3_ref, w2_ref, y_ref, w1b, w3b, w2b):
    i = pl.program_id(0)

    @pl.when(jnp.logical_or(i == 0, be_ref[i] != be_ref[jnp.maximum(i - 1, 0)]))
    def _():
        w1b[...] = w1_ref[0, 0].astype(_bf16)
        w3b[...] = w3_ref[0, 0].astype(_bf16)
        w2b[...] = w2_ref[0, 0].astype(_bf16)

    @pl.when(i < nb_ref[0])
    def _():
        chains = [pl.ds(c * EXPERT_ROWS, EXPERT_ROWS) for c in range(ROW_BLOCK // EXPERT_ROWS)]
        xs = [jnp.concatenate(_unpack_rows(xs_ref[rows, :]), axis=1) for rows in chains]
        acts = [(_silu(_dot(x, w1b[...])) * _dot(x, w3b[...])).astype(_bf16) for x in xs]
        for rows, act in zip(chains, acts):
            y_ref[rows, :] = _pack_rows(_dot(act, w2b[...]))

    @pl.when(i >= nb_ref[0])
    def _():
        y_ref[...] = jnp.zeros_like(y_ref)


def moe_experts(xs, blk_e, n_used, w1, w3, w2, layer):
    p, half = xs.shape
    d = 2 * half
    ff = w1.shape[3]
    nb = p // ROW_BLOCK
    row = lambda i, be, nu: (jnp.minimum(i, nu[0] - 1), 0)
    wsel = lambda i, be, nu: (layer, be[i], 0, 0)
    grid_spec = pltpu.PrefetchScalarGridSpec(
        num_scalar_prefetch=2,
        grid=(nb,),
        in_specs=[
            pl.BlockSpec((ROW_BLOCK, half), row),
            pl.BlockSpec((1, 1, d, ff), wsel),
            pl.BlockSpec((1, 1, d, ff), wsel),
            pl.BlockSpec((1, 1, ff, d), wsel),
        ],
        out_specs=pl.BlockSpec((ROW_BLOCK, half), lambda i, be, nu: (i, 0)),
        scratch_shapes=[pltpu.VMEM((d, ff), _bf16), pltpu.VMEM((d, ff), _bf16),
                        pltpu.VMEM((ff, d), _bf16)],
    )
    return pl.pallas_call(
        _expert_kernel,
        grid_spec=grid_spec,
        out_shape=jax.ShapeDtypeStruct((p, half), jnp.uint32),
        compiler_params=_cparams("arbitrary"),
        name="moe_experts",
    )(blk_e, n_used, xs, w1, w3, w2)


def _combine_kernel(piece_ref, next_piece_ref, ys_ref, idx_ref, idx_t_ref, w_ref, shared_ref, x_ref,
                    gate_ref, fn_ref, o_ref, ybuf_ref, sem, *, final_norm):
    tm = x_ref.shape[1]
    buffer_rows = ybuf_ref.shape[1] * SEG_ALIGN
    step = pl.program_id(0) * pl.num_programs(1) + pl.program_id(1)
    n_steps = pl.num_programs(0) * pl.num_programs(1)
    slot = step % 2

    def fetch(table_ref, s):
        def one(g, src):
            pltpu.make_async_copy(ys_ref.at[src], ybuf_ref.at[s, g], sem.at[s]).start()

        _for_each_piece(table_ref, one)

    @pl.when(step == 0)
    def _():
        fetch(piece_ref, 0)

    for s in range(2):
        @pl.when(jnp.logical_and(step + 1 < n_steps, slot == 1 - s))
        def _(s=s):
            fetch(next_piece_ref, s)

    total = _tile_rows(piece_ref)
    _wait_pieces(total // SEG_ALIGN, lambda k: pltpu.make_async_copy(
        ys_ref.at[pl.ds(0, k)], ybuf_ref.at[slot, pl.ds(0, k)], sem.at[slot]))

    idx_t = idx_t_ref[...]
    member = _member_by_token(idx_t)
    before_b = _dot(_strict_lower(tm), member.astype(_bf16)).astype(_bf16)
    lane_e = lax.broadcasted_iota(jnp.int32, member.shape, 1)
    w_of_expert = jnp.zeros(member.shape, _f32)
    for k in range(TOP_K):
        w_of_expert = jnp.where(lane_e == idx_t[:, k:k + 1], w_ref[:, k:k + 1], w_of_expert)
    w_of_expert = w_of_expert.astype(_bf16)
    count = jnp.sum(_member_by_expert(idx_ref[...]), axis=1, keepdims=True)
    seg_lo, seg_hi = _segment_bounds(count, _prefix_over_experts)

    def weights_of_chunk(c):
        expert_of_row, rank_of_row = _experts_of_chunk(c * PERM_ROWS, seg_lo, seg_hi)
        hit = _dot(before_b, expert_of_row) == rank_of_row
        return jnp.where(hit, _dot(w_of_expert, expert_of_row), 0.0).astype(_bf16)

    wm = jnp.concatenate([weights_of_chunk(c) for c in range(buffer_rows // PERM_ROWS)], axis=1)
    packed = ybuf_ref[slot].reshape(buffer_rows, ybuf_ref.shape[3])
    row = lax.broadcasted_iota(jnp.int32, packed.shape, 0)
    packed = jnp.where(row < total, packed, jnp.uint32(0))
    y_lo, y_hi = _unpack_rows(packed)
    routed = jnp.concatenate([_dot(wm, y_lo), _dot(wm, y_hi)], axis=1)

    xn = x_ref[0] + gate_ref[0] * (shared_ref[...] + routed)
    if final_norm:
        xn = xn * lax.rsqrt(jnp.mean(xn * xn, axis=-1, keepdims=True) + EPS) * fn_ref[...]
    o_ref[0] = xn


def moe_combine(pieces, idx, idx_t, ys, top_w, shared, x, gate, fn_g, final_norm):
    b, s, d = x.shape
    tm = TOKEN_TILE
    n_s = s // tm
    tile = lambda i, m: (i * n_s + m, 0)
    last_tile = b * n_s - 1
    return pl.pallas_call(
        functools.partial(_combine_kernel, final_norm=final_norm),
        grid=(b, n_s),
        in_specs=[
            pl.BlockSpec((1, 1, PIECE_SLOTS), lambda i, m: (i * n_s + m, 0, 0),
                         memory_space=pltpu.SMEM),
            pl.BlockSpec((1, 1, PIECE_SLOTS),
                         lambda i, m: (jnp.minimum(i * n_s + m + 1, last_tile), 0, 0),
                         memory_space=pltpu.SMEM),
            pl.BlockSpec(memory_space=pl.ANY),
            pl.BlockSpec((TOP_K, tm), lambda i, m: (0, i * n_s + m)),
            pl.BlockSpec((tm, TOP_K), tile),
            pl.BlockSpec((tm, TOP_K), tile),
            pl.BlockSpec((tm, d), tile),
            pl.BlockSpec((1, tm, d), lambda i, m: (i, m, 0)),
            pl.BlockSpec((1, 1, d), lambda i, m: (i, 0, 0)),
            pl.BlockSpec((1, d), lambda i, m: (0, 0)),
        ],
        out_specs=pl.BlockSpec((1, tm, d), lambda i, m: (i, m, 0)),
        out_shape=jax.ShapeDtypeStruct((b, s, d), _f32),
        scratch_shapes=[pltpu.VMEM((2, _tile_buffer_rows(tm) // SEG_ALIGN, SEG_ALIGN, d // 2),
                                   jnp.uint32),
                        pltpu.SemaphoreType.DMA((2,))],
        compiler_params=_cparams("arbitrary", "arbitrary"),
        name="moe_combine",
    )(pieces, pieces, _as_groups(ys), idx, idx_t, top_w, shared, x, gate, fn_g.reshape(1, d))


def moe_layer(y, w_out, x, gate_a, g, sc, sh, gate, router_w, router_b, w1, w3, w2, layer,
              s_w1, s_w3, s_w2, fn_g, final_norm):
    b, s, d = x.shape
    t = b * s
    x, hb, top_idx, top_w, shared = moe_router(y, w_out, x, gate_a, g, sc, sh, router_w, router_b,
                                               s_w1, s_w3, s_w2)
    pieces, rows = moe_rank(top_idx)
    idx_t = top_idx.T
    n_tiles = pieces.shape[0]

    rows = rows.reshape(N_EXPERTS)
    nblk = (rows + ROW_BLOCK - 1) // ROW_BLOCK
    blk_ends = jnp.cumsum(nblk)
    max_rows = t * TOP_K + n_tiles * N_EXPERTS * (SEG_ALIGN - 1)
    n_blocks = -(-max_rows // ROW_BLOCK) + N_EXPERTS
    blk = jnp.arange(n_blocks, dtype=jnp.int32)
    blk_e = jnp.minimum(jnp.sum(blk[:, None] >= blk_ends[None, :], axis=1),
                        N_EXPERTS - 1).astype(jnp.int32)
    n_used = blk_ends[-1:].astype(jnp.int32)
    pad_from = ((blk_ends - nblk) * ROW_BLOCK + rows).astype(jnp.int32)
    pad_n = (nblk * ROW_BLOCK - rows).astype(jnp.int32)

    xs = moe_dispatch(pieces, top_idx, idx_t, hb, pad_from, pad_n, n_used, n_blocks)
    ys = moe_experts(xs, blk_e, n_used, w1, w3, w2, layer)
    return moe_combine(pieces, top_idx, idx_t, ys, top_w.T, shared, x, gate, fn_g, final_norm)


def kernel(x, c, norm_mix, norm_ffn, ada_w, ada_b, da_w_in, da_w_out, da_lq1, da_lk1, da_lq2,
           da_lk2, da_subln, ret_w_in, ret_w_out, ret_gn, router_w, router_b, exp_w1, exp_w3,
           exp_w2, sh_w1, sh_w3, sh_w2, final_norm):
    depth, d = norm_mix.shape
    b = x.shape[0]
    mod = ada_modulation(c, ada_w, ada_b).reshape(depth, b, 6, 1, d)
    for i in range(depth):
        sh_a, sc_a, g_a, sh_f, sc_f, g_f = (mod[i, :, j] for j in range(6))
        j = i // 2
        if i % 2 == 0:
            proj = norm_proj(x, norm_mix[i], sc_a, sh_a, da_w_in[j].astype(_bf16))
            y = diff_attention_core(proj, da_lq1[j], da_lk1[j], da_lq2[j], da_lk2[j],
                                    da_subln[j], i)
            w_out = da_w_out[j].astype(_bf16)
        else:
            proj = norm_proj(x, norm_mix[i], sc_a, sh_a, ret_w_in[j].astype(_bf16))
            y = retention_core(proj, ret_gn[j])
            w_out = ret_w_out[j].astype(_bf16)
        x = moe_layer(y, w_out, x, g_a, norm_ffn[i], sc_f, sh_f, g_f, router_w[i], router_b[i],
                      exp_w1, exp_w3, exp_w2, i,
                      sh_w1[i], sh_w3[i], sh_w2[i], final_norm, i == depth - 1)
    return x
```

```python
import functools
import math

import jax
import jax.numpy as jnp
from jax import lax
from jax.experimental import pallas as pl
from jax.experimental.pallas import tpu as pltpu

EPS = 1e-6
DA_HEADS = 8
DA_HEAD_DIM = 64
DA_V_DIM = 2 * DA_HEAD_DIM
RET_HEADS = 4
RET_DK = 256
RET_DV = 512
RET_CHUNK = 128
N_EXPERTS = 64
TOP_K = 8
N_GROUPS = 8
GROUP_SIZE = N_EXPERTS // N_GROUPS
TOPK_GROUPS = 4
ROUTED_SCALE = 2.5
ROW_BLOCK_LOG2 = 10
ROW_BLOCK = 1 << ROW_BLOCK_LOG2
EXPERT_ROWS = 256
TOKEN_TILE = 256
SEG_ALIGN = 8
PERM_ROWS = 512
PIECE_SLOTS = 384
PIECE_UNROLL = 8

VMEM_LIMIT = 56 * 1024 * 1024
NEG_BIG = -1e30

_f32 = jnp.float32
_bf16 = jnp.bfloat16


def _cparams(*sem):
    return pltpu.CompilerParams(dimension_semantics=sem, vmem_limit_bytes=VMEM_LIMIT)


def _silu(v):
    return v * (1.0 / (1.0 + jnp.exp(-v)))


def _dot(a, b):
    return jnp.dot(a, b, preferred_element_type=_f32)


def _dot_nt(a, b):
    return lax.dot_general(a, b, (((1,), (1,)), ((), ())), preferred_element_type=_f32)


def _dot_tn(a, b):
    return lax.dot_general(a, b, (((0,), (0,)), ((), ())), preferred_element_type=_f32)


def _ada_kernel(c_ref, w_ref, b_ref, o_ref):
    a = _silu(c_ref[...]).astype(_bf16)
    o_ref[0] = _dot(a, w_ref[0].astype(_bf16)) + b_ref[0]


def ada_modulation(c, ada_w, ada_b):
    depth, d, n = ada_w.shape
    b = c.shape[0]
    tn = 1024
    return pl.pallas_call(
        _ada_kernel,
        grid=(depth, n // tn),
        in_specs=[
            pl.BlockSpec((b, d), lambda l, j: (0, 0)),
            pl.BlockSpec((1, d, tn), lambda l, j: (l, 0, j)),
            pl.BlockSpec((1, 1, tn), lambda l, j: (l, 0, j)),
        ],
        out_specs=pl.BlockSpec((1, b, tn), lambda l, j: (l, 0, j)),
        out_shape=jax.ShapeDtypeStruct((depth, b, n), _f32),
        compiler_params=_cparams("parallel", "parallel"),
        name="ada_modulation",
    )(c, ada_w, ada_b.reshape(depth, 1, n))


def _norm_mod(x, g, sc, sh):
    y = x * lax.rsqrt(jnp.mean(x * x, axis=-1, keepdims=True) + EPS)
    return (y * g) * (1.0 + sc) + sh


def _norm_proj_kernel(x_ref, g_ref, sc_ref, sh_ref, w_ref, o_ref, h_ref):
    @pl.when(pl.program_id(2) == 0)
    def _():
        h_ref[...] = _norm_mod(x_ref[0], g_ref[...], sc_ref[0], sh_ref[0]).astype(_bf16)

    o_ref[0] = _dot(h_ref[...], w_ref[...]).astype(o_ref.dtype)


def norm_proj(x, g, sc, sh, w):
    b, s, d = x.shape
    n = w.shape[1]
    tm = min(2048, s)
    tn = 1024
    return pl.pallas_call(
        _norm_proj_kernel,
        grid=(b, s // tm, n // tn),
        in_specs=[
            pl.BlockSpec((1, tm, d), lambda i, m, j: (i, m, 0)),
            pl.BlockSpec((1, d), lambda i, m, j: (0, 0)),
            pl.BlockSpec((1, 1, d), lambda i, m, j: (i, 0, 0)),
            pl.BlockSpec((1, 1, d), lambda i, m, j: (i, 0, 0)),
            pl.BlockSpec((d, tn), lambda i, m, j: (0, j)),
        ],
        out_specs=pl.BlockSpec((1, tm, tn), lambda i, m, j: (i, m, j)),
        out_shape=jax.ShapeDtypeStruct((b, s, n), _bf16),
        scratch_shapes=[pltpu.VMEM((tm, d), _bf16)],
        compiler_params=_cparams("parallel", "parallel", "arbitrary"),
        name="norm_proj",
    )(x, g.reshape(1, d), sc, sh, w)


def _diff_attn_kernel(q_ref, k_ref, v_ref, lq1_ref, lk1_ref, lq2_ref, lk2_ref, sg_ref, o_ref,
                      bias_ref, *, tq, lam_init):
    h = pl.program_id(0)
    dh, dv = DA_HEAD_DIM, DA_V_DIM
    n_q = k_ref.shape[1] // tq

    @pl.when(pl.program_id(1) == 0)
    def _():
        hp1 = jnp.full((1, 1), h + 1, jnp.int32).astype(_f32)
        slope = jnp.exp2(hp1 * (-8.0 / DA_HEADS))
        dist = ((n_q - 1) * tq + lax.broadcasted_iota(jnp.int32, bias_ref.shape, 0)
                - lax.broadcasted_iota(jnp.int32, bias_ref.shape, 1)).astype(_f32)
        bias_ref[...] = -slope * dist

    lam = (jnp.exp(jnp.sum(lq1_ref[...] * lk1_ref[...], axis=-1, keepdims=True))
           - jnp.exp(jnp.sum(lq2_ref[...] * lk2_ref[...], axis=-1, keepdims=True)) + lam_init)
    causal = (lax.broadcasted_iota(jnp.int32, (tq, tq), 0)
              >= lax.broadcasted_iota(jnp.int32, (tq, tq), 1))
    lane = lax.broadcasted_iota(jnp.int32, (tq, 2 * dh), 1)

    def attend(c):
        q = q_ref[0, c * tq:(c + 1) * tq, :] * (dh ** -0.5)
        zero = jnp.zeros_like(q)
        q_maps = (jnp.where(lane < dh, q, zero), jnp.where(lane >= dh, q, zero))
        first = (n_q - 1 - c) * tq
        blocks = [slice(j * tq, (j + 1) * tq) for j in range(c + 1)]
        tiles = [[_dot_nt(q_m, k_ref[0, blk, :])
                  + bias_ref[:, first + blk.start:first + blk.stop] for q_m in q_maps]
                 for blk in blocks]
        tiles[c] = [jnp.where(causal, s, NEG_BIG) for s in tiles[c]]
        outs = []
        for mi in range(2):
            parts = [row[mi] for row in tiles]
            m = functools.reduce(jnp.maximum, [jnp.max(x, axis=-1, keepdims=True) for x in parts])
            ps = [jnp.exp(x - m) for x in parts]
            l = functools.reduce(jnp.add, [jnp.sum(x, axis=-1, keepdims=True) for x in ps])
            o = functools.reduce(jnp.add, [_dot(x.astype(_bf16), v_ref[0, blk, :])
                                           for x, blk in zip(ps, blocks)])
            outs.append(o / l)
        o = outs[0] - lam * outs[1]
        o = o * lax.rsqrt(jnp.mean(o * o, axis=-1, keepdims=True) + EPS)
        o = (o * sg_ref[...]) * (1.0 - lam_init)
        o_ref[0, c * tq:(c + 1) * tq, :] = o.astype(o_ref.dtype)

    for c in range(n_q):
        attend(c)


def diff_attention_core(proj, lq1, lk1, lq2, lk2, subln_g, layer_idx):
    b, s, _ = proj.shape
    nh, dh, dv = DA_HEADS, DA_HEAD_DIM, DA_V_DIM
    tq = min(512, s)
    lam_init = 0.8 - 0.6 * math.exp(-0.3 * layer_idx)
    vec = lambda n: pl.BlockSpec((1, n), lambda h, i: (0, 0))
    return pl.pallas_call(
        functools.partial(_diff_attn_kernel, tq=tq, lam_init=lam_init),
        grid=(nh, b),
        in_specs=[
            pl.BlockSpec((1, s, 2 * dh), lambda h, i: (i, 0, h)),
            pl.BlockSpec((1, s, 2 * dh), lambda h, i: (i, 0, nh + h)),
            pl.BlockSpec((1, s, dv), lambda h, i: (i, 0, 2 * nh + h)),
            vec(dh), vec(dh), vec(dh), vec(dh), vec(dv),
        ],
        out_specs=pl.BlockSpec((1, s, dv), lambda h, i: (i, 0, h)),
        out_shape=jax.ShapeDtypeStruct((b, s, nh * dv), _bf16),
        scratch_shapes=[pltpu.VMEM((tq, s), _f32)],
        compiler_params=_cparams("parallel", "arbitrary"),
        name="diff_attention",
    )(proj, proj, proj, lq1.reshape(1, dh), lk1.reshape(1, dh), lq2.reshape(1, dh),
      lk2.reshape(1, dh), subln_g.reshape(1, dv))


def _retention_kernel(q_ref, k_ref, v_ref, g_ref, gn_ref, o_ref, r_ref, *, chunk):
    h = pl.program_id(1)
    c = pl.program_id(2)
    dk = RET_DK

    @pl.when(c == 0)
    def _():
        r_ref[...] = jnp.zeros_like(r_ref)

    hf = jnp.full((1, 1), h, jnp.int32).astype(_f32)
    log_g = jnp.log(1.0 - jnp.exp2(-5.0 - hf))
    ri = lax.broadcasted_iota(jnp.int32, (chunk, chunk), 0)
    ci = lax.broadcasted_iota(jnp.int32, (chunk, chunk), 1)
    diff = (ri - ci).astype(_f32)
    decay = jnp.where(diff >= 0, jnp.exp(log_g * jnp.maximum(diff, 0.0)), 0.0)
    idx = lax.broadcasted_iota(jnp.int32, (chunk, 1), 0).astype(_f32)
    zeta = jnp.exp(log_g * (chunk - 1 - idx))
    xi = jnp.exp(log_g * (idx + 1.0))
    g_c = jnp.exp(log_g * chunk)

    decay = decay * (dk ** -0.5)
    zeta = zeta * (dk ** -0.5)
    chunks = [pl.ds(sub * chunk, chunk) for sub in range(q_ref.shape[1] // chunk)]
    intra, updates = [], []
    for rows in chunks:
        q, k, v = q_ref[0, rows, :], k_ref[0, rows, :], v_ref[0, rows, :]
        intra.append(_dot((_dot_nt(q, k) * decay).astype(_bf16), v))
        updates.append(_dot_tn((k.astype(_f32) * zeta).astype(_bf16), v))
    r = r_ref[...]
    for rows, o, upd in zip(chunks, intra, updates):
        o = o + _dot(q_ref[0, rows, :], r.astype(_bf16)) * xi
        r = g_c * r + upd
        o = o * lax.rsqrt(jnp.mean(o * o, axis=-1, keepdims=True) + EPS) * gn_ref[...]
        o_ref[0, rows, :] = (_silu(g_ref[0, rows, :].astype(_f32)) * o).astype(o_ref.dtype)
    r_ref[...] = r


def retention_core(proj, gn_g):
    b, s, _ = proj.shape
    nh, dk, dv = RET_HEADS, RET_DK, RET_DV
    chunk = RET_CHUNK
    rows = min(16 * chunk, s)
    return pl.pallas_call(
        functools.partial(_retention_kernel, chunk=chunk),
        grid=(b, nh, s // rows),
        in_specs=[
            pl.BlockSpec((1, rows, dk), lambda i, h, c: (i, c, h)),
            pl.BlockSpec((1, rows, dk), lambda i, h, c: (i, c, nh + h)),
            pl.BlockSpec((1, rows, dv), lambda i, h, c: (i, c, (2 * nh * dk) // dv + h)),
            pl.BlockSpec((1, rows, dv), lambda i, h, c: (i, c, (2 * nh * dk) // dv + nh + h)),
            pl.BlockSpec((1, dv), lambda i, h, c: (0, h)),
        ],
        out_specs=pl.BlockSpec((1, rows, dv), lambda i, h, c: (i, c, h)),
        out_shape=jax.ShapeDtypeStruct((b, s, nh * dv), _bf16),
        scratch_shapes=[pltpu.VMEM((dk, dv), _f32)],
        compiler_params=_cparams("parallel", "parallel", "arbitrary"),
        name="retention",
    )(proj, proj, proj, proj, gn_g.reshape(1, nh * dv))


def _pack_rows(h):
    n = h.shape[1] // 2
    lo = pltpu.bitcast(h[:, :n].astype(_bf16).astype(_f32), jnp.uint32)
    hi = pltpu.bitcast(h[:, n:].astype(_bf16).astype(_f32), jnp.uint32)
    return (lo >> 16) | (hi & jnp.uint32(0xFFFF0000))


def _unpack_rows(p):
    lo = pltpu.bitcast(p << 16, _f32).astype(_bf16)
    hi = pltpu.bitcast(p & jnp.uint32(0xFFFF0000), _f32).astype(_bf16)
    return lo, hi


def _first_argmax(v, iota, axis, big):
    m = jnp.max(v, axis=axis, keepdims=True)
    i = jnp.min(jnp.where(v == m, iota, big), axis=axis, keepdims=True)
    return m, i


def _router_kernel(y_ref, wo_ref, x_ref, ga_ref, g_ref, sc_ref, sh_ref, rwt_ref, rb_ref, s1_ref,
                   s3_ref, s2_ref, xo_ref, hb_ref, idx_ref, wgt_ref, shared_ref):
    tm = x_ref.shape[1]
    x = x_ref[0] + ga_ref[0] * _dot(y_ref[0], wo_ref[...])
    xo_ref[0] = x
    h = _norm_mod(x, g_ref[...], sc_ref[0], sh_ref[0])
    hb = h.astype(_bf16)
    hb_ref[...] = hb
    act = (_silu(_dot(hb, s1_ref[...])) * _dot(hb, s3_ref[...])).astype(_bf16)
    shared_ref[...] = _dot(act, s2_ref[...])

    scores = 1.0 / (1.0 + jnp.exp(-_dot_nt(rwt_ref[...], h)))
    choice = scores + rb_ref[...]
    c3 = choice.reshape(N_GROUPS, GROUP_SIZE, tm)
    mem = lax.broadcasted_iota(jnp.int32, c3.shape, 1)
    m1, i1 = _first_argmax(c3, mem, 1, GROUP_SIZE)
    m2 = jnp.max(jnp.where(mem == i1, -jnp.inf, c3), axis=1, keepdims=True)
    gs = m1 + m2
    gi = lax.broadcasted_iota(jnp.int32, gs.shape, 0)
    sel = jnp.zeros(gs.shape, _f32)
    for _ in range(TOPK_GROUPS):
        _, ig = _first_argmax(gs, gi, 0, N_GROUPS)
        hit = gi == ig
        sel = jnp.where(hit, 1.0, sel)
        gs = jnp.where(hit, -jnp.inf, gs)
    cm = jnp.where(sel > 0.0, c3, -jnp.inf).reshape(N_EXPERTS, tm)
    ei = lax.broadcasted_iota(jnp.int32, cm.shape, 0)
    idxs, wgts = [], []
    for _ in range(TOP_K):
        _, ie = _first_argmax(cm, ei, 0, N_EXPERTS)
        hit = ei == ie
        idxs.append(ie)
        wgts.append(jnp.sum(jnp.where(hit, scores, 0.0), axis=0, keepdims=True))
        cm = jnp.where(hit, -jnp.inf, cm)
    w = jnp.concatenate(wgts, axis=0)
    idx_ref[...] = jnp.concatenate(idxs, axis=0)
    wgt_ref[...] = w / jnp.sum(w, axis=0, keepdims=True) * ROUTED_SCALE


def moe_router(y, w_out, x, gate_a, g, sc, sh, router_w, router_b, s_w1, s_w3, s_w2):
    b, s, d = x.shape
    t = b * s
    kdim = y.shape[2]
    tm = min(512, s)
    n_s = s // tm
    sf = s_w1.shape[1]
    const = lambda shape: pl.BlockSpec(shape, lambda i, m: (0,) * len(shape))
    return pl.pallas_call(
        _router_kernel,
        grid=(b, n_s),
        in_specs=[
            pl.BlockSpec((1, tm, kdim), lambda i, m: (i, m, 0)),
            const((kdim, d)),
            pl.BlockSpec((1, tm, d), lambda i, m: (i, m, 0)),
            pl.BlockSpec((1, 1, d), lambda i, m: (i, 0, 0)),
            const((1, d)),
            pl.BlockSpec((1, 1, d), lambda i, m: (i, 0, 0)),
            pl.BlockSpec((1, 1, d), lambda i, m: (i, 0, 0)),
            const((N_EXPERTS, d)), const((N_EXPERTS, 1)),
            const((d, sf)), const((d, sf)), const((sf, d)),
        ],
        out_specs=[
            pl.BlockSpec((1, tm, d), lambda i, m: (i, m, 0)),
            pl.BlockSpec((tm, d), lambda i, m: (i * n_s + m, 0)),
            pl.BlockSpec((TOP_K, tm), lambda i, m: (0, i * n_s + m)),
            pl.BlockSpec((TOP_K, tm), lambda i, m: (0, i * n_s + m)),
            pl.BlockSpec((tm, d), lambda i, m: (i * n_s + m, 0)),
        ],
        out_shape=[
            jax.ShapeDtypeStruct((b, s, d), _f32),
            jax.ShapeDtypeStruct((t, d), _bf16),
            jax.ShapeDtypeStruct((TOP_K, t), jnp.int32),
            jax.ShapeDtypeStruct((TOP_K, t), _f32),
            jax.ShapeDtypeStruct((t, d), _f32),
        ],
        compiler_params=_cparams("parallel", "parallel"),
        name="moe_router",
    )(y, w_out, x, gate_a, g.reshape(1, d), sc, sh, router_w.T, router_b.reshape(N_EXPERTS, 1),
      s_w1.astype(_bf16), s_w3.astype(_bf16), s_w2.astype(_bf16))


def _strict_lower(n):
    return jnp.where(lax.broadcasted_iota(jnp.int32, (n, n), 1)
                     < lax.broadcasted_iota(jnp.int32, (n, n), 0), 1.0, 0.0).astype(_bf16)


def _prefix_over_experts(col):
    wide = jnp.broadcast_to(col, (N_EXPERTS, 128)).astype(_bf16)
    return _dot(_strict_lower(N_EXPERTS), wide)[:, :1]


def _rank_kernel(idx_ref, piece_ref, tot_ref, base_ref):
    phase = pl.program_id(0)
    i = pl.program_id(1)
    tm = TOKEN_TILE

    @pl.when(jnp.logical_and(phase == 0, i == 0))
    def _():
        base_ref[...] = jnp.zeros_like(base_ref)

    @pl.when(jnp.logical_and(phase == 1, i == 0))
    def _():
        tot = base_ref[...].astype(jnp.int32)
        tot_ref[...] = tot
        nblk = (tot + (ROW_BLOCK - 1)) >> ROW_BLOCK_LOG2
        start_blk = (32.0 * _prefix_over_experts((nblk >> 5).astype(_f32))
                     + _prefix_over_experts((nblk & 31).astype(_f32)))
        base_ref[...] = start_blk * float(ROW_BLOCK)

    ei = lax.broadcasted_iota(jnp.int32, (N_EXPERTS, tm), 0)
    for sub in range(idx_ref.shape[1] // tm):
        idx = idx_ref[:, sub * tm:(sub + 1) * tm]
        memf = jnp.zeros((N_EXPERTS, tm), _f32)
        for k in range(TOP_K):
            memf = jnp.where(ei == idx[k:k + 1, :], 1.0, memf)
        tile_cnt = jnp.sum(memf, axis=1, keepdims=True).astype(jnp.int32)
        seg_rows = ((tile_cnt + (SEG_ALIGN - 1)) & (-SEG_ALIGN)).astype(_f32)

        @pl.when(phase == 0)
        def _(seg_rows=seg_rows):
            base_ref[...] += seg_rows

        @pl.when(phase == 1)
        def _(seg_rows=seg_rows, sub=sub):
            seg_i = seg_rows.astype(jnp.int32)
            lo = (16.0 * _prefix_over_experts((seg_i >> 4).astype(_f32))
                  + _prefix_over_experts((seg_i & 15).astype(_f32)))
            lane = lax.broadcasted_iota(jnp.int32, (N_EXPERTS, PIECE_SLOTS), 1)
            row = (lane * SEG_ALIGN).astype(_f32)
            inside = jnp.logical_and(row >= lo, row < lo + seg_rows)
            dst = jnp.sum(jnp.where(inside, base_ref[...] + (row - lo), 0.0), axis=0,
                          keepdims=True) * (1.0 / SEG_ALIGN)
            total = jnp.sum(seg_rows, axis=0, keepdims=True)
            piece_ref[sub] = jnp.where(lane[:1] == PIECE_SLOTS - 1, total, dst).astype(jnp.int32)
            base_ref[...] += seg_rows


def moe_rank(top_idx):
    k, t = top_idx.shape
    tm = TOKEN_TILE
    assert _tile_buffer_rows(tm) // SEG_ALIGN < PIECE_SLOTS
    per_step = 8 if t % (8 * tm) == 0 else 1
    return pl.pallas_call(
        _rank_kernel,
        grid=(2, t // (per_step * tm)),
        in_specs=[pl.BlockSpec((k, per_step * tm), lambda p, i: (0, i))],
        out_specs=[pl.BlockSpec((per_step, 1, PIECE_SLOTS), lambda p, i: (i * p, 0, 0)),
                   pl.BlockSpec((N_EXPERTS, 1), lambda p, i: (0, 0))],
        out_shape=[jax.ShapeDtypeStruct((t // tm, 1, PIECE_SLOTS), jnp.int32),
                   jax.ShapeDtypeStruct((N_EXPERTS, 1), jnp.int32)],
        scratch_shapes=[pltpu.VMEM((N_EXPERTS, 1), _f32)],
        compiler_params=_cparams("arbitrary", "arbitrary"),
        name="moe_rank",
    )(top_idx)


def _as_groups(a):
    return a.reshape(a.shape[0] // SEG_ALIGN, SEG_ALIGN, a.shape[1])


def _tile_rows(piece_ref):
    return piece_ref[0, 0, PIECE_SLOTS - 1]


def _for_each_piece(piece_ref, fn):
    def piece(g, c, priority=0):
        fn(g, piece_ref[0, 0, g], priority)
        return c

    def group(i, c):
        for u in range(PIECE_UNROLL):
            piece(i * PIECE_UNROLL + u, c, u % 2)
        return c

    n = _tile_rows(piece_ref) // SEG_ALIGN
    lax.fori_loop(0, n // PIECE_UNROLL, group, 0)
    lax.fori_loop(n // PIECE_UNROLL * PIECE_UNROLL, n, piece, 0)


def _wait_pieces(n_pieces, make_copy):
    for bit in reversed(range(PIECE_SLOTS.bit_length())):
        @pl.when((n_pieces >> bit) & 1 == 1)
        def _(bit=bit):
            make_copy(1 << bit).wait()


def _prefix_over_lanes(row):
    n = row.shape[1]
    wide = jnp.broadcast_to(row, (SEG_ALIGN, n)).astype(_bf16)
    return _dot_nt(wide, _strict_lower(n))[:1, :]


def _member_by_token(idx_t):
    lane_e = lax.broadcasted_iota(jnp.int32, (idx_t.shape[0], N_EXPERTS), 1)
    member = jnp.zeros(lane_e.shape, _f32)
    for k in range(TOP_K):
        member = jnp.where(lane_e == idx_t[:, k:k + 1], 1.0, member)
    return member


def _member_by_expert(idx):
    sub_e = lax.broadcasted_iota(jnp.int32, (N_EXPERTS, idx.shape[1]), 0)
    member = jnp.zeros(sub_e.shape, _f32)
    for k in range(TOP_K):
        member = jnp.where(sub_e == idx[k:k + 1, :], 1.0, member)
    return member


def _segment_bounds(count, prefix):
    seg_rows = (count.astype(jnp.int32) + (SEG_ALIGN - 1)) & (-SEG_ALIGN)
    lo = 16.0 * prefix((seg_rows >> 4).astype(_f32)) + prefix((seg_rows & 15).astype(_f32))
    return lo, lo + seg_rows.astype(_f32)


def _rows_of_chunk(r0, seg_lo, seg_hi):
    row = (lax.broadcasted_iota(jnp.int32, (PERM_ROWS, N_EXPERTS), 0) + r0).astype(_f32)
    inside = jnp.logical_and(row >= seg_lo, row < seg_hi)
    rank = jnp.sum(jnp.where(inside, row - seg_lo, 0.0), axis=1, keepdims=True)
    return jnp.where(inside, 1.0, 0.0).astype(_bf16), rank


def _experts_of_chunk(r0, seg_lo, seg_hi):
    row = (lax.broadcasted_iota(jnp.int32, (N_EXPERTS, PERM_ROWS), 1) + r0).astype(_f32)
    inside = jnp.logical_and(row >= seg_lo, row < seg_hi)
    rank = jnp.sum(jnp.where(inside, row - seg_lo, 0.0), axis=0, keepdims=True)
    return jnp.where(inside, 1.0, 0.0).astype(_bf16), rank


def _dispatch_kernel(pad_from_ref, pad_n_ref, nb_ref, piece_ref, idx_ref, idx_t_ref, hb_ref,
                     xs_ref, xbuf_ref, zero_ref, sent_ref, sem, zsem):
    tm = hb_ref.shape[0]
    block_groups = zero_ref.shape[0]
    n_blocks = xs_ref.shape[0] // block_groups

    @pl.when(pl.program_id(0) == 0)
    def _():
        zero_ref[...] = jnp.zeros_like(zero_ref)

        def pad_expert(e, carry):
            off = pad_from_ref[e]

            def pad_piece(r, c):
                pltpu.make_async_copy(zero_ref.at[0], xs_ref.at[off + r], zsem).start()
                return c + 1

            return lax.fori_loop(0, pad_n_ref[e], pad_piece, carry)

        n_pieces = lax.fori_loop(0, N_EXPERTS, pad_expert, 0)

        def pad_tail(b, carry):
            pltpu.make_async_copy(zero_ref, xs_ref.at[pl.ds(b * block_groups, block_groups)],
                                  zsem).start()
            return carry

        lax.fori_loop(nb_ref[0], n_blocks, pad_tail, 0)

        def wait_piece(r, c):
            pltpu.make_async_copy(zero_ref.at[0], xs_ref.at[0], zsem).wait()
            return c

        def wait_tail(b, c):
            pltpu.make_async_copy(zero_ref, xs_ref.at[pl.ds(0, block_groups)], zsem).wait()
            return c

        lax.fori_loop(0, n_pieces, wait_piece, 0)
        lax.fori_loop(nb_ref[0], n_blocks, wait_tail, 0)

    step = pl.program_id(0)
    slot = step % 2
    total = _tile_rows(piece_ref)
    count = jnp.sum(_member_by_token(idx_t_ref[...]), axis=0, keepdims=True)
    seg_lo, seg_hi = _segment_bounds(count, _prefix_over_lanes)
    member = _member_by_expert(idx_ref[...])
    before = _dot_nt(member.astype(_bf16), _strict_lower(tm))
    place = jnp.where(member > 0.0, before + 1.0, -1.0).astype(_bf16)

    def select(c):
        r0 = c * PERM_ROWS
        expert_of_row, rank_of_row = _rows_of_chunk(r0, seg_lo, seg_hi)
        row = lax.broadcasted_iota(jnp.int32, rank_of_row.shape, 0) + r0
        want = jnp.where(row < total, rank_of_row + 1.0, -5.0)
        return jnp.where(_dot(expert_of_row, place) == want, 1.0, 0.0).astype(_bf16)

    buffer_rows = xbuf_ref.shape[1] * SEG_ALIGN
    sel = jnp.concatenate([select(c) for c in range(buffer_rows // PERM_ROWS)], axis=0)
    picked = _dot(sel, hb_ref[...])
    half = picked.shape[1] // 2
    xbuf_ref[slot] = _as_groups((pltpu.bitcast(picked[:, :half], jnp.uint32) >> 16)
                                | pltpu.bitcast(picked[:, half:], jnp.uint32))

    wait_sent = lambda n_pieces: _wait_pieces(n_pieces, lambda k: pltpu.make_async_copy(
        xbuf_ref.at[slot, pl.ds(0, k)], xs_ref.at[pl.ds(0, k)], sem))

    @pl.when(step > 0)
    def _():
        wait_sent(sent_ref[0])

    for s in range(2):
        @pl.when(slot == s)
        def _(s=s):
            def send(g, dst, priority):
                pltpu.make_async_copy(xbuf_ref.at[s, g], xs_ref.at[dst], sem).start(priority)

            _for_each_piece(piece_ref, send)

    sent_ref[0] = total // SEG_ALIGN

    @pl.when(step == pl.num_programs(0) - 1)
    def _():
        wait_sent(total // SEG_ALIGN)


def _tile_buffer_rows(tm):
    worst = tm * TOP_K + N_EXPERTS * (SEG_ALIGN - 1)
    return -(-worst // PERM_ROWS) * PERM_ROWS


def moe_dispatch(pieces, idx, idx_t, hb, pad_from, pad_n, n_used, n_blocks):
    t, d = hb.shape
    tm = TOKEN_TILE
    grid_spec = pltpu.PrefetchScalarGridSpec(
        num_scalar_prefetch=3,
        grid=(t // tm,),
        in_specs=[
            pl.BlockSpec((1, 1, PIECE_SLOTS), lambda i, *_: (i, 0, 0), memory_space=pltpu.SMEM),
            pl.BlockSpec((TOP_K, tm), lambda i, *_: (0, i)),
            pl.BlockSpec((tm, TOP_K), lambda i, *_: (i, 0)),
            pl.BlockSpec((tm, d), lambda i, *_: (i, 0)),
        ],
        out_specs=pl.BlockSpec(memory_space=pl.ANY),
        scratch_shapes=[pltpu.VMEM((2, _tile_buffer_rows(tm) // SEG_ALIGN, SEG_ALIGN, d // 2),
                                   jnp.uint32),
                        pltpu.VMEM((ROW_BLOCK // SEG_ALIGN, SEG_ALIGN, d // 2), jnp.uint32),
                        pltpu.SMEM((1,), jnp.int32),
                        pltpu.SemaphoreType.DMA(()), pltpu.SemaphoreType.DMA(())],
    )
    xs = pl.pallas_call(
        _dispatch_kernel,
        grid_spec=grid_spec,
        out_shape=jax.ShapeDtypeStruct((n_blocks * ROW_BLOCK // SEG_ALIGN, SEG_ALIGN, d // 2),
                                       jnp.uint32),
        compiler_params=_cparams("arbitrary"),
        name="moe_dispatch",
    )(pad_from // SEG_ALIGN, pad_n // SEG_ALIGN, n_used, pieces, idx, idx_t, hb)
    return xs.reshape(n_blocks * ROW_BLOCK, d // 2)


def _expert_kernel(be_ref, nb_ref, xs_ref, w1_ref, w3_ref, w2_ref, y_ref, w1b, w3b, w2b):
    i = pl.program_id(0)

    @pl.when(jnp.logical_or(i == 0, be_ref[i] != be_ref[jnp.maximum(i - 1, 0)]))
    def _():
        w1b[...] = w1_ref[0, 0].astype(_bf16)
        w3b[...] = w3_ref[0, 0].astype(_bf16)
        w2b[...] = w2_ref[0, 0].astype(_bf16)

    @pl.when(i < nb_ref[0])
    def _():
        chains = [pl.ds(c * EXPERT_ROWS, EXPERT_ROWS) for c in range(ROW_BLOCK // EXPERT_ROWS)]
        xs = [jnp.concatenate(_unpack_rows(xs_ref[rows, :]), axis=1) for rows in chains]
        acts = [(_silu(_dot(x, w1b[...])) * _dot(x, w3b[...])).astype(_bf16) for x in xs]
        for rows, act in zip(chains, acts):
            y_ref[rows, :] = _pack_rows(_dot(act, w2b[...]))

    @pl.when(i >= nb_ref[0])
    def _():
        y_ref[...] = jnp.zeros_like(y_ref)


def moe_experts(xs, blk_e, n_used, w1, w3, w2, layer):
    p, half = xs.shape
    d = 2 * half
    ff = w1.shape[3]
    nb = p // ROW_BLOCK
    row = lambda i, be, nu: (jnp.minimum(i, nu[0] - 1), 0)
    wsel = lambda i, be, nu: (layer, be[i], 0, 0)
    grid_spec = pltpu.PrefetchScalarGridSpec(
        num_scalar_prefetch=2,
        grid=(nb,),
        in_specs=[
            pl.BlockSpec((ROW_BLOCK, half), row),
            pl.BlockSpec((1, 1, d, ff), wsel),
            pl.BlockSpec((1, 1, d, ff), wsel),
            pl.BlockSpec((1, 1, ff, d), wsel),
        ],
        out_specs=pl.BlockSpec((ROW_BLOCK, half), lambda i, be, nu: (i, 0)),
        scratch_shapes=[pltpu.VMEM((d, ff), _bf16), pltpu.VMEM((d, ff), _bf16),
                        pltpu.VMEM((ff, d), _bf16)],
    )
    return pl.pallas_call(
        _expert_kernel,
        grid_spec=grid_spec,
        out_shape=jax.ShapeDtypeStruct((p, half), jnp.uint32),
        compiler_params=_cparams("arbitrary"),
        name="moe_experts",
    )(blk_e, n_used, xs, w1, w3, w2)


def _combine_kernel(piece_ref, next_piece_ref, ys_ref, idx_ref, idx_t_ref, w_ref, shared_ref, x_ref,
                    gate_ref, fn_ref, o_ref, ybuf_ref, sem, *, final_norm):
    tm = x_ref.shape[1]
    buffer_rows = ybuf_ref.shape[1] * SEG_ALIGN
    step = pl.program_id(0) * pl.num_programs(1) + pl.program_id(1)
    n_steps = pl.num_programs(0) * pl.num_programs(1)
    slot = step % 2

    def fetch(table_ref, s):
        def one(g, src, priority):
            pltpu.make_async_copy(ys_ref.at[src], ybuf_ref.at[s, g], sem.at[s]).start(priority)

        _for_each_piece(table_ref, one)

    @pl.when(step == 0)
    def _():
        fetch(piece_ref, 0)

    for s in range(2):
        @pl.when(jnp.logical_and(step + 1 < n_steps, slot == 1 - s))
        def _(s=s):
            fetch(next_piece_ref, s)

    total = _tile_rows(piece_ref)
    _wait_pieces(total // SEG_ALIGN, lambda k: pltpu.make_async_copy(
        ys_ref.at[pl.ds(0, k)], ybuf_ref.at[slot, pl.ds(0, k)], sem.at[slot]))

    idx_t = idx_t_ref[...]
    member = _member_by_token(idx_t)
    before_b = _dot(_strict_lower(tm), member.astype(_bf16)).astype(_bf16)
    lane_e = lax.broadcasted_iota(jnp.int32, member.shape, 1)
    w_of_expert = jnp.zeros(member.shape, _f32)
    for k in range(TOP_K):
        w_of_expert = jnp.where(lane_e == idx_t[:, k:k + 1], w_ref[:, k:k + 1], w_of_expert)
    w_of_expert = w_of_expert.astype(_bf16)
    count = jnp.sum(_member_by_expert(idx_ref[...]), axis=1, keepdims=True)
    seg_lo, seg_hi = _segment_bounds(count, _prefix_over_experts)

    def weights_of_chunk(c):
        expert_of_row, rank_of_row = _experts_of_chunk(c * PERM_ROWS, seg_lo, seg_hi)
        hit = _dot(before_b, expert_of_row) == rank_of_row
        return jnp.where(hit, _dot(w_of_expert, expert_of_row), 0.0).astype(_bf16)

    wm = jnp.concatenate([weights_of_chunk(c) for c in range(buffer_rows // PERM_ROWS)], axis=1)
    packed = ybuf_ref[slot].reshape(buffer_rows, ybuf_ref.shape[3])
    row = lax.broadcasted_iota(jnp.int32, packed.shape, 0)
    packed = jnp.where(row < total, packed, jnp.uint32(0))
    y_lo, y_hi = _unpack_rows(packed)
    routed = jnp.concatenate([_dot(wm, y_lo), _dot(wm, y_hi)], axis=1)

    xn = x_ref[0] + gate_ref[0] * (shared_ref[...] + routed)
    if final_norm:
        xn = xn * lax.rsqrt(jnp.mean(xn * xn, axis=-1, keepdims=True) + EPS) * fn_ref[...]
    o_ref[0] = xn


def moe_combine(pieces, idx, idx_t, ys, top_w, shared, x, gate, fn_g, final_norm):
    b, s, d = x.shape
    tm = TOKEN_TILE
    n_s = s // tm
    tile = lambda i, m: (i * n_s + m, 0)
    last_tile = b * n_s - 1
    return pl.pallas_call(
        functools.partial(_combine_kernel, final_norm=final_norm),
        grid=(b, n_s),
        in_specs=[
            pl.BlockSpec((1, 1, PIECE_SLOTS), lambda i, m: (i * n_s + m, 0, 0),
                         memory_space=pltpu.SMEM),
            pl.BlockSpec((1, 1, PIECE_SLOTS),
                         lambda i, m: (jnp.minimum(i * n_s + m + 1, last_tile), 0, 0),
                         memory_space=pltpu.SMEM),
            pl.BlockSpec(memory_space=pl.ANY),
            pl.BlockSpec((TOP_K, tm), lambda i, m: (0, i * n_s + m)),
            pl.BlockSpec((tm, TOP_K), tile),
            pl.BlockSpec((tm, TOP_K), tile),
            pl.BlockSpec((tm, d), tile),
            pl.BlockSpec((1, tm, d), lambda i, m: (i, m, 0)),
            pl.BlockSpec((1, 1, d), lambda i, m: (i, 0, 0)),
            pl.BlockSpec((1, d), lambda i, m: (0, 0)),
        ],
        out_specs=pl.BlockSpec((1, tm, d), lambda i, m: (i, m, 0)),
        out_shape=jax.ShapeDtypeStruct((b, s, d), _f32),
        scratch_shapes=[pltpu.VMEM((2, _tile_buffer_rows(tm) // SEG_ALIGN, SEG_ALIGN, d // 2),
                                   jnp.uint32),
                        pltpu.SemaphoreType.DMA((2,))],
        compiler_params=_cparams("arbitrary", "arbitrary"),
        name="moe_combine",
    )(pieces, pieces, _as_groups(ys), idx, idx_t, top_w, shared, x, gate, fn_g.reshape(1, d))


def moe_layer(y, w_out, x, gate_a, g, sc, sh, gate, router_w, router_b, w1, w3, w2, layer,
              s_w1, s_w3, s_w2, fn_g, final_norm):
    b, s, d = x.shape
    t = b * s
    x, hb, top_idx, top_w, shared = moe_router(y, w_out, x, gate_a, g, sc, sh, router_w, router_b,
                                               s_w1, s_w3, s_w2)
    pieces, rows = moe_rank(top_idx)
    idx_t = top_idx.T
    n_tiles = pieces.shape[0]

    rows = rows.reshape(N_EXPERTS)
    nblk = (rows + ROW_BLOCK - 1) // ROW_BLOCK
    blk_ends = jnp.cumsum(nblk)
    max_rows = t * TOP_K + n_tiles * N_EXPERTS * (SEG_ALIGN - 1)
    n_blocks = -(-max_rows // ROW_BLOCK) + N_EXPERTS
    blk = jnp.arange(n_blocks, dtype=jnp.int32)
    blk_e = jnp.minimum(jnp.sum(blk[:, None] >= blk_ends[None, :], axis=1),
                        N_EXPERTS - 1).astype(jnp.int32)
    n_used = blk_ends[-1:].astype(jnp.int32)
    pad_from = ((blk_ends - nblk) * ROW_BLOCK + rows).astype(jnp.int32)
    pad_n = (nblk * ROW_BLOCK - rows).astype(jnp.int32)

    xs = moe_dispatch(pieces, top_idx, idx_t, hb, pad_from, pad_n, n_used, n_blocks)
    ys = moe_experts(xs, blk_e, n_used, w1, w3, w2, layer)
    return moe_combine(pieces, top_idx, idx_t, ys, top_w.T, shared, x, gate, fn_g, final_norm)


def kernel(x, c, norm_mix, norm_ffn, ada_w, ada_b, da_w_in, da_w_out, da_lq1, da_lk1, da_lq2,
           da_lk2, da_subln, ret_w_in, ret_w_out, ret_gn, router_w, router_b, exp_w1, exp_w3,
           exp_w2, sh_w1, sh_w3, sh_w2, final_norm):
    depth, d = norm_mix.shape
    b = x.shape[0]
    mod = ada_modulation(c, ada_w, ada_b).reshape(depth, b, 6, 1, d)
    for i in range(depth):
        sh_a, sc_a, g_a, sh_f, sc_f, g_f = (mod[i, :, j] for j in range(6))
        j = i // 2
        if i % 2 == 0:
            proj = norm_proj(x, norm_mix[i], sc_a, sh_a, da_w_in[j].astype(_bf16))
            y = diff_attention_core(proj, da_lq1[j], da_lk1[j], da_lq2[j], da_lk2[j],
                                    da_subln[j], i)
            w_out = da_w_out[j].astype(_bf16)
        else:
            proj = norm_proj(x, norm_mix[i], sc_a, sh_a, ret_w_in[j].astype(_bf16))
            y = retention_core(proj, ret_gn[j])
            w_out = ret_w_out[j].astype(_bf16)
        x = moe_layer(y, w_out, x, g_a, norm_ffn[i], sc_f, sh_f, g_f, router_w[i], router_b[i],
                      exp_w1, exp_w3, exp_w2, i,
                      sh_w1[i], sh_w3[i], sh_w2[i], final_norm, i == depth - 1)
    return x
```
